```python
import jax
import jax.numpy as jnp
from jax import lax
import numpy as np

D_MODEL = 2048
BATCH = 4
SEQ = 4096
DEPTH = 1

CTX_LEN = 256
GRID_W = 64
EPS = 1e-6

GLA_HEADS = 4
GLA_DK = 256
GLA_DV = 512
GLA_LOWRANK = 16
GLA_TAU = 16.0
GLA_CHUNK = 64
GLA_QK_W = GLA_HEADS * GLA_DK
GLA_V_W = GLA_HEADS * GLA_DV

MLA_HEADS = 16
MLA_Q_RANK = 512
MLA_KV_RANK = 512
MLA_NOPE = 128
MLA_ROPE = 64
MLA_V = 128
MLA_QK_DIM = MLA_NOPE + MLA_ROPE
MLA_OUT_W = MLA_HEADS * MLA_V
ROPE_THETA = 10000.0
Q_BLOCK = 128

N_GROUPS = 8
EXP_PER_GROUP = 8
N_EXPERTS = N_GROUPS * EXP_PER_GROUP
TOP_K_IN_GROUP = 2
D_EXPERT = 512
DISPATCH_BLOCK = 128

IN_SPLITS = (GLA_QK_W, GLA_QK_W, GLA_V_W, GLA_V_W, GLA_LOWRANK, GLA_LOWRANK,
             MLA_Q_RANK, MLA_KV_RANK, MLA_ROPE, D_MODEL, D_MODEL)
D_IN = 2 * GLA_QK_W + 2 * GLA_V_W + 2 * GLA_LOWRANK + MLA_Q_RANK + MLA_KV_RANK + MLA_ROPE + 2 * D_MODEL

kernel_name = 'hybrid_gla_mla_hmoe_dit_block'


def rmsnorm(x, g):
    xf = x.astype(jnp.float32)
    y = xf * lax.rsqrt(jnp.mean(xf * xf, axis=-1, keepdims=True) + EPS)
    return (y * g.astype(jnp.float32)).astype(x.dtype)


def modulate(x, g, shift, scale):
    return rmsnorm(x, g) * (1 + scale) + shift


def split_heads(x, n):
    b, l, _ = x.shape
    return x.reshape(b, l, n, -1).transpose(0, 2, 1, 3)


def merge_heads(x):
    b, h, l, d = x.shape
    return x.transpose(0, 2, 1, 3).reshape(b, l, h * d)


def flip_seq(t):
    return jnp.flip(t, axis=2)


def split_in(p):
    offsets = np.cumsum(IN_SPLITS)[:-1].tolist()
    return jnp.split(p, offsets, axis=-1)


def axial_angles(length):
    rows = length // GRID_W
    row = jnp.repeat(jnp.arange(rows), GRID_W).astype(jnp.float32)
    col = jnp.tile(jnp.arange(GRID_W), rows).astype(jnp.float32)
    n_freq = MLA_ROPE // 4
    inv = 1.0 / (ROPE_THETA ** (jnp.arange(n_freq, dtype=jnp.float32) / n_freq))
    return row[:, None] * inv, col[:, None] * inv


def rope_half(x, ang):
    half = x.shape[-1] // 2
    cos = jnp.cos(ang).astype(x.dtype)
    sin = jnp.sin(ang).astype(x.dtype)
    x1, x2 = x[..., :half], x[..., half:]
    return jnp.concatenate([x1 * cos - x2 * sin, x1 * sin + x2 * cos], axis=-1)


def rope_tail_2d(x, angles):
    ang_r, ang_c = angles
    half = MLA_ROPE // 2
    x_nope = x[..., :MLA_NOPE]
    x_row = x[..., MLA_NOPE:MLA_NOPE + half]
    x_col = x[..., MLA_NOPE + half:]
    return jnp.concatenate([x_nope, rope_half(x_row, ang_r), rope_half(x_col, ang_c)], axis=-1)


def to_heads_f32(t, n):
    return split_heads(t, n).astype(jnp.float32)


def gla_log_decay(a_low, w2, b2):
    z = (a_low @ w2 + b2).astype(jnp.float32)
    return split_heads(jax.nn.log_sigmoid(z) / GLA_TAU, GLA_HEADS)


def gla_chunked(q, k, v, g, s0):
    b_, h_, l_, dk = q.shape
    dv = v.shape[-1]
    n = l_ // GLA_CHUNK
    q, k, g = (a.reshape(b_, h_, n, GLA_CHUNK, dk) for a in (q, k, g))
    v = v.reshape(b_, h_, n, GLA_CHUNK, dv)
    bc = jnp.cumsum(g, axis=3)
    b_end = bc[:, :, :, -1:, :]
    q_dec = q * jnp.exp(bc)
    k_inv = k * jnp.exp(-bc)
    k_to_end = k * jnp.exp(b_end - bc)
    mask = jnp.tril(jnp.ones((GLA_CHUNK, GLA_CHUNK), dtype=bool))
    att = jnp.where(mask, jnp.einsum('bhnid,bhnjd->bhnij', q_dec, k_inv), 0.0)
    o_intra = jnp.einsum('bhnij,bhnje->bhnie', att, v)

    def step(state, xs):
        qd, ke, vc, de = xs
        o = jnp.einsum('bhid,bhde->bhie', qd, state)
        state = state * de[..., None] + jnp.einsum('bhjd,bhje->bhde', ke, vc)
        return state, o

    xs = tuple(jnp.moveaxis(t, 2, 0) for t in (q_dec, k_to_end, v, jnp.exp(b_end[:, :, :, 0, :])))
    s_end, o_inter = lax.scan(step, s0, xs)
    o = o_intra + jnp.moveaxis(o_inter, 0, 2)
    return o.reshape(b_, h_, l_, dv), s_end


def gla_final_state(k, v, g):
    bc = jnp.cumsum(g, axis=2)
    return jnp.einsum('bhld,bhle->bhde', k * jnp.exp(bc[:, :, -1:, :] - bc), v)


def gla_readout(o, norm_g, r):
    return merge_heads(rmsnorm(o, norm_g)).astype(r.dtype) * jax.nn.silu(r)


def mla_queries(cq, qa_g, w_uq, qn_g, angles):
    q = split_heads(rmsnorm(cq, qa_g) @ w_uq, MLA_HEADS)
    q = rmsnorm(q, qn_g)
    return q if angles is None else rope_tail_2d(q, angles)


def mla_keys_values(ckv, kr, kva_g, w_ukv, kn_g, angles):
    kv = split_heads(rmsnorm(ckv, kva_g) @ w_ukv, MLA_HEADS)
    k_nope, v = kv[..., :MLA_NOPE], kv[..., MLA_NOPE:]
    b_, h_, l_, _ = k_nope.shape
    k_rope = jnp.broadcast_to(kr[:, None, :, :], (b_, h_, l_, MLA_ROPE))
    k = rmsnorm(jnp.concatenate([k_nope, k_rope], axis=-1), kn_g)
    return (k if angles is None else rope_tail_2d(k, angles)), v


def softmax_attention(q, k, v):
    s = jnp.einsum('bhqd,bhkd->bhqk', q, k).astype(jnp.float32) * (MLA_QK_DIM ** -0.5)
    p = jax.nn.softmax(s, axis=-1).astype(v.dtype)
    return jnp.einsum('bhqk,bhkd->bhqd', p, v)


def blockwise_attention(q, k, v):
    b_, h_, l_, d_ = q.shape
    nb = l_ // Q_BLOCK
    qb = q.reshape(b_, h_, nb, Q_BLOCK, d_).transpose(2, 0, 1, 3, 4)
    ob = lax.map(lambda qi: softmax_attention(qi, k, v), qb)
    return ob.transpose(1, 2, 0, 3, 4).reshape(b_, h_, l_, v.shape[-1])


def branch_merge(y_gla, y_mla, gate_gla, gate_mla, w_o_gla, w_o_mla, w_out):
    y = jax.nn.sigmoid(gate_gla) * (y_gla @ w_o_gla) + jax.nn.sigmoid(gate_mla) * (y_mla @ w_o_mla)
    return y @ w_out


def hierarchical_moe(h, w_rg, b_rg, w_re, b_re, w_gate, w_up, w_down):
    b_, l_, d_ = h.shape
    t = b_ * l_
    hf = h.reshape(t, d_)
    p_grp = jax.nn.softmax((hf @ w_rg + b_rg).astype(jnp.float32), axis=-1)
    p_g, g_idx = lax.top_k(p_grp, 1)
    logit_e = (hf @ w_re + b_re).astype(jnp.float32).reshape(t, N_GROUPS, EXP_PER_GROUP)
    logit_e = jnp.take_along_axis(logit_e, g_idx[:, :, None], axis=1)[:, 0, :]
    p_e, e_idx = lax.top_k(jax.nn.softmax(logit_e, axis=-1), TOP_K_IN_GROUP)
    wts = (p_g * p_e / jnp.sum(p_e, axis=-1, keepdims=True)).reshape(-1)
    eid = (g_idx * EXP_PER_GROUP + e_idx).reshape(-1)
    n_assign = t * TOP_K_IN_GROUP
    tok = jnp.repeat(jnp.arange(t, dtype=jnp.int32), TOP_K_IN_GROUP)
    order = jnp.argsort(eid)
    e_s, tok_s, w_s = eid[order], tok[order], wts[order]
    counts = jnp.zeros((N_EXPERTS,), jnp.int32).at[eid].add(1)
    start = jnp.cumsum(counts) - counts
    padded = (counts + DISPATCH_BLOCK - 1) // DISPATCH_BLOCK * DISPATCH_BLOCK
    pend = jnp.cumsum(padded)
    pstart = pend - padded
    dest = pstart[e_s] + jnp.arange(n_assign, dtype=jnp.int32) - start[e_s]
    n_blocks = (n_assign + DISPATCH_BLOCK - 1) // DISPATCH_BLOCK + N_EXPERTS
    slot_tok = jnp.full((n_blocks * DISPATCH_BLOCK,), t, jnp.int32).at[dest].set(tok_s)
    slot_w = jnp.zeros((n_blocks * DISPATCH_BLOCK,), h.dtype).at[dest].set(w_s.astype(h.dtype))
    block_e = jnp.minimum(
        jnp.searchsorted(pend, jnp.arange(n_blocks, dtype=jnp.int32) * DISPATCH_BLOCK, side='right'),
        N_EXPERTS - 1)
    h_pad = jnp.concatenate([hf, jnp.zeros((1, d_), h.dtype)], axis=0)

    def expert_block(args):
        idx, w, e = args
        xb = h_pad[idx]
        y = (jax.nn.silu(xb @ w_gate[e]) * (xb @ w_up[e])) @ w_down[e]
        return y * w[:, None]

    y = lax.map(expert_block, (slot_tok.reshape(n_blocks, DISPATCH_BLOCK),
                               slot_w.reshape(n_blocks, DISPATCH_BLOCK), block_e))
    out = jnp.zeros((t + 1, d_), h.dtype).at[slot_tok].add(y.reshape(-1, d_))
    return out[:t].reshape(b_, l_, d_)


def setup_inputs(seed: int = 0) -> dict:
    key = jax.random.key(seed)
    ks = jax.random.split(key, 32)
    d = D_MODEL

    def nrm(k, shape, scale):
        return jax.random.normal(k, shape, jnp.float32) * scale

    def gain(k, n):
        return 1.0 + nrm(k, (DEPTH, n), 0.02)

    return {
        'x': nrm(ks[0], (BATCH, SEQ, d), 1.0),
        'c': nrm(ks[1], (BATCH, d), 1.0),
        'ctx': nrm(ks[2], (BATCH, CTX_LEN, d), 1.0),
        'c_ctx': nrm(ks[3], (d,), 1.0),
        'w_mod': nrm(ks[4], (DEPTH, d, 6 * d), 0.5 * d ** -0.5),
        'b_mod': nrm(ks[5], (DEPTH, 6 * d), 0.02),
        'norm1_g': gain(ks[6], d),
        'norm2_g': gain(ks[7], d),
        'w_in': nrm(ks[8], (DEPTH, d, D_IN), d ** -0.5),
        'w_decay_f': nrm(ks[9], (DEPTH, GLA_LOWRANK, GLA_QK_W), GLA_LOWRANK ** -0.5),
        'b_decay_f': nrm(ks[10], (DEPTH, GLA_QK_W), 0.1),
        'w_decay_b': nrm(ks[11], (DEPTH, GLA_LOWRANK, GLA_QK_W), GLA_LOWRANK ** -0.5),
        'b_decay_b': nrm(ks[12], (DEPTH, GLA_QK_W), 0.1),
        'gla_norm_g': gain(ks[13], GLA_DV),
        'q_a_norm_g': gain(ks[14], MLA_Q_RANK),
        'w_uq': nrm(ks[15], (DEPTH, MLA_Q_RANK, MLA_HEADS * MLA_QK_DIM), MLA_Q_RANK ** -0.5),
        'kv_a_norm_g': gain(ks[16], MLA_KV_RANK),
        'w_ukv': nrm(ks[17], (DEPTH, MLA_KV_RANK, MLA_HEADS * (MLA_NOPE + MLA_V)), MLA_KV_RANK ** -0.5),
        'q_norm_g': gain(ks[18], MLA_QK_DIM),
        'k_norm_g': gain(ks[19], MLA_QK_DIM),
        'w_o_gla': nrm(ks[20], (DEPTH, GLA_V_W, d), GLA_V_W ** -0.5),
        'w_o_mla': nrm(ks[21], (DEPTH, MLA_OUT_W, d), MLA_OUT_W ** -0.5),
        'w_out': nrm(ks[22], (DEPTH, d, d), d ** -0.5),
        'w_router_group': nrm(ks[23], (DEPTH, d, N_GROUPS), d ** -0.5),
        'b_router_group': nrm(ks[24], (DEPTH, N_GROUPS), 0.01),
        'w_router_expert': nrm(ks[25], (DEPTH, d, N_EXPERTS), d ** -0.5),
        'b_router_expert': nrm(ks[26], (DEPTH, N_EXPERTS), 0.01),
        'w_exp_gate': nrm(ks[27], (DEPTH, N_EXPERTS, d, D_EXPERT), d ** -0.5),
        'w_exp_up': nrm(ks[28], (DEPTH, N_EXPERTS, d, D_EXPERT), d ** -0.5),
        'w_exp_down': nrm(ks[29], (DEPTH, N_EXPERTS, D_EXPERT, d), D_EXPERT ** -0.5),
    }


def reference(x, c, ctx, c_ctx, w_mod, b_mod, norm1_g, norm2_g, w_in,
              w_decay_f, b_decay_f, w_decay_b, b_decay_b, gla_norm_g,
              q_a_norm_g, w_uq, kv_a_norm_g, w_ukv, q_norm_g, k_norm_g,
              w_o_gla, w_o_mla, w_out,
              w_router_group, b_router_group, w_router_expert, b_router_expert,
              w_exp_gate, w_exp_up, w_exp_down):
    angles = axial_angles(x.shape[1])
    for l in range(DEPTH):
        update_ctx = l < DEPTH - 1
        sh1, sc1, gt1, sh2, sc2, gt2 = jnp.split(
            (jax.nn.silu(c) @ w_mod[l] + b_mod[l])[:, None, :], 6, axis=-1)
        sh1c, sc1c, gt1c, sh2c, sc2c, gt2c = jnp.split(
            (jax.nn.silu(c_ctx) @ w_mod[l] + b_mod[l])[None, None, :], 6, axis=-1)

        h = modulate(x, norm1_g[l], sh1, sc1)
        hc = modulate(ctx, norm1_g[l], sh1c, sc1c)
        qg, kg, vg, rg, af, ab, cq, ckv, kr, ga, gb = split_in(h @ w_in[l])
        qgc, kgc, vgc, rgc, afc, abc, cqc, ckvc, krc, gac, gbc = split_in(hc @ w_in[l])

        q = to_heads_f32(qg, GLA_HEADS) * GLA_DK ** -0.5
        k = to_heads_f32(kg, GLA_HEADS)
        v = to_heads_f32(vg, GLA_HEADS)
        g_f = gla_log_decay(af, w_decay_f[l], b_decay_f[l])
        g_b = gla_log_decay(ab, w_decay_b[l], b_decay_b[l])
        kc = to_heads_f32(kgc, GLA_HEADS)
        vc = to_heads_f32(vgc, GLA_HEADS)
        gc_f = gla_log_decay(afc, w_decay_f[l], b_decay_f[l])
        gc_b = gla_log_decay(abc, w_decay_b[l], b_decay_b[l])
        if update_ctx:
            qc = to_heads_f32(qgc, GLA_HEADS) * GLA_DK ** -0.5
            zero = jnp.zeros(kc.shape[:2] + (GLA_DK, GLA_DV), jnp.float32)
            oc_f, s_f = gla_chunked(qc, kc, vc, gc_f, zero)
            oc_b, s_b = gla_chunked(flip_seq(qc), flip_seq(kc), flip_seq(vc), flip_seq(gc_b), zero)
            oc_gla = oc_f + flip_seq(oc_b)
        else:
            s_f = gla_final_state(kc, vc, gc_f)
            s_b = gla_final_state(flip_seq(kc), flip_seq(vc), flip_seq(gc_b))
        o_f, _ = gla_chunked(q, k, v, g_f, s_f)
        o_b, _ = gla_chunked(flip_seq(q), flip_seq(k), flip_seq(v), flip_seq(g_b), s_b)
        y_gla = gla_readout(o_f + flip_seq(o_b), gla_norm_g[l], rg)

        q_m = mla_queries(cq, q_a_norm_g[l], w_uq[l], q_norm_g[l], angles)
        k_m, v_m = mla_keys_values(ckv, kr, kv_a_norm_g[l], w_ukv[l], k_norm_g[l], angles)
        k_mc, v_mc = mla_keys_values(ckvc, krc, kv_a_norm_g[l], w_ukv[l], k_norm_g[l], None)
        y_mla = merge_heads(blockwise_attention(
            q_m, jnp.concatenate([k_m, k_mc], axis=2), jnp.concatenate([v_m, v_mc], axis=2)))

        if update_ctx:
            q_mc = mla_queries(cqc, q_a_norm_g[l], w_uq[l], q_norm_g[l], None)
            yc_mla = merge_heads(softmax_attention(q_mc, k_mc, v_mc))
            yc_gla = gla_readout(oc_gla, gla_norm_g[l], rgc)
            ctx = ctx + gt1c * branch_merge(yc_gla, yc_mla, gac, gbc, w_o_gla[l], w_o_mla[l], w_out[l])
            ctx = ctx + gt2c * hierarchical_moe(
                modulate(ctx, norm2_g[l], sh2c, sc2c), w_router_group[l], b_router_group[l],
                w_router_expert[l], b_router_expert[l], w_exp_gate[l], w_exp_up[l], w_exp_down[l])

        x = x + gt1 * branch_merge(y_gla, y_mla, ga, gb, w_o_gla[l], w_o_mla[l], w_out[l])
        x = x + gt2 * hierarchical_moe(
            modulate(x, norm2_g[l], sh2, sc2), w_router_group[l], b_router_group[l],
            w_router_expert[l], b_router_expert[l], w_exp_gate[l], w_exp_up[l], w_exp_down[l])
    return x
```

```python
import functools

import jax
import jax.numpy as jnp
from jax import lax
from jax.experimental import pallas as pl
from jax.experimental.pallas import tpu as pltpu

F32 = jnp.float32
BF16 = jnp.bfloat16

D_MODEL = 2048
GRID_W = 64
EPS = 1e-6

GLA_HEADS = 4
GLA_DK = 256
GLA_DV = 512
GLA_LOWRANK = 16
GLA_TAU = 16.0
GLA_CHUNK = 64
GLA_QK_W = GLA_HEADS * GLA_DK
GLA_V_W = GLA_HEADS * GLA_DV

MLA_HEADS = 16
MLA_Q_RANK = 512
MLA_KV_RANK = 512
MLA_NOPE = 128
MLA_ROPE = 64
MLA_V = 128
MLA_QK_DIM = MLA_NOPE + MLA_ROPE
MLA_QK_PAD = 256
ROPE_THETA = 10000.0

N_GROUPS = 8
EXP_PER_GROUP = 8
N_EXPERTS = N_GROUPS * EXP_PER_GROUP
D_EXPERT = 512
ROUTER_W = 128
MOE_BLOCK = 256

LANE = 128
VMEM_LIMIT = 56 * 1024 * 1024

COL_Q = 0
COL_K = COL_Q + GLA_QK_W
COL_V = COL_K + GLA_QK_W
COL_R = COL_V + GLA_V_W
COL_GA = COL_R + GLA_V_W
COL_GB = COL_GA + D_MODEL
COL_CQ = COL_GB + D_MODEL
COL_CKV = COL_CQ + MLA_Q_RANK
COL_SMALL = COL_CKV + MLA_KV_RANK
SMALL_W = 256
NP_LAT = COL_SMALL + SMALL_W
SM_AF = MLA_ROPE
SM_AB = MLA_ROPE + GLA_LOWRANK
CCOL_K = 0
CCOL_V = CCOL_K + GLA_QK_W
CCOL_CKV = CCOL_V + GLA_V_W
CCOL_SMALL = CCOL_CKV + MLA_KV_RANK
NP_CTX = CCOL_SMALL + SMALL_W
INPROJ_TN = 1280


def _cparams(sem):
    return pltpu.CompilerParams(dimension_semantics=sem, vmem_limit_bytes=VMEM_LIMIT)


def _silu(x):
    return x * jax.nn.sigmoid(x)


def _mod_kernel(c_ref, w_ref, b_ref, o_ref):
    a = _silu(c_ref[...]).astype(BF16)
    o_ref[...] = jnp.dot(a, w_ref[...].astype(BF16), preferred_element_type=F32) + b_ref[...]


def _modulation(c_rows, w_mod, b_mod):
    m, d = c_rows.shape
    n = w_mod.shape[1]
    tn = 1024
    return pl.pallas_call(
        _mod_kernel,
        grid=(n // tn,),
        in_specs=[pl.BlockSpec((m, d), lambda j: (0, 0)),
                  pl.BlockSpec((d, tn), lambda j: (0, j)),
                  pl.BlockSpec((1, tn), lambda j: (0, j))],
        out_specs=pl.BlockSpec((m, tn), lambda j: (0, j)),
        out_shape=jax.ShapeDtypeStruct((m, n), F32),
        compiler_params=_cparams(("parallel",)),
        name="modulation",
    )(c_rows, w_mod, b_mod.reshape(1, n))


def _inproj_kernel(x_ref, g_ref, sc_ref, sh_ref, w_ref, o_ref, h_ref):
    @pl.when(pl.program_id(2) == 0)
    def _():
        x = x_ref[...]
        ms = jnp.mean(x * x, axis=-1, keepdims=True)
        y = x * lax.rsqrt(ms + EPS) * g_ref[...]
        h_ref[...] = (y * (1.0 + sc_ref[...]) + sh_ref[...]).astype(BF16)

    o_ref[...] = jnp.dot(h_ref[...], w_ref[...], preferred_element_type=F32).astype(o_ref.dtype)


def _in_projection(x, g, sc, sh, w, tm):
    b, l, d = x.shape
    tm = min(tm, l)
    n = w.shape[1]
    tn = INPROJ_TN
    return pl.pallas_call(
        _inproj_kernel,
        grid=(b, l // tm, n // tn),
        in_specs=[pl.BlockSpec((None, tm, d), lambda bi, i, j: (bi, i, 0)),
                  pl.BlockSpec((1, d), lambda bi, i, j: (0, 0)),
                  pl.BlockSpec((None, 1, d), lambda bi, i, j: (bi, 0, 0)),
                  pl.BlockSpec((None, 1, d), lambda bi, i, j: (bi, 0, 0)),
                  pl.BlockSpec((d, tn), lambda bi, i, j: (0, j))],
        out_specs=pl.BlockSpec((None, tm, tn), lambda bi, i, j: (bi, i, j)),
        out_shape=jax.ShapeDtypeStruct((b, l, n), BF16),
        scratch_shapes=[pltpu.VMEM((tm, d), BF16)],
        compiler_params=_cparams(("parallel", "parallel", "arbitrary")),
        name="in_projection",
    )(x, g, sc, sh, w)


def _gla_kernel(q_ref, k_ref, v_ref, r_ref, sm_ref, kc_ref, vc_ref, smc_ref,
                wf_ref, bf_ref, wb_ref, bb_ref, gn_ref, y_ref, of_ref, st_ref):
    c = GLA_CHUNK
    n_lat = q_ref.shape[0] // c
    n_ctx = kc_ref.shape[0] // c
    row = lax.broadcasted_iota(jnp.int32, (c, c), 0)
    col = lax.broadcasted_iota(jnp.int32, (c, c), 1)
    lower = row >= col
    upper = row <= col
    nt = (((1,), (1,)), ((), ()))
    tn = (((0,), (0,)), ((), ()))

    def cum_decay(sm, w_ref, b_ref, tri):
        z = jnp.dot(sm, w_ref[...], preferred_element_type=F32) + b_ref[...]
        g = (jnp.minimum(z, 0.0) - jnp.log(1.0 + jnp.exp(-jnp.abs(z)))) * (1.0 / GLA_TAU)
        g_hi = g.astype(BF16)
        g_lo = (g - g_hi.astype(F32)).astype(BF16)
        t = jnp.where(tri, 1.0, 0.0).astype(BF16)
        return (jnp.dot(t, g_hi, preferred_element_type=F32)
                + jnp.dot(t, g_lo, preferred_element_type=F32))

    def state_update(k, v, bc, bend):
        ke = (k * jnp.exp(bend - bc)).astype(BF16)
        st_ref[...] = (st_ref[...] * jnp.exp(bend)
                       + lax.dot_general(v, ke, tn, preferred_element_type=F32))

    def ctx_step(n, w_ref, b_ref, tri, end_row):
        r0 = pl.multiple_of(n * c, c)
        bc = cum_decay(smc_ref[pl.ds(r0, c), :], w_ref, b_ref, tri)
        bend = bc[end_row:end_row + 1, :]
        state_update(kc_ref[pl.ds(r0, c), :].astype(F32), vc_ref[pl.ds(r0, c), :], bc, bend)

    def lat_step(n, w_ref, b_ref, tri, mask, end_row):
        r0 = pl.multiple_of(n * c, c)
        bc = cum_decay(sm_ref[pl.ds(r0, c), :], w_ref, b_ref, tri)
        bend = bc[end_row:end_row + 1, :]
        q = q_ref[pl.ds(r0, c), :].astype(F32) * (GLA_DK ** -0.5)
        k = k_ref[pl.ds(r0, c), :].astype(F32)
        v = v_ref[pl.ds(r0, c), :]
        qd = (q * jnp.exp(bc)).astype(BF16)
        ki = (k * jnp.exp(-bc)).astype(BF16)
        att = lax.dot_general(qd, ki, nt, preferred_element_type=F32)
        att = jnp.where(mask, att, 0.0).astype(BF16)
        o = (jnp.dot(att, v, preferred_element_type=F32)
             + lax.dot_general(qd, st_ref[...].astype(BF16), nt, preferred_element_type=F32))
        state_update(k, v, bc, bend)
        return r0, o

    st_ref[...] = jnp.zeros_like(st_ref)

    def ctx_fwd(n, carry):
        ctx_step(n, wf_ref, bf_ref, lower, c - 1)
        return carry

    lax.fori_loop(0, n_ctx, ctx_fwd, 0)

    def lat_fwd(n, carry):
        r0, o = lat_step(n, wf_ref, bf_ref, lower, lower, c - 1)
        of_ref[pl.ds(r0, c), :] = o
        return carry

    lax.fori_loop(0, n_lat, lat_fwd, 0)

    st_ref[...] = jnp.zeros_like(st_ref)

    def ctx_bwd(n, carry):
        ctx_step(n_ctx - 1 - n, wb_ref, bb_ref, upper, 0)
        return carry

    lax.fori_loop(0, n_ctx, ctx_bwd, 0)

    def lat_bwd(n, carry):
        r0, o_b = lat_step(n_lat - 1 - n, wb_ref, bb_ref, upper, upper, 0)
        o = of_ref[pl.ds(r0, c), :] + o_b
        ms = jnp.mean(o * o, axis=-1, keepdims=True)
        yn = o * lax.rsqrt(ms + EPS) * gn_ref[...]
        r = r_ref[pl.ds(r0, c), :].astype(F32)
        y_ref[pl.ds(r0, c), :] = (yn * _silu(r)).astype(y_ref.dtype)
        return carry

    lax.fori_loop(0, n_lat, lat_bwd, 0)


def _gla(p, pc, wf, bf, wb, bb, gn):
    b, l, _ = p.shape
    lc = pc.shape[1]
    dk, dv = GLA_DK, GLA_DV
    hmap = lambda off: (lambda bi, h: (bi, 0, off + h))
    wmap = lambda bi, h: (0, h)
    return pl.pallas_call(
        _gla_kernel,
        grid=(b, GLA_HEADS),
        in_specs=[pl.BlockSpec((None, l, dk), hmap(COL_Q // dk)),
                  pl.BlockSpec((None, l, dk), hmap(COL_K // dk)),
                  pl.BlockSpec((None, l, dv), hmap(COL_V // dv)),
                  pl.BlockSpec((None, l, dv), hmap(COL_R // dv)),
                  pl.BlockSpec((None, l, LANE), lambda bi, h: (bi, 0, COL_SMALL // LANE)),
                  pl.BlockSpec((None, lc, dk), hmap(CCOL_K // dk)),
                  pl.BlockSpec((None, lc, dv), hmap(CCOL_V // dv)),
                  pl.BlockSpec((None, lc, LANE), lambda bi, h: (bi, 0, CCOL_SMALL // LANE)),
                  pl.BlockSpec((LANE, dk), wmap),
                  pl.BlockSpec((1, dk), wmap),
                  pl.BlockSpec((LANE, dk), wmap),
                  pl.BlockSpec((1, dk), wmap),
                  pl.BlockSpec((1, dv), lambda bi, h: (0, 0))],
        out_specs=pl.BlockSpec((None, l, dv), lambda bi, h: (bi, 0, h)),
        out_shape=jax.ShapeDtypeStruct((b, l, GLA_V_W), BF16),
        scratch_shapes=[pltpu.VMEM((l, dv), F32), pltpu.VMEM((dv, dk), F32)],
        compiler_params=_cparams(("parallel", "parallel")),
        name="gla",
    )(p, p, p, p, p, pc, pc, pc, wf, bf, wb, bb, gn)


def _rope_partner(t):
    lane = lax.broadcasted_iota(jnp.int32, t.shape, 1)
    first = (lane % 32) < 16
    return jnp.where(first, pltpu.roll(t, LANE - 16, 1), pltpu.roll(t, 16, 1))


def _mla_q_kernel(cq_ref, ga_ref, w_ref, gn_ref, cos_ref, sin_ref, o_ref):
    cq = cq_ref[...].astype(F32)
    cn = (cq * lax.rsqrt(jnp.mean(cq * cq, axis=-1, keepdims=True) + EPS) * ga_ref[...]).astype(BF16)
    cos = cos_ref[...]
    sin = sin_ref[...]
    scale = MLA_QK_DIM ** -0.5
    for h in range(MLA_HEADS):
        qh = jnp.dot(cn, w_ref[:, h * MLA_QK_PAD:(h + 1) * MLA_QK_PAD], preferred_element_type=F32)
        ms = jnp.sum(qh * qh, axis=-1, keepdims=True) * (1.0 / MLA_QK_DIM)
        qn = qh * lax.rsqrt(ms + EPS) * gn_ref[...]
        t = qn[:, MLA_NOPE:]
        rot = t * cos + _rope_partner(t) * sin
        o_ref[h, :, :MLA_NOPE] = (qn[:, :MLA_NOPE] * scale).astype(o_ref.dtype)
        o_ref[h, :, MLA_NOPE:] = (rot * scale).astype(o_ref.dtype)


def _mla_queries(p, ga, wq, gn, cos, sin, tm):
    b, l, _ = p.shape
    tm = min(tm, l)
    return pl.pallas_call(
        _mla_q_kernel,
        grid=(b, l // tm),
        in_specs=[pl.BlockSpec((None, tm, MLA_Q_RANK), lambda bi, i: (bi, i, COL_CQ // MLA_Q_RANK)),
                  pl.BlockSpec((1, MLA_Q_RANK), lambda bi, i: (0, 0)),
                  pl.BlockSpec(wq.shape, lambda bi, i: (0, 0)),
                  pl.BlockSpec((1, MLA_QK_PAD), lambda bi, i: (0, 0)),
                  pl.BlockSpec((tm, LANE), lambda bi, i: (i, 0)),
                  pl.BlockSpec((tm, LANE), lambda bi, i: (i, 0))],
        out_specs=pl.BlockSpec((None, MLA_HEADS, tm, MLA_QK_PAD), lambda bi, i: (bi, 0, i, 0)),
        out_shape=jax.ShapeDtypeStruct((b, MLA_HEADS, l, MLA_QK_PAD), BF16),
        compiler_params=_cparams(("parallel", "parallel")),
        name="mla_queries",
    )(p, ga, wq, gn, cos, sin)


def _mla_kv_kernel(ckv_ref, sm_ref, ga_ref, w_ref, gk_ref, gr_ref, cos_ref, sin_ref, k_ref, v_ref):
    ckv = ckv_ref[...].astype(F32)
    cn = (ckv * lax.rsqrt(jnp.mean(ckv * ckv, axis=-1, keepdims=True) + EPS) * ga_ref[...]).astype(BF16)
    sm = sm_ref[...].astype(F32)
    lane = lax.broadcasted_iota(jnp.int32, sm.shape, 1)
    kr = jnp.where(lane < MLA_ROPE, sm, 0.0)
    ss_r = jnp.sum(kr * kr, axis=-1, keepdims=True)
    krg = kr * gr_ref[...]
    rot = krg * cos_ref[...] + _rope_partner(krg) * sin_ref[...]
    hw = MLA_NOPE + MLA_V
    for h in range(MLA_HEADS):
        kvh = jnp.dot(cn, w_ref[:, h * hw:(h + 1) * hw], preferred_element_type=F32)
        kn = kvh[:, :MLA_NOPE]
        ms = (jnp.sum(kn * kn, axis=-1, keepdims=True) + ss_r) * (1.0 / MLA_QK_DIM)
        rs = lax.rsqrt(ms + EPS)
        k_ref[h, :, :MLA_NOPE] = (kn * rs * gk_ref[...]).astype(k_ref.dtype)
        k_ref[h, :, MLA_NOPE:] = (rot * rs).astype(k_ref.dtype)
        v_ref[h] = kvh[:, MLA_NOPE:].astype(v_ref.dtype)


def _mla_keys_values(p, ckv_blk, small_blk, ga, wkv, gk, gr, cos, sin, tm):
    b, l, _ = p.shape
    tm = min(tm, l)
    return pl.pallas_call(
        _mla_kv_kernel,
        grid=(b, l // tm),
        in_specs=[pl.BlockSpec((None, tm, MLA_KV_RANK), lambda bi, i: (bi, i, ckv_blk)),
                  pl.BlockSpec((None, tm, LANE), lambda bi, i: (bi, i, small_blk)),
                  pl.BlockSpec((1, MLA_KV_RANK), lambda bi, i: (0, 0)),
                  pl.BlockSpec(wkv.shape, lambda bi, i: (0, 0)),
                  pl.BlockSpec((1, LANE), lambda bi, i: (0, 0)),
                  pl.BlockSpec((1, LANE), lambda bi, i: (0, 0)),
                  pl.BlockSpec((tm, LANE), lambda bi, i: (i, 0)),
                  pl.BlockSpec((tm, LANE), lambda bi, i: (i, 0))],
        out_specs=[pl.BlockSpec((None, MLA_HEADS, tm, MLA_QK_PAD), lambda bi, i: (bi, 0, i, 0)),
                   pl.BlockSpec((None, MLA_HEADS, tm, MLA_V), lambda bi, i: (bi, 0, i, 0))],
        out_shape=[jax.ShapeDtypeStruct((b, MLA_HEADS, l, MLA_QK_PAD), BF16),
                   jax.ShapeDtypeStruct((b, MLA_HEADS, l, MLA_V), BF16)],
        compiler_params=_cparams(("parallel", "parallel")),
        name="mla_keys_values",
    )(p, p, ga, wkv, gk, gr, cos, sin)


def _attn_kernel(q_ref, k_ref, v_ref, kc_ref, vc_ref, o_ref):
    nt = (((1,), (1,)), ((), ()))
    q = q_ref[...]
    s1 = lax.dot_general(q, k_ref[...], nt, preferred_element_type=F32)
    s2 = lax.dot_general(q, kc_ref[...], nt, preferred_element_type=F32)
    m = jnp.maximum(jnp.max(s1, axis=-1, keepdims=True), jnp.max(s2, axis=-1, keepdims=True))
    p1 = jnp.exp(s1 - m)
    p2 = jnp.exp(s2 - m)
    den = jnp.sum(p1, axis=-1, keepdims=True) + jnp.sum(p2, axis=-1, keepdims=True)
    o = (jnp.dot(p1.astype(BF16), v_ref[...], preferred_element_type=F32)
         + jnp.dot(p2.astype(BF16), vc_ref[...], preferred_element_type=F32))
    o_ref[...] = (o / den).astype(o_ref.dtype)


def _attention(q, k, v, kc, vc, tq):
    b, h, l, dq = q.shape
    tq = min(tq, l)
    lc = kc.shape[2]
    dv = v.shape[3]
    kvmap = lambda bi, hi, i: (bi, hi, 0, 0)
    return pl.pallas_call(
        _attn_kernel,
        grid=(b, h, l // tq),
        in_specs=[pl.BlockSpec((None, None, tq, dq), lambda bi, hi, i: (bi, hi, i, 0)),
                  pl.BlockSpec((None, None, l, dq), kvmap),
                  pl.BlockSpec((None, None, l, dv), kvmap),
                  pl.BlockSpec((None, None, lc, dq), kvmap),
                  pl.BlockSpec((None, None, lc, dv), kvmap)],
        out_specs=pl.BlockSpec((None, tq, dv), lambda bi, hi, i: (bi, i, hi)),
        out_shape=jax.ShapeDtypeStruct((b, l, h * dv), BF16),
        compiler_params=_cparams(("parallel", "parallel", "arbitrary")),
        name="mla_attention",
    )(q, k, v, kc, vc)


def _merge_kernel(yg_ref, ym_ref, wg_ref, wm_ref, ga_ref, gb_ref, o_ref):
    a = jnp.dot(yg_ref[...], wg_ref[...], preferred_element_type=F32)
    m = jnp.dot(ym_ref[...], wm_ref[...], preferred_element_type=F32)
    y = (jax.nn.sigmoid(ga_ref[...].astype(F32)) * a
         + jax.nn.sigmoid(gb_ref[...].astype(F32)) * m)
    o_ref[...] = y.astype(o_ref.dtype)


def _branch_merge(yg, ym, wg, wm, p, tm, tn):
    b, l, d = yg.shape
    tm = min(tm, l)
    n = wg.shape[1]
    return pl.pallas_call(
        _merge_kernel,
        grid=(b, l // tm, n // tn),
        in_specs=[pl.BlockSpec((None, tm, d), lambda bi, i, j: (bi, i, 0)),
                  pl.BlockSpec((None, tm, d), lambda bi, i, j: (bi, i, 0)),
                  pl.BlockSpec((d, tn), lambda bi, i, j: (0, j)),
                  pl.BlockSpec((d, tn), lambda bi, i, j: (0, j)),
                  pl.BlockSpec((None, tm, tn), lambda bi, i, j: (bi, i, COL_GA // tn + j)),
                  pl.BlockSpec((None, tm, tn), lambda bi, i, j: (bi, i, COL_GB // tn + j))],
        out_specs=pl.BlockSpec((None, tm, tn), lambda bi, i, j: (bi, i, j)),
        out_shape=jax.ShapeDtypeStruct((b, l, n), BF16),
        compiler_params=_cparams(("parallel", "parallel", "arbitrary")),
        name="branch_merge",
    )(yg, ym, wg, wm, p, p)


def _outproj_kernel(y_ref, w_ref, x_ref, gt_ref, g2_ref, sc_ref, sh_ref, wr_ref, br_ref,
                    x1_ref, h2_ref, lg_ref):
    x1 = x_ref[...] + gt_ref[...] * jnp.dot(y_ref[...], w_ref[...], preferred_element_type=F32)
    x1_ref[...] = x1
    ms = jnp.mean(x1 * x1, axis=-1, keepdims=True)
    h2 = x1 * lax.rsqrt(ms + EPS) * g2_ref[...] * (1.0 + sc_ref[...]) + sh_ref[...]
    h2_ref[...] = h2
    lg_ref[...] = jnp.dot(h2.astype(BF16), wr_ref[...], preferred_element_type=F32) + br_ref[...]


def _out_projection(y, w, x, gt, g2, sc, sh, wr, br, tm):
    b, l, d = x.shape
    tm = min(tm, l)
    row = lambda bi, i: (bi, i, 0)
    per_b = lambda bi, i: (bi, 0, 0)
    const = lambda bi, i: (0, 0)
    return pl.pallas_call(
        _outproj_kernel,
        grid=(b, l // tm),
        in_specs=[pl.BlockSpec((None, tm, d), row),
                  pl.BlockSpec((d, d), const),
                  pl.BlockSpec((None, tm, d), row),
                  pl.BlockSpec((None, 1, d), per_b),
                  pl.BlockSpec((1, d), const),
                  pl.BlockSpec((None, 1, d), per_b),
                  pl.BlockSpec((None, 1, d), per_b),
                  pl.BlockSpec((d, ROUTER_W), const),
                  pl.BlockSpec((1, ROUTER_W), const)],
        out_specs=[pl.BlockSpec((None, tm, d), row),
                   pl.BlockSpec((None, tm, d), row),
                   pl.BlockSpec((None, tm, ROUTER_W), row)],
        out_shape=[jax.ShapeDtypeStruct((b, l, d), F32),
                   jax.ShapeDtypeStruct((b, l, d), F32),
                   jax.ShapeDtypeStruct((b, l, ROUTER_W), F32)],
        compiler_params=_cparams(("parallel", "parallel")),
        name="out_projection",
    )(y, w, x, gt, g2, sc, sh, wr, br)


def _route_kernel(lg_ref, e_ref, w_ref):
    lg = lg_ref[...]
    lane = lax.broadcasted_iota(jnp.int32, lg.shape, 1)
    neg = jnp.float32(-jnp.inf)
    big = jnp.int32(1 << 20)
    is_g = lane < N_GROUPS
    gl = jnp.where(is_g, lg, neg)
    gm = jnp.max(gl, axis=-1, keepdims=True)
    gidx = jnp.min(jnp.where(gl == gm, lane, big), axis=-1, keepdims=True)
    gsum = jnp.sum(jnp.where(is_g, jnp.exp(gl - gm), 0.0), axis=-1, keepdims=True)
    p_g = 1.0 / gsum
    g_lo = N_GROUPS + gidx * EXP_PER_GROUP
    in_grp = (lane >= g_lo) & (lane < g_lo + EXP_PER_GROUP)
    el = jnp.where(in_grp, lg, neg)
    em = jnp.max(el, axis=-1, keepdims=True)
    z = jnp.sum(jnp.where(in_grp, jnp.exp(el - em), 0.0), axis=-1, keepdims=True)
    i1 = jnp.min(jnp.where(el == em, lane, big), axis=-1, keepdims=True)
    el2 = jnp.where(lane == i1, neg, el)
    em2 = jnp.max(el2, axis=-1, keepdims=True)
    i2 = jnp.min(jnp.where(el2 == em2, lane, big), axis=-1, keepdims=True)
    p1 = 1.0 / z
    p2 = jnp.exp(em2 - em) / z
    tot = p1 + p2
    e_ref[...] = jnp.where(lane == 0, i1 - N_GROUPS, jnp.where(lane == 1, i2 - N_GROUPS, 0))
    w_ref[...] = jnp.where(lane == 0, p_g * p1 / tot, jnp.where(lane == 1, p_g * p2 / tot, 0.0))


def _route(logits, tm):
    t = logits.shape[0]
    tm = min(tm, t)
    spec = pl.BlockSpec((tm, ROUTER_W), lambda i: (i, 0))
    return pl.pallas_call(
        _route_kernel,
        grid=(t // tm,),
        in_specs=[spec],
        out_specs=[spec, spec],
        out_shape=[jax.ShapeDtypeStruct((t, ROUTER_W), jnp.int32),
                   jax.ShapeDtypeStruct((t, ROUTER_W), F32)],
        compiler_params=_cparams(("parallel",)),
        name="moe_route",
    )(logits)


def _expert_kernel(be_ref, nv_ref, tok_ref, dst_ref,
                   sw_ref, h_hbm, wg_ref, wu_ref, wd_ref, yp_hbm,
                   xbuf, ybuf, wgb, wub, wdb, sem_in, sem_out):
    i = pl.program_id(0)
    n = pl.num_programs(0)
    blk = MOE_BLOCK
    slot = i % 2

    def gather_copy(b, s, r):
        tok = tok_ref[b * blk + r]
        return pltpu.make_async_copy(h_hbm.at[pl.ds(tok, 1)], xbuf.at[s, pl.ds(r, 1)], sem_in.at[s])

    def scatter_copy(b, s, r):
        a = dst_ref[b * blk + r]
        return pltpu.make_async_copy(ybuf.at[s, pl.ds(r, 1)], yp_hbm.at[a % 2, pl.ds(a // 2, 1)],
                                     sem_out.at[s])

    def start_gather(b, s):
        def body(r, carry):
            gather_copy(b, s, r).start()
            return carry
        lax.fori_loop(0, blk, body, 0)

    def wait_rows(copy_fn, b, s, count):
        def body(r, carry):
            copy_fn(b, s, r).wait()
            return carry
        lax.fori_loop(0, count, body, 0)

    valid = nv_ref[i] > 0

    @pl.when(jnp.logical_and(i == 0, valid))
    def _():
        start_gather(0, 0)

    @pl.when(valid)
    def _():
        prev_e = be_ref[jnp.maximum(i - 1, 0)]

        @pl.when(jnp.logical_or(i == 0, be_ref[i] != prev_e))
        def _():
            wgb[...] = wg_ref[...].astype(BF16)
            wub[...] = wu_ref[...].astype(BF16)
            wdb[...] = wd_ref[...].astype(BF16)

        wait_rows(gather_copy, i, slot, blk)
        nxt = jnp.minimum(i + 1, n - 1)
        has_next = jnp.logical_and(i + 1 < n, nv_ref[nxt] > 0)

        @pl.when(has_next)
        def _():
            start_gather(i + 1, 1 - slot)

        x = xbuf[slot].astype(BF16)
        g = jnp.dot(x, wgb[...], preferred_element_type=F32)
        u = jnp.dot(x, wub[...], preferred_element_type=F32)
        a = (_silu(g) * u).astype(BF16)
        y = jnp.dot(a, wdb[...], preferred_element_type=F32) * sw_ref[...]

        @pl.when(i > 0)
        def _():
            wait_rows(scatter_copy, i - 1, 1 - slot, nv_ref[jnp.maximum(i - 1, 0)])

        ybuf[slot] = y

        def body(r, carry):
            scatter_copy(i, slot, r).start()
            return carry
        lax.fori_loop(0, nv_ref[i], body, 0)

        @pl.when(jnp.logical_not(has_next))
        def _():
            wait_rows(scatter_copy, i, slot, nv_ref[i])


def _experts(block_e, block_nv, slot_tok, slot_dst, slot_w, h2, w_gate, w_up, w_down):
    t, d = h2.shape
    n_blocks = block_e.shape[0]
    blk = MOE_BLOCK
    de = w_gate.shape[2]
    wmap = lambda i, be, bv, tok, dst: (be[i], 0, 0)
    return pl.pallas_call(
        _expert_kernel,
        grid_spec=pltpu.PrefetchScalarGridSpec(
            num_scalar_prefetch=4,
            grid=(n_blocks,),
            in_specs=[pl.BlockSpec((blk, 1), lambda i, be, bv, tok, dst: (i, 0)),
                      pl.BlockSpec(memory_space=pl.ANY),
                      pl.BlockSpec((None, d, de), wmap),
                      pl.BlockSpec((None, d, de), wmap),
                      pl.BlockSpec((None, de, d), wmap)],
            out_specs=pl.BlockSpec(memory_space=pl.ANY),
            scratch_shapes=[pltpu.VMEM((2, blk, d), F32),
                            pltpu.VMEM((2, blk, d), F32),
                            pltpu.VMEM((d, de), BF16),
                            pltpu.VMEM((d, de), BF16),
                            pltpu.VMEM((de, d), BF16),
                            pltpu.SemaphoreType.DMA((2,)),
                            pltpu.SemaphoreType.DMA((2,))],
        ),
        out_shape=jax.ShapeDtypeStruct((2, t, d), F32),
        compiler_params=_cparams(("arbitrary",)),
        name="moe_experts",
    )(block_e, block_nv, slot_tok, slot_dst, slot_w, h2, w_gate, w_up, w_down)


def _combine_kernel(x_ref, gt_ref, yp_ref, o_ref):
    o_ref[...] = x_ref[...] + gt_ref[...] * (yp_ref[0] + yp_ref[1])


def _combine(x1, gt, yp, tm):
    b, l, d = x1.shape
    tm = min(tm, l)
    nb = l // tm
    return pl.pallas_call(
        _combine_kernel,
        grid=(b, nb),
        in_specs=[pl.BlockSpec((None, tm, d), lambda bi, i: (bi, i, 0)),
                  pl.BlockSpec((None, 1, d), lambda bi, i: (bi, 0, 0)),
                  pl.BlockSpec((2, tm, d), lambda bi, i: (0, bi * nb + i, 0))],
        out_specs=pl.BlockSpec((None, tm, d), lambda bi, i: (bi, i, 0)),
        out_shape=jax.ShapeDtypeStruct((b, l, d), F32),
        compiler_params=_cparams(("parallel", "parallel")),
        name="moe_combine",
    )(x1, gt, yp)


def _dispatch_plan(eid, wts):
    blk = MOE_BLOCK
    n_assign = eid.shape[0]
    n_blocks = (n_assign + blk - 1) // blk + N_EXPERTS
    n_slots = n_blocks * blk
    order = jnp.argsort(eid).astype(jnp.int32)
    e_s = eid[order]
    counts = jnp.zeros((N_EXPERTS,), jnp.int32).at[eid].add(1)
    start = jnp.cumsum(counts) - counts
    padded = (counts + blk - 1) // blk * blk
    pend = jnp.cumsum(padded)
    pstart = pend - padded
    dest = pstart[e_s] + jnp.arange(n_assign, dtype=jnp.int32) - start[e_s]
    slot_asg = jnp.full((n_slots,), -1, jnp.int32).at[dest].set(order)
    filled = slot_asg >= 0
    safe = jnp.maximum(slot_asg, 0)
    slot_tok = jnp.where(filled, safe // 2, 0)
    slot_w = jnp.where(filled, wts[safe], 0.0).astype(F32)
    bstart = jnp.arange(n_blocks, dtype=jnp.int32) * blk
    block_e = jnp.minimum(jnp.searchsorted(pend, bstart, side='right'), N_EXPERTS - 1).astype(jnp.int32)
    block_nv = jnp.sum(filled.reshape(n_blocks, blk), axis=1, dtype=jnp.int32)
    return block_e, block_nv, slot_tok, safe, slot_w.reshape(n_slots, 1)


def _rope_tables(length, with_rope):
    half = MLA_ROPE // 4
    if with_rope:
        rows = length // GRID_W
        row = jnp.repeat(jnp.arange(rows), GRID_W).astype(F32)
        col = jnp.tile(jnp.arange(GRID_W), rows).astype(F32)
        inv = 1.0 / (ROPE_THETA ** (jnp.arange(half, dtype=F32) / half))
        ang_r = row[:, None] * inv
        ang_c = col[:, None] * inv
        cos = jnp.concatenate([jnp.cos(ang_r), jnp.cos(ang_r), jnp.cos(ang_c), jnp.cos(ang_c)], axis=1)
        sin = jnp.concatenate([-jnp.sin(ang_r), jnp.sin(ang_r), -jnp.sin(ang_c), jnp.sin(ang_c)], axis=1)
    else:
        cos = jnp.ones((length, MLA_ROPE), F32)
        sin = jnp.zeros((length, MLA_ROPE), F32)
    pad = ((0, 0), (0, LANE - MLA_ROPE))
    return jnp.pad(cos, pad), jnp.pad(sin, pad)


def _split_w_in(w_in):
    sizes = (GLA_QK_W, GLA_QK_W, GLA_V_W, GLA_V_W, GLA_LOWRANK, GLA_LOWRANK,
             MLA_Q_RANK, MLA_KV_RANK, MLA_ROPE, D_MODEL, D_MODEL)
    parts, off = [], 0
    for s in sizes:
        parts.append(w_in[:, off:off + s])
        off += s
    return parts


def kernel(x, c, ctx, c_ctx, w_mod, b_mod, norm1_g, norm2_g, w_in, w_decay_f, b_decay_f, w_decay_b, b_decay_b, gla_norm_g, q_a_norm_g, w_uq, kv_a_norm_g, w_ukv, q_norm_g, k_norm_g, w_o_gla, w_o_mla, w_out, w_router_group, b_router_group, w_router_expert, b_router_expert, w_exp_gate, w_exp_up, w_exp_down):
    assert w_mod.shape[0] == 1, "single-layer block"
    b, l, d = x.shape
    lc = ctx.shape[1]
    t = b * l

    c_rows = jnp.concatenate([c, c_ctx[None, :], jnp.zeros((8 - b - 1, d), F32)], axis=0)
    mod = _modulation(c_rows, w_mod[0], b_mod[0])
    sh1, sc1, gt1, sh2, sc2, gt2 = [mod[:b, i * d:(i + 1) * d].reshape(b, 1, d) for i in range(6)]
    sh1c = jnp.broadcast_to(mod[b:b + 1, 0:d].reshape(1, 1, d), (b, 1, d))
    sc1c = jnp.broadcast_to(mod[b:b + 1, d:2 * d].reshape(1, 1, d), (b, 1, d))

    wq_, wk_, wv_, wr_, waf, wab, wcq, wckv, wkr, wga, wgb = _split_w_in(w_in[0])
    small = jnp.concatenate(
        [wkr, waf, wab, jnp.zeros((d, SMALL_W - MLA_ROPE - 2 * GLA_LOWRANK), F32)], axis=1)
    w_lat = jnp.concatenate([wq_, wk_, wv_, wr_, wga, wgb, wcq, wckv, small], axis=1).astype(BF16)
    w_ctx = jnp.concatenate([wk_, wv_, wckv, small], axis=1).astype(BF16)
    g1 = norm1_g[0].reshape(1, d)
    p = _in_projection(x, g1, sc1, sh1, w_lat, tm=1024)
    pc = _in_projection(ctx, g1, sc1c, sh1c, w_ctx, tm=lc)

    def decay_w(w, off):
        return jnp.zeros((LANE, GLA_QK_W), F32).at[off:off + GLA_LOWRANK].set(w).astype(BF16)

    y_gla = _gla(p, pc,
                 decay_w(w_decay_f[0], SM_AF), b_decay_f[0].reshape(1, -1),
                 decay_w(w_decay_b[0], SM_AB), b_decay_b[0].reshape(1, -1),
                 gla_norm_g[0].reshape(1, -1))

    wq = jnp.pad(w_uq[0].reshape(MLA_Q_RANK, MLA_HEADS, MLA_QK_DIM),
                 ((0, 0), (0, 0), (0, MLA_QK_PAD - MLA_QK_DIM))).reshape(MLA_Q_RANK, -1).astype(BF16)
    wkv = w_ukv[0].astype(BF16)
    qn_g = jnp.pad(q_norm_g[0], (0, MLA_QK_PAD - MLA_QK_DIM)).reshape(1, -1)
    kn_nope = k_norm_g[0][:MLA_NOPE].reshape(1, -1)
    kn_rope = jnp.pad(k_norm_g[0][MLA_NOPE:], (0, LANE - MLA_ROPE)).reshape(1, -1)
    cos, sin = _rope_tables(l, True)
    cos_c, sin_c = _rope_tables(lc, False)
    q_m = _mla_queries(p, q_a_norm_g[0].reshape(1, -1), wq, qn_g, cos, sin, tm=512)
    kva = kv_a_norm_g[0].reshape(1, -1)
    k_m, v_m = _mla_keys_values(p, COL_CKV // MLA_KV_RANK, COL_SMALL // LANE, kva, wkv,
                                kn_nope, kn_rope, cos, sin, tm=512)
    k_mc, v_mc = _mla_keys_values(pc, CCOL_CKV // MLA_KV_RANK, CCOL_SMALL // LANE, kva, wkv,
                                  kn_nope, kn_rope, cos_c, sin_c, tm=lc)
    y_mla = _attention(q_m, k_m, v_m, k_mc, v_mc, tq=256)

    y = _branch_merge(y_gla, y_mla, w_o_gla[0].astype(BF16), w_o_mla[0].astype(BF16), p, tm=1024, tn=512)
    w_router = jnp.concatenate(
        [w_router_group[0], w_router_expert[0],
         jnp.zeros((d, ROUTER_W - N_GROUPS - N_EXPERTS), F32)], axis=1).astype(BF16)
    b_router = jnp.concatenate(
        [b_router_group[0], b_router_expert[0],
         jnp.zeros((ROUTER_W - N_GROUPS - N_EXPERTS,), F32)]).reshape(1, -1)
    x1, h2, logits = _out_projection(y, w_out[0].astype(BF16), x, gt1, norm2_g[0].reshape(1, d),
                                     sc2, sh2, w_router, b_router, tm=256)

    e_pick, w_pick = _route(logits.reshape(t, ROUTER_W), tm=1024)
    eid = e_pick[:, :2].reshape(-1)
    wts = w_pick[:, :2].reshape(-1)
    block_e, block_nv, slot_tok, slot_dst, slot_w = _dispatch_plan(eid, wts)
    yp = _experts(block_e, block_nv, slot_tok, slot_dst, slot_w, h2.reshape(t, d),
                  w_exp_gate[0], w_exp_up[0], w_exp_down[0])
    return _combine(x1, gt2, yp, tm=512)
```

```python
import functools
import math

import jax
import jax.numpy as jnp
from jax import lax
from jax.experimental import pallas as pl
from jax.experimental.pallas import tpu as pltpu

F32 = jnp.float32
BF16 = jnp.bfloat16

D_MODEL = 2048
GRID_W = 64
EPS = 1e-6

GLA_HEADS = 4
GLA_DK = 256
GLA_DV = 512
GLA_LOWRANK = 16
GLA_TAU = 16.0
GLA_CHUNK = 64
GLA_QK_W = GLA_HEADS * GLA_DK
GLA_V_W = GLA_HEADS * GLA_DV

MLA_HEADS = 16
MLA_Q_RANK = 512
MLA_KV_RANK = 512
MLA_NOPE = 128
MLA_ROPE = 64
MLA_V = 128
MLA_QK_DIM = MLA_NOPE + MLA_ROPE
MLA_QK_PAD = 256
ROPE_THETA = 10000.0
LOG2E = math.log2(math.e)

N_GROUPS = 8
EXP_PER_GROUP = 8
N_EXPERTS = N_GROUPS * EXP_PER_GROUP
D_EXPERT = 512
ROUTER_W = 128
MOE_BLOCK = 256

LANE = 128
VMEM_LIMIT = 56 * 1024 * 1024

COL_Q = 0
COL_K = COL_Q + GLA_QK_W
COL_V = COL_K + GLA_QK_W
COL_R = COL_V + GLA_V_W
COL_GA = COL_R + GLA_V_W
COL_GB = COL_GA + D_MODEL
COL_CQ = COL_GB + D_MODEL
COL_CKV = COL_CQ + MLA_Q_RANK
COL_SMALL = COL_CKV + MLA_KV_RANK
SMALL_W = 256
NP_LAT = COL_SMALL + SMALL_W
SM_AF = MLA_ROPE
SM_AB = MLA_ROPE + GLA_LOWRANK
CCOL_K = 0
CCOL_V = CCOL_K + GLA_QK_W
CCOL_CKV = CCOL_V + GLA_V_W
CCOL_SMALL = CCOL_CKV + MLA_KV_RANK
NP_CTX = CCOL_SMALL + SMALL_W
INPROJ_TN = 1280


def _cparams(sem):
    return pltpu.CompilerParams(dimension_semantics=sem, vmem_limit_bytes=VMEM_LIMIT)


def _silu(x):
    return x * jax.nn.sigmoid(x)


def _mod_kernel(c_ref, w_ref, b_ref, o_ref):
    a = _silu(c_ref[...]).astype(BF16)
    o_ref[...] = jnp.dot(a, w_ref[...].astype(BF16), preferred_element_type=F32) + b_ref[...]


def _modulation(c_rows, w_mod, b_mod):
    m, d = c_rows.shape
    n = w_mod.shape[1]
    tn = 1024
    return pl.pallas_call(
        _mod_kernel,
        grid=(n // tn,),
        in_specs=[pl.BlockSpec((m, d), lambda j: (0, 0)),
                  pl.BlockSpec((d, tn), lambda j: (0, j)),
                  pl.BlockSpec((1, tn), lambda j: (0, j))],
        out_specs=pl.BlockSpec((m, tn), lambda j: (0, j)),
        out_shape=jax.ShapeDtypeStruct((m, n), F32),
        compiler_params=_cparams(("parallel",)),
        name="modulation",
    )(c_rows, w_mod, b_mod.reshape(1, n))


def _inproj_kernel(x_ref, g_ref, sc_ref, sh_ref, w_ref, o_ref, h_ref):
    @pl.when(pl.program_id(2) == 0)
    def _():
        x = x_ref[...]
        ms = jnp.mean(x * x, axis=-1, keepdims=True)
        y = x * lax.rsqrt(ms + EPS) * g_ref[...]
        h_ref[...] = (y * (1.0 + sc_ref[...]) + sh_ref[...]).astype(BF16)

    o_ref[...] = jnp.dot(h_ref[...], w_ref[...], preferred_element_type=F32).astype(o_ref.dtype)


def _in_projection(x, g, sc, sh, w, tm):
    b, l, d = x.shape
    tm = min(tm, l)
    n = w.shape[1]
    tn = INPROJ_TN
    return pl.pallas_call(
        _inproj_kernel,
        grid=(b, l // tm, n // tn),
        in_specs=[pl.BlockSpec((None, tm, d), lambda bi, i, j: (bi, i, 0)),
                  pl.BlockSpec((1, d), lambda bi, i, j: (0, 0)),
                  pl.BlockSpec((None, 1, d), lambda bi, i, j: (bi, 0, 0)),
                  pl.BlockSpec((None, 1, d), lambda bi, i, j: (bi, 0, 0)),
                  pl.BlockSpec((d, tn), lambda bi, i, j: (0, j))],
        out_specs=pl.BlockSpec((None, tm, tn), lambda bi, i, j: (bi, i, j)),
        out_shape=jax.ShapeDtypeStruct((b, l, n), BF16),
        scratch_shapes=[pltpu.VMEM((tm, d), BF16)],
        compiler_params=_cparams(("parallel", "parallel", "arbitrary")),
        name="in_projection",
    )(x, g, sc, sh, w)


def _gla_kernel(q_ref, k_ref, v_ref, r_ref, sm_ref, kc_ref, vc_ref, smc_ref,
                wf_ref, bf_ref, wb_ref, bb_ref, gn_ref, y_ref, of_ref, st_ref):
    c = GLA_CHUNK
    n_lat = q_ref.shape[0] // c
    n_ctx = kc_ref.shape[0] // c
    row = lax.broadcasted_iota(jnp.int32, (c, c), 0)
    col = lax.broadcasted_iota(jnp.int32, (c, c), 1)
    lower = row >= col
    upper = row <= col
    nt = (((1,), (1,)), ((), ()))
    tn = (((0,), (0,)), ((), ()))

    def cum_decay(sm, w_ref, b_ref, tri):
        z = jnp.dot(sm, w_ref[...], preferred_element_type=F32) + b_ref[...]
        g = (jnp.minimum(z, 0.0) - jnp.log(1.0 + jnp.exp(-jnp.abs(z)))) * (1.0 / GLA_TAU)
        g_hi = g.astype(BF16)
        g_lo = (g - g_hi.astype(F32)).astype(BF16)
        t = jnp.where(tri, 1.0, 0.0).astype(BF16)
        return (jnp.dot(t, g_hi, preferred_element_type=F32)
                + jnp.dot(t, g_lo, preferred_element_type=F32))

    def state_update(k, v, bc, bend):
        ke = (k * jnp.exp(bend - bc)).astype(BF16)
        st_ref[...] = (st_ref[...] * jnp.exp(bend)
                       + lax.dot_general(v, ke, tn, preferred_element_type=F32))

    def ctx_step(n, w_ref, b_ref, tri, end_row):
        r0 = pl.multiple_of(n * c, c)
        bc = cum_decay(smc_ref[pl.ds(r0, c), :], w_ref, b_ref, tri)
        bend = bc[end_row:end_row + 1, :]
        state_update(kc_ref[pl.ds(r0, c), :].astype(F32), vc_ref[pl.ds(r0, c), :], bc, bend)

    def lat_step(n, w_ref, b_ref, tri, mask, end_row):
        r0 = pl.multiple_of(n * c, c)
        bc = cum_decay(sm_ref[pl.ds(r0, c), :], w_ref, b_ref, tri)
        bend = bc[end_row:end_row + 1, :]
        q = q_ref[pl.ds(r0, c), :].astype(F32) * (GLA_DK ** -0.5)
        k = k_ref[pl.ds(r0, c), :].astype(F32)
        v = v_ref[pl.ds(r0, c), :]
        qd = (q * jnp.exp(bc)).astype(BF16)
        ki = (k * jnp.exp(-bc)).astype(BF16)
        att = lax.dot_general(qd, ki, nt, preferred_element_type=F32)
        att = jnp.where(mask, att, 0.0).astype(BF16)
        o = (jnp.dot(att, v, preferred_element_type=F32)
             + lax.dot_general(qd, st_ref[...].astype(BF16), nt, preferred_element_type=F32))
        state_update(k, v, bc, bend)
        return r0, o

    st_ref[...] = jnp.zeros_like(st_ref)

    def ctx_fwd(n, carry):
        ctx_step(n, wf_ref, bf_ref, lower, c - 1)
        return carry

    lax.fori_loop(0, n_ctx, ctx_fwd, 0)

    def lat_fwd(n, carry):
        r0, o = lat_step(n, wf_ref, bf_ref, lower, lower, c - 1)
        of_ref[pl.ds(r0, c), :] = o
        return carry

    lax.fori_loop(0, n_lat, lat_fwd, 0)

    st_ref[...] = jnp.zeros_like(st_ref)

    def ctx_bwd(n, carry):
        ctx_step(n_ctx - 1 - n, wb_ref, bb_ref, upper, 0)
        return carry

    lax.fori_loop(0, n_ctx, ctx_bwd, 0)

    def lat_bwd(n, carry):
        r0, o_b = lat_step(n_lat - 1 - n, wb_ref, bb_ref, upper, upper, 0)
        o = of_ref[pl.ds(r0, c), :] + o_b
        ms = jnp.mean(o * o, axis=-1, keepdims=True)
        yn = o * lax.rsqrt(ms + EPS) * gn_ref[...]
        r = r_ref[pl.ds(r0, c), :].astype(F32)
        y_ref[pl.ds(r0, c), :] = (yn * _silu(r)).astype(y_ref.dtype)
        return carry

    lax.fori_loop(0, n_lat, lat_bwd, 0)


def _gla(p, pc, wf, bf, wb, bb, gn):
    b, l, _ = p.shape
    lc = pc.shape[1]
    dk, dv = GLA_DK, GLA_DV
    hmap = lambda off: (lambda bi, h: (bi, 0, off + h))
    wmap = lambda bi, h: (0, h)
    return pl.pallas_call(
        _gla_kernel,
        grid=(b, GLA_HEADS),
        in_specs=[pl.BlockSpec((None, l, dk), hmap(COL_Q // dk)),
                  pl.BlockSpec((None, l, dk), hmap(COL_K // dk)),
                  pl.BlockSpec((None, l, dv), hmap(COL_V // dv)),
                  pl.BlockSpec((None, l, dv), hmap(COL_R // dv)),
                  pl.BlockSpec((None, l, LANE), lambda bi, h: (bi, 0, COL_SMALL // LANE)),
                  pl.BlockSpec((None, lc, dk), hmap(CCOL_K // dk)),
                  pl.BlockSpec((None, lc, dv), hmap(CCOL_V // dv)),
                  pl.BlockSpec((None, lc, LANE), lambda bi, h: (bi, 0, CCOL_SMALL // LANE)),
                  pl.BlockSpec((LANE, dk), wmap),
                  pl.BlockSpec((1, dk), wmap),
                  pl.BlockSpec((LANE, dk), wmap),
                  pl.BlockSpec((1, dk), wmap),
                  pl.BlockSpec((1, dv), lambda bi, h: (0, 0))],
        out_specs=pl.BlockSpec((None, l, dv), lambda bi, h: (bi, 0, h)),
        out_shape=jax.ShapeDtypeStruct((b, l, GLA_V_W), BF16),
        scratch_shapes=[pltpu.VMEM((l, dv), F32), pltpu.VMEM((dv, dk), F32)],
        compiler_params=_cparams(("parallel", "parallel")),
        name="gla",
    )(p, p, p, p, p, pc, pc, pc, wf, bf, wb, bb, gn)


def _rope_partner(t):
    lane = lax.broadcasted_iota(jnp.int32, t.shape, 1)
    first = (lane % 32) < 16
    return jnp.where(first, pltpu.roll(t, LANE - 16, 1), pltpu.roll(t, 16, 1))


def _mla_q_kernel(cq_ref, ga_ref, w_ref, gn_ref, cos_ref, sin_ref, o_ref):
    cq = cq_ref[...].astype(F32)
    cn = (cq * lax.rsqrt(jnp.mean(cq * cq, axis=-1, keepdims=True) + EPS) * ga_ref[...]).astype(BF16)
    cos = cos_ref[...]
    sin = sin_ref[...]
    scale = MLA_QK_DIM ** -0.5 * LOG2E
    for h in range(MLA_HEADS):
        qh = jnp.dot(cn, w_ref[:, h * MLA_QK_PAD:(h + 1) * MLA_QK_PAD], preferred_element_type=F32)
        ms = jnp.sum(qh * qh, axis=-1, keepdims=True) * (1.0 / MLA_QK_DIM)
        qn = qh * lax.rsqrt(ms + EPS) * gn_ref[...]
        t = qn[:, MLA_NOPE:]
        rot = t * cos + _rope_partner(t) * sin
        o_ref[h, :, :MLA_NOPE] = (qn[:, :MLA_NOPE] * scale).astype(o_ref.dtype)
        o_ref[h, :, MLA_NOPE:] = (rot * scale).astype(o_ref.dtype)


def _mla_queries(p, ga, wq, gn, cos, sin, tm):
    b, l, _ = p.shape
    tm = min(tm, l)
    return pl.pallas_call(
        _mla_q_kernel,
        grid=(b, l // tm),
        in_specs=[pl.BlockSpec((None, tm, MLA_Q_RANK), lambda bi, i: (bi, i, COL_CQ // MLA_Q_RANK)),
                  pl.BlockSpec((1, MLA_Q_RANK), lambda bi, i: (0, 0)),
                  pl.BlockSpec(wq.shape, lambda bi, i: (0, 0)),
                  pl.BlockSpec((1, MLA_QK_PAD), lambda bi, i: (0, 0)),
                  pl.BlockSpec((tm, LANE), lambda bi, i: (i, 0)),
                  pl.BlockSpec((tm, LANE), lambda bi, i: (i, 0))],
        out_specs=pl.BlockSpec((None, MLA_HEADS, tm, MLA_QK_PAD), lambda bi, i: (bi, 0, i, 0)),
        out_shape=jax.ShapeDtypeStruct((b, MLA_HEADS, l, MLA_QK_PAD), BF16),
        compiler_params=_cparams(("parallel", "parallel")),
        name="mla_queries",
    )(p, ga, wq, gn, cos, sin)


def _mla_kv_kernel(ckv_ref, sm_ref, ckvc_ref, smc_ref, ga_ref, w_ref, gk_ref, gr_ref,
                   cos_ref, sin_ref, k_ref, v_ref, *, n_lat):
    is_ctx = pl.program_id(1) >= n_lat
    ckv = jnp.where(is_ctx, ckvc_ref[...], ckv_ref[...]).astype(F32)
    cn = (ckv * lax.rsqrt(jnp.mean(ckv * ckv, axis=-1, keepdims=True) + EPS) * ga_ref[...]).astype(BF16)
    sm = jnp.where(is_ctx, smc_ref[...], sm_ref[...]).astype(F32)
    lane = lax.broadcasted_iota(jnp.int32, sm.shape, 1)
    kr = jnp.where(lane < MLA_ROPE, sm, 0.0)
    ss_r = jnp.sum(kr * kr, axis=-1, keepdims=True)
    krg = kr * gr_ref[...]
    rot = krg * cos_ref[...] + _rope_partner(krg) * sin_ref[...]
    hw = MLA_NOPE + MLA_V
    ones_col = jnp.where(lane == 0, 1.0, 0.0).astype(v_ref.dtype)
    for h in range(MLA_HEADS):
        kvh = jnp.dot(cn, w_ref[:, h * hw:(h + 1) * hw], preferred_element_type=F32)
        kn = kvh[:, :MLA_NOPE]
        ms = (jnp.sum(kn * kn, axis=-1, keepdims=True) + ss_r) * (1.0 / MLA_QK_DIM)
        rs = lax.rsqrt(ms + EPS)
        k_ref[h, :, :MLA_NOPE] = (kn * rs * gk_ref[...]).astype(k_ref.dtype)
        k_ref[h, :, MLA_NOPE:] = (rot * rs).astype(k_ref.dtype)
        v_ref[h, :, :MLA_V] = kvh[:, MLA_NOPE:].astype(v_ref.dtype)
        v_ref[h, :, MLA_V:] = ones_col


def _mla_keys_values(p, pc, ga, wkv, gk, gr, cos, sin):
    b, l, _ = p.shape
    lc = pc.shape[1]
    tm = min(lc, l)
    assert l % tm == 0 and lc % tm == 0
    n_lat, n_ctx = l // tm, lc // tm
    lat = lambda blk: (lambda bi, i: (bi, jnp.minimum(i, n_lat - 1), blk))
    ctx = lambda blk: (lambda bi, i: (bi, jnp.maximum(i - n_lat, 0), blk))
    const = lambda bi, i: (0, 0)
    out_spec = pl.BlockSpec((None, MLA_HEADS, tm, MLA_QK_PAD), lambda bi, i: (bi, 0, i, 0))
    shape = jax.ShapeDtypeStruct((b, MLA_HEADS, l + lc, MLA_QK_PAD), BF16)
    return pl.pallas_call(
        functools.partial(_mla_kv_kernel, n_lat=n_lat),
        grid=(b, n_lat + n_ctx),
        in_specs=[pl.BlockSpec((None, tm, MLA_KV_RANK), lat(COL_CKV // MLA_KV_RANK)),
                  pl.BlockSpec((None, tm, LANE), lat(COL_SMALL // LANE)),
                  pl.BlockSpec((None, tm, MLA_KV_RANK), ctx(CCOL_CKV // MLA_KV_RANK)),
                  pl.BlockSpec((None, tm, LANE), ctx(CCOL_SMALL // LANE)),
                  pl.BlockSpec((1, MLA_KV_RANK), const),
                  pl.BlockSpec(wkv.shape, const),
                  pl.BlockSpec((1, LANE), const),
                  pl.BlockSpec((1, LANE), const),
                  pl.BlockSpec((tm, LANE), lambda bi, i: (i, 0)),
                  pl.BlockSpec((tm, LANE), lambda bi, i: (i, 0))],
        out_specs=[out_spec, out_spec],
        out_shape=[shape, shape],
        compiler_params=_cparams(("parallel", "arbitrary")),
        name="mla_keys_values",
    )(p, p, pc, pc, ga, wkv, gk, gr, cos, sin)


def _attn_kernel(q_ref, k_ref, v_ref, o_ref, s0_ref, s1_ref, m0_ref, m1_ref, *, tq):
    nt = (((1,), (1,)), ((), ()))
    n = q_ref.shape[0] // tq
    bufs = ((s0_ref, m0_ref), (s1_ref, m1_ref))

    def scores(i, slot):
        s_ref, m_ref = bufs[slot]
        r0 = pl.multiple_of(i * tq, tq)
        s = lax.dot_general(q_ref[pl.ds(r0, tq), :], k_ref[...], nt, preferred_element_type=F32)
        s_ref[...] = s
        m_ref[...] = jnp.max(s, axis=-1, keepdims=True)

    def outputs(i, slot):
        s_ref, m_ref = bufs[slot]
        r0 = pl.multiple_of(i * tq, tq)
        p = jnp.exp2(s_ref[...] - m_ref[...]).astype(BF16)
        o = jnp.dot(p, v_ref[...], preferred_element_type=F32)
        o_ref[pl.ds(r0, tq), :] = (o[:, :MLA_V] / o[:, MLA_V:MLA_V + 1]).astype(o_ref.dtype)

    scores(0, 0)
    if n % 2 == 0:
        def body(j, carry):
            scores(2 * j + 1, 1)
            outputs(2 * j, 0)
            scores(2 * j + 2, 0)
            outputs(2 * j + 1, 1)
            return carry

        lax.fori_loop(0, n // 2 - 1, body, 0)
        scores(n - 1, 1)
        outputs(n - 2, 0)
        outputs(n - 1, 1)
    else:
        assert n == 1
        outputs(0, 0)


def _attention(q, k, v, tq):
    b, h, l, dq = q.shape
    tq = min(tq, l)
    lk = k.shape[2]
    bh = lambda bi, hi: (bi, hi, 0, 0)
    return pl.pallas_call(
        functools.partial(_attn_kernel, tq=tq),
        grid=(b, h),
        in_specs=[pl.BlockSpec((None, None, l, dq), bh),
                  pl.BlockSpec((None, None, lk, dq), bh),
                  pl.BlockSpec((None, None, lk, dq), bh)],
        out_specs=pl.BlockSpec((None, l, MLA_V), lambda bi, hi: (bi, 0, hi)),
        out_shape=jax.ShapeDtypeStruct((b, l, h * MLA_V), BF16),
        scratch_shapes=[pltpu.VMEM((tq, lk), F32), pltpu.VMEM((tq, lk), F32),
                        pltpu.VMEM((tq, 1), F32), pltpu.VMEM((tq, 1), F32)],
        compiler_params=_cparams(("parallel", "parallel")),
        name="mla_attention",
    )(q, k, v)


def _merge_kernel(yg_ref, ym_ref, wg_ref, wm_ref, ga_ref, gb_ref, o_ref):
    a = jnp.dot(yg_ref[...], wg_ref[...], preferred_element_type=F32)
    m = jnp.dot(ym_ref[...], wm_ref[...], preferred_element_type=F32)
    y = (jax.nn.sigmoid(ga_ref[...].astype(F32)) * a
         + jax.nn.sigmoid(gb_ref[...].astype(F32)) * m)
    o_ref[...] = y.astype(o_ref.dtype)


def _branch_merge(yg, ym, wg, wm, p, tm, tn):
    b, l, d = yg.shape
    tm = min(tm, l)
    n = wg.shape[1]
    return pl.pallas_call(
        _merge_kernel,
        grid=(b, l // tm, n // tn),
        in_specs=[pl.BlockSpec((None, tm, d), lambda bi, i, j: (bi, i, 0)),
                  pl.BlockSpec((None, tm, d), lambda bi, i, j: (bi, i, 0)),
                  pl.BlockSpec((d, tn), lambda bi, i, j: (0, j)),
                  pl.BlockSpec((d, tn), lambda bi, i, j: (0, j)),
                  pl.BlockSpec((None, tm, tn), lambda bi, i, j: (bi, i, COL_GA // tn + j)),
                  pl.BlockSpec((None, tm, tn), lambda bi, i, j: (bi, i, COL_GB // tn + j))],
        out_specs=pl.BlockSpec((None, tm, tn), lambda bi, i, j: (bi, i, j)),
        out_shape=jax.ShapeDtypeStruct((b, l, n), BF16),
        compiler_params=_cparams(("parallel", "parallel", "arbitrary")),
        name="branch_merge",
    )(yg, ym, wg, wm, p, p)


def _outproj_kernel(y_ref, w_ref, x_ref, gt_ref, g2_ref, sc_ref, sh_ref, wr_ref, br_ref,
                    x1_ref, h2_ref, lg_ref):
    x1 = x_ref[...] + gt_ref[...] * jnp.dot(y_ref[...], w_ref[...], preferred_element_type=F32)
    x1_ref[...] = x1
    ms = jnp.mean(x1 * x1, axis=-1, keepdims=True)
    h2 = x1 * lax.rsqrt(ms + EPS) * g2_ref[...] * (1.0 + sc_ref[...]) + sh_ref[...]
    h2_ref[...] = h2
    lg_ref[...] = jnp.dot(h2.astype(BF16), wr_ref[...], preferred_element_type=F32) + br_ref[...]


def _out_projection(y, w, x, gt, g2, sc, sh, wr, br, tm):
    b, l, d = x.shape
    tm = min(tm, l)
    row = lambda bi, i: (bi, i, 0)
    per_b = lambda bi, i: (bi, 0, 0)
    const = lambda bi, i: (0, 0)
    return pl.pallas_call(
        _outproj_kernel,
        grid=(b, l // tm),
        in_specs=[pl.BlockSpec((None, tm, d), row),
                  pl.BlockSpec((d, d), const),
                  pl.BlockSpec((None, tm, d), row),
                  pl.BlockSpec((None, 1, d), per_b),
                  pl.BlockSpec((1, d), const),
                  pl.BlockSpec((None, 1, d), per_b),
                  pl.BlockSpec((None, 1, d), per_b),
                  pl.BlockSpec((d, ROUTER_W), const),
                  pl.BlockSpec((1, ROUTER_W), const)],
        out_specs=[pl.BlockSpec((None, tm, d), row),
                   pl.BlockSpec((None, tm, d), row),
                   pl.BlockSpec((None, tm, ROUTER_W), row)],
        out_shape=[jax.ShapeDtypeStruct((b, l, d), F32),
                   jax.ShapeDtypeStruct((b, l, d), F32),
                   jax.ShapeDtypeStruct((b, l, ROUTER_W), F32)],
        compiler_params=_cparams(("parallel", "parallel")),
        name="out_projection",
    )(y, w, x, gt, g2, sc, sh, wr, br)


def _route_kernel(lg_ref, e_ref, w_ref):
    lg = lg_ref[...]
    lane = lax.broadcasted_iota(jnp.int32, lg.shape, 1)
    neg = jnp.float32(-jnp.inf)
    big = jnp.int32(1 << 20)
    is_g = lane < N_GROUPS
    gl = jnp.where(is_g, lg, neg)
    gm = jnp.max(gl, axis=-1, keepdims=True)
    gidx = jnp.min(jnp.where(gl == gm, lane, big), axis=-1, keepdims=True)
    gsum = jnp.sum(jnp.where(is_g, jnp.exp(gl - gm), 0.0), axis=-1, keepdims=True)
    p_g = 1.0 / gsum
    g_lo = N_GROUPS + gidx * EXP_PER_GROUP
    in_grp = (lane >= g_lo) & (lane < g_lo + EXP_PER_GROUP)
    el = jnp.where(in_grp, lg, neg)
    em = jnp.max(el, axis=-1, keepdims=True)
    z = jnp.sum(jnp.where(in_grp, jnp.exp(el - em), 0.0), axis=-1, keepdims=True)
    i1 = jnp.min(jnp.where(el == em, lane, big), axis=-1, keepdims=True)
    el2 = jnp.where(lane == i1, neg, el)
    em2 = jnp.max(el2, axis=-1, keepdims=True)
    i2 = jnp.min(jnp.where(el2 == em2, lane, big), axis=-1, keepdims=True)
    p1 = 1.0 / z
    p2 = jnp.exp(em2 - em) / z
    tot = p1 + p2
    e_ref[...] = jnp.where(lane == 0, i1 - N_GROUPS, jnp.where(lane == 1, i2 - N_GROUPS, 0))
    w_ref[...] = jnp.where(lane == 0, p_g * p1 / tot, jnp.where(lane == 1, p_g * p2 / tot, 0.0))


def _route(logits, tm):
    t = logits.shape[0]
    tm = min(tm, t)
    spec = pl.BlockSpec((tm, ROUTER_W), lambda i: (i, 0))
    return pl.pallas_call(
        _route_kernel,
        grid=(t // tm,),
        in_specs=[spec],
        out_specs=[spec, spec],
        out_shape=[jax.ShapeDtypeStruct((t, ROUTER_W), jnp.int32),
                   jax.ShapeDtypeStruct((t, ROUTER_W), F32)],
        compiler_params=_cparams(("parallel",)),
        name="moe_route",
    )(logits)


def _expert_kernel(be_ref, nv_ref, first_ref, nxe_ref, ws_ref, tok_ref, dst_ref,
                   sw_ref, h_hbm, wg_hbm, wu_hbm, wd_hbm, yp_hbm,
                   xbuf0, xbuf1, ybuf0, ybuf1, wgv, wuv, wdv, sem_in, sem_out, sem_w, sem_init):
    i = pl.program_id(0)
    n = pl.num_programs(0)
    blk = MOE_BLOCK
    rt = h_hbm.shape[0] // (yp_hbm.shape[0] // 2 - blk)
    plane = yp_hbm.shape[0] // 2
    t_rows = plane - blk
    xbufs = (xbuf0, xbuf1)
    ybufs = (ybuf0, ybuf1)

    def weight_copies(e, s):
        return (pltpu.make_async_copy(wg_hbm.at[e], wgv.at[s], sem_w.at[s, 0]),
                pltpu.make_async_copy(wu_hbm.at[e], wuv.at[s], sem_w.at[s, 1]),
                pltpu.make_async_copy(wd_hbm.at[e], wdv.at[s], sem_w.at[s, 2]))

    def gather_start(b, s):
        base = b * blk
        for r in range(blk):
            row0 = pl.multiple_of(tok_ref[base + r] * rt, rt)
            pltpu.make_async_copy(h_hbm.at[pl.ds(row0, rt)],
                                  xbufs[s].at[pl.ds(r * rt, rt)], sem_in.at[s]).start()

    def gather_wait(s):
        pltpu.make_async_copy(h_hbm.at[pl.ds(0, blk * rt)], xbufs[s], sem_in.at[s]).wait()

    def scatter_start(b, s):
        base = b * blk
        for r in range(blk):
            pltpu.make_async_copy(ybufs[s].at[pl.ds(r, 1)],
                                  yp_hbm.at[pl.ds(dst_ref[base + r], 1)], sem_out.at[s]).start()

    def scatter_wait(s):
        pltpu.make_async_copy(ybufs[s], yp_hbm.at[pl.ds(0, blk)], sem_out.at[s]).wait()

    @pl.when(i == 0)
    def _():
        for cp in weight_copies(be_ref[0], 0):
            cp.start()
        gather_start(0, 0)
        ybuf1[...] = jnp.zeros_like(ybuf1)
        pltpu.make_async_copy(ybuf1, yp_hbm.at[pl.ds(t_rows, blk)], sem_out.at[0]).start()
        spare = pltpu.make_async_copy(ybuf1, yp_hbm.at[pl.ds(plane + t_rows, blk)], sem_init)
        spare.start()
        spare.wait()

    valid = nv_ref[i] > 0
    nxt = jnp.minimum(i + 1, n - 1)
    has_next = jnp.logical_and(i + 1 < n, nv_ref[nxt] > 0)

    @pl.when(jnp.logical_and(valid, first_ref[i] == 1))
    def _():
        ws = ws_ref[i]
        for cp in weight_copies(be_ref[i], ws):
            cp.wait()

        @pl.when(nxe_ref[i] >= 0)
        def _():
            for cp in weight_copies(nxe_ref[i], 1 - ws):
                cp.start()

    for par in (0, 1):
        @pl.when(jnp.logical_and(valid, i % 2 == par))
        def _(par=par):
            ws = ws_ref[i]
            gather_wait(par)
            scatter_wait(par)
            gather_start(nxt, 1 - par)
            scatter_start(jnp.maximum(i - 1, 0), 1 - par)
            x = jnp.concatenate([xbufs[par][pl.ds(j, blk, stride=rt), :] for j in range(rt)],
                                axis=1).astype(BF16)
            g = jnp.dot(x, wgv[ws].astype(BF16), preferred_element_type=F32)
            u = jnp.dot(x, wuv[ws].astype(BF16), preferred_element_type=F32)
            a = (_silu(g) * u).astype(BF16)
            y = jnp.dot(a, wdv[ws].astype(BF16), preferred_element_type=F32) * sw_ref[...]
            ybufs[par][...] = y

            @pl.when(jnp.logical_not(has_next))
            def _():
                scatter_wait(1 - par)
                scatter_start(i, par)
                gather_wait(1 - par)
                scatter_wait(par)


def _experts(plan, slot_w, h2, w_gate, w_up, w_down):
    d = w_gate.shape[1]
    rt = d // LANE
    t = h2.shape[0] // rt
    n_blocks = plan[0].shape[0]
    blk = MOE_BLOCK
    de = w_gate.shape[2]
    any_spec = pl.BlockSpec(memory_space=pl.ANY)
    return pl.pallas_call(
        _expert_kernel,
        grid_spec=pltpu.PrefetchScalarGridSpec(
            num_scalar_prefetch=len(plan),
            grid=(n_blocks,),
            in_specs=[pl.BlockSpec((blk, 1), lambda i, *_: (i, 0)),
                      any_spec, any_spec, any_spec, any_spec],
            out_specs=any_spec,
            scratch_shapes=[pltpu.VMEM((blk * rt, LANE), F32), pltpu.VMEM((blk * rt, LANE), F32),
                            pltpu.VMEM((blk, d), F32), pltpu.VMEM((blk, d), F32),
                            pltpu.VMEM((2, d, de), F32),
                            pltpu.VMEM((2, d, de), F32),
                            pltpu.VMEM((2, de, d), F32),
                            pltpu.SemaphoreType.DMA((2,)),
                            pltpu.SemaphoreType.DMA((2,)),
                            pltpu.SemaphoreType.DMA((2, 3)),
                            pltpu.SemaphoreType.DMA],
        ),
        out_shape=jax.ShapeDtypeStruct((2 * (t + blk), d), F32),
        compiler_params=_cparams(("arbitrary",)),
        name="moe_experts",
    )(*plan, slot_w, h2, w_gate, w_up, w_down).reshape(2, t + blk, d)


def _combine_kernel(x_ref, gt_ref, yp_ref, o_ref):
    o_ref[...] = x_ref[...] + gt_ref[...] * (yp_ref[0] + yp_ref[1])


def _combine(x1, gt, yp, tm):
    b, l, d = x1.shape
    tm = min(tm, l)
    nb = l // tm
    return pl.pallas_call(
        _combine_kernel,
        grid=(b, nb),
        in_specs=[pl.BlockSpec((None, tm, d), lambda bi, i: (bi, i, 0)),
                  pl.BlockSpec((None, 1, d), lambda bi, i: (bi, 0, 0)),
                  pl.BlockSpec((2, tm, d), lambda bi, i: (0, bi * nb + i, 0))],
        out_specs=pl.BlockSpec((None, tm, d), lambda bi, i: (bi, i, 0)),
        out_shape=jax.ShapeDtypeStruct((b, l, d), F32),
        compiler_params=_cparams(("parallel", "parallel")),
        name="moe_combine",
    )(x1, gt, yp)


def _dispatch_plan(eid, wts):
    blk = MOE_BLOCK
    n_assign = eid.shape[0]
    n_blocks = (n_assign + blk - 1) // blk + N_EXPERTS
    n_slots = n_blocks * blk
    order = jnp.argsort(eid).astype(jnp.int32)
    e_s = eid[order]
    counts = jnp.zeros((N_EXPERTS,), jnp.int32).at[eid].add(1)
    start = jnp.cumsum(counts) - counts
    padded = (counts + blk - 1) // blk * blk
    pend = jnp.cumsum(padded)
    pstart = pend - padded
    dest = pstart[e_s] + jnp.arange(n_assign, dtype=jnp.int32) - start[e_s]
    slot_asg = jnp.full((n_slots,), -1, jnp.int32).at[dest].set(order)
    filled = slot_asg >= 0
    safe = jnp.maximum(slot_asg, 0)
    slot_tok = jnp.where(filled, safe // 2, 0)
    n_tok = n_assign // 2
    spare = n_tok + jnp.arange(n_slots, dtype=jnp.int32) % blk
    slot_dst = jnp.where(filled, (safe % 2) * (n_tok + blk) + safe // 2, spare)
    slot_w = jnp.where(filled, wts[safe], 0.0).astype(F32)
    bstart = jnp.arange(n_blocks, dtype=jnp.int32) * blk
    block_e = jnp.minimum(jnp.searchsorted(pend, bstart, side='right'), N_EXPERTS - 1).astype(jnp.int32)
    block_nv = jnp.sum(filled.reshape(n_blocks, blk), axis=1, dtype=jnp.int32)
    first = jnp.concatenate([jnp.ones((1,), jnp.int32),
                             (block_e[1:] != block_e[:-1]).astype(jnp.int32)])
    w_slot = (jnp.cumsum(first) - 1) % 2
    e_ids = jnp.arange(N_EXPERTS, dtype=jnp.int32)
    nonempty = jnp.where(counts > 0, e_ids, N_EXPERTS)
    later = jnp.flip(lax.cummin(jnp.flip(nonempty)))
    next_e = jnp.concatenate([later[1:], jnp.full((1,), N_EXPERTS, jnp.int32)])
    next_e = jnp.where(next_e >= N_EXPERTS, -1, next_e)[block_e]
    plan = (block_e, block_nv, first, next_e, w_slot.astype(jnp.int32), slot_tok, slot_dst)
    return plan, slot_w.reshape(n_slots, 1)


def _rope_tables(length, with_rope):
    half = MLA_ROPE // 4
    if with_rope:
        rows = length // GRID_W
        row = jnp.repeat(jnp.arange(rows), GRID_W).astype(F32)
        col = jnp.tile(jnp.arange(GRID_W), rows).astype(F32)
        inv = 1.0 / (ROPE_THETA ** (jnp.arange(half, dtype=F32) / half))
        ang_r = row[:, None] * inv
        ang_c = col[:, None] * inv
        cos = jnp.concatenate([jnp.cos(ang_r), jnp.cos(ang_r), jnp.cos(ang_c), jnp.cos(ang_c)], axis=1)
        sin = jnp.concatenate([-jnp.sin(ang_r), jnp.sin(ang_r), -jnp.sin(ang_c), jnp.sin(ang_c)], axis=1)
    else:
        cos = jnp.ones((length, MLA_ROPE), F32)
        sin = jnp.zeros((length, MLA_ROPE), F32)
    pad = ((0, 0), (0, LANE - MLA_ROPE))
    return jnp.pad(cos, pad), jnp.pad(sin, pad)


def _split_w_in(w_in):
    sizes = (GLA_QK_W, GLA_QK_W, GLA_V_W, GLA_V_W, GLA_LOWRANK, GLA_LOWRANK,
             MLA_Q_RANK, MLA_KV_RANK, MLA_ROPE, D_MODEL, D_MODEL)
    parts, off = [], 0
    for s in sizes:
        parts.append(w_in[:, off:off + s])
        off += s
    return parts


def kernel(x, c, ctx, c_ctx, w_mod, b_mod, norm1_g, norm2_g, w_in, w_decay_f, b_decay_f, w_decay_b, b_decay_b, gla_norm_g, q_a_norm_g, w_uq, kv_a_norm_g, w_ukv, q_norm_g, k_norm_g, w_o_gla, w_o_mla, w_out, w_router_group, b_router_group, w_router_expert, b_router_expert, w_exp_gate, w_exp_up, w_exp_down):
    assert w_mod.shape[0] == 1, "single-layer block"
    b, l, d = x.shape
    lc = ctx.shape[1]
    t = b * l

    c_rows = jnp.concatenate([c, c_ctx[None, :], jnp.zeros((8 - b - 1, d), F32)], axis=0)
    mod = _modulation(c_rows, w_mod[0], b_mod[0])
    sh1, sc1, gt1, sh2, sc2, gt2 = [mod[:b, i * d:(i + 1) * d].reshape(b, 1, d) for i in range(6)]
    sh1c = jnp.broadcast_to(mod[b:b + 1, 0:d].reshape(1, 1, d), (b, 1, d))
    sc1c = jnp.broadcast_to(mod[b:b + 1, d:2 * d].reshape(1, 1, d), (b, 1, d))

    wq_, wk_, wv_, wr_, waf, wab, wcq, wckv, wkr, wga, wgb = _split_w_in(w_in[0])
    small = jnp.concatenate(
        [wkr, waf, wab, jnp.zeros((d, SMALL_W - MLA_ROPE - 2 * GLA_LOWRANK), F32)], axis=1)
    w_lat = jnp.concatenate([wq_, wk_, wv_, wr_, wga, wgb, wcq, wckv, small], axis=1).astype(BF16)
    w_ctx = jnp.concatenate([wk_, wv_, wckv, small], axis=1).astype(BF16)
    g1 = norm1_g[0].reshape(1, d)
    p = _in_projection(x, g1, sc1, sh1, w_lat, tm=1024)
    pc = _in_projection(ctx, g1, sc1c, sh1c, w_ctx, tm=lc)

    def decay_w(w, off):
        return jnp.zeros((LANE, GLA_QK_W), F32).at[off:off + GLA_LOWRANK].set(w).astype(BF16)

    y_gla = _gla(p, pc,
                 decay_w(w_decay_f[0], SM_AF), b_decay_f[0].reshape(1, -1),
                 decay_w(w_decay_b[0], SM_AB), b_decay_b[0].reshape(1, -1),
                 gla_norm_g[0].reshape(1, -1))

    wq = jnp.pad(w_uq[0].reshape(MLA_Q_RANK, MLA_HEADS, MLA_QK_DIM),
                 ((0, 0), (0, 0), (0, MLA_QK_PAD - MLA_QK_DIM))).reshape(MLA_Q_RANK, -1).astype(BF16)
    wkv = w_ukv[0].astype(BF16)
    qn_g = jnp.pad(q_norm_g[0], (0, MLA_QK_PAD - MLA_QK_DIM)).reshape(1, -1)
    kn_nope = k_norm_g[0][:MLA_NOPE].reshape(1, -1)
    kn_rope = jnp.pad(k_norm_g[0][MLA_NOPE:], (0, LANE - MLA_ROPE)).reshape(1, -1)
    cos, sin = _rope_tables(l, True)
    cos_c, sin_c = _rope_tables(lc, False)
    q_m = _mla_queries(p, q_a_norm_g[0].reshape(1, -1), wq, qn_g, cos, sin, tm=512)
    kva = kv_a_norm_g[0].reshape(1, -1)
    k_m, v_m = _mla_keys_values(p, pc, kva, wkv, kn_nope, kn_rope,
                                jnp.concatenate([cos, cos_c]), jnp.concatenate([sin, sin_c]))
    y_mla = _attention(q_m, k_m, v_m, tq=256)

    y = _branch_merge(y_gla, y_mla, w_o_gla[0].astype(BF16), w_o_mla[0].astype(BF16), p, tm=1024, tn=512)
    w_router = jnp.concatenate(
        [w_router_group[0], w_router_expert[0],
         jnp.zeros((d, ROUTER_W - N_GROUPS - N_EXPERTS), F32)], axis=1).astype(BF16)
    b_router = jnp.concatenate(
        [b_router_group[0], b_router_expert[0],
         jnp.zeros((ROUTER_W - N_GROUPS - N_EXPERTS,), F32)]).reshape(1, -1)
    x1, h2, logits = _out_projection(y, w_out[0].astype(BF16), x, gt1, norm2_g[0].reshape(1, d),
                                     sc2, sh2, w_router, b_router, tm=256)

    e_pick, w_pick = _route(logits.reshape(t, ROUTER_W), tm=1024)
    eid = e_pick[:, :2].reshape(-1)
    wts = w_pick[:, :2].reshape(-1)
    plan, slot_w = _dispatch_plan(eid, wts)
    yp = _experts(plan, slot_w, h2.reshape(t * (d // LANE), LANE),
                  w_exp_gate[0], w_exp_up[0], w_exp_down[0])
    return _combine(x1, gt2, yp, tm=512)
```

```python
import functools
import math

import jax
import jax.numpy as jnp
from jax import lax
from jax.experimental import pallas as pl
from jax.experimental.pallas import tpu as pltpu

F32 = jnp.float32
BF16 = jnp.bfloat16

D_MODEL = 2048
GRID_W = 64
EPS = 1e-6

GLA_HEADS = 4
GLA_DK = 256
GLA_DV = 512
GLA_LOWRANK = 16
GLA_TAU = 16.0
GLA_CHUNK = 64
GLA_GROUP = 8
GLA_QK_W = GLA_HEADS * GLA_DK
GLA_V_W = GLA_HEADS * GLA_DV

MLA_HEADS = 16
MLA_Q_RANK = 512
MLA_KV_RANK = 512
MLA_NOPE = 128
MLA_ROPE = 64
MLA_V = 128
MLA_QK_DIM = MLA_NOPE + MLA_ROPE
MLA_QK_PAD = 256
ROPE_THETA = 10000.0
LOG2E = math.log2(math.e)

N_GROUPS = 8
EXP_PER_GROUP = 8
N_EXPERTS = N_GROUPS * EXP_PER_GROUP
D_EXPERT = 512
ROUTER_W = 128
MOE_BLOCK = 256

LANE = 128
ROW_TILES = D_MODEL // LANE
ROW_PITCH = 20
VMEM_LIMIT = 56 * 1024 * 1024

COL_Q = 0
COL_K = COL_Q + GLA_QK_W
COL_V = COL_K + GLA_QK_W
COL_R = COL_V + GLA_V_W
COL_GA = COL_R + GLA_V_W
COL_GB = COL_GA + D_MODEL
COL_CQ = COL_GB + D_MODEL
COL_CKV = COL_CQ + MLA_Q_RANK
COL_SMALL = COL_CKV + MLA_KV_RANK
SMALL_W = 256
NP_LAT = COL_SMALL + SMALL_W
SM_AF = MLA_ROPE
SM_AB = MLA_ROPE + GLA_LOWRANK
CCOL_K = 0
CCOL_V = CCOL_K + GLA_QK_W
CCOL_CKV = CCOL_V + GLA_V_W
CCOL_SMALL = CCOL_CKV + MLA_KV_RANK
NP_CTX = CCOL_SMALL + SMALL_W
INPROJ_TN = 1280


def _cparams(sem):
    return pltpu.CompilerParams(dimension_semantics=sem, vmem_limit_bytes=VMEM_LIMIT)


def _silu(x):
    return x * jax.nn.sigmoid(x)


def _mod_kernel(c_ref, w_ref, b_ref, o_ref):
    a = _silu(c_ref[...]).astype(BF16)
    o_ref[...] = jnp.dot(a, w_ref[...].astype(BF16), preferred_element_type=F32) + b_ref[...]


def _modulation(c_rows, w_mod, b_mod):
    m, d = c_rows.shape
    n = w_mod.shape[1]
    tn = 1024
    return pl.pallas_call(
        _mod_kernel,
        grid=(n // tn,),
        in_specs=[pl.BlockSpec((m, d), lambda j: (0, 0)),
                  pl.BlockSpec((d, tn), lambda j: (0, j)),
                  pl.BlockSpec((1, tn), lambda j: (0, j))],
        out_specs=pl.BlockSpec((m, tn), lambda j: (0, j)),
        out_shape=jax.ShapeDtypeStruct((m, n), F32),
        compiler_params=_cparams(("parallel",)),
        name="modulation",
    )(c_rows, w_mod, b_mod.reshape(1, n))


def _inproj_kernel(x_ref, g_ref, sc_ref, sh_ref, w_ref, o_ref, h_ref):
    @pl.when(pl.program_id(2) == 0)
    def _():
        x = x_ref[...]
        ms = jnp.mean(x * x, axis=-1, keepdims=True)
        y = x * lax.rsqrt(ms + EPS) * g_ref[...]
        h_ref[...] = (y * (1.0 + sc_ref[...]) + sh_ref[...]).astype(BF16)

    o_ref[...] = jnp.dot(h_ref[...], w_ref[...], preferred_element_type=F32).astype(o_ref.dtype)


def _in_projection(x, g, sc, sh, w, tm):
    b, l, d = x.shape
    tm = min(tm, l)
    n = w.shape[1]
    tn = INPROJ_TN
    return pl.pallas_call(
        _inproj_kernel,
        grid=(b, l // tm, n // tn),
        in_specs=[pl.BlockSpec((None, tm, d), lambda bi, i, j: (bi, i, 0)),
                  pl.BlockSpec((1, d), lambda bi, i, j: (0, 0)),
                  pl.BlockSpec((None, 1, d), lambda bi, i, j: (bi, 0, 0)),
                  pl.BlockSpec((None, 1, d), lambda bi, i, j: (bi, 0, 0)),
                  pl.BlockSpec((d, tn), lambda bi, i, j: (0, j))],
        out_specs=pl.BlockSpec((None, tm, tn), lambda bi, i, j: (bi, i, j)),
        out_shape=jax.ShapeDtypeStruct((b, l, n), BF16),
        scratch_shapes=[pltpu.VMEM((tm, d), BF16)],
        compiler_params=_cparams(("parallel", "parallel", "arbitrary")),
        name="in_projection",
    )(x, g, sc, sh, w)


def _gla_kernel(q_ref, k_ref, v_ref, r_ref, sm_ref, kc_ref, vc_ref, smc_ref,
                wf_ref, bf_ref, wb_ref, bb_ref, gn_ref, y_ref, o_ref, sf_ref, sb_ref):
    c = GLA_CHUNK
    n_lat = q_ref.shape[0] // c
    n_ctx = kc_ref.shape[0] // c
    row = lax.broadcasted_iota(jnp.int32, (c, c), 0)
    col = lax.broadcasted_iota(jnp.int32, (c, c), 1)
    lower = row >= col
    upper = row <= col
    nt = (((1,), (1,)), ((), ()))
    tn = (((0,), (0,)), ((), ()))

    def cum_decay(sm, w_ref, b_ref, tri, n_chunks):
        z = jnp.dot(sm, w_ref[...], preferred_element_type=F32) + b_ref[...]
        g = (jnp.minimum(z, 0.0) - jnp.log(1.0 + jnp.exp(-jnp.abs(z)))) * (1.0 / GLA_TAU)
        g_hi = g.astype(BF16)
        g_lo = (g - g_hi.astype(F32)).astype(BF16)
        t = jnp.where(tri, 1.0, 0.0).astype(BF16)
        return [jnp.dot(t, g_hi[i * c:(i + 1) * c], preferred_element_type=F32)
                + jnp.dot(t, g_lo[i * c:(i + 1) * c], preferred_element_type=F32)
                for i in range(n_chunks)]

    fwd = (wf_ref, bf_ref, lower, c - 1, sf_ref)
    bwd = (wb_ref, bb_ref, upper, 0, sb_ref)

    def group(r0, n_chunks, direction, sm_r, k_r, v_r, q_r):
        w_ref, b_ref, tri, end_row, st_ref = direction
        rows = n_chunks * c
        k_all = k_r[pl.ds(r0, rows), :].astype(F32)
        v_all = v_r[pl.ds(r0, rows), :]
        q_all = None if q_r is None else q_r[pl.ds(r0, rows), :].astype(F32) * (GLA_DK ** -0.5)
        bc_all = cum_decay(sm_r[pl.ds(r0, rows), :], w_ref, b_ref, tri, n_chunks)
        local = []
        for i in range(n_chunks):
            sl = slice(i * c, (i + 1) * c)
            bc = bc_all[i]
            bend = bc[end_row:end_row + 1, :]
            k, v = k_all[sl], v_all[sl]
            ke = (k * jnp.exp(bend - bc)).astype(BF16)
            upd = lax.dot_general(v, ke, tn, preferred_element_type=F32)
            qd = o_intra = None
            if q_all is not None:
                qd = (q_all[sl] * jnp.exp(bc)).astype(BF16)
                ki = (k * jnp.exp(-bc)).astype(BF16)
                att = lax.dot_general(qd, ki, nt, preferred_element_type=F32)
                att = jnp.where(tri, att, 0.0).astype(BF16)
                o_intra = jnp.dot(att, v, preferred_element_type=F32)
            local.append((jnp.exp(bend), upd, qd, o_intra))
        st = st_ref[...]
        outs = [None] * n_chunks
        for i in (range(n_chunks) if end_row else reversed(range(n_chunks))):
            decay, upd, qd, o_intra = local[i]
            if q_all is not None:
                outs[i] = o_intra + lax.dot_general(qd, st.astype(BF16), nt, preferred_element_type=F32)
            st = st * decay + upd
        st_ref[...] = st
        return None if q_all is None else jnp.concatenate(outs, axis=0)

    def readout(r0, o):
        ms = jnp.mean(o * o, axis=-1, keepdims=True)
        yn = o * lax.rsqrt(ms + EPS) * gn_ref[...]
        r = r_ref[pl.ds(r0, o.shape[0]), :].astype(F32)
        y_ref[pl.ds(r0, o.shape[0]), :] = (yn * _silu(r)).astype(y_ref.dtype)

    sf_ref[...] = jnp.zeros_like(sf_ref)
    sb_ref[...] = jnp.zeros_like(sb_ref)
    group(0, n_ctx, fwd, smc_ref, kc_ref, vc_ref, None)
    group(0, n_ctx, bwd, smc_ref, kc_ref, vc_ref, None)

    g_lat = min(GLA_GROUP, n_lat // 2)
    assert n_lat % (2 * g_lat) == 0
    n_groups = n_lat // g_lat
    rows = g_lat * c

    def scan(i, direction):
        r0 = pl.multiple_of(i * rows, rows)
        return r0, group(r0, g_lat, direction, sm_ref, k_ref, v_ref, q_ref)

    def first_half(i, carry):
        for r0, o in (scan(i, fwd), scan(n_groups - 1 - i, bwd)):
            o_ref[pl.ds(r0, rows), :] = o
        return carry

    lax.fori_loop(0, n_groups // 2, first_half, 0)

    def second_half(i, carry):
        for r0, o in (scan(i, fwd), scan(n_groups - 1 - i, bwd)):
            readout(r0, o_ref[pl.ds(r0, rows), :] + o)
        return carry

    lax.fori_loop(n_groups // 2, n_groups, second_half, 0)


def _gla(p, pc, wf, bf, wb, bb, gn):
    b, l, _ = p.shape
    lc = pc.shape[1]
    dk, dv = GLA_DK, GLA_DV
    hmap = lambda off: (lambda bi, h: (bi, 0, off + h))
    wmap = lambda bi, h: (0, h)
    return pl.pallas_call(
        _gla_kernel,
        grid=(b, GLA_HEADS),
        in_specs=[pl.BlockSpec((None, l, dk), hmap(COL_Q // dk)),
                  pl.BlockSpec((None, l, dk), hmap(COL_K // dk)),
                  pl.BlockSpec((None, l, dv), hmap(COL_V // dv)),
                  pl.BlockSpec((None, l, dv), hmap(COL_R // dv)),
                  pl.BlockSpec((None, l, LANE), lambda bi, h: (bi, 0, COL_SMALL // LANE)),
                  pl.BlockSpec((None, lc, dk), hmap(CCOL_K // dk)),
                  pl.BlockSpec((None, lc, dv), hmap(CCOL_V // dv)),
                  pl.BlockSpec((None, lc, LANE), lambda bi, h: (bi, 0, CCOL_SMALL // LANE)),
                  pl.BlockSpec((LANE, dk), wmap),
                  pl.BlockSpec((1, dk), wmap),
                  pl.BlockSpec((LANE, dk), wmap),
                  pl.BlockSpec((1, dk), wmap),
                  pl.BlockSpec((1, dv), lambda bi, h: (0, 0))],
        out_specs=pl.BlockSpec((None, l, dv), lambda bi, h: (bi, 0, h)),
        out_shape=jax.ShapeDtypeStruct((b, l, GLA_V_W), BF16),
        scratch_shapes=[pltpu.VMEM((l, dv), F32), pltpu.VMEM((dv, dk), F32), pltpu.VMEM((dv, dk), F32)],
        compiler_params=_cparams(("parallel", "parallel")),
        name="gla",
    )(p, p, p, p, p, pc, pc, pc, wf, bf, wb, bb, gn)


def _rope_partner(t):
    lane = lax.broadcasted_iota(jnp.int32, t.shape, 1)
    first = (lane % 32) < 16
    return jnp.where(first, pltpu.roll(t, LANE - 16, 1), pltpu.roll(t, 16, 1))


def _mla_q_kernel(cq_ref, ga_ref, w_ref, gn_ref, cos_ref, sin_ref, o_ref):
    cq = cq_ref[...].astype(F32)
    cn = (cq * lax.rsqrt(jnp.mean(cq * cq, axis=-1, keepdims=True) + EPS) * ga_ref[...]).astype(BF16)
    cos = cos_ref[...]
    sin = sin_ref[...]
    scale = MLA_QK_DIM ** -0.5 * LOG2E
    for h in range(MLA_HEADS):
        qh = jnp.dot(cn, w_ref[:, h * MLA_QK_PAD:(h + 1) * MLA_QK_PAD], preferred_element_type=F32)
        ms = jnp.sum(qh * qh, axis=-1, keepdims=True) * (1.0 / MLA_QK_DIM)
        qn = qh * lax.rsqrt(ms + EPS) * gn_ref[...]
        t = qn[:, MLA_NOPE:]
        rot = t * cos + _rope_partner(t) * sin
        o_ref[h, :, :MLA_NOPE] = (qn[:, :MLA_NOPE] * scale).astype(o_ref.dtype)
        o_ref[h, :, MLA_NOPE:] = (rot * scale).astype(o_ref.dtype)


def _mla_queries(p, ga, wq, gn, cos, sin, tm):
    b, l, _ = p.shape
    tm = min(tm, l)
    return pl.pallas_call(
        _mla_q_kernel,
        grid=(b, l // tm),
        in_specs=[pl.BlockSpec((None, tm, MLA_Q_RANK), lambda bi, i: (bi, i, COL_CQ // MLA_Q_RANK)),
                  pl.BlockSpec((1, MLA_Q_RANK), lambda bi, i: (0, 0)),
                  pl.BlockSpec(wq.shape, lambda bi, i: (0, 0)),
                  pl.BlockSpec((1, MLA_QK_PAD), lambda bi, i: (0, 0)),
                  pl.BlockSpec((tm, LANE), lambda bi, i: (i, 0)),
                  pl.BlockSpec((tm, LANE), lambda bi, i: (i, 0))],
        out_specs=pl.BlockSpec((None, MLA_HEADS, tm, MLA_QK_PAD), lambda bi, i: (bi, 0, i, 0)),
        out_shape=jax.ShapeDtypeStruct((b, MLA_HEADS, l, MLA_QK_PAD), BF16),
        compiler_params=_cparams(("parallel", "parallel")),
        name="mla_queries",
    )(p, ga, wq, gn, cos, sin)


def _mla_kv_kernel(ckv_ref, sm_ref, ckvc_ref, smc_ref, ga_ref, w_ref, gk_ref, gr_ref,
                   cos_ref, sin_ref, k_ref, v_ref, *, n_lat):
    is_ctx = pl.program_id(1) >= n_lat
    ckv = jnp.where(is_ctx, ckvc_ref[...], ckv_ref[...]).astype(F32)
    cn = (ckv * lax.rsqrt(jnp.mean(ckv * ckv, axis=-1, keepdims=True) + EPS) * ga_ref[...]).astype(BF16)
    sm = jnp.where(is_ctx, smc_ref[...], sm_ref[...]).astype(F32)
    lane = lax.broadcasted_iota(jnp.int32, sm.shape, 1)
    kr = jnp.where(lane < MLA_ROPE, sm, 0.0)
    ss_r = jnp.sum(kr * kr, axis=-1, keepdims=True)
    krg = kr * gr_ref[...]
    rot = krg * cos_ref[...] + _rope_partner(krg) * sin_ref[...]
    hw = MLA_NOPE + MLA_V
    ones_col = jnp.where(lane == 0, 1.0, 0.0).astype(v_ref.dtype)
    for h in range(MLA_HEADS):
        kvh = jnp.dot(cn, w_ref[:, h * hw:(h + 1) * hw], preferred_element_type=F32)
        kn = kvh[:, :MLA_NOPE]
        ms = (jnp.sum(kn * kn, axis=-1, keepdims=True) + ss_r) * (1.0 / MLA_QK_DIM)
        rs = lax.rsqrt(ms + EPS)
        k_ref[h, :, :MLA_NOPE] = (kn * rs * gk_ref[...]).astype(k_ref.dtype)
        k_ref[h, :, MLA_NOPE:] = (rot * rs).astype(k_ref.dtype)
        v_ref[h, :, :MLA_V] = kvh[:, MLA_NOPE:].astype(v_ref.dtype)
        v_ref[h, :, MLA_V:] = ones_col


def _mla_keys_values(p, pc, ga, wkv, gk, gr, cos, sin):
    b, l, _ = p.shape
    lc = pc.shape[1]
    tm = min(lc, l)
    assert l % tm == 0 and lc % tm == 0
    n_lat, n_ctx = l // tm, lc // tm
    lat = lambda blk: (lambda bi, i: (bi, jnp.minimum(i, n_lat - 1), blk))
    ctx = lambda blk: (lambda bi, i: (bi, jnp.maximum(i - n_lat, 0), blk))
    const = lambda bi, i: (0, 0)
    out_spec = pl.BlockSpec((None, MLA_HEADS, tm, MLA_QK_PAD), lambda bi, i: (bi, 0, i, 0))
    shape = jax.ShapeDtypeStruct((b, MLA_HEADS, l + lc, MLA_QK_PAD), BF16)
    return pl.pallas_call(
        functools.partial(_mla_kv_kernel, n_lat=n_lat),
        grid=(b, n_lat + n_ctx),
        in_specs=[pl.BlockSpec((None, tm, MLA_KV_RANK), lat(COL_CKV // MLA_KV_RANK)),
                  pl.BlockSpec((None, tm, LANE), lat(COL_SMALL // LANE)),
                  pl.BlockSpec((None, tm, MLA_KV_RANK), ctx(CCOL_CKV // MLA_KV_RANK)),
                  pl.BlockSpec((None, tm, LANE), ctx(CCOL_SMALL // LANE)),
                  pl.BlockSpec((1, MLA_KV_RANK), const),
                  pl.BlockSpec(wkv.shape, const),
                  pl.BlockSpec((1, LANE), const),
                  pl.BlockSpec((1, LANE), const),
                  pl.BlockSpec((tm, LANE), lambda bi, i: (i, 0)),
                  pl.BlockSpec((tm, LANE), lambda bi, i: (i, 0))],
        out_specs=[out_spec, out_spec],
        out_shape=[shape, shape],
        compiler_params=_cparams(("parallel", "arbitrary")),
        name="mla_keys_values",
    )(p, p, pc, pc, ga, wkv, gk, gr, cos, sin)


def _attn_kernel(q_ref, k_ref, v_ref, o_ref, s0_ref, s1_ref, m0_ref, m1_ref, *, tq):
    nt = (((1,), (1,)), ((), ()))
    n = q_ref.shape[0] // tq
    bufs = ((s0_ref, m0_ref), (s1_ref, m1_ref))

    def scores(i, slot):
        s_ref, m_ref = bufs[slot]
        r0 = pl.multiple_of(i * tq, tq)
        s = lax.dot_general(q_ref[pl.ds(r0, tq), :], k_ref[...], nt, preferred_element_type=F32)
        s_ref[...] = s
        m_ref[...] = jnp.max(s, axis=-1, keepdims=True)

    def outputs(i, slot):
        s_ref, m_ref = bufs[slot]
        r0 = pl.multiple_of(i * tq, tq)
        p = jnp.exp2(s_ref[...] - m_ref[...]).astype(BF16)
        o = jnp.dot(p, v_ref[...], preferred_element_type=F32)
        o_ref[pl.ds(r0, tq), :] = (o[:, :MLA_V] / o[:, MLA_V:MLA_V + 1]).astype(o_ref.dtype)

    scores(0, 0)
    if n % 2 == 0:
        def body(j, carry):
            scores(2 * j + 1, 1)
            outputs(2 * j, 0)
            scores(2 * j + 2, 0)
            outputs(2 * j + 1, 1)
            return carry

        lax.fori_loop(0, n // 2 - 1, body, 0)
        scores(n - 1, 1)
        outputs(n - 2, 0)
        outputs(n - 1, 1)
    else:
        assert n == 1
        outputs(0, 0)


def _attention(q, k, v, tq):
    b, h, l, dq = q.shape
    tq = min(tq, l)
    lk = k.shape[2]
    bh = lambda bi, hi: (bi, hi, 0, 0)
    return pl.pallas_call(
        functools.partial(_attn_kernel, tq=tq),
        grid=(b, h),
        in_specs=[pl.BlockSpec((None, None, l, dq), bh),
                  pl.BlockSpec((None, None, lk, dq), bh),
                  pl.BlockSpec((None, None, lk, dq), bh)],
        out_specs=pl.BlockSpec((None, l, MLA_V), lambda bi, hi: (bi, 0, hi)),
        out_shape=jax.ShapeDtypeStruct((b, l, h * MLA_V), BF16),
        scratch_shapes=[pltpu.VMEM((tq, lk), F32), pltpu.VMEM((tq, lk), F32),
                        pltpu.VMEM((tq, 1), F32), pltpu.VMEM((tq, 1), F32)],
        compiler_params=_cparams(("parallel", "parallel")),
        name="mla_attention",
    )(q, k, v)


def _merge_kernel(yg_ref, ym_ref, wg_ref, wm_ref, ga_ref, gb_ref, o_ref):
    a = jnp.dot(yg_ref[...], wg_ref[...], preferred_element_type=F32)
    m = jnp.dot(ym_ref[...], wm_ref[...], preferred_element_type=F32)
    y = (jax.nn.sigmoid(ga_ref[...].astype(F32)) * a
         + jax.nn.sigmoid(gb_ref[...].astype(F32)) * m)
    o_ref[...] = y.astype(o_ref.dtype)


def _branch_merge(yg, ym, wg, wm, p, tm, tn):
    b, l, d = yg.shape
    tm = min(tm, l)
    n = wg.shape[1]
    return pl.pallas_call(
        _merge_kernel,
        grid=(b, l // tm, n // tn),
        in_specs=[pl.BlockSpec((None, tm, d), lambda bi, i, j: (bi, i, 0)),
                  pl.BlockSpec((None, tm, d), lambda bi, i, j: (bi, i, 0)),
                  pl.BlockSpec((d, tn), lambda bi, i, j: (0, j)),
                  pl.BlockSpec((d, tn), lambda bi, i, j: (0, j)),
                  pl.BlockSpec((None, tm, tn), lambda bi, i, j: (bi, i, COL_GA // tn + j)),
                  pl.BlockSpec((None, tm, tn), lambda bi, i, j: (bi, i, COL_GB // tn + j))],
        out_specs=pl.BlockSpec((None, tm, tn), lambda bi, i, j: (bi, i, j)),
        out_shape=jax.ShapeDtypeStruct((b, l, n), BF16),
        compiler_params=_cparams(("parallel", "parallel", "arbitrary")),
        name="branch_merge",
    )(yg, ym, wg, wm, p, p)


def _outproj_kernel(y_ref, w_ref, x_ref, gt_ref, g2_ref, sc_ref, sh_ref, wr_ref, br_ref,
                    x1_ref, h2_ref, lg_ref):
    x1 = x_ref[...] + gt_ref[...] * jnp.dot(y_ref[...], w_ref[...], preferred_element_type=F32)
    x1_ref[...] = x1
    ms = jnp.mean(x1 * x1, axis=-1, keepdims=True)
    h2 = x1 * lax.rsqrt(ms + EPS) * g2_ref[...] * (1.0 + sc_ref[...]) + sh_ref[...]
    tm = x1.shape[0]
    for j in range(ROW_TILES):
        h2_ref[pl.ds(j, tm, stride=ROW_TILES), :] = h2[:, j * LANE:(j + 1) * LANE]
    lg_ref[...] = jnp.dot(h2.astype(BF16), wr_ref[...], preferred_element_type=F32) + br_ref[...]


def _out_projection(y, w, x, gt, g2, sc, sh, wr, br, tm):
    b, l, d = x.shape
    tm = min(tm, l)
    row = lambda bi, i: (bi, i, 0)
    per_b = lambda bi, i: (bi, 0, 0)
    const = lambda bi, i: (0, 0)
    return pl.pallas_call(
        _outproj_kernel,
        grid=(b, l // tm),
        in_specs=[pl.BlockSpec((None, tm, d), row),
                  pl.BlockSpec((d, d), const),
                  pl.BlockSpec((None, tm, d), row),
                  pl.BlockSpec((None, 1, d), per_b),
                  pl.BlockSpec((1, d), const),
                  pl.BlockSpec((None, 1, d), per_b),
                  pl.BlockSpec((None, 1, d), per_b),
                  pl.BlockSpec((d, ROUTER_W), const),
                  pl.BlockSpec((1, ROUTER_W), const)],
        out_specs=[pl.BlockSpec((None, tm, d), row),
                   pl.BlockSpec((None, tm * ROW_TILES, LANE), row),
                   pl.BlockSpec((None, tm, ROUTER_W), row)],
        out_shape=[jax.ShapeDtypeStruct((b, l, d), F32),
                   jax.ShapeDtypeStruct((b, l * ROW_TILES, LANE), F32),
                   jax.ShapeDtypeStruct((b, l, ROUTER_W), F32)],
        compiler_params=_cparams(("parallel", "parallel")),
        name="out_projection",
    )(y, w, x, gt, g2, sc, sh, wr, br)


def _route_kernel(lg_ref, e_ref, w_ref):
    lg = lg_ref[...]
    lane = lax.broadcasted_iota(jnp.int32, lg.shape, 1)
    neg = jnp.float32(-jnp.inf)
    big = jnp.int32(1 << 20)
    is_g = lane < N_GROUPS
    gl = jnp.where(is_g, lg, neg)
    gm = jnp.max(gl, axis=-1, keepdims=True)
    gidx = jnp.min(jnp.where(gl == gm, lane, big), axis=-1, keepdims=True)
    gsum = jnp.sum(jnp.where(is_g, jnp.exp(gl - gm), 0.0), axis=-1, keepdims=True)
    p_g = 1.0 / gsum
    g_lo = N_GROUPS + gidx * EXP_PER_GROUP
    in_grp = (lane >= g_lo) & (lane < g_lo + EXP_PER_GROUP)
    el = jnp.where(in_grp, lg, neg)
    em = jnp.max(el, axis=-1, keepdims=True)
    z = jnp.sum(jnp.where(in_grp, jnp.exp(el - em), 0.0), axis=-1, keepdims=True)
    i1 = jnp.min(jnp.where(el == em, lane, big), axis=-1, keepdims=True)
    el2 = jnp.where(lane == i1, neg, el)
    em2 = jnp.max(el2, axis=-1, keepdims=True)
    i2 = jnp.min(jnp.where(el2 == em2, lane, big), axis=-1, keepdims=True)
    p1 = 1.0 / z
    p2 = jnp.exp(em2 - em) / z
    tot = p1 + p2
    e_ref[...] = jnp.where(lane == 0, i1 - N_GROUPS, jnp.where(lane == 1, i2 - N_GROUPS, 0))
    w_ref[...] = jnp.where(lane == 0, p_g * p1 / tot, jnp.where(lane == 1, p_g * p2 / tot, 0.0))


def _route(logits, tm):
    t = logits.shape[0]
    tm = min(tm, t)
    spec = pl.BlockSpec((tm, ROUTER_W), lambda i: (i, 0))
    return pl.pallas_call(
        _route_kernel,
        grid=(t // tm,),
        in_specs=[spec],
        out_specs=[spec, spec],
        out_shape=[jax.ShapeDtypeStruct((t, ROUTER_W), jnp.int32),
                   jax.ShapeDtypeStruct((t, ROUTER_W), F32)],
        compiler_params=_cparams(("parallel",)),
        name="moe_route",
    )(logits)


def _expert_kernel(be_ref, nv_ref, first_ref, nxe_ref, ws_ref, tok_ref, dst_ref,
                   sw_ref, h_hbm, wg_hbm, wu_hbm, wd_hbm, yp_hbm,
                   xbuf0, xbuf1, ybuf0, ybuf1, wgv, wuv, wdv, sem_in, sem_out, sem_w, sem_init):
    i = pl.program_id(0)
    n = pl.num_programs(0)
    blk = MOE_BLOCK
    rt, pitch = ROW_TILES, ROW_PITCH
    plane = yp_hbm.shape[0] // (2 * rt)
    t_rows = plane - blk
    xbufs = (xbuf0, xbuf1)
    ybufs = (ybuf0, ybuf1)

    def weight_copies(e, s):
        return (pltpu.make_async_copy(wg_hbm.at[e], wgv.at[s], sem_w.at[s, 0]),
                pltpu.make_async_copy(wu_hbm.at[e], wuv.at[s], sem_w.at[s, 1]),
                pltpu.make_async_copy(wd_hbm.at[e], wdv.at[s], sem_w.at[s, 2]))

    def gather_start(b, s):
        base = b * blk
        for r in range(blk):
            row0 = pl.multiple_of(tok_ref[base + r] * rt, rt)
            pltpu.make_async_copy(h_hbm.at[pl.ds(row0, rt)],
                                  xbufs[s].at[pl.ds(r * pitch, rt)], sem_in.at[s]).start()

    def gather_wait(s):
        pltpu.make_async_copy(h_hbm.at[pl.ds(0, blk * rt)], xbufs[s].at[pl.ds(0, blk * rt)],
                              sem_in.at[s]).wait()

    def scatter_start(b, s):
        base = b * blk
        for r in range(blk):
            row0 = pl.multiple_of(dst_ref[base + r] * rt, rt)
            pltpu.make_async_copy(ybufs[s].at[pl.ds(r * pitch, rt)],
                                  yp_hbm.at[pl.ds(row0, rt)], sem_out.at[s]).start()

    def scatter_wait(s):
        pltpu.make_async_copy(ybufs[s].at[pl.ds(0, blk * rt)], yp_hbm.at[pl.ds(0, blk * rt)],
                              sem_out.at[s]).wait()

    @pl.when(i == 0)
    def _():
        for cp in weight_copies(be_ref[0], 0):
            cp.start()
        gather_start(0, 0)
        ybuf1[...] = jnp.zeros_like(ybuf1)
        zeros = ybuf1.at[pl.ds(0, blk * rt)]
        pltpu.make_async_copy(zeros, yp_hbm.at[pl.ds(t_rows * rt, blk * rt)], sem_out.at[0]).start()
        spare = pltpu.make_async_copy(zeros, yp_hbm.at[pl.ds((plane + t_rows) * rt, blk * rt)], sem_init)
        spare.start()
        spare.wait()

    valid = nv_ref[i] > 0
    nxt = jnp.minimum(i + 1, n - 1)
    has_next = jnp.logical_and(i + 1 < n, nv_ref[nxt] > 0)

    @pl.when(jnp.logical_and(valid, first_ref[i] == 1))
    def _():
        ws = ws_ref[i]
        for cp in weight_copies(be_ref[i], ws):
            cp.wait()

        @pl.when(nxe_ref[i] >= 0)
        def _():
            for cp in weight_copies(nxe_ref[i], 1 - ws):
                cp.start()

    for par in (0, 1):
        @pl.when(jnp.logical_and(valid, i % 2 == par))
        def _(par=par):
            ws = ws_ref[i]
            gather_wait(par)
            scatter_wait(par)
            gather_start(nxt, 1 - par)
            scatter_start(jnp.maximum(i - 1, 0), 1 - par)
            x = jnp.concatenate([xbufs[par][pl.ds(j, blk, stride=pitch), :] for j in range(rt)],
                                axis=1).astype(BF16)
            g = jnp.dot(x, wgv[ws].astype(BF16), preferred_element_type=F32)
            u = jnp.dot(x, wuv[ws].astype(BF16), preferred_element_type=F32)
            a = (_silu(g) * u).astype(BF16)
            y = jnp.dot(a, wdv[ws].astype(BF16), preferred_element_type=F32) * sw_ref[...]
            for j in range(rt):
                ybufs[par][pl.ds(j, blk, stride=pitch), :] = y[:, j * LANE:(j + 1) * LANE]

            @pl.when(jnp.logical_not(has_next))
            def _():
                scatter_wait(1 - par)
                scatter_start(i, par)
                gather_wait(1 - par)
                scatter_wait(par)


def _experts(plan, slot_w, h2, w_gate, w_up, w_down):
    d = w_gate.shape[1]
    rt, pitch = ROW_TILES, ROW_PITCH
    assert d == rt * LANE
    t = h2.shape[0] // rt
    n_blocks = plan[0].shape[0]
    blk = MOE_BLOCK
    de = w_gate.shape[2]
    any_spec = pl.BlockSpec(memory_space=pl.ANY)
    return pl.pallas_call(
        _expert_kernel,
        grid_spec=pltpu.PrefetchScalarGridSpec(
            num_scalar_prefetch=len(plan),
            grid=(n_blocks,),
            in_specs=[pl.BlockSpec((blk, 1), lambda i, *_: (i, 0)),
                      any_spec, any_spec, any_spec, any_spec],
            out_specs=any_spec,
            scratch_shapes=[pltpu.VMEM((blk * pitch, LANE), F32), pltpu.VMEM((blk * pitch, LANE), F32),
                            pltpu.VMEM((blk * pitch, LANE), F32), pltpu.VMEM((blk * pitch, LANE), F32),
                            pltpu.VMEM((2, d, de), F32),
                            pltpu.VMEM((2, d, de), F32),
                            pltpu.VMEM((2, de, d), F32),
                            pltpu.SemaphoreType.DMA((2,)),
                            pltpu.SemaphoreType.DMA((2,)),
                            pltpu.SemaphoreType.DMA((2, 3)),
                            pltpu.SemaphoreType.DMA],
        ),
        out_shape=jax.ShapeDtypeStruct((2 * (t + blk) * rt, LANE), F32),
        compiler_params=_cparams(("arbitrary",)),
        name="moe_experts",
    )(*plan, slot_w, h2, w_gate, w_up, w_down).reshape(2, (t + blk) * rt, LANE)


def _combine_kernel(x_ref, gt_ref, yp_ref, o_ref):
    tm = x_ref.shape[0]
    moe = jnp.concatenate(
        [yp_ref[0, pl.ds(j, tm, stride=ROW_TILES), :] + yp_ref[1, pl.ds(j, tm, stride=ROW_TILES), :]
         for j in range(ROW_TILES)], axis=1)
    o_ref[...] = x_ref[...] + gt_ref[...] * moe


def _combine(x1, gt, yp, tm):
    b, l, d = x1.shape
    tm = min(tm, l)
    nb = l // tm
    return pl.pallas_call(
        _combine_kernel,
        grid=(b, nb),
        in_specs=[pl.BlockSpec((None, tm, d), lambda bi, i: (bi, i, 0)),
                  pl.BlockSpec((None, 1, d), lambda bi, i: (bi, 0, 0)),
                  pl.BlockSpec((2, tm * ROW_TILES, LANE), lambda bi, i: (0, bi * nb + i, 0))],
        out_specs=pl.BlockSpec((None, tm, d), lambda bi, i: (bi, i, 0)),
        out_shape=jax.ShapeDtypeStruct((b, l, d), F32),
        compiler_params=_cparams(("parallel", "parallel")),
        name="moe_combine",
    )(x1, gt, yp)


def _dispatch_plan(eid, wts):
    blk = MOE_BLOCK
    n_assign = eid.shape[0]
    n_blocks = (n_assign + blk - 1) // blk + N_EXPERTS
    n_slots = n_blocks * blk
    order = jnp.argsort(eid).astype(jnp.int32)
    e_s = eid[order]
    counts = jnp.zeros((N_EXPERTS,), jnp.int32).at[eid].add(1)
    start = jnp.cumsum(counts) - counts
    padded = (counts + blk - 1) // blk * blk
    pend = jnp.cumsum(padded)
    pstart = pend - padded
    dest = pstart[e_s] + jnp.arange(n_assign, dtype=jnp.int32) - start[e_s]
    slot_asg = jnp.full((n_slots,), -1, jnp.int32).at[dest].set(order)
    filled = slot_asg >= 0
    safe = jnp.maximum(slot_asg, 0)
    slot_tok = jnp.where(filled, safe // 2, 0)
    n_tok = n_assign // 2
    spare = n_tok + jnp.arange(n_slots, dtype=jnp.int32) % blk
    slot_dst = jnp.where(filled, (safe % 2) * (n_tok + blk) + safe // 2, spare)
    slot_w = jnp.where(filled, wts[safe], 0.0).astype(F32)
    bstart = jnp.arange(n_blocks, dtype=jnp.int32) * blk
    block_e = jnp.minimum(jnp.searchsorted(pend, bstart, side='right'), N_EXPERTS - 1).astype(jnp.int32)
    block_nv = jnp.sum(filled.reshape(n_blocks, blk), axis=1, dtype=jnp.int32)
    first = jnp.concatenate([jnp.ones((1,), jnp.int32),
                             (block_e[1:] != block_e[:-1]).astype(jnp.int32)])
    w_slot = (jnp.cumsum(first) - 1) % 2
    e_ids = jnp.arange(N_EXPERTS, dtype=jnp.int32)
    nonempty = jnp.where(counts > 0, e_ids, N_EXPERTS)
    later = jnp.flip(lax.cummin(jnp.flip(nonempty)))
    next_e = jnp.concatenate([later[1:], jnp.full((1,), N_EXPERTS, jnp.int32)])
    next_e = jnp.where(next_e >= N_EXPERTS, -1, next_e)[block_e]
    plan = (block_e, block_nv, first, next_e, w_slot.astype(jnp.int32), slot_tok, slot_dst)
    return plan, slot_w.reshape(n_slots, 1)


def _rope_tables(length, with_rope):
    half = MLA_ROPE // 4
    if with_rope:
        rows = length // GRID_W
        row = jnp.repeat(jnp.arange(rows), GRID_W).astype(F32)
        col = jnp.tile(jnp.arange(GRID_W), rows).astype(F32)
        inv = 1.0 / (ROPE_THETA ** (jnp.arange(half, dtype=F32) / half))
        ang_r = row[:, None] * inv
        ang_c = col[:, None] * inv
        cos = jnp.concatenate([jnp.cos(ang_r), jnp.cos(ang_r), jnp.cos(ang_c), jnp.cos(ang_c)], axis=1)
        sin = jnp.concatenate([-jnp.sin(ang_r), jnp.sin(ang_r), -jnp.sin(ang_c), jnp.sin(ang_c)], axis=1)
    else:
        cos = jnp.ones((length, MLA_ROPE), F32)
        sin = jnp.zeros((length, MLA_ROPE), F32)
    pad = ((0, 0), (0, LANE - MLA_ROPE))
    return jnp.pad(cos, pad), jnp.pad(sin, pad)


def _split_w_in(w_in):
    sizes = (GLA_QK_W, GLA_QK_W, GLA_V_W, GLA_V_W, GLA_LOWRANK, GLA_LOWRANK,
             MLA_Q_RANK, MLA_KV_RANK, MLA_ROPE, D_MODEL, D_MODEL)
    parts, off = [], 0
    for s in sizes:
        parts.append(w_in[:, off:off + s])
        off += s
    return parts


def kernel(x, c, ctx, c_ctx, w_mod, b_mod, norm1_g, norm2_g, w_in, w_decay_f, b_decay_f, w_decay_b, b_decay_b, gla_norm_g, q_a_norm_g, w_uq, kv_a_norm_g, w_ukv, q_norm_g, k_norm_g, w_o_gla, w_o_mla, w_out, w_router_group, b_router_group, w_router_expert, b_router_expert, w_exp_gate, w_exp_up, w_exp_down):
    assert w_mod.shape[0] == 1, "single-layer block"
    b, l, d = x.shape
    lc = ctx.shape[1]
    t = b * l

    c_rows = jnp.concatenate([c, c_ctx[None, :], jnp.zeros((8 - b - 1, d), F32)], axis=0)
    mod = _modulation(c_rows, w_mod[0], b_mod[0])
    sh1, sc1, gt1, sh2, sc2, gt2 = [mod[:b, i * d:(i + 1) * d].reshape(b, 1, d) for i in range(6)]
    sh1c = jnp.broadcast_to(mod[b:b + 1, 0:d].reshape(1, 1, d), (b, 1, d))
    sc1c = jnp.broadcast_to(mod[b:b + 1, d:2 * d].reshape(1, 1, d), (b, 1, d))

    wq_, wk_, wv_, wr_, waf, wab, wcq, wckv, wkr, wga, wgb = _split_w_in(w_in[0])
    small = jnp.concatenate(
        [wkr, waf, wab, jnp.zeros((d, SMALL_W - MLA_ROPE - 2 * GLA_LOWRANK), F32)], axis=1)
    w_lat = jnp.concatenate([wq_, wk_, wv_, wr_, wga, wgb, wcq, wckv, small], axis=1).astype(BF16)
    w_ctx = jnp.concatenate([wk_, wv_, wckv, small], axis=1).astype(BF16)
    g1 = norm1_g[0].reshape(1, d)
    p = _in_projection(x, g1, sc1, sh1, w_lat, tm=1024)
    pc = _in_projection(ctx, g1, sc1c, sh1c, w_ctx, tm=lc)

    def decay_w(w, off):
        return jnp.zeros((LANE, GLA_QK_W), F32).at[off:off + GLA_LOWRANK].set(w).astype(BF16)

    y_gla = _gla(p, pc,
                 decay_w(w_decay_f[0], SM_AF), b_decay_f[0].reshape(1, -1),
                 decay_w(w_decay_b[0], SM_AB), b_decay_b[0].reshape(1, -1),
                 gla_norm_g[0].reshape(1, -1))

    wq = jnp.pad(w_uq[0].reshape(MLA_Q_RANK, MLA_HEADS, MLA_QK_DIM),
                 ((0, 0), (0, 0), (0, MLA_QK_PAD - MLA_QK_DIM))).reshape(MLA_Q_RANK, -1).astype(BF16)
    wkv = w_ukv[0].astype(BF16)
    qn_g = jnp.pad(q_norm_g[0], (0, MLA_QK_PAD - MLA_QK_DIM)).reshape(1, -1)
    kn_nope = k_norm_g[0][:MLA_NOPE].reshape(1, -1)
    kn_rope = jnp.pad(k_norm_g[0][MLA_NOPE:], (0, LANE - MLA_ROPE)).reshape(1, -1)
    cos, sin = _rope_tables(l, True)
    cos_c, sin_c = _rope_tables(lc, False)
    q_m = _mla_queries(p, q_a_norm_g[0].reshape(1, -1), wq, qn_g, cos, sin, tm=512)
    kva = kv_a_norm_g[0].reshape(1, -1)
    k_m, v_m = _mla_keys_values(p, pc, kva, wkv, kn_nope, kn_rope,
                                jnp.concatenate([cos, cos_c]), jnp.concatenate([sin, sin_c]))
    y_mla = _attention(q_m, k_m, v_m, tq=256)

    y = _branch_merge(y_gla, y_mla, w_o_gla[0].astype(BF16), w_o_mla[0].astype(BF16), p, tm=1024, tn=512)
    w_router = jnp.concatenate(
        [w_router_group[0], w_router_expert[0],
         jnp.zeros((d, ROUTER_W - N_GROUPS - N_EXPERTS), F32)], axis=1).astype(BF16)
    b_router = jnp.concatenate(
        [b_router_group[0], b_router_expert[0],
         jnp.zeros((ROUTER_W - N_GROUPS - N_EXPERTS,), F32)]).reshape(1, -1)
    x1, h2, logits = _out_projection(y, w_out[0].astype(BF16), x, gt1, norm2_g[0].reshape(1, d),
                                     sc2, sh2, w_router, b_router, tm=256)

    e_pick, w_pick = _route(logits.reshape(t, ROUTER_W), tm=1024)
    eid = e_pick[:, :2].reshape(-1)
    wts = w_pick[:, :2].reshape(-1)
    plan, slot_w = _dispatch_plan(eid, wts)
    yp = _experts(plan, slot_w, h2.reshape(t * ROW_TILES, LANE),
                  w_exp_gate[0], w_exp_up[0], w_exp_down[0])
    return _combine(x1, gt2, yp, tm=512)
```

```python
import functools
import math

import jax
import jax.numpy as jnp
from jax import lax
from jax.experimental import pallas as pl
from jax.experimental.pallas import tpu as pltpu

F32 = jnp.float32
BF16 = jnp.bfloat16

D_MODEL = 2048
GRID_W = 64
EPS = 1e-6

GLA_HEADS = 4
GLA_DK = 256
GLA_DV = 512
GLA_LOWRANK = 16
GLA_TAU = 16.0
GLA_CHUNK = 64
GLA_GROUP = 8
GLA_QK_W = GLA_HEADS * GLA_DK
GLA_V_W = GLA_HEADS * GLA_DV

MLA_HEADS = 16
MLA_Q_RANK = 512
MLA_KV_RANK = 512
MLA_NOPE = 128
MLA_ROPE = 64
MLA_V = 128
MLA_QK_DIM = MLA_NOPE + MLA_ROPE
MLA_QK_PAD = 256
ROPE_THETA = 10000.0
LOG2E = math.log2(math.e)

N_GROUPS = 8
EXP_PER_GROUP = 8
N_EXPERTS = N_GROUPS * EXP_PER_GROUP
D_EXPERT = 512
ROUTER_W = 128
MOE_BLOCK = 256

LANE = 128
ROW_TILES = D_MODEL // LANE
ROW_PITCH = 20
VMEM_LIMIT = 56 * 1024 * 1024

COL_Q = 0
COL_K = COL_Q + GLA_QK_W
COL_V = COL_K + GLA_QK_W
COL_R = COL_V + GLA_V_W
COL_GA = COL_R + GLA_V_W
COL_GB = COL_GA + D_MODEL
COL_CQ = COL_GB + D_MODEL
COL_CKV = COL_CQ + MLA_Q_RANK
COL_SMALL = COL_CKV + MLA_KV_RANK
SMALL_W = 256
NP_LAT = COL_SMALL + SMALL_W
SM_AF = MLA_ROPE
SM_AB = MLA_ROPE + GLA_LOWRANK
CCOL_K = 0
CCOL_V = CCOL_K + GLA_QK_W
CCOL_CKV = CCOL_V + GLA_V_W
CCOL_SMALL = CCOL_CKV + MLA_KV_RANK
NP_CTX = CCOL_SMALL + SMALL_W
INPROJ_TN = 1280


def _cparams(sem):
    return pltpu.CompilerParams(dimension_semantics=sem, vmem_limit_bytes=VMEM_LIMIT)


def _silu(x):
    return x * jax.nn.sigmoid(x)


def _mod_kernel(c_ref, w_ref, b_ref, o_ref):
    a = _silu(c_ref[...]).astype(BF16)
    o_ref[...] = jnp.dot(a, w_ref[...].astype(BF16), preferred_element_type=F32) + b_ref[...]


def _modulation(c_rows, w_mod, b_mod):
    m, d = c_rows.shape
    n = w_mod.shape[1]
    tn = 1024
    return pl.pallas_call(
        _mod_kernel,
        grid=(n // tn,),
        in_specs=[pl.BlockSpec((m, d), lambda j: (0, 0)),
                  pl.BlockSpec((d, tn), lambda j: (0, j)),
                  pl.BlockSpec((1, tn), lambda j: (0, j))],
        out_specs=pl.BlockSpec((m, tn), lambda j: (0, j)),
        out_shape=jax.ShapeDtypeStruct((m, n), F32),
        compiler_params=_cparams(("parallel",)),
        name="modulation",
    )(c_rows, w_mod, b_mod.reshape(1, n))


def _inproj_kernel(x_ref, g_ref, sc_ref, sh_ref, w_ref, o_ref, h_ref):
    @pl.when(pl.program_id(2) == 0)
    def _():
        x = x_ref[...]
        ms = jnp.mean(x * x, axis=-1, keepdims=True)
        y = x * lax.rsqrt(ms + EPS) * g_ref[...]
        h_ref[...] = (y * (1.0 + sc_ref[...]) + sh_ref[...]).astype(BF16)

    o_ref[...] = jnp.dot(h_ref[...], w_ref[...], preferred_element_type=F32).astype(o_ref.dtype)


def _in_projection(x, g, sc, sh, w, tm):
    b, l, d = x.shape
    tm = min(tm, l)
    n = w.shape[1]
    tn = INPROJ_TN
    return pl.pallas_call(
        _inproj_kernel,
        grid=(b, l // tm, n // tn),
        in_specs=[pl.BlockSpec((None, tm, d), lambda bi, i, j: (bi, i, 0)),
                  pl.BlockSpec((1, d), lambda bi, i, j: (0, 0)),
                  pl.BlockSpec((None, 1, d), lambda bi, i, j: (bi, 0, 0)),
                  pl.BlockSpec((None, 1, d), lambda bi, i, j: (bi, 0, 0)),
                  pl.BlockSpec((d, tn), lambda bi, i, j: (0, j))],
        out_specs=pl.BlockSpec((None, tm, tn), lambda bi, i, j: (bi, i, j)),
        out_shape=jax.ShapeDtypeStruct((b, l, n), BF16),
        scratch_shapes=[pltpu.VMEM((tm, d), BF16)],
        compiler_params=_cparams(("parallel", "parallel", "arbitrary")),
        name="in_projection",
    )(x, g, sc, sh, w)


def _gla_kernel(q_ref, k_ref, v_ref, r_ref, sm_ref, kc_ref, vc_ref, smc_ref,
                wf_ref, bf_ref, wb_ref, bb_ref, gn_ref, y_ref, o_ref, sf_ref, sb_ref):
    c = GLA_CHUNK
    n_lat = q_ref.shape[0] // c
    n_ctx = kc_ref.shape[0] // c
    row = lax.broadcasted_iota(jnp.int32, (c, c), 0)
    col = lax.broadcasted_iota(jnp.int32, (c, c), 1)
    lower = row >= col
    upper = row <= col
    nt = (((1,), (1,)), ((), ()))
    tn = (((0,), (0,)), ((), ()))

    def cum_decay(sm, w_ref, b_ref, tri, n_chunks):
        z = jnp.dot(sm, w_ref[...], preferred_element_type=F32) + b_ref[...]
        g = (jnp.minimum(z, 0.0) - jnp.log(1.0 + jnp.exp(-jnp.abs(z)))) * (1.0 / GLA_TAU)
        g_hi = g.astype(BF16)
        g_lo = (g - g_hi.astype(F32)).astype(BF16)
        t = jnp.where(tri, 1.0, 0.0).astype(BF16)
        return [jnp.dot(t, g_hi[i * c:(i + 1) * c], preferred_element_type=F32)
                + jnp.dot(t, g_lo[i * c:(i + 1) * c], preferred_element_type=F32)
                for i in range(n_chunks)]

    fwd = (wf_ref, bf_ref, lower, c - 1, sf_ref)
    bwd = (wb_ref, bb_ref, upper, 0, sb_ref)

    def group(r0, n_chunks, direction, sm_r, k_r, v_r, q_r):
        w_ref, b_ref, tri, end_row, st_ref = direction
        rows = n_chunks * c
        k_all = k_r[pl.ds(r0, rows), :].astype(F32)
        v_all = v_r[pl.ds(r0, rows), :]
        q_all = None if q_r is None else q_r[pl.ds(r0, rows), :].astype(F32) * (GLA_DK ** -0.5)
        bc_all = cum_decay(sm_r[pl.ds(r0, rows), :], w_ref, b_ref, tri, n_chunks)
        local = []
        for i in range(n_chunks):
            sl = slice(i * c, (i + 1) * c)
            bc = bc_all[i]
            bend = bc[end_row:end_row + 1, :]
            k, v = k_all[sl], v_all[sl]
            ke = (k * jnp.exp(bend - bc)).astype(BF16)
            upd = lax.dot_general(v, ke, tn, preferred_element_type=F32)
            qd = o_intra = None
            if q_all is not None:
                qd = (q_all[sl] * jnp.exp(bc)).astype(BF16)
                ki = (k * jnp.exp(-bc)).astype(BF16)
                att = lax.dot_general(qd, ki, nt, preferred_element_type=F32)
                att = jnp.where(tri, att, 0.0).astype(BF16)
                o_intra = jnp.dot(att, v, preferred_element_type=F32)
            local.append((jnp.exp(bend), upd, qd, o_intra))
        st = st_ref[...]
        outs = [None] * n_chunks
        for i in (range(n_chunks) if end_row else reversed(range(n_chunks))):
            decay, upd, qd, o_intra = local[i]
            if q_all is not None:
                outs[i] = o_intra + lax.dot_general(qd, st.astype(BF16), nt, preferred_element_type=F32)
            st = st * decay + upd
        st_ref[...] = st
        return None if q_all is None else jnp.concatenate(outs, axis=0)

    def readout(r0, o):
        ms = jnp.mean(o * o, axis=-1, keepdims=True)
        yn = o * lax.rsqrt(ms + EPS) * gn_ref[...]
        r = r_ref[pl.ds(r0, o.shape[0]), :].astype(F32)
        y_ref[pl.ds(r0, o.shape[0]), :] = (yn * _silu(r)).astype(y_ref.dtype)

    sf_ref[...] = jnp.zeros_like(sf_ref)
    sb_ref[...] = jnp.zeros_like(sb_ref)
    group(0, n_ctx, fwd, smc_ref, kc_ref, vc_ref, None)
    group(0, n_ctx, bwd, smc_ref, kc_ref, vc_ref, None)

    g_lat = min(GLA_GROUP, n_lat // 2)
    assert n_lat % (2 * g_lat) == 0
    n_groups = n_lat // g_lat
    rows = g_lat * c

    def scan(i, direction):
        r0 = pl.multiple_of(i * rows, rows)
        return r0, group(r0, g_lat, direction, sm_ref, k_ref, v_ref, q_ref)

    def first_half(i, carry):
        for r0, o in (scan(i, fwd), scan(n_groups - 1 - i, bwd)):
            o_ref[pl.ds(r0, rows), :] = o
        return carry

    lax.fori_loop(0, n_groups // 2, first_half, 0)

    def second_half(i, carry):
        for r0, o in (scan(i, fwd), scan(n_groups - 1 - i, bwd)):
            readout(r0, o_ref[pl.ds(r0, rows), :] + o)
        return carry

    lax.fori_loop(n_groups // 2, n_groups, second_half, 0)


def _gla(p, pc, wf, bf, wb, bb, gn):
    b, l, _ = p.shape
    lc = pc.shape[1]
    dk, dv = GLA_DK, GLA_DV
    hmap = lambda off: (lambda bi, h: (bi, 0, off + h))
    wmap = lambda bi, h: (0, h)
    return pl.pallas_call(
        _gla_kernel,
        grid=(b, GLA_HEADS),
        in_specs=[pl.BlockSpec((None, l, dk), hmap(COL_Q // dk)),
                  pl.BlockSpec((None, l, dk), hmap(COL_K // dk)),
                  pl.BlockSpec((None, l, dv), hmap(COL_V // dv)),
                  pl.BlockSpec((None, l, dv), hmap(COL_R // dv)),
                  pl.BlockSpec((None, l, LANE), lambda bi, h: (bi, 0, COL_SMALL // LANE)),
                  pl.BlockSpec((None, lc, dk), hmap(CCOL_K // dk)),
                  pl.BlockSpec((None, lc, dv), hmap(CCOL_V // dv)),
                  pl.BlockSpec((None, lc, LANE), lambda bi, h: (bi, 0, CCOL_SMALL // LANE)),
                  pl.BlockSpec((LANE, dk), wmap),
                  pl.BlockSpec((1, dk), wmap),
                  pl.BlockSpec((LANE, dk), wmap),
                  pl.BlockSpec((1, dk), wmap),
                  pl.BlockSpec((1, dv), lambda bi, h: (0, 0))],
        out_specs=pl.BlockSpec((None, l, dv), lambda bi, h: (bi, 0, h)),
        out_shape=jax.ShapeDtypeStruct((b, l, GLA_V_W), BF16),
        scratch_shapes=[pltpu.VMEM((l, dv), F32), pltpu.VMEM((dv, dk), F32), pltpu.VMEM((dv, dk), F32)],
        compiler_params=_cparams(("parallel", "parallel")),
        name="gla",
    )(p, p, p, p, p, pc, pc, pc, wf, bf, wb, bb, gn)


def _rope_partner(t):
    lane = lax.broadcasted_iota(jnp.int32, t.shape, 1)
    first = (lane % 32) < 16
    return jnp.where(first, pltpu.roll(t, LANE - 16, 1), pltpu.roll(t, 16, 1))


def _mla_q_kernel(cq_ref, ga_ref, w_ref, gn_ref, cos_ref, sin_ref, o_ref):
    cq = cq_ref[...].astype(F32)
    cn = (cq * lax.rsqrt(jnp.mean(cq * cq, axis=-1, keepdims=True) + EPS) * ga_ref[...]).astype(BF16)
    cos = cos_ref[...]
    sin = sin_ref[...]
    scale = MLA_QK_DIM ** -0.5 * LOG2E
    for h in range(MLA_HEADS):
        qh = jnp.dot(cn, w_ref[:, h * MLA_QK_PAD:(h + 1) * MLA_QK_PAD], preferred_element_type=F32)
        ms = jnp.sum(qh * qh, axis=-1, keepdims=True) * (1.0 / MLA_QK_DIM)
        qn = qh * lax.rsqrt(ms + EPS) * gn_ref[...]
        t = qn[:, MLA_NOPE:]
        rot = t * cos + _rope_partner(t) * sin
        o_ref[h, :, :MLA_NOPE] = (qn[:, :MLA_NOPE] * scale).astype(o_ref.dtype)
        o_ref[h, :, MLA_NOPE:] = (rot * scale).astype(o_ref.dtype)


def _mla_queries(p, ga, wq, gn, cos, sin, tm):
    b, l, _ = p.shape
    tm = min(tm, l)
    return pl.pallas_call(
        _mla_q_kernel,
        grid=(b, l // tm),
        in_specs=[pl.BlockSpec((None, tm, MLA_Q_RANK), lambda bi, i: (bi, i, COL_CQ // MLA_Q_RANK)),
                  pl.BlockSpec((1, MLA_Q_RANK), lambda bi, i: (0, 0)),
                  pl.BlockSpec(wq.shape, lambda bi, i: (0, 0)),
                  pl.BlockSpec((1, MLA_QK_PAD), lambda bi, i: (0, 0)),
                  pl.BlockSpec((tm, LANE), lambda bi, i: (i, 0)),
                  pl.BlockSpec((tm, LANE), lambda bi, i: (i, 0))],
        out_specs=pl.BlockSpec((None, MLA_HEADS, tm, MLA_QK_PAD), lambda bi, i: (bi, 0, i, 0)),
        out_shape=jax.ShapeDtypeStruct((b, MLA_HEADS, l, MLA_QK_PAD), BF16),
        compiler_params=_cparams(("parallel", "parallel")),
        name="mla_queries",
    )(p, ga, wq, gn, cos, sin)


def _mla_kv_kernel(ckv_ref, sm_ref, ckvc_ref, smc_ref, ga_ref, w_ref, gk_ref, gr_ref,
                   cos_ref, sin_ref, k_ref, v_ref, *, n_lat):
    is_ctx = pl.program_id(1) >= n_lat
    ckv = jnp.where(is_ctx, ckvc_ref[...], ckv_ref[...]).astype(F32)
    cn = (ckv * lax.rsqrt(jnp.mean(ckv * ckv, axis=-1, keepdims=True) + EPS) * ga_ref[...]).astype(BF16)
    sm = jnp.where(is_ctx, smc_ref[...], sm_ref[...]).astype(F32)
    lane = lax.broadcasted_iota(jnp.int32, sm.shape, 1)
    kr = jnp.where(lane < MLA_ROPE, sm, 0.0)
    ss_r = jnp.sum(kr * kr, axis=-1, keepdims=True)
    krg = kr * gr_ref[...]
    rot = krg * cos_ref[...] + _rope_partner(krg) * sin_ref[...]
    hw = MLA_NOPE + MLA_V
    ones_col = jnp.where(lane == 0, 1.0, 0.0).astype(v_ref.dtype)
    for h in range(MLA_HEADS):
        kvh = jnp.dot(cn, w_ref[:, h * hw:(h + 1) * hw], preferred_element_type=F32)
        kn = kvh[:, :MLA_NOPE]
        ms = (jnp.sum(kn * kn, axis=-1, keepdims=True) + ss_r) * (1.0 / MLA_QK_DIM)
        rs = lax.rsqrt(ms + EPS)
        k_ref[h, :, :MLA_NOPE] = (kn * rs * gk_ref[...]).astype(k_ref.dtype)
        k_ref[h, :, MLA_NOPE:] = (rot * rs).astype(k_ref.dtype)
        v_ref[h, :, :MLA_V] = kvh[:, MLA_NOPE:].astype(v_ref.dtype)
        v_ref[h, :, MLA_V:] = ones_col


def _mla_keys_values(p, pc, ga, wkv, gk, gr, cos, sin):
    b, l, _ = p.shape
    lc = pc.shape[1]
    tm = min(lc, l)
    assert l % tm == 0 and lc % tm == 0
    n_lat, n_ctx = l // tm, lc // tm
    lat = lambda blk: (lambda bi, i: (bi, jnp.minimum(i, n_lat - 1), blk))
    ctx = lambda blk: (lambda bi, i: (bi, jnp.maximum(i - n_lat, 0), blk))
    const = lambda bi, i: (0, 0)
    out_spec = pl.BlockSpec((None, MLA_HEADS, tm, MLA_QK_PAD), lambda bi, i: (bi, 0, i, 0))
    shape = jax.ShapeDtypeStruct((b, MLA_HEADS, l + lc, MLA_QK_PAD), BF16)
    return pl.pallas_call(
        functools.partial(_mla_kv_kernel, n_lat=n_lat),
        grid=(b, n_lat + n_ctx),
        in_specs=[pl.BlockSpec((None, tm, MLA_KV_RANK), lat(COL_CKV // MLA_KV_RANK)),
                  pl.BlockSpec((None, tm, LANE), lat(COL_SMALL // LANE)),
                  pl.BlockSpec((None, tm, MLA_KV_RANK), ctx(CCOL_CKV // MLA_KV_RANK)),
                  pl.BlockSpec((None, tm, LANE), ctx(CCOL_SMALL // LANE)),
                  pl.BlockSpec((1, MLA_KV_RANK), const),
                  pl.BlockSpec(wkv.shape, const),
                  pl.BlockSpec((1, LANE), const),
                  pl.BlockSpec((1, LANE), const),
                  pl.BlockSpec((tm, LANE), lambda bi, i: (i, 0)),
                  pl.BlockSpec((tm, LANE), lambda bi, i: (i, 0))],
        out_specs=[out_spec, out_spec],
        out_shape=[shape, shape],
        compiler_params=_cparams(("parallel", "arbitrary")),
        name="mla_keys_values",
    )(p, p, pc, pc, ga, wkv, gk, gr, cos, sin)


def _attn_kernel(q_ref, k_ref, v_ref, o_ref, s0_ref, s1_ref, m0_ref, m1_ref, *, tq):
    nt = (((1,), (1,)), ((), ()))
    n = q_ref.shape[0] // tq
    bufs = ((s0_ref, m0_ref), (s1_ref, m1_ref))

    def scores(i, slot):
        s_ref, m_ref = bufs[slot]
        r0 = pl.multiple_of(i * tq, tq)
        s = lax.dot_general(q_ref[pl.ds(r0, tq), :], k_ref[...], nt, preferred_element_type=F32)
        s_ref[...] = s
        m_ref[...] = jnp.max(s, axis=-1, keepdims=True)

    def outputs(i, slot):
        s_ref, m_ref = bufs[slot]
        r0 = pl.multiple_of(i * tq, tq)
        p = jnp.exp2(s_ref[...] - m_ref[...]).astype(BF16)
        o = jnp.dot(p, v_ref[...], preferred_element_type=F32)
        o_ref[pl.ds(r0, tq), :] = (o[:, :MLA_V] / o[:, MLA_V:MLA_V + 1]).astype(o_ref.dtype)

    scores(0, 0)
    if n % 2 == 0:
        def body(j, carry):
            scores(2 * j + 1, 1)
            outputs(2 * j, 0)
            scores(2 * j + 2, 0)
            outputs(2 * j + 1, 1)
            return carry

        lax.fori_loop(0, n // 2 - 1, body, 0)
        scores(n - 1, 1)
        outputs(n - 2, 0)
        outputs(n - 1, 1)
    else:
        assert n == 1
        outputs(0, 0)


def _attention(q, k, v, tq):
    b, h, l, dq = q.shape
    tq = min(tq, l)
    lk = k.shape[2]
    bh = lambda bi, hi: (bi, hi, 0, 0)
    return pl.pallas_call(
        functools.partial(_attn_kernel, tq=tq),
        grid=(b, h),
        in_specs=[pl.BlockSpec((None, None, l, dq), bh),
                  pl.BlockSpec((None, None, lk, dq), bh),
                  pl.BlockSpec((None, None, lk, dq), bh)],
        out_specs=pl.BlockSpec((None, l, MLA_V), lambda bi, hi: (bi, 0, hi)),
        out_shape=jax.ShapeDtypeStruct((b, l, h * MLA_V), BF16),
        scratch_shapes=[pltpu.VMEM((tq, lk), F32), pltpu.VMEM((tq, lk), F32),
                        pltpu.VMEM((tq, 1), F32), pltpu.VMEM((tq, 1), F32)],
        compiler_params=_cparams(("parallel", "parallel")),
        name="mla_attention",
    )(q, k, v)


def _merge_kernel(yg_ref, ym_ref, wg_ref, wm_ref, ga_ref, gb_ref, o_ref):
    a = jnp.dot(yg_ref[...], wg_ref[...], preferred_element_type=F32)
    m = jnp.dot(ym_ref[...], wm_ref[...], preferred_element_type=F32)
    y = (jax.nn.sigmoid(ga_ref[...].astype(F32)) * a
         + jax.nn.sigmoid(gb_ref[...].astype(F32)) * m)
    o_ref[...] = y.astype(o_ref.dtype)


def _branch_merge(yg, ym, wg, wm, p, tm, tn):
    b, l, d = yg.shape
    tm = min(tm, l)
    n = wg.shape[1]
    return pl.pallas_call(
        _merge_kernel,
        grid=(b, l // tm, n // tn),
        in_specs=[pl.BlockSpec((None, tm, d), lambda bi, i, j: (bi, i, 0)),
                  pl.BlockSpec((None, tm, d), lambda bi, i, j: (bi, i, 0)),
                  pl.BlockSpec((d, tn), lambda bi, i, j: (0, j)),
                  pl.BlockSpec((d, tn), lambda bi, i, j: (0, j)),
                  pl.BlockSpec((None, tm, tn), lambda bi, i, j: (bi, i, COL_GA // tn + j)),
                  pl.BlockSpec((None, tm, tn), lambda bi, i, j: (bi, i, COL_GB // tn + j))],
        out_specs=pl.BlockSpec((None, tm, tn), lambda bi, i, j: (bi, i, j)),
        out_shape=jax.ShapeDtypeStruct((b, l, n), BF16),
        compiler_params=_cparams(("parallel", "parallel", "arbitrary")),
        name="branch_merge",
    )(yg, ym, wg, wm, p, p)


def _outproj_kernel(y_ref, w_ref, x_ref, gt_ref, g2_ref, sc_ref, sh_ref, wr_ref, br_ref,
                    x1_ref, h2_ref, lg_ref):
    x1 = x_ref[...] + gt_ref[...] * jnp.dot(y_ref[...], w_ref[...], preferred_element_type=F32)
    x1_ref[...] = x1
    ms = jnp.mean(x1 * x1, axis=-1, keepdims=True)
    h2 = x1 * lax.rsqrt(ms + EPS) * g2_ref[...] * (1.0 + sc_ref[...]) + sh_ref[...]
    tm = x1.shape[0]
    for j in range(ROW_TILES):
        h2_ref[pl.ds(j, tm, stride=ROW_TILES), :] = h2[:, j * LANE:(j + 1) * LANE]
    lg_ref[...] = jnp.dot(h2.astype(BF16), wr_ref[...], preferred_element_type=F32) + br_ref[...]


def _out_projection(y, w, x, gt, g2, sc, sh, wr, br, tm):
    b, l, d = x.shape
    tm = min(tm, l)
    row = lambda bi, i: (bi, i, 0)
    per_b = lambda bi, i: (bi, 0, 0)
    const = lambda bi, i: (0, 0)
    return pl.pallas_call(
        _outproj_kernel,
        grid=(b, l // tm),
        in_specs=[pl.BlockSpec((None, tm, d), row),
                  pl.BlockSpec((d, d), const),
                  pl.BlockSpec((None, tm, d), row),
                  pl.BlockSpec((None, 1, d), per_b),
                  pl.BlockSpec((1, d), const),
                  pl.BlockSpec((None, 1, d), per_b),
                  pl.BlockSpec((None, 1, d), per_b),
                  pl.BlockSpec((d, ROUTER_W), const),
                  pl.BlockSpec((1, ROUTER_W), const)],
        out_specs=[pl.BlockSpec((None, tm, d), row),
                   pl.BlockSpec((None, tm * ROW_TILES, LANE), row),
                   pl.BlockSpec((None, tm, ROUTER_W), row)],
        out_shape=[jax.ShapeDtypeStruct((b, l, d), F32),
                   jax.ShapeDtypeStruct((b, l * ROW_TILES, LANE), F32),
                   jax.ShapeDtypeStruct((b, l, ROUTER_W), F32)],
        compiler_params=_cparams(("parallel", "parallel")),
        name="out_projection",
    )(y, w, x, gt, g2, sc, sh, wr, br)


def _route_kernel(lg_ref, e_ref, w_ref):
    lg = lg_ref[...]
    lane = lax.broadcasted_iota(jnp.int32, lg.shape, 1)
    neg = jnp.float32(-jnp.inf)
    big = jnp.int32(1 << 20)
    is_g = lane < N_GROUPS
    gl = jnp.where(is_g, lg, neg)
    gm = jnp.max(gl, axis=-1, keepdims=True)
    gidx = jnp.min(jnp.where(gl == gm, lane, big), axis=-1, keepdims=True)
    gsum = jnp.sum(jnp.where(is_g, jnp.exp(gl - gm), 0.0), axis=-1, keepdims=True)
    p_g = 1.0 / gsum
    g_lo = N_GROUPS + gidx * EXP_PER_GROUP
    in_grp = (lane >= g_lo) & (lane < g_lo + EXP_PER_GROUP)
    el = jnp.where(in_grp, lg, neg)
    em = jnp.max(el, axis=-1, keepdims=True)
    z = jnp.sum(jnp.where(in_grp, jnp.exp(el - em), 0.0), axis=-1, keepdims=True)
    i1 = jnp.min(jnp.where(el == em, lane, big), axis=-1, keepdims=True)
    el2 = jnp.where(lane == i1, neg, el)
    em2 = jnp.max(el2, axis=-1, keepdims=True)
    i2 = jnp.min(jnp.where(el2 == em2, lane, big), axis=-1, keepdims=True)
    p1 = 1.0 / z
    p2 = jnp.exp(em2 - em) / z
    tot = p1 + p2
    e_ref[...] = jnp.where(lane == 0, i1 - N_GROUPS, jnp.where(lane == 1, i2 - N_GROUPS, 0))
    w_ref[...] = jnp.where(lane == 0, p_g * p1 / tot, jnp.where(lane == 1, p_g * p2 / tot, 0.0))


def _route(logits, tm):
    t = logits.shape[0]
    tm = min(tm, t)
    spec = pl.BlockSpec((tm, ROUTER_W), lambda i: (i, 0))
    return pl.pallas_call(
        _route_kernel,
        grid=(t // tm,),
        in_specs=[spec],
        out_specs=[spec, spec],
        out_shape=[jax.ShapeDtypeStruct((t, ROUTER_W), jnp.int32),
                   jax.ShapeDtypeStruct((t, ROUTER_W), F32)],
        compiler_params=_cparams(("parallel",)),
        name="moe_route",
    )(logits)


def _expert_kernel(be_ref, nv_ref, first_ref, nxe_ref, ws_ref, j0_ref, tok_ref, dst_ref,
                   h_hbm, wg_hbm, wu_hbm, wd_hbm, yp_hbm,
                   xbuf0, xbuf1, ybuf0, ybuf1, wgv, wuv, wdv, sem_in, sem_out, sem_w, sem_init):
    i = pl.program_id(0)
    n = pl.num_programs(0)
    blk = MOE_BLOCK
    last = tok_ref.shape[0] - 1
    rt, pitch = ROW_TILES, ROW_PITCH
    plane = yp_hbm.shape[0] // (2 * rt)
    t_rows = plane - blk
    xbufs = (xbuf0, xbuf1)
    ybufs = (ybuf0, ybuf1)

    def weight_copies(e, s):
        return (pltpu.make_async_copy(wg_hbm.at[e], wgv.at[s], sem_w.at[s, 0]),
                pltpu.make_async_copy(wu_hbm.at[e], wuv.at[s], sem_w.at[s, 1]),
                pltpu.make_async_copy(wd_hbm.at[e], wdv.at[s], sem_w.at[s, 2]))

    def gather_start(b, s):
        base = j0_ref[b]
        for r in range(blk):
            row0 = pl.multiple_of(tok_ref[jnp.minimum(base + r, last)] * rt, rt)
            pltpu.make_async_copy(h_hbm.at[pl.ds(row0, rt)],
                                  xbufs[s].at[pl.ds(r * pitch, rt)], sem_in.at[s]).start()

    def gather_wait(s):
        pltpu.make_async_copy(h_hbm.at[pl.ds(0, blk * rt)], xbufs[s].at[pl.ds(0, blk * rt)],
                              sem_in.at[s]).wait()

    def scatter_start(b, s):
        base = j0_ref[b]
        filled = nv_ref[b]
        for r in range(blk):
            dst = jnp.where(r < filled, dst_ref[jnp.minimum(base + r, last)], t_rows + r)
            row0 = pl.multiple_of(dst * rt, rt)
            pltpu.make_async_copy(ybufs[s].at[pl.ds(r * pitch, rt)],
                                  yp_hbm.at[pl.ds(row0, rt)], sem_out.at[s]).start()

    def scatter_wait(s):
        pltpu.make_async_copy(ybufs[s].at[pl.ds(0, blk * rt)], yp_hbm.at[pl.ds(0, blk * rt)],
                              sem_out.at[s]).wait()

    @pl.when(i == 0)
    def _():
        for cp in weight_copies(be_ref[0], 0):
            cp.start()
        gather_start(0, 0)
        ybuf1[...] = jnp.zeros_like(ybuf1)
        zeros = ybuf1.at[pl.ds(0, blk * rt)]
        pltpu.make_async_copy(zeros, yp_hbm.at[pl.ds(t_rows * rt, blk * rt)], sem_out.at[0]).start()
        spare = pltpu.make_async_copy(zeros, yp_hbm.at[pl.ds((plane + t_rows) * rt, blk * rt)], sem_init)
        spare.start()
        spare.wait()

    valid = nv_ref[i] > 0
    nxt = jnp.minimum(i + 1, n - 1)
    has_next = jnp.logical_and(i + 1 < n, nv_ref[nxt] > 0)

    @pl.when(jnp.logical_and(valid, first_ref[i] == 1))
    def _():
        ws = ws_ref[i]
        for cp in weight_copies(be_ref[i], ws):
            cp.wait()

        @pl.when(nxe_ref[i] >= 0)
        def _():
            for cp in weight_copies(nxe_ref[i], 1 - ws):
                cp.start()

    for par in (0, 1):
        @pl.when(jnp.logical_and(valid, i % 2 == par))
        def _(par=par):
            gather_wait(par)
            scatter_wait(par)
            gather_start(nxt, 1 - par)
            scatter_start(jnp.maximum(i - 1, 0), 1 - par)

    for par in (0, 1):
        @pl.when(jnp.logical_and(valid, i % 2 == par))
        def _(par=par):
            ws = ws_ref[i]
            x = jnp.concatenate([xbufs[par][pl.ds(j, blk, stride=pitch), :] for j in range(rt)],
                                axis=1).astype(BF16)
            g = jnp.dot(x, wgv[ws].astype(BF16), preferred_element_type=F32)
            u = jnp.dot(x, wuv[ws].astype(BF16), preferred_element_type=F32)
            a = (_silu(g) * u).astype(BF16)
            y = jnp.dot(a, wdv[ws].astype(BF16), preferred_element_type=F32)
            for j in range(rt):
                ybufs[par][pl.ds(j, blk, stride=pitch), :] = y[:, j * LANE:(j + 1) * LANE]

            @pl.when(jnp.logical_not(has_next))
            def _():
                scatter_wait(1 - par)
                scatter_start(i, par)
                gather_wait(1 - par)
                scatter_wait(par)


def _experts(plan, h2, w_gate, w_up, w_down):
    d = w_gate.shape[1]
    rt, pitch = ROW_TILES, ROW_PITCH
    assert d == rt * LANE
    t = h2.shape[0] // rt
    n_blocks = plan[0].shape[0]
    blk = MOE_BLOCK
    de = w_gate.shape[2]
    any_spec = pl.BlockSpec(memory_space=pl.ANY)
    return pl.pallas_call(
        _expert_kernel,
        grid_spec=pltpu.PrefetchScalarGridSpec(
            num_scalar_prefetch=len(plan),
            grid=(n_blocks,),
            in_specs=[any_spec, any_spec, any_spec, any_spec],
            out_specs=any_spec,
            scratch_shapes=[pltpu.VMEM((blk * pitch, LANE), F32), pltpu.VMEM((blk * pitch, LANE), F32),
                            pltpu.VMEM((blk * pitch, LANE), F32), pltpu.VMEM((blk * pitch, LANE), F32),
                            pltpu.VMEM((2, d, de), F32),
                            pltpu.VMEM((2, d, de), F32),
                            pltpu.VMEM((2, de, d), F32),
                            pltpu.SemaphoreType.DMA((2,)),
                            pltpu.SemaphoreType.DMA((2,)),
                            pltpu.SemaphoreType.DMA((2, 3)),
                            pltpu.SemaphoreType.DMA],
        ),
        out_shape=jax.ShapeDtypeStruct((2 * (t + blk) * rt, LANE), F32),
        compiler_params=_cparams(("arbitrary",)),
        name="moe_experts",
    )(*plan, h2, w_gate, w_up, w_down).reshape(2, (t + blk) * rt, LANE)


def _combine_kernel(x_ref, gt_ref, w_ref, yp_ref, o_ref):
    tm = x_ref.shape[0]
    w = w_ref[...]
    w0, w1 = w[:, 0:1], w[:, 1:2]
    moe = jnp.concatenate(
        [w0 * yp_ref[0, pl.ds(j, tm, stride=ROW_TILES), :]
         + w1 * yp_ref[1, pl.ds(j, tm, stride=ROW_TILES), :] for j in range(ROW_TILES)], axis=1)
    o_ref[...] = x_ref[...] + gt_ref[...] * moe


def _combine(x1, gt, w_pick, yp, tm):
    b, l, d = x1.shape
    tm = min(tm, l)
    nb = l // tm
    return pl.pallas_call(
        _combine_kernel,
        grid=(b, nb),
        in_specs=[pl.BlockSpec((None, tm, d), lambda bi, i: (bi, i, 0)),
                  pl.BlockSpec((None, 1, d), lambda bi, i: (bi, 0, 0)),
                  pl.BlockSpec((tm, ROUTER_W), lambda bi, i: (bi * nb + i, 0)),
                  pl.BlockSpec((2, tm * ROW_TILES, LANE), lambda bi, i: (0, bi * nb + i, 0))],
        out_specs=pl.BlockSpec((None, tm, d), lambda bi, i: (bi, i, 0)),
        out_shape=jax.ShapeDtypeStruct((b, l, d), F32),
        compiler_params=_cparams(("parallel", "parallel")),
        name="moe_combine",
    )(x1, gt, w_pick, yp)


def _dispatch_plan(eid):
    blk = MOE_BLOCK
    n_assign = eid.shape[0]
    n_tok = n_assign // 2
    n_blocks = (n_assign + blk - 1) // blk + N_EXPERTS
    order = jnp.argsort(eid).astype(jnp.int32)
    e_ids = jnp.arange(N_EXPERTS, dtype=jnp.int32)
    counts = jnp.sum((eid[None, :] == e_ids[:, None]).astype(jnp.int32), axis=1)
    start = jnp.cumsum(counts) - counts
    nblk = (counts + blk - 1) // blk
    bend = jnp.cumsum(nblk)
    b_ids = jnp.arange(n_blocks, dtype=jnp.int32)
    block_e = jnp.minimum(jnp.searchsorted(bend, b_ids, side='right'), N_EXPERTS - 1).astype(jnp.int32)
    used = b_ids < bend[-1]
    within = b_ids - (bend - nblk)[block_e]
    block_nv = jnp.where(used, jnp.clip(counts[block_e] - within * blk, 0, blk), 0).astype(jnp.int32)
    block_j0 = jnp.where(used, start[block_e] + within * blk, 0).astype(jnp.int32)
    first = jnp.concatenate([jnp.ones((1,), jnp.int32),
                             (block_e[1:] != block_e[:-1]).astype(jnp.int32)])
    w_slot = ((jnp.cumsum(first) - 1) % 2).astype(jnp.int32)
    nonempty = jnp.where(counts > 0, e_ids, N_EXPERTS)
    later = jnp.flip(lax.cummin(jnp.flip(nonempty)))
    next_e = jnp.concatenate([later[1:], jnp.full((1,), N_EXPERTS, jnp.int32)])
    next_e = jnp.where(next_e >= N_EXPERTS, -1, next_e)[block_e]
    tok_sorted = order // 2
    dst_sorted = (order % 2) * (n_tok + blk) + order // 2
    return (block_e, block_nv, first, next_e, w_slot, block_j0, tok_sorted, dst_sorted)


def _rope_tables(length, with_rope):
    half = MLA_ROPE // 4
    if with_rope:
        rows = length // GRID_W
        row = jnp.repeat(jnp.arange(rows), GRID_W).astype(F32)
        col = jnp.tile(jnp.arange(GRID_W), rows).astype(F32)
        inv = 1.0 / (ROPE_THETA ** (jnp.arange(half, dtype=F32) / half))
        ang_r = row[:, None] * inv
        ang_c = col[:, None] * inv
        cos = jnp.concatenate([jnp.cos(ang_r), jnp.cos(ang_r), jnp.cos(ang_c), jnp.cos(ang_c)], axis=1)
        sin = jnp.concatenate([-jnp.sin(ang_r), jnp.sin(ang_r), -jnp.sin(ang_c), jnp.sin(ang_c)], axis=1)
    else:
        cos = jnp.ones((length, MLA_ROPE), F32)
        sin = jnp.zeros((length, MLA_ROPE), F32)
    pad = ((0, 0), (0, LANE - MLA_ROPE))
    return jnp.pad(cos, pad), jnp.pad(sin, pad)


def _split_w_in(w_in):
    sizes = (GLA_QK_W, GLA_QK_W, GLA_V_W, GLA_V_W, GLA_LOWRANK, GLA_LOWRANK,
             MLA_Q_RANK, MLA_KV_RANK, MLA_ROPE, D_MODEL, D_MODEL)
    parts, off = [], 0
    for s in sizes:
        parts.append(w_in[:, off:off + s])
        off += s
    return parts


def kernel(x, c, ctx, c_ctx, w_mod, b_mod, norm1_g, norm2_g, w_in, w_decay_f, b_decay_f, w_decay_b, b_decay_b, gla_norm_g, q_a_norm_g, w_uq, kv_a_norm_g, w_ukv, q_norm_g, k_norm_g, w_o_gla, w_o_mla, w_out, w_router_group, b_router_group, w_router_expert, b_router_expert, w_exp_gate, w_exp_up, w_exp_down):
    assert w_mod.shape[0] == 1, "single-layer block"
    b, l, d = x.shape
    lc = ctx.shape[1]
    t = b * l

    c_rows = jnp.concatenate([c, c_ctx[None, :], jnp.zeros((8 - b - 1, d), F32)], axis=0)
    mod = _modulation(c_rows, w_mod[0], b_mod[0])
    sh1, sc1, gt1, sh2, sc2, gt2 = [mod[:b, i * d:(i + 1) * d].reshape(b, 1, d) for i in range(6)]
    sh1c = jnp.broadcast_to(mod[b:b + 1, 0:d].reshape(1, 1, d), (b, 1, d))
    sc1c = jnp.broadcast_to(mod[b:b + 1, d:2 * d].reshape(1, 1, d), (b, 1, d))

    wq_, wk_, wv_, wr_, waf, wab, wcq, wckv, wkr, wga, wgb = _split_w_in(w_in[0])
    small = jnp.concatenate(
        [wkr, waf, wab, jnp.zeros((d, SMALL_W - MLA_ROPE - 2 * GLA_LOWRANK), F32)], axis=1)
    w_lat = jnp.concatenate([wq_, wk_, wv_, wr_, wga, wgb, wcq, wckv, small], axis=1).astype(BF16)
    w_ctx = jnp.concatenate([wk_, wv_, wckv, small], axis=1).astype(BF16)
    g1 = norm1_g[0].reshape(1, d)
    p = _in_projection(x, g1, sc1, sh1, w_lat, tm=1024)
    pc = _in_projection(ctx, g1, sc1c, sh1c, w_ctx, tm=lc)

    def decay_w(w, off):
        return jnp.zeros((LANE, GLA_QK_W), F32).at[off:off + GLA_LOWRANK].set(w).astype(BF16)

    y_gla = _gla(p, pc,
                 decay_w(w_decay_f[0], SM_AF), b_decay_f[0].reshape(1, -1),
                 decay_w(w_decay_b[0], SM_AB), b_decay_b[0].reshape(1, -1),
                 gla_norm_g[0].reshape(1, -1))

    wq = jnp.pad(w_uq[0].reshape(MLA_Q_RANK, MLA_HEADS, MLA_QK_DIM),
                 ((0, 0), (0, 0), (0, MLA_QK_PAD - MLA_QK_DIM))).reshape(MLA_Q_RANK, -1).astype(BF16)
    wkv = w_ukv[0].astype(BF16)
    qn_g = jnp.pad(q_norm_g[0], (0, MLA_QK_PAD - MLA_QK_DIM)).reshape(1, -1)
    kn_nope = k_norm_g[0][:MLA_NOPE].reshape(1, -1)
    kn_rope = jnp.pad(k_norm_g[0][MLA_NOPE:], (0, LANE - MLA_ROPE)).reshape(1, -1)
    cos, sin = _rope_tables(l, True)
    cos_c, sin_c = _rope_tables(lc, False)
    q_m = _mla_queries(p, q_a_norm_g[0].reshape(1, -1), wq, qn_g, cos, sin, tm=512)
    kva = kv_a_norm_g[0].reshape(1, -1)
    k_m, v_m = _mla_keys_values(p, pc, kva, wkv, kn_nope, kn_rope,
                                jnp.concatenate([cos, cos_c]), jnp.concatenate([sin, sin_c]))
    y_mla = _attention(q_m, k_m, v_m, tq=256)

    y = _branch_merge(y_gla, y_mla, w_o_gla[0].astype(BF16), w_o_mla[0].astype(BF16), p, tm=1024, tn=512)
    w_router = jnp.concatenate(
        [w_router_group[0], w_router_expert[0],
         jnp.zeros((d, ROUTER_W - N_GROUPS - N_EXPERTS), F32)], axis=1).astype(BF16)
    b_router = jnp.concatenate(
        [b_router_group[0], b_router_expert[0],
         jnp.zeros((ROUTER_W - N_GROUPS - N_EXPERTS,), F32)]).reshape(1, -1)
    x1, h2, logits = _out_projection(y, w_out[0].astype(BF16), x, gt1, norm2_g[0].reshape(1, d),
                                     sc2, sh2, w_router, b_router, tm=256)

    e_pick, w_pick = _route(logits.reshape(t, ROUTER_W), tm=1024)
    plan = _dispatch_plan(e_pick[:, :2].reshape(-1))
    yp = _experts(plan, h2.reshape(t * ROW_TILES, LANE),
                  w_exp_gate[0], w_exp_up[0], w_exp_down[0])
    return _combine(x1, gt2, w_pick, yp, tm=512)
```

```python
import functools
import math

import jax
import jax.numpy as jnp
from jax import lax
from jax.experimental import pallas as pl
from jax.experimental.pallas import tpu as pltpu

F32 = jnp.float32
BF16 = jnp.bfloat16

D_MODEL = 2048
GRID_W = 64
EPS = 1e-6

GLA_HEADS = 4
GLA_DK = 256
GLA_DV = 512
GLA_LOWRANK = 16
GLA_TAU = 16.0
GLA_CHUNK = 64
GLA_GROUP = 8
GLA_QK_W = GLA_HEADS * GLA_DK
GLA_V_W = GLA_HEADS * GLA_DV

MLA_HEADS = 16
MLA_Q_RANK = 512
MLA_KV_RANK = 512
MLA_NOPE = 128
MLA_ROPE = 64
MLA_V = 128
MLA_QK_DIM = MLA_NOPE + MLA_ROPE
MLA_QK_PAD = 256
MLA_VT_ROWS = MLA_V + 16
ROPE_THETA = 10000.0
LOG2E = math.log2(math.e)

N_GROUPS = 8
EXP_PER_GROUP = 8
N_EXPERTS = N_GROUPS * EXP_PER_GROUP
D_EXPERT = 512
ROUTER_W = 128
MOE_BLOCK = 256

LANE = 128
ROW_TILES = D_MODEL // LANE
ROW_PITCH = 20
VMEM_LIMIT = 56 * 1024 * 1024

COL_Q = 0
COL_K = COL_Q + GLA_QK_W
COL_V = COL_K + GLA_QK_W
COL_R = COL_V + GLA_V_W
COL_GA = COL_R + GLA_V_W
COL_GB = COL_GA + D_MODEL
COL_CQ = COL_GB + D_MODEL
COL_CKV = COL_CQ + MLA_Q_RANK
COL_SMALL = COL_CKV + MLA_KV_RANK
SMALL_W = 256
NP_LAT = COL_SMALL + SMALL_W
SM_AF = MLA_ROPE
SM_AB = MLA_ROPE + GLA_LOWRANK
CCOL_K = 0
CCOL_V = CCOL_K + GLA_QK_W
CCOL_CKV = CCOL_V + GLA_V_W
CCOL_SMALL = CCOL_CKV + MLA_KV_RANK
NP_CTX = CCOL_SMALL + SMALL_W
INPROJ_TN = 1280


def _cparams(sem):
    return pltpu.CompilerParams(dimension_semantics=sem, vmem_limit_bytes=VMEM_LIMIT)


def _silu(x):
    return x * jax.nn.sigmoid(x)


def _mod_kernel(c_ref, w_ref, b_ref, o_ref):
    a = _silu(c_ref[...]).astype(BF16)
    o_ref[...] = jnp.dot(a, w_ref[...].astype(BF16), preferred_element_type=F32) + b_ref[...]


def _modulation(c_rows, w_mod, b_mod):
    m, d = c_rows.shape
    n = w_mod.shape[1]
    tn = 1024
    return pl.pallas_call(
        _mod_kernel,
        grid=(n // tn,),
        in_specs=[pl.BlockSpec((m, d), lambda j: (0, 0)),
                  pl.BlockSpec((d, tn), lambda j: (0, j)),
                  pl.BlockSpec((1, tn), lambda j: (0, j))],
        out_specs=pl.BlockSpec((m, tn), lambda j: (0, j)),
        out_shape=jax.ShapeDtypeStruct((m, n), F32),
        compiler_params=_cparams(("parallel",)),
        name="modulation",
    )(c_rows, w_mod, b_mod.reshape(1, n))


def _inproj_kernel(x_ref, g_ref, sc_ref, sh_ref, w_ref, o_ref, h_ref):
    @pl.when(pl.program_id(2) == 0)
    def _():
        x = x_ref[...]
        ms = jnp.mean(x * x, axis=-1, keepdims=True)
        y = x * lax.rsqrt(ms + EPS) * g_ref[...]
        h_ref[...] = (y * (1.0 + sc_ref[...]) + sh_ref[...]).astype(BF16)

    o_ref[...] = jnp.dot(h_ref[...], w_ref[...], preferred_element_type=F32).astype(o_ref.dtype)


def _in_projection(x, g, sc, sh, w, tm):
    b, l, d = x.shape
    tm = min(tm, l)
    n = w.shape[1]
    tn = INPROJ_TN
    return pl.pallas_call(
        _inproj_kernel,
        grid=(b, l // tm, n // tn),
        in_specs=[pl.BlockSpec((None, tm, d), lambda bi, i, j: (bi, i, 0)),
                  pl.BlockSpec((1, d), lambda bi, i, j: (0, 0)),
                  pl.BlockSpec((None, 1, d), lambda bi, i, j: (bi, 0, 0)),
                  pl.BlockSpec((None, 1, d), lambda bi, i, j: (bi, 0, 0)),
                  pl.BlockSpec((d, tn), lambda bi, i, j: (0, j))],
        out_specs=pl.BlockSpec((None, tm, tn), lambda bi, i, j: (bi, i, j)),
        out_shape=jax.ShapeDtypeStruct((b, l, n), BF16),
        scratch_shapes=[pltpu.VMEM((tm, d), BF16)],
        compiler_params=_cparams(("parallel", "parallel", "arbitrary")),
        name="in_projection",
    )(x, g, sc, sh, w)


def _gla_kernel(q_ref, k_ref, v_ref, r_ref, sm_ref, kc_ref, vc_ref, smc_ref,
                wf_ref, bf_ref, wb_ref, bb_ref, gn_ref, y_ref, o_ref, sf_ref, sb_ref):
    c = GLA_CHUNK
    n_lat = q_ref.shape[0] // c
    n_ctx = kc_ref.shape[0] // c
    row = lax.broadcasted_iota(jnp.int32, (c, c), 0)
    col = lax.broadcasted_iota(jnp.int32, (c, c), 1)
    lower = row >= col
    upper = row <= col
    nt = (((1,), (1,)), ((), ()))
    tn = (((0,), (0,)), ((), ()))

    def cum_decay(sm, w_ref, b_ref, tri, n_chunks):
        z = jnp.dot(sm, w_ref[...], preferred_element_type=F32) + b_ref[...]
        g = (jnp.minimum(z, 0.0) - jnp.log(1.0 + jnp.exp(-jnp.abs(z)))) * (1.0 / GLA_TAU)
        g_hi = g.astype(BF16)
        g_lo = (g - g_hi.astype(F32)).astype(BF16)
        t = jnp.where(tri, 1.0, 0.0).astype(BF16)
        return [jnp.dot(t, g_hi[i * c:(i + 1) * c], preferred_element_type=F32)
                + jnp.dot(t, g_lo[i * c:(i + 1) * c], preferred_element_type=F32)
                for i in range(n_chunks)]

    fwd = (wf_ref, bf_ref, lower, c - 1, sf_ref)
    bwd = (wb_ref, bb_ref, upper, 0, sb_ref)

    def group(r0, n_chunks, direction, sm_r, k_r, v_r, q_r):
        w_ref, b_ref, tri, end_row, st_ref = direction
        rows = n_chunks * c
        k_all = k_r[pl.ds(r0, rows), :].astype(F32)
        v_all = v_r[pl.ds(r0, rows), :]
        q_all = None if q_r is None else q_r[pl.ds(r0, rows), :].astype(F32) * (GLA_DK ** -0.5)
        bc_all = cum_decay(sm_r[pl.ds(r0, rows), :], w_ref, b_ref, tri, n_chunks)
        local = []
        for i in range(n_chunks):
            sl = slice(i * c, (i + 1) * c)
            bc = bc_all[i]
            bend = bc[end_row:end_row + 1, :]
            k, v = k_all[sl], v_all[sl]
            ke = (k * jnp.exp(bend - bc)).astype(BF16)
            upd = lax.dot_general(v, ke, tn, preferred_element_type=F32)
            qd = o_intra = None
            if q_all is not None:
                qd = (q_all[sl] * jnp.exp(bc)).astype(BF16)
                ki = (k * jnp.exp(-bc)).astype(BF16)
                att = lax.dot_general(qd, ki, nt, preferred_element_type=F32)
                att = jnp.where(tri, att, 0.0).astype(BF16)
                o_intra = jnp.dot(att, v, preferred_element_type=F32)
            local.append((jnp.exp(bend), upd, qd, o_intra))
        st = st_ref[...]
        outs = [None] * n_chunks
        for i in (range(n_chunks) if end_row else reversed(range(n_chunks))):
            decay, upd, qd, o_intra = local[i]
            if q_all is not None:
                outs[i] = o_intra + lax.dot_general(qd, st.astype(BF16), nt, preferred_element_type=F32)
            st = st * decay + upd
        st_ref[...] = st
        return None if q_all is None else jnp.concatenate(outs, axis=0)

    def readout(r0, o):
        ms = jnp.mean(o * o, axis=-1, keepdims=True)
        yn = o * lax.rsqrt(ms + EPS) * gn_ref[...]
        r = r_ref[pl.ds(r0, o.shape[0]), :].astype(F32)
        y_ref[pl.ds(r0, o.shape[0]), :] = (yn * _silu(r)).astype(y_ref.dtype)

    sf_ref[...] = jnp.zeros_like(sf_ref)
    sb_ref[...] = jnp.zeros_like(sb_ref)
    group(0, n_ctx, fwd, smc_ref, kc_ref, vc_ref, None)
    group(0, n_ctx, bwd, smc_ref, kc_ref, vc_ref, None)

    g_lat = min(GLA_GROUP, n_lat // 2)
    assert n_lat % (2 * g_lat) == 0
    n_groups = n_lat // g_lat
    rows = g_lat * c

    def scan(i, direction):
        r0 = pl.multiple_of(i * rows, rows)
        return r0, group(r0, g_lat, direction, sm_ref, k_ref, v_ref, q_ref)

    def first_half(i, carry):
        for r0, o in (scan(i, fwd), scan(n_groups - 1 - i, bwd)):
            o_ref[pl.ds(r0, rows), :] = o
        return carry

    lax.fori_loop(0, n_groups // 2, first_half, 0)

    def second_half(i, carry):
        for r0, o in (scan(i, fwd), scan(n_groups - 1 - i, bwd)):
            readout(r0, o_ref[pl.ds(r0, rows), :] + o)
        return carry

    lax.fori_loop(n_groups // 2, n_groups, second_half, 0)


def _gla(p, pc, wf, bf, wb, bb, gn):
    b, l, _ = p.shape
    lc = pc.shape[1]
    dk, dv = GLA_DK, GLA_DV
    hmap = lambda off: (lambda bi, h: (bi, 0, off + h))
    wmap = lambda bi, h: (0, h)
    return pl.pallas_call(
        _gla_kernel,
        grid=(b, GLA_HEADS),
        in_specs=[pl.BlockSpec((None, l, dk), hmap(COL_Q // dk)),
                  pl.BlockSpec((None, l, dk), hmap(COL_K // dk)),
                  pl.BlockSpec((None, l, dv), hmap(COL_V // dv)),
                  pl.BlockSpec((None, l, dv), hmap(COL_R // dv)),
                  pl.BlockSpec((None, l, LANE), lambda bi, h: (bi, 0, COL_SMALL // LANE)),
                  pl.BlockSpec((None, lc, dk), hmap(CCOL_K // dk)),
                  pl.BlockSpec((None, lc, dv), hmap(CCOL_V // dv)),
                  pl.BlockSpec((None, lc, LANE), lambda bi, h: (bi, 0, CCOL_SMALL // LANE)),
                  pl.BlockSpec((LANE, dk), wmap),
                  pl.BlockSpec((1, dk), wmap),
                  pl.BlockSpec((LANE, dk), wmap),
                  pl.BlockSpec((1, dk), wmap),
                  pl.BlockSpec((1, dv), lambda bi, h: (0, 0))],
        out_specs=pl.BlockSpec((None, l, dv), lambda bi, h: (bi, 0, h)),
        out_shape=jax.ShapeDtypeStruct((b, l, GLA_V_W), BF16),
        scratch_shapes=[pltpu.VMEM((l, dv), F32), pltpu.VMEM((dv, dk), F32), pltpu.VMEM((dv, dk), F32)],
        compiler_params=_cparams(("parallel", "parallel")),
        name="gla",
    )(p, p, p, p, p, pc, pc, pc, wf, bf, wb, bb, gn)


def _rope_partner(t):
    lane = lax.broadcasted_iota(jnp.int32, t.shape, 1)
    first = (lane % 32) < 16
    return jnp.where(first, pltpu.roll(t, LANE - 16, 1), pltpu.roll(t, 16, 1))


def _mla_q_kernel(cq_ref, ga_ref, w_ref, gn_ref, cos_ref, sin_ref, o_ref):
    cq = cq_ref[...].astype(F32)
    cn = (cq * lax.rsqrt(jnp.mean(cq * cq, axis=-1, keepdims=True) + EPS) * ga_ref[...]).astype(BF16)
    cos = cos_ref[...]
    sin = sin_ref[...]
    scale = MLA_QK_DIM ** -0.5 * LOG2E
    for h in range(MLA_HEADS):
        qh = jnp.dot(cn, w_ref[:, h * MLA_QK_PAD:(h + 1) * MLA_QK_PAD], preferred_element_type=F32)
        ms = jnp.sum(qh * qh, axis=-1, keepdims=True) * (1.0 / MLA_QK_DIM)
        qn = qh * lax.rsqrt(ms + EPS) * gn_ref[...]
        t = qn[:, MLA_NOPE:]
        rot = t * cos + _rope_partner(t) * sin
        o_ref[h, :, :MLA_NOPE] = (qn[:, :MLA_NOPE] * scale).astype(o_ref.dtype)
        o_ref[h, :, MLA_NOPE:] = (rot * scale).astype(o_ref.dtype)


def _mla_queries(p, ga, wq, gn, cos, sin, tm):
    b, l, _ = p.shape
    tm = min(tm, l)
    return pl.pallas_call(
        _mla_q_kernel,
        grid=(b, l // tm),
        in_specs=[pl.BlockSpec((None, tm, MLA_Q_RANK), lambda bi, i: (bi, i, COL_CQ // MLA_Q_RANK)),
                  pl.BlockSpec((1, MLA_Q_RANK), lambda bi, i: (0, 0)),
                  pl.BlockSpec(wq.shape, lambda bi, i: (0, 0)),
                  pl.BlockSpec((1, MLA_QK_PAD), lambda bi, i: (0, 0)),
                  pl.BlockSpec((tm, LANE), lambda bi, i: (i, 0)),
                  pl.BlockSpec((tm, LANE), lambda bi, i: (i, 0))],
        out_specs=pl.BlockSpec((None, MLA_HEADS, tm, MLA_QK_PAD), lambda bi, i: (bi, 0, i, 0)),
        out_shape=jax.ShapeDtypeStruct((b, MLA_HEADS, l, MLA_QK_PAD), BF16),
        compiler_params=_cparams(("parallel", "parallel")),
        name="mla_queries",
    )(p, ga, wq, gn, cos, sin)


def _mla_kv_kernel(ckv_ref, sm_ref, ckvc_ref, smc_ref, ga_ref, w_ref, gk_ref, gr_ref,
                   cos_ref, sin_ref, k_ref, v_ref, *, n_lat):
    is_ctx = pl.program_id(1) >= n_lat
    ckv = jnp.where(is_ctx, ckvc_ref[...], ckv_ref[...]).astype(F32)
    cn = (ckv * lax.rsqrt(jnp.mean(ckv * ckv, axis=-1, keepdims=True) + EPS) * ga_ref[...]).astype(BF16)
    sm = jnp.where(is_ctx, smc_ref[...], sm_ref[...]).astype(F32)
    lane = lax.broadcasted_iota(jnp.int32, sm.shape, 1)
    kr = jnp.where(lane < MLA_ROPE, sm, 0.0)
    ss_r = jnp.sum(kr * kr, axis=-1, keepdims=True)
    krg = kr * gr_ref[...]
    rot = krg * cos_ref[...] + _rope_partner(krg) * sin_ref[...]
    hw = MLA_NOPE + MLA_V
    tm = ckv.shape[0]
    sub = lax.broadcasted_iota(jnp.int32, (MLA_VT_ROWS - MLA_V, tm), 0)
    ones_rows = jnp.where(sub == 0, 1.0, 0.0).astype(v_ref.dtype)
    for h in range(MLA_HEADS):
        kvh = jnp.dot(cn, w_ref[:, h * hw:(h + 1) * hw], preferred_element_type=F32)
        kn = kvh[:, :MLA_NOPE]
        ms = (jnp.sum(kn * kn, axis=-1, keepdims=True) + ss_r) * (1.0 / MLA_QK_DIM)
        rs = lax.rsqrt(ms + EPS)
        k_ref[h, :, :MLA_NOPE] = (kn * rs * gk_ref[...]).astype(k_ref.dtype)
        k_ref[h, :, MLA_NOPE:] = (rot * rs).astype(k_ref.dtype)
        v_ref[h, :MLA_V, :] = kvh[:, MLA_NOPE:].T.astype(v_ref.dtype)
        v_ref[h, MLA_V:, :] = ones_rows


def _mla_keys_values(p, pc, ga, wkv, gk, gr, cos, sin):
    b, l, _ = p.shape
    lc = pc.shape[1]
    tm = min(lc, l)
    assert l % tm == 0 and lc % tm == 0
    n_lat, n_ctx = l // tm, lc // tm
    lat = lambda blk: (lambda bi, i: (bi, jnp.minimum(i, n_lat - 1), blk))
    ctx = lambda blk: (lambda bi, i: (bi, jnp.maximum(i - n_lat, 0), blk))
    const = lambda bi, i: (0, 0)
    k_spec = pl.BlockSpec((None, MLA_HEADS, tm, MLA_QK_PAD), lambda bi, i: (bi, 0, i, 0))
    vt_spec = pl.BlockSpec((None, MLA_HEADS, MLA_VT_ROWS, tm), lambda bi, i: (bi, 0, 0, i))
    k_shape = jax.ShapeDtypeStruct((b, MLA_HEADS, l + lc, MLA_QK_PAD), BF16)
    vt_shape = jax.ShapeDtypeStruct((b, MLA_HEADS, MLA_VT_ROWS, l + lc), BF16)
    return pl.pallas_call(
        functools.partial(_mla_kv_kernel, n_lat=n_lat),
        grid=(b, n_lat + n_ctx),
        in_specs=[pl.BlockSpec((None, tm, MLA_KV_RANK), lat(COL_CKV // MLA_KV_RANK)),
                  pl.BlockSpec((None, tm, LANE), lat(COL_SMALL // LANE)),
                  pl.BlockSpec((None, tm, MLA_KV_RANK), ctx(CCOL_CKV // MLA_KV_RANK)),
                  pl.BlockSpec((None, tm, LANE), ctx(CCOL_SMALL // LANE)),
                  pl.BlockSpec((1, MLA_KV_RANK), const),
                  pl.BlockSpec(wkv.shape, const),
                  pl.BlockSpec((1, LANE), const),
                  pl.BlockSpec((1, LANE), const),
                  pl.BlockSpec((tm, LANE), lambda bi, i: (i, 0)),
                  pl.BlockSpec((tm, LANE), lambda bi, i: (i, 0))],
        out_specs=[k_spec, vt_spec],
        out_shape=[k_shape, vt_shape],
        compiler_params=_cparams(("parallel", "arbitrary")),
        name="mla_keys_values",
    )(p, p, pc, pc, ga, wkv, gk, gr, cos, sin)


def _attn_kernel(q_ref, k_ref, vt_ref, o_ref, s0_ref, s1_ref, m0_ref, m1_ref, *, tq):
    nt = (((1,), (1,)), ((), ()))
    n = q_ref.shape[0] // tq
    bufs = ((s0_ref, m0_ref), (s1_ref, m1_ref))

    def scores(i, slot):
        s_ref, m_ref = bufs[slot]
        r0 = pl.multiple_of(i * tq, tq)
        s = lax.dot_general(k_ref[...], q_ref[pl.ds(r0, tq), :], nt, preferred_element_type=F32)
        s_ref[...] = s
        m_ref[...] = jnp.max(s, axis=0, keepdims=True)

    def outputs(i, slot):
        s_ref, m_ref = bufs[slot]
        r0 = pl.multiple_of(i * tq, tq)
        p = jnp.exp2(s_ref[...] - m_ref[...]).astype(BF16)
        o = jnp.dot(vt_ref[...], p, preferred_element_type=F32)
        o_ref[pl.ds(r0, tq), :] = (o[:MLA_V] / o[MLA_V:MLA_V + 1]).T.astype(o_ref.dtype)

    scores(0, 0)
    if n % 2 == 0:
        def body(j, carry):
            scores(2 * j + 1, 1)
            outputs(2 * j, 0)
            scores(2 * j + 2, 0)
            outputs(2 * j + 1, 1)
            return carry

        lax.fori_loop(0, n // 2 - 1, body, 0)
        scores(n - 1, 1)
        outputs(n - 2, 0)
        outputs(n - 1, 1)
    else:
        assert n == 1
        outputs(0, 0)


def _attention(q, k, v, tq):
    b, h, l, dq = q.shape
    tq = min(tq, l)
    lk = k.shape[2]
    bh = lambda bi, hi: (bi, hi, 0, 0)
    return pl.pallas_call(
        functools.partial(_attn_kernel, tq=tq),
        grid=(b, h),
        in_specs=[pl.BlockSpec((None, None, l, dq), bh),
                  pl.BlockSpec((None, None, lk, dq), bh),
                  pl.BlockSpec((None, None, MLA_VT_ROWS, lk), bh)],
        out_specs=pl.BlockSpec((None, l, MLA_V), lambda bi, hi: (bi, 0, hi)),
        out_shape=jax.ShapeDtypeStruct((b, l, h * MLA_V), BF16),
        scratch_shapes=[pltpu.VMEM((lk, tq), F32), pltpu.VMEM((lk, tq), F32),
                        pltpu.VMEM((1, tq), F32), pltpu.VMEM((1, tq), F32)],
        compiler_params=_cparams(("parallel", "parallel")),
        name="mla_attention",
    )(q, k, v)


def _merge_kernel(yg_ref, ym_ref, wg_ref, wm_ref, ga_ref, gb_ref, o_ref):
    a = jnp.dot(yg_ref[...], wg_ref[...], preferred_element_type=F32)
    m = jnp.dot(ym_ref[...], wm_ref[...], preferred_element_type=F32)
    y = (jax.nn.sigmoid(ga_ref[...].astype(F32)) * a
         + jax.nn.sigmoid(gb_ref[...].astype(F32)) * m)
    o_ref[...] = y.astype(o_ref.dtype)


def _branch_merge(yg, ym, wg, wm, p, tm, tn):
    b, l, d = yg.shape
    tm = min(tm, l)
    n = wg.shape[1]
    return pl.pallas_call(
        _merge_kernel,
        grid=(b, l // tm, n // tn),
        in_specs=[pl.BlockSpec((None, tm, d), lambda bi, i, j: (bi, i, 0)),
                  pl.BlockSpec((None, tm, d), lambda bi, i, j: (bi, i, 0)),
                  pl.BlockSpec((d, tn), lambda bi, i, j: (0, j)),
                  pl.BlockSpec((d, tn), lambda bi, i, j: (0, j)),
                  pl.BlockSpec((None, tm, tn), lambda bi, i, j: (bi, i, COL_GA // tn + j)),
                  pl.BlockSpec((None, tm, tn), lambda bi, i, j: (bi, i, COL_GB // tn + j))],
        out_specs=pl.BlockSpec((None, tm, tn), lambda bi, i, j: (bi, i, j)),
        out_shape=jax.ShapeDtypeStruct((b, l, n), BF16),
        compiler_params=_cparams(("parallel", "parallel", "arbitrary")),
        name="branch_merge",
    )(yg, ym, wg, wm, p, p)


def _outproj_kernel(y_ref, w_ref, x_ref, gt_ref, g2_ref, sc_ref, sh_ref, wr_ref, br_ref,
                    x1_ref, h2_ref, lg_ref):
    x1 = x_ref[...] + gt_ref[...] * jnp.dot(y_ref[...], w_ref[...], preferred_element_type=F32)
    x1_ref[...] = x1
    ms = jnp.mean(x1 * x1, axis=-1, keepdims=True)
    h2 = x1 * lax.rsqrt(ms + EPS) * g2_ref[...] * (1.0 + sc_ref[...]) + sh_ref[...]
    tm = x1.shape[0]
    for j in range(ROW_TILES):
        h2_ref[pl.ds(j, tm, stride=ROW_TILES), :] = h2[:, j * LANE:(j + 1) * LANE]
    lg_ref[...] = jnp.dot(h2.astype(BF16), wr_ref[...], preferred_element_type=F32) + br_ref[...]


def _out_projection(y, w, x, gt, g2, sc, sh, wr, br, tm):
    b, l, d = x.shape
    tm = min(tm, l)
    row = lambda bi, i: (bi, i, 0)
    per_b = lambda bi, i: (bi, 0, 0)
    const = lambda bi, i: (0, 0)
    return pl.pallas_call(
        _outproj_kernel,
        grid=(b, l // tm),
        in_specs=[pl.BlockSpec((None, tm, d), row),
                  pl.BlockSpec((d, d), const),
                  pl.BlockSpec((None, tm, d), row),
                  pl.BlockSpec((None, 1, d), per_b),
                  pl.BlockSpec((1, d), const),
                  pl.BlockSpec((None, 1, d), per_b),
                  pl.BlockSpec((None, 1, d), per_b),
                  pl.BlockSpec((d, ROUTER_W), const),
                  pl.BlockSpec((1, ROUTER_W), const)],
        out_specs=[pl.BlockSpec((None, tm, d), row),
                   pl.BlockSpec((None, tm * ROW_TILES, LANE), row),
                   pl.BlockSpec((None, tm, ROUTER_W), row)],
        out_shape=[jax.ShapeDtypeStruct((b, l, d), F32),
                   jax.ShapeDtypeStruct((b, l * ROW_TILES, LANE), F32),
                   jax.ShapeDtypeStruct((b, l, ROUTER_W), F32)],
        compiler_params=_cparams(("parallel", "parallel")),
        name="out_projection",
    )(y, w, x, gt, g2, sc, sh, wr, br)


def _route_kernel(lg_ref, e_ref, w_ref):
    lg = lg_ref[...]
    lane = lax.broadcasted_iota(jnp.int32, lg.shape, 1)
    neg = jnp.float32(-jnp.inf)
    big = jnp.int32(1 << 20)
    is_g = lane < N_GROUPS
    gl = jnp.where(is_g, lg, neg)
    gm = jnp.max(gl, axis=-1, keepdims=True)
    gidx = jnp.min(jnp.where(gl == gm, lane, big), axis=-1, keepdims=True)
    gsum = jnp.sum(jnp.where(is_g, jnp.exp(gl - gm), 0.0), axis=-1, keepdims=True)
    p_g = 1.0 / gsum
    g_lo = N_GROUPS + gidx * EXP_PER_GROUP
    in_grp = (lane >= g_lo) & (lane < g_lo + EXP_PER_GROUP)
    el = jnp.where(in_grp, lg, neg)
    em = jnp.max(el, axis=-1, keepdims=True)
    z = jnp.sum(jnp.where(in_grp, jnp.exp(el - em), 0.0), axis=-1, keepdims=True)
    i1 = jnp.min(jnp.where(el == em, lane, big), axis=-1, keepdims=True)
    el2 = jnp.where(lane == i1, neg, el)
    em2 = jnp.max(el2, axis=-1, keepdims=True)
    i2 = jnp.min(jnp.where(el2 == em2, lane, big), axis=-1, keepdims=True)
    p1 = 1.0 / z
    p2 = jnp.exp(em2 - em) / z
    tot = p1 + p2
    e_ref[...] = jnp.where(lane == 0, i1 - N_GROUPS, jnp.where(lane == 1, i2 - N_GROUPS, 0))
    w_ref[...] = jnp.where(lane == 0, p_g * p1 / tot, jnp.where(lane == 1, p_g * p2 / tot, 0.0))


def _route(logits, tm):
    t = logits.shape[0]
    tm = min(tm, t)
    spec = pl.BlockSpec((tm, ROUTER_W), lambda i: (i, 0))
    return pl.pallas_call(
        _route_kernel,
        grid=(t // tm,),
        in_specs=[spec],
        out_specs=[spec, spec],
        out_shape=[jax.ShapeDtypeStruct((t, ROUTER_W), jnp.int32),
                   jax.ShapeDtypeStruct((t, ROUTER_W), F32)],
        compiler_params=_cparams(("parallel",)),
        name="moe_route",
    )(logits)


def _expert_kernel(be_ref, nv_ref, first_ref, nxe_ref, ws_ref, j0_ref, tok_ref, dst_ref,
                   h_hbm, wg_hbm, wu_hbm, wd_hbm, yp_hbm,
                   xbuf0, xbuf1, ybuf0, ybuf1, wgv, wuv, wdv, sem_in, sem_out, sem_w, sem_init):
    i = pl.program_id(0)
    n = pl.num_programs(0)
    blk = MOE_BLOCK
    last = tok_ref.shape[0] - 1
    rt, pitch = ROW_TILES, ROW_PITCH
    plane = yp_hbm.shape[0] // (2 * rt)
    t_rows = plane - blk
    xbufs = (xbuf0, xbuf1)
    ybufs = (ybuf0, ybuf1)

    def weight_copies(e, s):
        return (pltpu.make_async_copy(wg_hbm.at[e], wgv.at[s], sem_w.at[s, 0]),
                pltpu.make_async_copy(wu_hbm.at[e], wuv.at[s], sem_w.at[s, 1]),
                pltpu.make_async_copy(wd_hbm.at[e], wdv.at[s], sem_w.at[s, 2]))

    def gather_start(b, s):
        base = j0_ref[b]
        for r in range(blk):
            row0 = pl.multiple_of(tok_ref[jnp.minimum(base + r, last)] * rt, rt)
            pltpu.make_async_copy(h_hbm.at[pl.ds(row0, rt)],
                                  xbufs[s].at[pl.ds(r * pitch, rt)], sem_in.at[s]).start()

    def gather_wait(s):
        pltpu.make_async_copy(h_hbm.at[pl.ds(0, blk * rt)], xbufs[s].at[pl.ds(0, blk * rt)],
                              sem_in.at[s]).wait()

    def scatter_start(b, s):
        base = j0_ref[b]
        filled = nv_ref[b]
        for r in range(blk):
            dst = jnp.where(r < filled, dst_ref[jnp.minimum(base + r, last)], t_rows + r)
            row0 = pl.multiple_of(dst * rt, rt)
            pltpu.make_async_copy(ybufs[s].at[pl.ds(r * pitch, rt)],
                                  yp_hbm.at[pl.ds(row0, rt)], sem_out.at[s]).start()

    def scatter_wait(s):
        pltpu.make_async_copy(ybufs[s].at[pl.ds(0, blk * rt)], yp_hbm.at[pl.ds(0, blk * rt)],
                              sem_out.at[s]).wait()

    @pl.when(i == 0)
    def _():
        for cp in weight_copies(be_ref[0], 0):
            cp.start()
        gather_start(0, 0)
        ybuf1[...] = jnp.zeros_like(ybuf1)
        zeros = ybuf1.at[pl.ds(0, blk * rt)]
        pltpu.make_async_copy(zeros, yp_hbm.at[pl.ds(t_rows * rt, blk * rt)], sem_out.at[0]).start()
        spare = pltpu.make_async_copy(zeros, yp_hbm.at[pl.ds((plane + t_rows) * rt, blk * rt)], sem_init)
        spare.start()
        spare.wait()

    valid = nv_ref[i] > 0
    nxt = jnp.minimum(i + 1, n - 1)
    has_next = jnp.logical_and(i + 1 < n, nv_ref[nxt] > 0)

    @pl.when(jnp.logical_and(valid, first_ref[i] == 1))
    def _():
        ws = ws_ref[i]
        for cp in weight_copies(be_ref[i], ws):
            cp.wait()

        @pl.when(nxe_ref[i] >= 0)
        def _():
            for cp in weight_copies(nxe_ref[i], 1 - ws):
                cp.start()

    for par in (0, 1):
        @pl.when(jnp.logical_and(valid, i % 2 == par))
        def _(par=par):
            gather_wait(par)
            scatter_wait(par)
            gather_start(nxt, 1 - par)
            scatter_start(jnp.maximum(i - 1, 0), 1 - par)

    for par in (0, 1):
        @pl.when(jnp.logical_and(valid, i % 2 == par))
        def _(par=par):
            ws = ws_ref[i]
            x = jnp.concatenate([xbufs[par][pl.ds(j, blk, stride=pitch), :] for j in range(rt)],
                                axis=1).astype(BF16)
            g = jnp.dot(x, wgv[ws].astype(BF16), preferred_element_type=F32)
            u = jnp.dot(x, wuv[ws].astype(BF16), preferred_element_type=F32)
            a = (_silu(g) * u).astype(BF16)
            y = jnp.dot(a, wdv[ws].astype(BF16), preferred_element_type=F32)
            for j in range(rt):
                ybufs[par][pl.ds(j, blk, stride=pitch), :] = y[:, j * LANE:(j + 1) * LANE]

            @pl.when(jnp.logical_not(has_next))
            def _():
                scatter_wait(1 - par)
                scatter_start(i, par)
                gather_wait(1 - par)
                scatter_wait(par)


def _experts(plan, h2, w_gate, w_up, w_down):
    d = w_gate.shape[1]
    rt, pitch = ROW_TILES, ROW_PITCH
    assert d == rt * LANE
    t = h2.shape[0] // rt
    n_blocks = plan[0].shape[0]
    blk = MOE_BLOCK
    de = w_gate.shape[2]
    any_spec = pl.BlockSpec(memory_space=pl.ANY)
    return pl.pallas_call(
        _expert_kernel,
        grid_spec=pltpu.PrefetchScalarGridSpec(
            num_scalar_prefetch=len(plan),
            grid=(n_blocks,),
            in_specs=[any_spec, any_spec, any_spec, any_spec],
            out_specs=any_spec,
            scratch_shapes=[pltpu.VMEM((blk * pitch, LANE), F32), pltpu.VMEM((blk * pitch, LANE), F32),
                            pltpu.VMEM((blk * pitch, LANE), F32), pltpu.VMEM((blk * pitch, LANE), F32),
                            pltpu.VMEM((2, d, de), F32),
                            pltpu.VMEM((2, d, de), F32),
                            pltpu.VMEM((2, de, d), F32),
                            pltpu.SemaphoreType.DMA((2,)),
                            pltpu.SemaphoreType.DMA((2,)),
                            pltpu.SemaphoreType.DMA((2, 3)),
                            pltpu.SemaphoreType.DMA],
        ),
        out_shape=jax.ShapeDtypeStruct((2 * (t + blk) * rt, LANE), F32),
        compiler_params=_cparams(("arbitrary",)),
        name="moe_experts",
    )(*plan, h2, w_gate, w_up, w_down).reshape(2, (t + blk) * rt, LANE)


def _combine_kernel(x_ref, gt_ref, w_ref, yp_ref, o_ref):
    tm = x_ref.shape[0]
    w = w_ref[...]
    w0, w1 = w[:, 0:1], w[:, 1:2]
    moe = jnp.concatenate(
        [w0 * yp_ref[0, pl.ds(j, tm, stride=ROW_TILES), :]
         + w1 * yp_ref[1, pl.ds(j, tm, stride=ROW_TILES), :] for j in range(ROW_TILES)], axis=1)
    o_ref[...] = x_ref[...] + gt_ref[...] * moe


def _combine(x1, gt, w_pick, yp, tm):
    b, l, d = x1.shape
    tm = min(tm, l)
    nb = l // tm
    return pl.pallas_call(
        _combine_kernel,
        grid=(b, nb),
        in_specs=[pl.BlockSpec((None, tm, d), lambda bi, i: (bi, i, 0)),
                  pl.BlockSpec((None, 1, d), lambda bi, i: (bi, 0, 0)),
                  pl.BlockSpec((tm, ROUTER_W), lambda bi, i: (bi * nb + i, 0)),
                  pl.BlockSpec((2, tm * ROW_TILES, LANE), lambda bi, i: (0, bi * nb + i, 0))],
        out_specs=pl.BlockSpec((None, tm, d), lambda bi, i: (bi, i, 0)),
        out_shape=jax.ShapeDtypeStruct((b, l, d), F32),
        compiler_params=_cparams(("parallel", "parallel")),
        name="moe_combine",
    )(x1, gt, w_pick, yp)


def _dispatch_plan(eid):
    blk = MOE_BLOCK
    n_assign = eid.shape[0]
    n_tok = n_assign // 2
    n_blocks = (n_assign + blk - 1) // blk + N_EXPERTS
    order = jnp.argsort(eid).astype(jnp.int32)
    e_ids = jnp.arange(N_EXPERTS, dtype=jnp.int32)
    counts = jnp.sum((eid[None, :] == e_ids[:, None]).astype(jnp.int32), axis=1)
    start = jnp.cumsum(counts) - counts
    nblk = (counts + blk - 1) // blk
    bend = jnp.cumsum(nblk)
    b_ids = jnp.arange(n_blocks, dtype=jnp.int32)
    block_e = jnp.minimum(jnp.searchsorted(bend, b_ids, side='right'), N_EXPERTS - 1).astype(jnp.int32)
    used = b_ids < bend[-1]
    within = b_ids - (bend - nblk)[block_e]
    block_nv = jnp.where(used, jnp.clip(counts[block_e] - within * blk, 0, blk), 0).astype(jnp.int32)
    block_j0 = jnp.where(used, start[block_e] + within * blk, 0).astype(jnp.int32)
    first = jnp.concatenate([jnp.ones((1,), jnp.int32),
                             (block_e[1:] != block_e[:-1]).astype(jnp.int32)])
    w_slot = ((jnp.cumsum(first) - 1) % 2).astype(jnp.int32)
    nonempty = jnp.where(counts > 0, e_ids, N_EXPERTS)
    later = jnp.flip(lax.cummin(jnp.flip(nonempty)))
    next_e = jnp.concatenate([later[1:], jnp.full((1,), N_EXPERTS, jnp.int32)])
    next_e = jnp.where(next_e >= N_EXPERTS, -1, next_e)[block_e]
    tok_sorted = order // 2
    dst_sorted = (order % 2) * (n_tok + blk) + order // 2
    return (block_e, block_nv, first, next_e, w_slot, block_j0, tok_sorted, dst_sorted)


def _rope_tables(length, with_rope):
    half = MLA_ROPE // 4
    if with_rope:
        rows = length // GRID_W
        row = jnp.repeat(jnp.arange(rows), GRID_W).astype(F32)
        col = jnp.tile(jnp.arange(GRID_W), rows).astype(F32)
        inv = 1.0 / (ROPE_THETA ** (jnp.arange(half, dtype=F32) / half))
        ang_r = row[:, None] * inv
        ang_c = col[:, None] * inv
        cos = jnp.concatenate([jnp.cos(ang_r), jnp.cos(ang_r), jnp.cos(ang_c), jnp.cos(ang_c)], axis=1)
        sin = jnp.concatenate([-jnp.sin(ang_r), jnp.sin(ang_r), -jnp.sin(ang_c), jnp.sin(ang_c)], axis=1)
    else:
        cos = jnp.ones((length, MLA_ROPE), F32)
        sin = jnp.zeros((length, MLA_ROPE), F32)
    pad = ((0, 0), (0, LANE - MLA_ROPE))
    return jnp.pad(cos, pad), jnp.pad(sin, pad)


_IN_SIZES = (GLA_QK_W, GLA_QK_W, GLA_V_W, GLA_V_W, GLA_LOWRANK, GLA_LOWRANK,
             MLA_Q_RANK, MLA_KV_RANK, MLA_ROPE, D_MODEL, D_MODEL)
(SRC_Q, SRC_K, SRC_V, SRC_R, SRC_AF, SRC_AB, SRC_CQ, SRC_CKV, SRC_KR, SRC_GA, SRC_GB,
 SRC_END) = [sum(_IN_SIZES[:i]) for i in range(len(_IN_SIZES) + 1)]


def _w_in_layout_kernel(w_ref, lat_ref, ctx_ref):
    w = w_ref[...]

    def cols(a, b):
        return w[:, a:b].astype(BF16)

    rows = w.shape[0]
    small = jnp.concatenate([cols(SRC_KR, SRC_GA), cols(SRC_AF, SRC_CQ),
                             jnp.zeros((rows, SMALL_W - MLA_ROPE - 2 * GLA_LOWRANK), BF16)], axis=1)
    lat_ref[:, COL_Q:COL_GA] = cols(SRC_Q, SRC_AF)
    lat_ref[:, COL_GA:COL_CQ] = cols(SRC_GA, SRC_END)
    lat_ref[:, COL_CQ:COL_SMALL] = cols(SRC_CQ, SRC_KR)
    lat_ref[:, COL_SMALL:] = small
    ctx_ref[:, CCOL_K:CCOL_CKV] = cols(SRC_K, SRC_R)
    ctx_ref[:, CCOL_CKV:CCOL_SMALL] = cols(SRC_CKV, SRC_KR)
    ctx_ref[:, CCOL_SMALL:] = small


def _w_in_layout(w_in, tk=128):
    d, n = w_in.shape
    assert n == SRC_END
    return pl.pallas_call(
        _w_in_layout_kernel,
        grid=(d // tk,),
        in_specs=[pl.BlockSpec((tk, n), lambda i: (i, 0))],
        out_specs=[pl.BlockSpec((tk, NP_LAT), lambda i: (i, 0)),
                   pl.BlockSpec((tk, NP_CTX), lambda i: (i, 0))],
        out_shape=[jax.ShapeDtypeStruct((d, NP_LAT), BF16),
                   jax.ShapeDtypeStruct((d, NP_CTX), BF16)],
        compiler_params=_cparams(("parallel",)),
        name="w_in_layout",
    )(w_in)


def kernel(x, c, ctx, c_ctx, w_mod, b_mod, norm1_g, norm2_g, w_in, w_decay_f, b_decay_f, w_decay_b, b_decay_b, gla_norm_g, q_a_norm_g, w_uq, kv_a_norm_g, w_ukv, q_norm_g, k_norm_g, w_o_gla, w_o_mla, w_out, w_router_group, b_router_group, w_router_expert, b_router_expert, w_exp_gate, w_exp_up, w_exp_down):
    assert w_mod.shape[0] == 1, "single-layer block"
    b, l, d = x.shape
    lc = ctx.shape[1]
    t = b * l

    c_rows = jnp.concatenate([c, c_ctx[None, :], jnp.zeros((8 - b - 1, d), F32)], axis=0)
    mod = _modulation(c_rows, w_mod[0], b_mod[0])
    sh1, sc1, gt1, sh2, sc2, gt2 = [mod[:b, i * d:(i + 1) * d].reshape(b, 1, d) for i in range(6)]
    sh1c = jnp.broadcast_to(mod[b:b + 1, 0:d].reshape(1, 1, d), (b, 1, d))
    sc1c = jnp.broadcast_to(mod[b:b + 1, d:2 * d].reshape(1, 1, d), (b, 1, d))

    w_lat, w_ctx = _w_in_layout(w_in[0])
    g1 = norm1_g[0].reshape(1, d)
    p = _in_projection(x, g1, sc1, sh1, w_lat, tm=1024)
    pc = _in_projection(ctx, g1, sc1c, sh1c, w_ctx, tm=lc)

    def decay_w(w, off):
        return jnp.zeros((LANE, GLA_QK_W), F32).at[off:off + GLA_LOWRANK].set(w).astype(BF16)

    y_gla = _gla(p, pc,
                 decay_w(w_decay_f[0], SM_AF), b_decay_f[0].reshape(1, -1),
                 decay_w(w_decay_b[0], SM_AB), b_decay_b[0].reshape(1, -1),
                 gla_norm_g[0].reshape(1, -1))

    wq = jnp.pad(w_uq[0].reshape(MLA_Q_RANK, MLA_HEADS, MLA_QK_DIM),
                 ((0, 0), (0, 0), (0, MLA_QK_PAD - MLA_QK_DIM))).reshape(MLA_Q_RANK, -1).astype(BF16)
    wkv = w_ukv[0].astype(BF16)
    qn_g = jnp.pad(q_norm_g[0], (0, MLA_QK_PAD - MLA_QK_DIM)).reshape(1, -1)
    kn_nope = k_norm_g[0][:MLA_NOPE].reshape(1, -1)
    kn_rope = jnp.pad(k_norm_g[0][MLA_NOPE:], (0, LANE - MLA_ROPE)).reshape(1, -1)
    cos, sin = _rope_tables(l, True)
    cos_c, sin_c = _rope_tables(lc, False)
    q_m = _mla_queries(p, q_a_norm_g[0].reshape(1, -1), wq, qn_g, cos, sin, tm=512)
    kva = kv_a_norm_g[0].reshape(1, -1)
    k_m, v_m = _mla_keys_values(p, pc, kva, wkv, kn_nope, kn_rope,
                                jnp.concatenate([cos, cos_c]), jnp.concatenate([sin, sin_c]))
    y_mla = _attention(q_m, k_m, v_m, tq=512)

    y = _branch_merge(y_gla, y_mla, w_o_gla[0].astype(BF16), w_o_mla[0].astype(BF16), p, tm=1024, tn=512)
    w_router = jnp.concatenate(
        [w_router_group[0], w_router_expert[0],
         jnp.zeros((d, ROUTER_W - N_GROUPS - N_EXPERTS), F32)], axis=1).astype(BF16)
    b_router = jnp.concatenate(
        [b_router_group[0], b_router_expert[0],
         jnp.zeros((ROUTER_W - N_GROUPS - N_EXPERTS,), F32)]).reshape(1, -1)
    x1, h2, logits = _out_projection(y, w_out[0].astype(BF16), x, gt1, norm2_g[0].reshape(1, d),
                                     sc2, sh2, w_router, b_router, tm=256)

    e_pick, w_pick = _route(logits.reshape(t, ROUTER_W), tm=1024)
    plan = _dispatch_plan(e_pick[:, :2].reshape(-1))
    yp = _experts(plan, h2.reshape(t * ROW_TILES, LANE),
                  w_exp_gate[0], w_exp_up[0], w_exp_down[0])
    return _combine(x1, gt2, w_pick, yp, tm=512)
```

```python
import functools
import math

import jax
import jax.numpy as jnp
from jax import lax
from jax.experimental import pallas as pl
from jax.experimental.pallas import tpu as pltpu

F32 = jnp.float32
BF16 = jnp.bfloat16

D_MODEL = 2048
GRID_W = 64
EPS = 1e-6

GLA_HEADS = 4
GLA_DK = 256
GLA_DV = 512
GLA_LOWRANK = 16
GLA_TAU = 16.0
GLA_CHUNK = 64
GLA_GROUP = 8
GLA_QK_W = GLA_HEADS * GLA_DK
GLA_V_W = GLA_HEADS * GLA_DV

MLA_HEADS = 16
MLA_Q_RANK = 512
MLA_KV_RANK = 512
MLA_NOPE = 128
MLA_ROPE = 64
MLA_V = 128
MLA_QK_DIM = MLA_NOPE + MLA_ROPE
MLA_QK_PAD = 256
MLA_VT_ROWS = MLA_V + 16
ROPE_THETA = 10000.0
LOG2E = math.log2(math.e)

N_GROUPS = 8
EXP_PER_GROUP = 8
N_EXPERTS = N_GROUPS * EXP_PER_GROUP
D_EXPERT = 512
ROUTER_W = 128
MOE_BLOCK = 256
BULK_DMA = 1

LANE = 128
ROW_TILES = D_MODEL // LANE
ROW_PITCH = 20
VMEM_LIMIT = 56 * 1024 * 1024

COL_Q = 0
COL_K = COL_Q + GLA_QK_W
COL_V = COL_K + GLA_QK_W
COL_R = COL_V + GLA_V_W
COL_GA = COL_R + GLA_V_W
COL_GB = COL_GA + D_MODEL
COL_CQ = COL_GB + D_MODEL
COL_CKV = COL_CQ + MLA_Q_RANK
COL_SMALL = COL_CKV + MLA_KV_RANK
SMALL_W = 256
NP_LAT = COL_SMALL + SMALL_W
SM_AF = MLA_ROPE
SM_AB = MLA_ROPE + GLA_LOWRANK
CCOL_K = 0
CCOL_V = CCOL_K + GLA_QK_W
CCOL_CKV = CCOL_V + GLA_V_W
CCOL_SMALL = CCOL_CKV + MLA_KV_RANK
NP_CTX = CCOL_SMALL + SMALL_W
INPROJ_TN = 1280


def _cparams(sem):
    return pltpu.CompilerParams(dimension_semantics=sem, vmem_limit_bytes=VMEM_LIMIT)


def _silu(x):
    return x * jax.nn.sigmoid(x)


def _mod_kernel(c_ref, w_ref, b_ref, o_ref):
    a = _silu(c_ref[...]).astype(BF16)
    o_ref[...] = jnp.dot(a, w_ref[...].astype(BF16), preferred_element_type=F32) + b_ref[...]


def _modulation(c_rows, w_mod, b_mod):
    m, d = c_rows.shape
    n = w_mod.shape[1]
    tn = 1024
    return pl.pallas_call(
        _mod_kernel,
        grid=(n // tn,),
        in_specs=[pl.BlockSpec((m, d), lambda j: (0, 0)),
                  pl.BlockSpec((d, tn), lambda j: (0, j)),
                  pl.BlockSpec((1, tn), lambda j: (0, j))],
        out_specs=pl.BlockSpec((m, tn), lambda j: (0, j)),
        out_shape=jax.ShapeDtypeStruct((m, n), F32),
        compiler_params=_cparams(("parallel",)),
        name="modulation",
    )(c_rows, w_mod, b_mod.reshape(1, n))


def _inproj_kernel(x_ref, g_ref, sc_ref, sh_ref, w_ref, o_ref, h_ref):
    @pl.when(pl.program_id(2) == 0)
    def _():
        x = x_ref[...]
        ms = jnp.mean(x * x, axis=-1, keepdims=True)
        y = x * lax.rsqrt(ms + EPS) * g_ref[...]
        h_ref[...] = (y * (1.0 + sc_ref[...]) + sh_ref[...]).astype(BF16)

    o_ref[...] = lax.dot_general(h_ref[...], w_ref[...], (((1,), (1,)), ((), ())),
                                 preferred_element_type=F32).astype(o_ref.dtype)


def _in_projection(x, g, sc, sh, w, tm):
    b, l, d = x.shape
    tm = min(tm, l)
    n = w.shape[0]
    tn = INPROJ_TN
    return pl.pallas_call(
        _inproj_kernel,
        grid=(b, l // tm, n // tn),
        in_specs=[pl.BlockSpec((None, tm, d), lambda bi, i, j: (bi, i, 0)),
                  pl.BlockSpec((1, d), lambda bi, i, j: (0, 0)),
                  pl.BlockSpec((None, 1, d), lambda bi, i, j: (bi, 0, 0)),
                  pl.BlockSpec((None, 1, d), lambda bi, i, j: (bi, 0, 0)),
                  pl.BlockSpec((tn, d), lambda bi, i, j: (j, 0))],
        out_specs=pl.BlockSpec((None, tm, tn), lambda bi, i, j: (bi, i, j)),
        out_shape=jax.ShapeDtypeStruct((b, l, n), BF16),
        scratch_shapes=[pltpu.VMEM((tm, d), BF16)],
        compiler_params=_cparams(("parallel", "parallel", "arbitrary")),
        name="in_projection",
    )(x, g, sc, sh, w)


def _gla_kernel(q_ref, k_ref, v_ref, r_ref, sm_ref, kc_ref, vc_ref, smc_ref,
                wf_ref, bf_ref, wb_ref, bb_ref, gn_ref, y_ref, o_ref, sf_ref, sb_ref):
    c = GLA_CHUNK
    n_lat = q_ref.shape[0] // c
    n_ctx = kc_ref.shape[0] // c
    row = lax.broadcasted_iota(jnp.int32, (c, c), 0)
    col = lax.broadcasted_iota(jnp.int32, (c, c), 1)
    lower = row >= col
    upper = row <= col
    nt = (((1,), (1,)), ((), ()))
    tn = (((0,), (0,)), ((), ()))

    def cum_decay(sm, w_ref, b_ref, tri, n_chunks):
        z = jnp.dot(sm, w_ref[...], preferred_element_type=F32) + b_ref[...]
        g = (jnp.minimum(z, 0.0) - jnp.log(1.0 + jnp.exp(-jnp.abs(z)))) * (1.0 / GLA_TAU)
        g_hi = g.astype(BF16)
        g_lo = (g - g_hi.astype(F32)).astype(BF16)
        t = jnp.where(tri, 1.0, 0.0).astype(BF16)
        return [jnp.dot(t, g_hi[i * c:(i + 1) * c], preferred_element_type=F32)
                + jnp.dot(t, g_lo[i * c:(i + 1) * c], preferred_element_type=F32)
                for i in range(n_chunks)]

    fwd = (wf_ref, bf_ref, lower, c - 1, sf_ref)
    bwd = (wb_ref, bb_ref, upper, 0, sb_ref)

    def group(r0, n_chunks, direction, sm_r, k_r, v_r, q_r):
        w_ref, b_ref, tri, end_row, st_ref = direction
        rows = n_chunks * c
        k_all = k_r[pl.ds(r0, rows), :].astype(F32)
        v_all = v_r[pl.ds(r0, rows), :]
        q_all = None if q_r is None else q_r[pl.ds(r0, rows), :].astype(F32) * (GLA_DK ** -0.5)
        bc_all = cum_decay(sm_r[pl.ds(r0, rows), :], w_ref, b_ref, tri, n_chunks)
        local = []
        for i in range(n_chunks):
            sl = slice(i * c, (i + 1) * c)
            bc = bc_all[i]
            bend = bc[end_row:end_row + 1, :]
            k, v = k_all[sl], v_all[sl]
            ke = (k * jnp.exp(bend - bc)).astype(BF16)
            upd = lax.dot_general(v, ke, tn, preferred_element_type=F32)
            qd = o_intra = None
            if q_all is not None:
                qd = (q_all[sl] * jnp.exp(bc)).astype(BF16)
                ki = (k * jnp.exp(-bc)).astype(BF16)
                att = lax.dot_general(qd, ki, nt, preferred_element_type=F32)
                att = jnp.where(tri, att, 0.0).astype(BF16)
                o_intra = jnp.dot(att, v, preferred_element_type=F32)
            local.append((jnp.exp(bend), upd, qd, o_intra))
        st = st_ref[...]
        outs = [None] * n_chunks
        for i in (range(n_chunks) if end_row else reversed(range(n_chunks))):
            decay, upd, qd, o_intra = local[i]
            if q_all is not None:
                outs[i] = o_intra + lax.dot_general(qd, st.astype(BF16), nt, preferred_element_type=F32)
            st = st * decay + upd
        st_ref[...] = st
        return None if q_all is None else jnp.concatenate(outs, axis=0)

    def readout(r0, o):
        ms = jnp.mean(o * o, axis=-1, keepdims=True)
        yn = o * lax.rsqrt(ms + EPS) * gn_ref[...]
        r = r_ref[pl.ds(r0, o.shape[0]), :].astype(F32)
        y_ref[pl.ds(r0, o.shape[0]), :] = (yn * _silu(r)).astype(y_ref.dtype)

    sf_ref[...] = jnp.zeros_like(sf_ref)
    sb_ref[...] = jnp.zeros_like(sb_ref)
    group(0, n_ctx, fwd, smc_ref, kc_ref, vc_ref, None)
    group(0, n_ctx, bwd, smc_ref, kc_ref, vc_ref, None)

    g_lat = min(GLA_GROUP, n_lat // 2)
    assert n_lat % (2 * g_lat) == 0
    n_groups = n_lat // g_lat
    rows = g_lat * c

    def scan(i, direction):
        r0 = pl.multiple_of(i * rows, rows)
        return r0, group(r0, g_lat, direction, sm_ref, k_ref, v_ref, q_ref)

    def first_half(i, carry):
        for r0, o in (scan(i, fwd), scan(n_groups - 1 - i, bwd)):
            o_ref[pl.ds(r0, rows), :] = o
        return carry

    lax.fori_loop(0, n_groups // 2, first_half, 0)

    def second_half(i, carry):
        for r0, o in (scan(i, fwd), scan(n_groups - 1 - i, bwd)):
            readout(r0, o_ref[pl.ds(r0, rows), :] + o)
        return carry

    lax.fori_loop(n_groups // 2, n_groups, second_half, 0)


def _gla(p, pc, wf, bf, wb, bb, gn):
    b, l, _ = p.shape
    lc = pc.shape[1]
    dk, dv = GLA_DK, GLA_DV
    hmap = lambda off: (lambda bi, h: (bi, 0, off + h))
    wmap = lambda bi, h: (0, h)
    return pl.pallas_call(
        _gla_kernel,
        grid=(b, GLA_HEADS),
        in_specs=[pl.BlockSpec((None, l, dk), hmap(COL_Q // dk)),
                  pl.BlockSpec((None, l, dk), hmap(COL_K // dk)),
                  pl.BlockSpec((None, l, dv), hmap(COL_V // dv)),
                  pl.BlockSpec((None, l, dv), hmap(COL_R // dv)),
                  pl.BlockSpec((None, l, LANE), lambda bi, h: (bi, 0, COL_SMALL // LANE)),
                  pl.BlockSpec((None, lc, dk), hmap(CCOL_K // dk)),
                  pl.BlockSpec((None, lc, dv), hmap(CCOL_V // dv)),
                  pl.BlockSpec((None, lc, LANE), lambda bi, h: (bi, 0, CCOL_SMALL // LANE)),
                  pl.BlockSpec((LANE, dk), wmap),
                  pl.BlockSpec((1, dk), wmap),
                  pl.BlockSpec((LANE, dk), wmap),
                  pl.BlockSpec((1, dk), wmap),
                  pl.BlockSpec((1, dv), lambda bi, h: (0, 0))],
        out_specs=pl.BlockSpec((None, l, dv), lambda bi, h: (bi, 0, h)),
        out_shape=jax.ShapeDtypeStruct((b, l, GLA_V_W), BF16),
        scratch_shapes=[pltpu.VMEM((l, dv), F32), pltpu.VMEM((dv, dk), F32), pltpu.VMEM((dv, dk), F32)],
        compiler_params=_cparams(("parallel", "parallel")),
        name="gla",
    )(p, p, p, p, p, pc, pc, pc, wf, bf, wb, bb, gn)


def _rope_partner(t):
    lane = lax.broadcasted_iota(jnp.int32, t.shape, 1)
    first = (lane % 32) < 16
    return jnp.where(first, pltpu.roll(t, LANE - 16, 1), pltpu.roll(t, 16, 1))


def _mla_q_kernel(cq_ref, ga_ref, w_ref, gn_ref, cos_ref, sin_ref, o_ref):
    cq = cq_ref[...].astype(F32)
    cn = (cq * lax.rsqrt(jnp.mean(cq * cq, axis=-1, keepdims=True) + EPS) * ga_ref[...]).astype(BF16)
    cos = cos_ref[...]
    sin = sin_ref[...]
    scale = MLA_QK_DIM ** -0.5 * LOG2E
    for h in range(MLA_HEADS):
        qh = jnp.dot(cn, w_ref[:, h * MLA_QK_PAD:(h + 1) * MLA_QK_PAD], preferred_element_type=F32)
        ms = jnp.sum(qh * qh, axis=-1, keepdims=True) * (1.0 / MLA_QK_DIM)
        qn = qh * lax.rsqrt(ms + EPS) * gn_ref[...]
        t = qn[:, MLA_NOPE:]
        rot = t * cos + _rope_partner(t) * sin
        o_ref[h, :, :MLA_NOPE] = (qn[:, :MLA_NOPE] * scale).astype(o_ref.dtype)
        o_ref[h, :, MLA_NOPE:] = (rot * scale).astype(o_ref.dtype)


def _mla_queries(p, ga, wq, gn, cos, sin, tm):
    b, l, _ = p.shape
    tm = min(tm, l)
    return pl.pallas_call(
        _mla_q_kernel,
        grid=(b, l // tm),
        in_specs=[pl.BlockSpec((None, tm, MLA_Q_RANK), lambda bi, i: (bi, i, COL_CQ // MLA_Q_RANK)),
                  pl.BlockSpec((1, MLA_Q_RANK), lambda bi, i: (0, 0)),
                  pl.BlockSpec(wq.shape, lambda bi, i: (0, 0)),
                  pl.BlockSpec((1, MLA_QK_PAD), lambda bi, i: (0, 0)),
                  pl.BlockSpec((tm, LANE), lambda bi, i: (i, 0)),
                  pl.BlockSpec((tm, LANE), lambda bi, i: (i, 0))],
        out_specs=pl.BlockSpec((None, MLA_HEADS, tm, MLA_QK_PAD), lambda bi, i: (bi, 0, i, 0)),
        out_shape=jax.ShapeDtypeStruct((b, MLA_HEADS, l, MLA_QK_PAD), BF16),
        compiler_params=_cparams(("parallel", "parallel")),
        name="mla_queries",
    )(p, ga, wq, gn, cos, sin)


def _mla_kv_kernel(ckv_ref, sm_ref, ckvc_ref, smc_ref, ga_ref, w_ref, gk_ref, gr_ref,
                   cos_ref, sin_ref, k_ref, v_ref, *, n_lat):
    is_ctx = pl.program_id(1) >= n_lat
    ckv = jnp.where(is_ctx, ckvc_ref[...], ckv_ref[...]).astype(F32)
    cn = (ckv * lax.rsqrt(jnp.mean(ckv * ckv, axis=-1, keepdims=True) + EPS) * ga_ref[...]).astype(BF16)
    sm = jnp.where(is_ctx, smc_ref[...], sm_ref[...]).astype(F32)
    lane = lax.broadcasted_iota(jnp.int32, sm.shape, 1)
    kr = jnp.where(lane < MLA_ROPE, sm, 0.0)
    ss_r = jnp.sum(kr * kr, axis=-1, keepdims=True)
    krg = kr * gr_ref[...]
    rot = krg * cos_ref[...] + _rope_partner(krg) * sin_ref[...]
    hw = MLA_NOPE + MLA_V
    tm = ckv.shape[0]
    sub = lax.broadcasted_iota(jnp.int32, (MLA_VT_ROWS - MLA_V, tm), 0)
    ones_rows = jnp.where(sub == 0, 1.0, 0.0).astype(v_ref.dtype)
    for h in range(MLA_HEADS):
        kvh = jnp.dot(cn, w_ref[:, h * hw:(h + 1) * hw], preferred_element_type=F32)
        kn = kvh[:, :MLA_NOPE]
        ms = (jnp.sum(kn * kn, axis=-1, keepdims=True) + ss_r) * (1.0 / MLA_QK_DIM)
        rs = lax.rsqrt(ms + EPS)
        k_ref[h, :, :MLA_NOPE] = (kn * rs * gk_ref[...]).astype(k_ref.dtype)
        k_ref[h, :, MLA_NOPE:] = (rot * rs).astype(k_ref.dtype)
        v_ref[h, :MLA_V, :] = kvh[:, MLA_NOPE:].T.astype(v_ref.dtype)
        v_ref[h, MLA_V:, :] = ones_rows


def _mla_keys_values(p, pc, ga, wkv, gk, gr, cos, sin):
    b, l, _ = p.shape
    lc = pc.shape[1]
    tm = min(lc, l)
    assert l % tm == 0 and lc % tm == 0
    n_lat, n_ctx = l // tm, lc // tm
    lat = lambda blk: (lambda bi, i: (bi, jnp.minimum(i, n_lat - 1), blk))
    ctx = lambda blk: (lambda bi, i: (bi, jnp.maximum(i - n_lat, 0), blk))
    const = lambda bi, i: (0, 0)
    k_spec = pl.BlockSpec((None, MLA_HEADS, tm, MLA_QK_PAD), lambda bi, i: (bi, 0, i, 0))
    vt_spec = pl.BlockSpec((None, MLA_HEADS, MLA_VT_ROWS, tm), lambda bi, i: (bi, 0, 0, i))
    k_shape = jax.ShapeDtypeStruct((b, MLA_HEADS, l + lc, MLA_QK_PAD), BF16)
    vt_shape = jax.ShapeDtypeStruct((b, MLA_HEADS, MLA_VT_ROWS, l + lc), BF16)
    return pl.pallas_call(
        functools.partial(_mla_kv_kernel, n_lat=n_lat),
        grid=(b, n_lat + n_ctx),
        in_specs=[pl.BlockSpec((None, tm, MLA_KV_RANK), lat(COL_CKV // MLA_KV_RANK)),
                  pl.BlockSpec((None, tm, LANE), lat(COL_SMALL // LANE)),
                  pl.BlockSpec((None, tm, MLA_KV_RANK), ctx(CCOL_CKV // MLA_KV_RANK)),
                  pl.BlockSpec((None, tm, LANE), ctx(CCOL_SMALL // LANE)),
                  pl.BlockSpec((1, MLA_KV_RANK), const),
                  pl.BlockSpec(wkv.shape, const),
                  pl.BlockSpec((1, LANE), const),
                  pl.BlockSpec((1, LANE), const),
                  pl.BlockSpec((tm, LANE), lambda bi, i: (i, 0)),
                  pl.BlockSpec((tm, LANE), lambda bi, i: (i, 0))],
        out_specs=[k_spec, vt_spec],
        out_shape=[k_shape, vt_shape],
        compiler_params=_cparams(("parallel", "arbitrary")),
        name="mla_keys_values",
    )(p, p, pc, pc, ga, wkv, gk, gr, cos, sin)


def _attn_kernel(q_ref, k_ref, vt_ref, o_ref, s0_ref, s1_ref, m0_ref, m1_ref, *, tq):
    nt = (((1,), (1,)), ((), ()))
    n = q_ref.shape[0] // tq
    bufs = ((s0_ref, m0_ref), (s1_ref, m1_ref))

    def scores(i, slot):
        s_ref, m_ref = bufs[slot]
        r0 = pl.multiple_of(i * tq, tq)
        s = lax.dot_general(k_ref[...], q_ref[pl.ds(r0, tq), :], nt, preferred_element_type=F32)
        s_ref[...] = s
        m_ref[...] = jnp.max(s, axis=0, keepdims=True)

    def outputs(i, slot):
        s_ref, m_ref = bufs[slot]
        r0 = pl.multiple_of(i * tq, tq)
        p = jnp.exp2(s_ref[...] - m_ref[...]).astype(BF16)
        o = jnp.dot(vt_ref[...], p, preferred_element_type=F32)
        o_ref[pl.ds(r0, tq), :] = (o[:MLA_V] / o[MLA_V:MLA_V + 1]).T.astype(o_ref.dtype)

    scores(0, 0)
    if n % 2 == 0:
        def body(j, carry):
            scores(2 * j + 1, 1)
            outputs(2 * j, 0)
            scores(2 * j + 2, 0)
            outputs(2 * j + 1, 1)
            return carry

        lax.fori_loop(0, n // 2 - 1, body, 0)
        scores(n - 1, 1)
        outputs(n - 2, 0)
        outputs(n - 1, 1)
    else:
        assert n == 1
        outputs(0, 0)


def _attention(q, k, v, tq):
    b, h, l, dq = q.shape
    tq = min(tq, l)
    lk = k.shape[2]
    bh = lambda bi, hi: (bi, hi, 0, 0)
    return pl.pallas_call(
        functools.partial(_attn_kernel, tq=tq),
        grid=(b, h),
        in_specs=[pl.BlockSpec((None, None, l, dq), bh),
                  pl.BlockSpec((None, None, lk, dq), bh),
                  pl.BlockSpec((None, None, MLA_VT_ROWS, lk), bh)],
        out_specs=pl.BlockSpec((None, l, MLA_V), lambda bi, hi: (bi, 0, hi)),
        out_shape=jax.ShapeDtypeStruct((b, l, h * MLA_V), BF16),
        scratch_shapes=[pltpu.VMEM((lk, tq), F32), pltpu.VMEM((lk, tq), F32),
                        pltpu.VMEM((1, tq), F32), pltpu.VMEM((1, tq), F32)],
        compiler_params=_cparams(("parallel", "parallel")),
        name="mla_attention",
    )(q, k, v)


def _merge_kernel(yg_ref, ym_ref, wg_ref, wm_ref, ga_ref, gb_ref, o_ref):
    a = jnp.dot(yg_ref[...], wg_ref[...], preferred_element_type=F32)
    m = jnp.dot(ym_ref[...], wm_ref[...], preferred_element_type=F32)
    y = (jax.nn.sigmoid(ga_ref[...].astype(F32)) * a
         + jax.nn.sigmoid(gb_ref[...].astype(F32)) * m)
    o_ref[...] = y.astype(o_ref.dtype)


def _branch_merge(yg, ym, wg, wm, p, tm, tn):
    b, l, d = yg.shape
    tm = min(tm, l)
    n = wg.shape[1]
    return pl.pallas_call(
        _merge_kernel,
        grid=(b, l // tm, n // tn),
        in_specs=[pl.BlockSpec((None, tm, d), lambda bi, i, j: (bi, i, 0)),
                  pl.BlockSpec((None, tm, d), lambda bi, i, j: (bi, i, 0)),
                  pl.BlockSpec((d, tn), lambda bi, i, j: (0, j)),
                  pl.BlockSpec((d, tn), lambda bi, i, j: (0, j)),
                  pl.BlockSpec((None, tm, tn), lambda bi, i, j: (bi, i, COL_GA // tn + j)),
                  pl.BlockSpec((None, tm, tn), lambda bi, i, j: (bi, i, COL_GB // tn + j))],
        out_specs=pl.BlockSpec((None, tm, tn), lambda bi, i, j: (bi, i, j)),
        out_shape=jax.ShapeDtypeStruct((b, l, n), BF16),
        compiler_params=_cparams(("parallel", "parallel", "arbitrary")),
        name="branch_merge",
    )(yg, ym, wg, wm, p, p)


def _outproj_kernel(y_ref, w_ref, x_ref, gt_ref, g2_ref, sc_ref, sh_ref, wr_ref, br_ref,
                    x1_ref, h2_ref, lg_ref):
    x1 = x_ref[...] + gt_ref[...] * jnp.dot(y_ref[...], w_ref[...], preferred_element_type=F32)
    x1_ref[...] = x1
    ms = jnp.mean(x1 * x1, axis=-1, keepdims=True)
    h2 = x1 * lax.rsqrt(ms + EPS) * g2_ref[...] * (1.0 + sc_ref[...]) + sh_ref[...]
    tm = x1.shape[0]
    for j in range(ROW_TILES):
        h2_ref[pl.ds(j, tm, stride=ROW_TILES), :] = h2[:, j * LANE:(j + 1) * LANE]
    lg_ref[...] = jnp.dot(h2.astype(BF16), wr_ref[...], preferred_element_type=F32) + br_ref[...]


def _out_projection(y, w, x, gt, g2, sc, sh, wr, br, tm):
    b, l, d = x.shape
    tm = min(tm, l)
    row = lambda bi, i: (bi, i, 0)
    per_b = lambda bi, i: (bi, 0, 0)
    const = lambda bi, i: (0, 0)
    return pl.pallas_call(
        _outproj_kernel,
        grid=(b, l // tm),
        in_specs=[pl.BlockSpec((None, tm, d), row),
                  pl.BlockSpec((d, d), const),
                  pl.BlockSpec((None, tm, d), row),
                  pl.BlockSpec((None, 1, d), per_b),
                  pl.BlockSpec((1, d), const),
                  pl.BlockSpec((None, 1, d), per_b),
                  pl.BlockSpec((None, 1, d), per_b),
                  pl.BlockSpec((d, ROUTER_W), const),
                  pl.BlockSpec((1, ROUTER_W), const)],
        out_specs=[pl.BlockSpec((None, tm, d), row),
                   pl.BlockSpec((None, tm * ROW_TILES, LANE), row),
                   pl.BlockSpec((None, tm, ROUTER_W), row)],
        out_shape=[jax.ShapeDtypeStruct((b, l, d), F32),
                   jax.ShapeDtypeStruct((b, l * ROW_TILES, LANE), F32),
                   jax.ShapeDtypeStruct((b, l, ROUTER_W), F32)],
        compiler_params=_cparams(("parallel", "parallel")),
        name="out_projection",
    )(y, w, x, gt, g2, sc, sh, wr, br)


def _route_kernel(lg_ref, e_ref, w_ref):
    lg = lg_ref[...]
    lane = lax.broadcasted_iota(jnp.int32, lg.shape, 1)
    neg = jnp.float32(-jnp.inf)
    big = jnp.int32(1 << 20)
    is_g = lane < N_GROUPS
    gl = jnp.where(is_g, lg, neg)
    gm = jnp.max(gl, axis=-1, keepdims=True)
    gidx = jnp.min(jnp.where(gl == gm, lane, big), axis=-1, keepdims=True)
    gsum = jnp.sum(jnp.where(is_g, jnp.exp(gl - gm), 0.0), axis=-1, keepdims=True)
    p_g = 1.0 / gsum
    g_lo = N_GROUPS + gidx * EXP_PER_GROUP
    in_grp = (lane >= g_lo) & (lane < g_lo + EXP_PER_GROUP)
    el = jnp.where(in_grp, lg, neg)
    em = jnp.max(el, axis=-1, keepdims=True)
    z = jnp.sum(jnp.where(in_grp, jnp.exp(el - em), 0.0), axis=-1, keepdims=True)
    i1 = jnp.min(jnp.where(el == em, lane, big), axis=-1, keepdims=True)
    el2 = jnp.where(lane == i1, neg, el)
    em2 = jnp.max(el2, axis=-1, keepdims=True)
    i2 = jnp.min(jnp.where(el2 == em2, lane, big), axis=-1, keepdims=True)
    p1 = 1.0 / z
    p2 = jnp.exp(em2 - em) / z
    tot = p1 + p2
    e_ref[...] = jnp.where(lane == 0, i1 - N_GROUPS, jnp.where(lane == 1, i2 - N_GROUPS, 0))
    w_ref[...] = jnp.where(lane == 0, p_g * p1 / tot, jnp.where(lane == 1, p_g * p2 / tot, 0.0))


def _route(logits, tm):
    t = logits.shape[0]
    tm = min(tm, t)
    spec = pl.BlockSpec((tm, ROUTER_W), lambda i: (i, 0))
    return pl.pallas_call(
        _route_kernel,
        grid=(t // tm,),
        in_specs=[spec],
        out_specs=[spec, spec],
        out_shape=[jax.ShapeDtypeStruct((t, ROUTER_W), jnp.int32),
                   jax.ShapeDtypeStruct((t, ROUTER_W), F32)],
        compiler_params=_cparams(("parallel",)),
        name="moe_route",
    )(logits)


def _expert_kernel(be_ref, nv_ref, first_ref, nxe_ref, ws_ref, j0_ref, tok_ref, dst_ref,
                   h_hbm, wg_hbm, wu_hbm, wd_hbm, yp_hbm,
                   xbuf0, xbuf1, ybuf0, ybuf1, wgv, wuv, wdv, sem_in, sem_out, sem_w, sem_init):
    i = pl.program_id(0)
    n = pl.num_programs(0)
    blk = MOE_BLOCK
    last = tok_ref.shape[0] - 1
    rt, pitch = ROW_TILES, ROW_PITCH
    plane = yp_hbm.shape[0] // (2 * rt)
    t_rows = plane - blk
    xbufs = (xbuf0, xbuf1)
    ybufs = (ybuf0, ybuf1)

    def weight_copies(e, s):
        return (pltpu.make_async_copy(wg_hbm.at[e], wgv.at[s], sem_w.at[s, 0]),
                pltpu.make_async_copy(wu_hbm.at[e], wuv.at[s], sem_w.at[s, 1]),
                pltpu.make_async_copy(wd_hbm.at[e], wdv.at[s], sem_w.at[s, 2]))

    def gather_start(b, s):
        base = j0_ref[b]
        for r in range(blk):
            row0 = pl.multiple_of(tok_ref[jnp.minimum(base + r, last)] * rt, rt)
            pltpu.make_async_copy(h_hbm.at[pl.ds(row0, rt)],
                                  xbufs[s].at[pl.ds(r * pitch, rt)], sem_in.at[s]).start()

    def gather_wait(s):
        pltpu.make_async_copy(h_hbm.at[pl.ds(0, blk * rt)], xbufs[s].at[pl.ds(0, blk * rt)],
                              sem_in.at[s]).wait()

    def scatter_start(b, s):
        base = j0_ref[b]
        filled = nv_ref[b]
        for r in range(blk):
            dst = jnp.where(r < filled, dst_ref[jnp.minimum(base + r, last)], t_rows + r)
            row0 = pl.multiple_of(dst * rt, rt)
            pltpu.make_async_copy(ybufs[s].at[pl.ds(r * pitch, rt)],
                                  yp_hbm.at[pl.ds(row0, rt)], sem_out.at[s]).start(priority=BULK_DMA)

    def scatter_wait(s):
        pltpu.make_async_copy(ybufs[s].at[pl.ds(0, blk * rt)], yp_hbm.at[pl.ds(0, blk * rt)],
                              sem_out.at[s]).wait()

    @pl.when(i == 0)
    def _():
        for cp in weight_copies(be_ref[0], 0):
            cp.start(priority=BULK_DMA)
        gather_start(0, 0)
        ybuf1[...] = jnp.zeros_like(ybuf1)
        zeros = ybuf1.at[pl.ds(0, blk * rt)]
        pltpu.make_async_copy(zeros, yp_hbm.at[pl.ds(t_rows * rt, blk * rt)], sem_out.at[0]).start()
        spare = pltpu.make_async_copy(zeros, yp_hbm.at[pl.ds((plane + t_rows) * rt, blk * rt)], sem_init)
        spare.start()
        spare.wait()

    valid = nv_ref[i] > 0
    nxt = jnp.minimum(i + 1, n - 1)
    has_next = jnp.logical_and(i + 1 < n, nv_ref[nxt] > 0)

    @pl.when(jnp.logical_and(valid, first_ref[i] == 1))
    def _():
        ws = ws_ref[i]
        for cp in weight_copies(be_ref[i], ws):
            cp.wait()

        @pl.when(nxe_ref[i] >= 0)
        def _():
            for cp in weight_copies(nxe_ref[i], 1 - ws):
                cp.start(priority=BULK_DMA)

    for par in (0, 1):
        @pl.when(jnp.logical_and(valid, i % 2 == par))
        def _(par=par):
            gather_wait(par)
            scatter_wait(par)
            gather_start(nxt, 1 - par)
            scatter_start(jnp.maximum(i - 1, 0), 1 - par)

    for par in (0, 1):
        @pl.when(jnp.logical_and(valid, i % 2 == par))
        def _(par=par):
            ws = ws_ref[i]
            x = jnp.concatenate([xbufs[par][pl.ds(j, blk, stride=pitch), :] for j in range(rt)],
                                axis=1).astype(BF16)
            g = jnp.dot(x, wgv[ws].astype(BF16), preferred_element_type=F32)
            u = jnp.dot(x, wuv[ws].astype(BF16), preferred_element_type=F32)
            a = (_silu(g) * u).astype(BF16)
            y = jnp.dot(a, wdv[ws].astype(BF16), preferred_element_type=F32)
            for j in range(rt):
                ybufs[par][pl.ds(j, blk, stride=pitch), :] = y[:, j * LANE:(j + 1) * LANE]

            @pl.when(jnp.logical_not(has_next))
            def _():
                scatter_wait(1 - par)
                scatter_start(i, par)
                gather_wait(1 - par)
                scatter_wait(par)


def _experts(plan, h2, w_gate, w_up, w_down):
    d = w_gate.shape[1]
    rt, pitch = ROW_TILES, ROW_PITCH
    assert d == rt * LANE
    t = h2.shape[0] // rt
    n_blocks = plan[0].shape[0]
    blk = MOE_BLOCK
    de = w_gate.shape[2]
    any_spec = pl.BlockSpec(memory_space=pl.ANY)
    return pl.pallas_call(
        _expert_kernel,
        grid_spec=pltpu.PrefetchScalarGridSpec(
            num_scalar_prefetch=len(plan),
            grid=(n_blocks,),
            in_specs=[any_spec, any_spec, any_spec, any_spec],
            out_specs=any_spec,
            scratch_shapes=[pltpu.VMEM((blk * pitch, LANE), F32), pltpu.VMEM((blk * pitch, LANE), F32),
                            pltpu.VMEM((blk * pitch, LANE), F32), pltpu.VMEM((blk * pitch, LANE), F32),
                            pltpu.VMEM((2, d, de), F32),
                            pltpu.VMEM((2, d, de), F32),
                            pltpu.VMEM((2, de, d), F32),
                            pltpu.SemaphoreType.DMA((2,)),
                            pltpu.SemaphoreType.DMA((2,)),
                            pltpu.SemaphoreType.DMA((2, 3)),
                            pltpu.SemaphoreType.DMA],
        ),
        out_shape=jax.ShapeDtypeStruct((2 * (t + blk) * rt, LANE), F32),
        compiler_params=_cparams(("arbitrary",)),
        name="moe_experts",
    )(*plan, h2, w_gate, w_up, w_down).reshape(2, (t + blk) * rt, LANE)


def _combine_kernel(x_ref, gt_ref, w_ref, yp_ref, o_ref):
    tm = x_ref.shape[0]
    w = w_ref[...]
    w0, w1 = w[:, 0:1], w[:, 1:2]
    moe = jnp.concatenate(
        [w0 * yp_ref[0, pl.ds(j, tm, stride=ROW_TILES), :]
         + w1 * yp_ref[1, pl.ds(j, tm, stride=ROW_TILES), :] for j in range(ROW_TILES)], axis=1)
    o_ref[...] = x_ref[...] + gt_ref[...] * moe


def _combine(x1, gt, w_pick, yp, tm):
    b, l, d = x1.shape
    tm = min(tm, l)
    nb = l // tm
    return pl.pallas_call(
        _combine_kernel,
        grid=(b, nb),
        in_specs=[pl.BlockSpec((None, tm, d), lambda bi, i: (bi, i, 0)),
                  pl.BlockSpec((None, 1, d), lambda bi, i: (bi, 0, 0)),
                  pl.BlockSpec((tm, ROUTER_W), lambda bi, i: (bi * nb + i, 0)),
                  pl.BlockSpec((2, tm * ROW_TILES, LANE), lambda bi, i: (0, bi * nb + i, 0))],
        out_specs=pl.BlockSpec((None, tm, d), lambda bi, i: (bi, i, 0)),
        out_shape=jax.ShapeDtypeStruct((b, l, d), F32),
        compiler_params=_cparams(("parallel", "parallel")),
        name="moe_combine",
    )(x1, gt, w_pick, yp)


def _dispatch_plan(eid):
    blk = MOE_BLOCK
    n_assign = eid.shape[0]
    n_tok = n_assign // 2
    n_blocks = (n_assign + blk - 1) // blk + N_EXPERTS
    order = jnp.argsort(eid).astype(jnp.int32)
    e_ids = jnp.arange(N_EXPERTS, dtype=jnp.int32)
    counts = jnp.sum((eid[None, :] == e_ids[:, None]).astype(jnp.int32), axis=1)
    start = jnp.cumsum(counts) - counts
    nblk = (counts + blk - 1) // blk
    bend = jnp.cumsum(nblk)
    b_ids = jnp.arange(n_blocks, dtype=jnp.int32)
    block_e = jnp.minimum(jnp.searchsorted(bend, b_ids, side='right'), N_EXPERTS - 1).astype(jnp.int32)
    used = b_ids < bend[-1]
    within = b_ids - (bend - nblk)[block_e]
    block_nv = jnp.where(used, jnp.clip(counts[block_e] - within * blk, 0, blk), 0).astype(jnp.int32)
    block_j0 = jnp.where(used, start[block_e] + within * blk, 0).astype(jnp.int32)
    first = jnp.concatenate([jnp.ones((1,), jnp.int32),
                             (block_e[1:] != block_e[:-1]).astype(jnp.int32)])
    w_slot = ((jnp.cumsum(first) - 1) % 2).astype(jnp.int32)
    nonempty = jnp.where(counts > 0, e_ids, N_EXPERTS)
    later = jnp.flip(lax.cummin(jnp.flip(nonempty)))
    next_e = jnp.concatenate([later[1:], jnp.full((1,), N_EXPERTS, jnp.int32)])
    next_e = jnp.where(next_e >= N_EXPERTS, -1, next_e)[block_e]
    tok_sorted = order // 2
    dst_sorted = (order % 2) * (n_tok + blk) + order // 2
    return (block_e, block_nv, first, next_e, w_slot, block_j0, tok_sorted, dst_sorted)


def _rope_tables(length, with_rope):
    half = MLA_ROPE // 4
    if with_rope:
        rows = length // GRID_W
        row = jnp.repeat(jnp.arange(rows), GRID_W).astype(F32)
        col = jnp.tile(jnp.arange(GRID_W), rows).astype(F32)
        inv = 1.0 / (ROPE_THETA ** (jnp.arange(half, dtype=F32) / half))
        ang_r = row[:, None] * inv
        ang_c = col[:, None] * inv
        cos = jnp.concatenate([jnp.cos(ang_r), jnp.cos(ang_r), jnp.cos(ang_c), jnp.cos(ang_c)], axis=1)
        sin = jnp.concatenate([-jnp.sin(ang_r), jnp.sin(ang_r), -jnp.sin(ang_c), jnp.sin(ang_c)], axis=1)
    else:
        cos = jnp.ones((length, MLA_ROPE), F32)
        sin = jnp.zeros((length, MLA_ROPE), F32)
    pad = ((0, 0), (0, LANE - MLA_ROPE))
    return jnp.pad(cos, pad), jnp.pad(sin, pad)


_IN_SIZES = (GLA_QK_W, GLA_QK_W, GLA_V_W, GLA_V_W, GLA_LOWRANK, GLA_LOWRANK,
             MLA_Q_RANK, MLA_KV_RANK, MLA_ROPE, D_MODEL, D_MODEL)
(SRC_Q, SRC_K, SRC_V, SRC_R, SRC_AF, SRC_AB, SRC_CQ, SRC_CKV, SRC_KR, SRC_GA, SRC_GB,
 SRC_END) = [sum(_IN_SIZES[:i]) for i in range(len(_IN_SIZES) + 1)]


def _w_in_layout_kernel(w_ref, lat_ref, ctx_ref):
    def rows(a, b):
        return w_ref[a:b, :].astype(BF16)

    small_pad = jnp.zeros((SMALL_W - MLA_ROPE - 2 * GLA_LOWRANK, w_ref.shape[1]), BF16)
    lat_ref[COL_Q:COL_GA, :] = rows(SRC_Q, SRC_AF)
    lat_ref[COL_GA:COL_CQ, :] = rows(SRC_GA, SRC_END)
    lat_ref[COL_CQ:COL_SMALL, :] = rows(SRC_CQ, SRC_KR)
    ctx_ref[CCOL_K:CCOL_CKV, :] = rows(SRC_K, SRC_R)
    ctx_ref[CCOL_CKV:CCOL_SMALL, :] = rows(SRC_CKV, SRC_KR)
    for ref, off in ((lat_ref, COL_SMALL), (ctx_ref, CCOL_SMALL)):
        ref[off:off + MLA_ROPE, :] = rows(SRC_KR, SRC_GA)
        ref[off + MLA_ROPE:off + SM_AB + GLA_LOWRANK, :] = rows(SRC_AF, SRC_CQ)
        ref[off + SM_AB + GLA_LOWRANK:off + SMALL_W, :] = small_pad


def _w_in_layout(w_in, tk=256):
    wt = jnp.swapaxes(w_in, 1, 2)
    _, n, d = wt.shape
    assert n == SRC_END
    return pl.pallas_call(
        _w_in_layout_kernel,
        grid=(d // tk,),
        in_specs=[pl.BlockSpec((None, n, tk), lambda i: (0, 0, i))],
        out_specs=[pl.BlockSpec((NP_LAT, tk), lambda i: (0, i)),
                   pl.BlockSpec((NP_CTX, tk), lambda i: (0, i))],
        out_shape=[jax.ShapeDtypeStruct((NP_LAT, d), BF16),
                   jax.ShapeDtypeStruct((NP_CTX, d), BF16)],
        compiler_params=_cparams(("parallel",)),
        name="w_in_layout",
    )(wt)


def kernel(x, c, ctx, c_ctx, w_mod, b_mod, norm1_g, norm2_g, w_in, w_decay_f, b_decay_f, w_decay_b, b_decay_b, gla_norm_g, q_a_norm_g, w_uq, kv_a_norm_g, w_ukv, q_norm_g, k_norm_g, w_o_gla, w_o_mla, w_out, w_router_group, b_router_group, w_router_expert, b_router_expert, w_exp_gate, w_exp_up, w_exp_down):
    assert w_mod.shape[0] == 1, "single-layer block"
    b, l, d = x.shape
    lc = ctx.shape[1]
    t = b * l

    c_rows = jnp.concatenate([c, c_ctx[None, :], jnp.zeros((8 - b - 1, d), F32)], axis=0)
    mod = _modulation(c_rows, w_mod[0], b_mod[0])
    sh1, sc1, gt1, sh2, sc2, gt2 = [mod[:b, i * d:(i + 1) * d].reshape(b, 1, d) for i in range(6)]
    sh1c = jnp.broadcast_to(mod[b:b + 1, 0:d].reshape(1, 1, d), (b, 1, d))
    sc1c = jnp.broadcast_to(mod[b:b + 1, d:2 * d].reshape(1, 1, d), (b, 1, d))

    w_lat, w_ctx = _w_in_layout(w_in)
    g1 = norm1_g[0].reshape(1, d)
    p = _in_projection(x, g1, sc1, sh1, w_lat, tm=1024)
    pc = _in_projection(ctx, g1, sc1c, sh1c, w_ctx, tm=lc)

    def decay_w(w, off):
        return jnp.zeros((LANE, GLA_QK_W), F32).at[off:off + GLA_LOWRANK].set(w).astype(BF16)

    y_gla = _gla(p, pc,
                 decay_w(w_decay_f[0], SM_AF), b_decay_f[0].reshape(1, -1),
                 decay_w(w_decay_b[0], SM_AB), b_decay_b[0].reshape(1, -1),
                 gla_norm_g[0].reshape(1, -1))

    wq = jnp.pad(w_uq[0].reshape(MLA_Q_RANK, MLA_HEADS, MLA_QK_DIM),
                 ((0, 0), (0, 0), (0, MLA_QK_PAD - MLA_QK_DIM))).reshape(MLA_Q_RANK, -1).astype(BF16)
    wkv = w_ukv[0].astype(BF16)
    qn_g = jnp.pad(q_norm_g[0], (0, MLA_QK_PAD - MLA_QK_DIM)).reshape(1, -1)
    kn_nope = k_norm_g[0][:MLA_NOPE].reshape(1, -1)
    kn_rope = jnp.pad(k_norm_g[0][MLA_NOPE:], (0, LANE - MLA_ROPE)).reshape(1, -1)
    cos, sin = _rope_tables(l, True)
    cos_c, sin_c = _rope_tables(lc, False)
    q_m = _mla_queries(p, q_a_norm_g[0].reshape(1, -1), wq, qn_g, cos, sin, tm=512)
    kva = kv_a_norm_g[0].reshape(1, -1)
    k_m, v_m = _mla_keys_values(p, pc, kva, wkv, kn_nope, kn_rope,
                                jnp.concatenate([cos, cos_c]), jnp.concatenate([sin, sin_c]))
    y_mla = _attention(q_m, k_m, v_m, tq=512)

    y = _branch_merge(y_gla, y_mla, w_o_gla[0].astype(BF16), w_o_mla[0].astype(BF16), p, tm=1024, tn=512)
    w_router = jnp.concatenate(
        [w_router_group[0], w_router_expert[0],
         jnp.zeros((d, ROUTER_W - N_GROUPS - N_EXPERTS), F32)], axis=1).astype(BF16)
    b_router = jnp.concatenate(
        [b_router_group[0], b_router_expert[0],
         jnp.zeros((ROUTER_W - N_GROUPS - N_EXPERTS,), F32)]).reshape(1, -1)
    x1, h2, logits = _out_projection(y, w_out[0].astype(BF16), x, gt1, norm2_g[0].reshape(1, d),
                                     sc2, sh2, w_router, b_router, tm=256)

    e_pick, w_pick = _route(logits.reshape(t, ROUTER_W), tm=1024)
    plan = _dispatch_plan(e_pick[:, :2].reshape(-1))
    yp = _experts(plan, h2.reshape(t * ROW_TILES, LANE),
                  w_exp_gate[0], w_exp_up[0], w_exp_down[0])
    return _combine(x1, gt2, w_pick, yp, tm=512)
```

```python
import functools
import math

import jax
import jax.numpy as jnp
from jax import lax
from jax.experimental import pallas as pl
from jax.experimental.pallas import tpu as pltpu

F32 = jnp.float32
BF16 = jnp.bfloat16

D_MODEL = 2048
GRID_W = 64
EPS = 1e-6

GLA_HEADS = 4
GLA_DK = 256
GLA_DV = 512
GLA_LOWRANK = 16
GLA_TAU = 16.0
GLA_CHUNK = 64
GLA_GROUP = 8
GLA_QK_W = GLA_HEADS * GLA_DK
GLA_V_W = GLA_HEADS * GLA_DV

MLA_HEADS = 16
MLA_Q_RANK = 512
MLA_KV_RANK = 512
MLA_NOPE = 128
MLA_ROPE = 64
MLA_V = 128
MLA_QK_DIM = MLA_NOPE + MLA_ROPE
MLA_QK_PAD = 256
MLA_VT_ROWS = MLA_V + 16
ROPE_THETA = 10000.0
LOG2E = math.log2(math.e)

N_GROUPS = 8
EXP_PER_GROUP = 8
N_EXPERTS = N_GROUPS * EXP_PER_GROUP
D_EXPERT = 512
ROUTER_W = 128
MOE_BLOCK = 256
BULK_DMA = 1

LANE = 128
ROW_TILES = D_MODEL // LANE
ROW_PITCH = 20
VMEM_LIMIT = 56 * 1024 * 1024

COL_Q = 0
COL_K = COL_Q + GLA_QK_W
COL_V = COL_K + GLA_QK_W
COL_R = COL_V + GLA_V_W
COL_GA = COL_R + GLA_V_W
COL_GB = COL_GA + D_MODEL
COL_CQ = COL_GB + D_MODEL
COL_CKV = COL_CQ + MLA_Q_RANK
COL_SMALL = COL_CKV + MLA_KV_RANK
SMALL_W = 256
NP_LAT = COL_SMALL + SMALL_W
SM_AF = MLA_ROPE
SM_AB = MLA_ROPE + GLA_LOWRANK
CCOL_K = 0
CCOL_V = CCOL_K + GLA_QK_W
CCOL_CKV = CCOL_V + GLA_V_W
CCOL_SMALL = CCOL_CKV + MLA_KV_RANK
NP_CTX = CCOL_SMALL + SMALL_W
INPROJ_TN = 1280


def _cparams(sem):
    return pltpu.CompilerParams(dimension_semantics=sem, vmem_limit_bytes=VMEM_LIMIT)


def _silu(x):
    return x * jax.nn.sigmoid(x)


def _mod_kernel(c_ref, w_ref, b_ref, o_ref):
    a = _silu(c_ref[...]).astype(BF16)
    o_ref[...] = jnp.dot(a, w_ref[...].astype(BF16), preferred_element_type=F32) + b_ref[...]


def _modulation(c_rows, w_mod, b_mod):
    m, d = c_rows.shape
    n = w_mod.shape[1]
    tn = 1024
    return pl.pallas_call(
        _mod_kernel,
        grid=(n // tn,),
        in_specs=[pl.BlockSpec((m, d), lambda j: (0, 0)),
                  pl.BlockSpec((d, tn), lambda j: (0, j)),
                  pl.BlockSpec((1, tn), lambda j: (0, j))],
        out_specs=pl.BlockSpec((m, tn), lambda j: (0, j)),
        out_shape=jax.ShapeDtypeStruct((m, n), F32),
        compiler_params=_cparams(("parallel",)),
        name="modulation",
    )(c_rows, w_mod, b_mod.reshape(1, n))


def _inproj_kernel(x_ref, g_ref, sc_ref, sh_ref, w_ref, o_ref, h_ref):
    @pl.when(pl.program_id(2) == 0)
    def _():
        x = x_ref[...]
        ms = jnp.mean(x * x, axis=-1, keepdims=True)
        y = x * lax.rsqrt(ms + EPS) * g_ref[...]
        h_ref[...] = (y * (1.0 + sc_ref[...]) + sh_ref[...]).astype(BF16)

    o_ref[...] = lax.dot_general(h_ref[...], w_ref[...], (((1,), (1,)), ((), ())),
                                 preferred_element_type=F32).astype(o_ref.dtype)


def _in_projection(x, g, sc, sh, w, tm):
    b, l, d = x.shape
    tm = min(tm, l)
    n = w.shape[0]
    tn = INPROJ_TN
    return pl.pallas_call(
        _inproj_kernel,
        grid=(b, l // tm, n // tn),
        in_specs=[pl.BlockSpec((None, tm, d), lambda bi, i, j: (bi, i, 0)),
                  pl.BlockSpec((1, d), lambda bi, i, j: (0, 0)),
                  pl.BlockSpec((None, 1, d), lambda bi, i, j: (bi, 0, 0)),
                  pl.BlockSpec((None, 1, d), lambda bi, i, j: (bi, 0, 0)),
                  pl.BlockSpec((tn, d), lambda bi, i, j: (j, 0))],
        out_specs=pl.BlockSpec((None, tm, tn), lambda bi, i, j: (bi, i, j)),
        out_shape=jax.ShapeDtypeStruct((b, l, n), BF16),
        scratch_shapes=[pltpu.VMEM((tm, d), BF16)],
        compiler_params=_cparams(("parallel", "parallel", "arbitrary")),
        name="in_projection",
    )(x, g, sc, sh, w)


def _gla_kernel(q_ref, k_ref, v_ref, r_ref, sm_ref, kc_ref, vc_ref, smc_ref,
                wf_ref, bf_ref, wb_ref, bb_ref, gn_ref, y_ref, o_ref, sf_ref, sb_ref):
    c = GLA_CHUNK
    n_lat = q_ref.shape[0] // c
    n_ctx = kc_ref.shape[0] // c
    row = lax.broadcasted_iota(jnp.int32, (c, c), 0)
    col = lax.broadcasted_iota(jnp.int32, (c, c), 1)
    lower = row >= col
    upper = row <= col
    nt = (((1,), (1,)), ((), ()))
    tn = (((0,), (0,)), ((), ()))

    def cum_decay(sm, w_ref, b_ref, tri, n_chunks):
        z = jnp.dot(sm, w_ref[...], preferred_element_type=F32) + b_ref[...]
        g = (jnp.minimum(z, 0.0) - jnp.log(1.0 + jnp.exp(-jnp.abs(z)))) * (1.0 / GLA_TAU)
        g_hi = g.astype(BF16)
        g_lo = (g - g_hi.astype(F32)).astype(BF16)
        t = jnp.where(tri, 1.0, 0.0).astype(BF16)
        return [jnp.dot(t, g_hi[i * c:(i + 1) * c], preferred_element_type=F32)
                + jnp.dot(t, g_lo[i * c:(i + 1) * c], preferred_element_type=F32)
                for i in range(n_chunks)]

    fwd = (wf_ref, bf_ref, lower, c - 1, sf_ref)
    bwd = (wb_ref, bb_ref, upper, 0, sb_ref)

    def group(r0, n_chunks, direction, sm_r, k_r, v_r, q_r):
        w_ref, b_ref, tri, end_row, st_ref = direction
        rows = n_chunks * c
        k_all = k_r[pl.ds(r0, rows), :].astype(F32)
        v_all = v_r[pl.ds(r0, rows), :]
        q_all = None if q_r is None else q_r[pl.ds(r0, rows), :].astype(F32) * (GLA_DK ** -0.5)
        bc_all = cum_decay(sm_r[pl.ds(r0, rows), :], w_ref, b_ref, tri, n_chunks)
        local = []
        for i in range(n_chunks):
            sl = slice(i * c, (i + 1) * c)
            bc = bc_all[i]
            bend = bc[end_row:end_row + 1, :]
            k, v = k_all[sl], v_all[sl]
            ke = (k * jnp.exp(bend - bc)).astype(BF16)
            upd = lax.dot_general(v, ke, tn, preferred_element_type=F32)
            qd = o_intra = None
            if q_all is not None:
                qd = (q_all[sl] * jnp.exp(bc)).astype(BF16)
                ki = (k * jnp.exp(-bc)).astype(BF16)
                att = lax.dot_general(qd, ki, nt, preferred_element_type=F32)
                att = jnp.where(tri, att, 0.0).astype(BF16)
                o_intra = jnp.dot(att, v, preferred_element_type=F32)
            local.append((jnp.exp(bend), upd, qd, o_intra))
        st = st_ref[...]
        outs = [None] * n_chunks
        for i in (range(n_chunks) if end_row else reversed(range(n_chunks))):
            decay, upd, qd, o_intra = local[i]
            if q_all is not None:
                outs[i] = o_intra + lax.dot_general(qd, st.astype(BF16), nt, preferred_element_type=F32)
            st = st * decay + upd
        st_ref[...] = st
        return None if q_all is None else jnp.concatenate(outs, axis=0)

    def readout(r0, o):
        ms = jnp.mean(o * o, axis=-1, keepdims=True)
        yn = o * lax.rsqrt(ms + EPS) * gn_ref[...]
        r = r_ref[pl.ds(r0, o.shape[0]), :].astype(F32)
        y_ref[pl.ds(r0, o.shape[0]), :] = (yn * _silu(r)).astype(y_ref.dtype)

    sf_ref[...] = jnp.zeros_like(sf_ref)
    sb_ref[...] = jnp.zeros_like(sb_ref)
    group(0, n_ctx, fwd, smc_ref, kc_ref, vc_ref, None)
    group(0, n_ctx, bwd, smc_ref, kc_ref, vc_ref, None)

    g_lat = min(GLA_GROUP, n_lat // 2)
    assert n_lat % (2 * g_lat) == 0
    n_groups = n_lat // g_lat
    rows = g_lat * c

    def scan(i, direction):
        r0 = pl.multiple_of(i * rows, rows)
        return r0, group(r0, g_lat, direction, sm_ref, k_ref, v_ref, q_ref)

    def first_half(i, carry):
        for r0, o in (scan(i, fwd), scan(n_groups - 1 - i, bwd)):
            o_ref[pl.ds(r0, rows), :] = o
        return carry

    lax.fori_loop(0, n_groups // 2, first_half, 0)

    def second_half(i, carry):
        for r0, o in (scan(i, fwd), scan(n_groups - 1 - i, bwd)):
            readout(r0, o_ref[pl.ds(r0, rows), :] + o)
        return carry

    lax.fori_loop(n_groups // 2, n_groups, second_half, 0)


def _gla(p, pc, wf, bf, wb, bb, gn):
    b, l, _ = p.shape
    lc = pc.shape[1]
    dk, dv = GLA_DK, GLA_DV
    hmap = lambda off: (lambda bi, h: (bi, 0, off + h))
    wmap = lambda bi, h: (0, h)
    return pl.pallas_call(
        _gla_kernel,
        grid=(b, GLA_HEADS),
        in_specs=[pl.BlockSpec((None, l, dk), hmap(COL_Q // dk)),
                  pl.BlockSpec((None, l, dk), hmap(COL_K // dk)),
                  pl.BlockSpec((None, l, dv), hmap(COL_V // dv)),
                  pl.BlockSpec((None, l, dv), hmap(COL_R // dv)),
                  pl.BlockSpec((None, l, LANE), lambda bi, h: (bi, 0, COL_SMALL // LANE)),
                  pl.BlockSpec((None, lc, dk), hmap(CCOL_K // dk)),
                  pl.BlockSpec((None, lc, dv), hmap(CCOL_V // dv)),
                  pl.BlockSpec((None, lc, LANE), lambda bi, h: (bi, 0, CCOL_SMALL // LANE)),
                  pl.BlockSpec((LANE, dk), wmap),
                  pl.BlockSpec((1, dk), wmap),
                  pl.BlockSpec((LANE, dk), wmap),
                  pl.BlockSpec((1, dk), wmap),
                  pl.BlockSpec((1, dv), lambda bi, h: (0, 0))],
        out_specs=pl.BlockSpec((None, l, dv), lambda bi, h: (bi, 0, h)),
        out_shape=jax.ShapeDtypeStruct((b, l, GLA_V_W), BF16),
        scratch_shapes=[pltpu.VMEM((l, dv), F32), pltpu.VMEM((dv, dk), F32), pltpu.VMEM((dv, dk), F32)],
        compiler_params=_cparams(("parallel", "parallel")),
        name="gla",
    )(p, p, p, p, p, pc, pc, pc, wf, bf, wb, bb, gn)


def _rope_partner(t):
    lane = lax.broadcasted_iota(jnp.int32, t.shape, 1)
    first = (lane % 32) < 16
    return jnp.where(first, pltpu.roll(t, LANE - 16, 1), pltpu.roll(t, 16, 1))


def _mla_q_kernel(cq_ref, ga_ref, w_ref, gn_ref, cos_ref, sin_ref, o_ref):
    cq = cq_ref[...].astype(F32)
    cn = (cq * lax.rsqrt(jnp.mean(cq * cq, axis=-1, keepdims=True) + EPS) * ga_ref[...]).astype(BF16)
    cos = cos_ref[...]
    sin = sin_ref[...]
    scale = MLA_QK_DIM ** -0.5 * LOG2E
    for h in range(MLA_HEADS):
        qh = jnp.dot(cn, w_ref[:, h * MLA_QK_PAD:(h + 1) * MLA_QK_PAD], preferred_element_type=F32)
        ms = jnp.sum(qh * qh, axis=-1, keepdims=True) * (1.0 / MLA_QK_DIM)
        qn = qh * lax.rsqrt(ms + EPS) * gn_ref[...]
        t = qn[:, MLA_NOPE:]
        rot = t * cos + _rope_partner(t) * sin
        o_ref[h, :, :MLA_NOPE] = (qn[:, :MLA_NOPE] * scale).astype(o_ref.dtype)
        o_ref[h, :, MLA_NOPE:] = (rot * scale).astype(o_ref.dtype)


def _mla_queries(p, ga, wq, gn, cos, sin, tm):
    b, l, _ = p.shape
    tm = min(tm, l)
    return pl.pallas_call(
        _mla_q_kernel,
        grid=(b, l // tm),
        in_specs=[pl.BlockSpec((None, tm, MLA_Q_RANK), lambda bi, i: (bi, i, COL_CQ // MLA_Q_RANK)),
                  pl.BlockSpec((1, MLA_Q_RANK), lambda bi, i: (0, 0)),
                  pl.BlockSpec(wq.shape, lambda bi, i: (0, 0)),
                  pl.BlockSpec((1, MLA_QK_PAD), lambda bi, i: (0, 0)),
                  pl.BlockSpec((tm, LANE), lambda bi, i: (i, 0)),
                  pl.BlockSpec((tm, LANE), lambda bi, i: (i, 0))],
        out_specs=pl.BlockSpec((None, MLA_HEADS, tm, MLA_QK_PAD), lambda bi, i: (bi, 0, i, 0)),
        out_shape=jax.ShapeDtypeStruct((b, MLA_HEADS, l, MLA_QK_PAD), BF16),
        compiler_params=_cparams(("parallel", "parallel")),
        name="mla_queries",
    )(p, ga, wq, gn, cos, sin)


def _mla_kv_kernel(ckv_ref, sm_ref, ckvc_ref, smc_ref, ga_ref, w_ref, gk_ref, gr_ref,
                   cos_ref, sin_ref, k_ref, v_ref, *, n_lat):
    is_ctx = pl.program_id(1) >= n_lat
    ckv = jnp.where(is_ctx, ckvc_ref[...], ckv_ref[...]).astype(F32)
    cn = (ckv * lax.rsqrt(jnp.mean(ckv * ckv, axis=-1, keepdims=True) + EPS) * ga_ref[...]).astype(BF16)
    sm = jnp.where(is_ctx, smc_ref[...], sm_ref[...]).astype(F32)
    lane = lax.broadcasted_iota(jnp.int32, sm.shape, 1)
    kr = jnp.where(lane < MLA_ROPE, sm, 0.0)
    ss_r = jnp.sum(kr * kr, axis=-1, keepdims=True)
    krg = kr * gr_ref[...]
    rot = krg * cos_ref[...] + _rope_partner(krg) * sin_ref[...]
    hw = MLA_NOPE + MLA_V
    tm = ckv.shape[0]
    sub = lax.broadcasted_iota(jnp.int32, (MLA_VT_ROWS - MLA_V, tm), 0)
    ones_rows = jnp.where(sub == 0, 1.0, 0.0).astype(v_ref.dtype)
    for h in range(MLA_HEADS):
        kvh = jnp.dot(cn, w_ref[:, h * hw:(h + 1) * hw], preferred_element_type=F32)
        kn = kvh[:, :MLA_NOPE]
        ms = (jnp.sum(kn * kn, axis=-1, keepdims=True) + ss_r) * (1.0 / MLA_QK_DIM)
        rs = lax.rsqrt(ms + EPS)
        k_ref[h, :, :MLA_NOPE] = (kn * rs * gk_ref[...]).astype(k_ref.dtype)
        k_ref[h, :, MLA_NOPE:] = (rot * rs).astype(k_ref.dtype)
        v_ref[h, :MLA_V, :] = kvh[:, MLA_NOPE:].T.astype(v_ref.dtype)
        v_ref[h, MLA_V:, :] = ones_rows


def _mla_keys_values(p, pc, ga, wkv, gk, gr, cos, sin):
    b, l, _ = p.shape
    lc = pc.shape[1]
    tm = min(lc, l)
    assert l % tm == 0 and lc % tm == 0
    n_lat, n_ctx = l // tm, lc // tm
    lat = lambda blk: (lambda bi, i: (bi, jnp.minimum(i, n_lat - 1), blk))
    ctx = lambda blk: (lambda bi, i: (bi, jnp.maximum(i - n_lat, 0), blk))
    const = lambda bi, i: (0, 0)
    k_spec = pl.BlockSpec((None, MLA_HEADS, tm, MLA_QK_PAD), lambda bi, i: (bi, 0, i, 0))
    vt_spec = pl.BlockSpec((None, MLA_HEADS, MLA_VT_ROWS, tm), lambda bi, i: (bi, 0, 0, i))
    k_shape = jax.ShapeDtypeStruct((b, MLA_HEADS, l + lc, MLA_QK_PAD), BF16)
    vt_shape = jax.ShapeDtypeStruct((b, MLA_HEADS, MLA_VT_ROWS, l + lc), BF16)
    return pl.pallas_call(
        functools.partial(_mla_kv_kernel, n_lat=n_lat),
        grid=(b, n_lat + n_ctx),
        in_specs=[pl.BlockSpec((None, tm, MLA_KV_RANK), lat(COL_CKV // MLA_KV_RANK)),
                  pl.BlockSpec((None, tm, LANE), lat(COL_SMALL // LANE)),
                  pl.BlockSpec((None, tm, MLA_KV_RANK), ctx(CCOL_CKV // MLA_KV_RANK)),
                  pl.BlockSpec((None, tm, LANE), ctx(CCOL_SMALL // LANE)),
                  pl.BlockSpec((1, MLA_KV_RANK), const),
                  pl.BlockSpec(wkv.shape, const),
                  pl.BlockSpec((1, LANE), const),
                  pl.BlockSpec((1, LANE), const),
                  pl.BlockSpec((tm, LANE), lambda bi, i: (i, 0)),
                  pl.BlockSpec((tm, LANE), lambda bi, i: (i, 0))],
        out_specs=[k_spec, vt_spec],
        out_shape=[k_shape, vt_shape],
        compiler_params=_cparams(("parallel", "arbitrary")),
        name="mla_keys_values",
    )(p, p, pc, pc, ga, wkv, gk, gr, cos, sin)


def _attn_kernel(q_ref, k_ref, vt_ref, o_ref, s0_ref, s1_ref, m0_ref, m1_ref, *, tq):
    nt = (((1,), (1,)), ((), ()))
    n = q_ref.shape[0] // tq
    bufs = ((s0_ref, m0_ref), (s1_ref, m1_ref))

    def scores(i, slot):
        s_ref, m_ref = bufs[slot]
        r0 = pl.multiple_of(i * tq, tq)
        s = lax.dot_general(k_ref[...], q_ref[pl.ds(r0, tq), :], nt, preferred_element_type=F32)
        s_ref[...] = s
        m_ref[...] = jnp.max(s, axis=0, keepdims=True)

    def outputs(i, slot):
        s_ref, m_ref = bufs[slot]
        r0 = pl.multiple_of(i * tq, tq)
        p = jnp.exp2(s_ref[...] - m_ref[...]).astype(BF16)
        o = jnp.dot(vt_ref[...], p, preferred_element_type=F32)
        o_ref[pl.ds(r0, tq), :] = (o[:MLA_V] / o[MLA_V:MLA_V + 1]).T.astype(o_ref.dtype)

    scores(0, 0)
    if n % 2 == 0:
        def body(j, carry):
            scores(2 * j + 1, 1)
            outputs(2 * j, 0)
            scores(2 * j + 2, 0)
            outputs(2 * j + 1, 1)
            return carry

        lax.fori_loop(0, n // 2 - 1, body, 0)
        scores(n - 1, 1)
        outputs(n - 2, 0)
        outputs(n - 1, 1)
    else:
        assert n == 1
        outputs(0, 0)


def _attention(q, k, v, tq):
    b, h, l, dq = q.shape
    tq = min(tq, l)
    lk = k.shape[2]
    bh = lambda bi, hi: (bi, hi, 0, 0)
    return pl.pallas_call(
        functools.partial(_attn_kernel, tq=tq),
        grid=(b, h),
        in_specs=[pl.BlockSpec((None, None, l, dq), bh),
                  pl.BlockSpec((None, None, lk, dq), bh),
                  pl.BlockSpec((None, None, MLA_VT_ROWS, lk), bh)],
        out_specs=pl.BlockSpec((None, l, MLA_V), lambda bi, hi: (bi, 0, hi)),
        out_shape=jax.ShapeDtypeStruct((b, l, h * MLA_V), BF16),
        scratch_shapes=[pltpu.VMEM((lk, tq), F32), pltpu.VMEM((lk, tq), F32),
                        pltpu.VMEM((1, tq), F32), pltpu.VMEM((1, tq), F32)],
        compiler_params=_cparams(("parallel", "parallel")),
        name="mla_attention",
    )(q, k, v)


def _merge_kernel(yg_ref, ym_ref, wg_ref, wm_ref, ga_ref, gb_ref, o_ref):
    a = jnp.dot(yg_ref[...], wg_ref[...], preferred_element_type=F32)
    m = jnp.dot(ym_ref[...], wm_ref[...], preferred_element_type=F32)
    y = (jax.nn.sigmoid(ga_ref[...].astype(F32)) * a
         + jax.nn.sigmoid(gb_ref[...].astype(F32)) * m)
    o_ref[...] = y.astype(o_ref.dtype)


def _branch_merge(yg, ym, wg, wm, p, tm, tn):
    b, l, d = yg.shape
    tm = min(tm, l)
    n = wg.shape[1]
    return pl.pallas_call(
        _merge_kernel,
        grid=(b, l // tm, n // tn),
        in_specs=[pl.BlockSpec((None, tm, d), lambda bi, i, j: (bi, i, 0)),
                  pl.BlockSpec((None, tm, d), lambda bi, i, j: (bi, i, 0)),
                  pl.BlockSpec((d, tn), lambda bi, i, j: (0, j)),
                  pl.BlockSpec((d, tn), lambda bi, i, j: (0, j)),
                  pl.BlockSpec((None, tm, tn), lambda bi, i, j: (bi, i, COL_GA // tn + j)),
                  pl.BlockSpec((None, tm, tn), lambda bi, i, j: (bi, i, COL_GB // tn + j))],
        out_specs=pl.BlockSpec((None, tm, tn), lambda bi, i, j: (bi, i, j)),
        out_shape=jax.ShapeDtypeStruct((b, l, n), BF16),
        compiler_params=_cparams(("parallel", "parallel", "arbitrary")),
        name="branch_merge",
    )(yg, ym, wg, wm, p, p)


def _outproj_kernel(y_ref, w_ref, x_ref, gt_ref, g2_ref, sc_ref, sh_ref, wr_ref, br_ref,
                    x1_ref, h2_ref, lg_ref):
    x1 = x_ref[...] + gt_ref[...] * jnp.dot(y_ref[...], w_ref[...], preferred_element_type=F32)
    x1_ref[...] = x1
    ms = jnp.mean(x1 * x1, axis=-1, keepdims=True)
    h2 = x1 * lax.rsqrt(ms + EPS) * g2_ref[...] * (1.0 + sc_ref[...]) + sh_ref[...]
    tm = x1.shape[0]
    for j in range(ROW_TILES):
        h2_ref[pl.ds(j, tm, stride=ROW_TILES), :] = h2[:, j * LANE:(j + 1) * LANE]
    lg_ref[...] = jnp.dot(h2.astype(BF16), wr_ref[...], preferred_element_type=F32) + br_ref[...]


def _out_projection(y, w, x, gt, g2, sc, sh, wr, br, tm):
    b, l, d = x.shape
    tm = min(tm, l)
    row = lambda bi, i: (bi, i, 0)
    per_b = lambda bi, i: (bi, 0, 0)
    const = lambda bi, i: (0, 0)
    return pl.pallas_call(
        _outproj_kernel,
        grid=(b, l // tm),
        in_specs=[pl.BlockSpec((None, tm, d), row),
                  pl.BlockSpec((d, d), const),
                  pl.BlockSpec((None, tm, d), row),
                  pl.BlockSpec((None, 1, d), per_b),
                  pl.BlockSpec((1, d), const),
                  pl.BlockSpec((None, 1, d), per_b),
                  pl.BlockSpec((None, 1, d), per_b),
                  pl.BlockSpec((d, ROUTER_W), const),
                  pl.BlockSpec((1, ROUTER_W), const)],
        out_specs=[pl.BlockSpec((None, tm, d), row),
                   pl.BlockSpec((None, tm * ROW_TILES, LANE), row),
                   pl.BlockSpec((None, tm, ROUTER_W), row)],
        out_shape=[jax.ShapeDtypeStruct((b, l, d), F32),
                   jax.ShapeDtypeStruct((b, l * ROW_TILES, LANE), F32),
                   jax.ShapeDtypeStruct((b, l, ROUTER_W), F32)],
        compiler_params=_cparams(("parallel", "parallel")),
        name="out_projection",
    )(y, w, x, gt, g2, sc, sh, wr, br)


def _route_kernel(lg_ref, e_ref, w_ref):
    lg = lg_ref[...]
    lane = lax.broadcasted_iota(jnp.int32, lg.shape, 1)
    neg = jnp.float32(-jnp.inf)
    big = jnp.int32(1 << 20)
    is_g = lane < N_GROUPS
    gl = jnp.where(is_g, lg, neg)
    gm = jnp.max(gl, axis=-1, keepdims=True)
    gidx = jnp.min(jnp.where(gl == gm, lane, big), axis=-1, keepdims=True)
    gsum = jnp.sum(jnp.where(is_g, jnp.exp(gl - gm), 0.0), axis=-1, keepdims=True)
    p_g = 1.0 / gsum
    g_lo = N_GROUPS + gidx * EXP_PER_GROUP
    in_grp = (lane >= g_lo) & (lane < g_lo + EXP_PER_GROUP)
    el = jnp.where(in_grp, lg, neg)
    em = jnp.max(el, axis=-1, keepdims=True)
    z = jnp.sum(jnp.where(in_grp, jnp.exp(el - em), 0.0), axis=-1, keepdims=True)
    i1 = jnp.min(jnp.where(el == em, lane, big), axis=-1, keepdims=True)
    el2 = jnp.where(lane == i1, neg, el)
    em2 = jnp.max(el2, axis=-1, keepdims=True)
    i2 = jnp.min(jnp.where(el2 == em2, lane, big), axis=-1, keepdims=True)
    p1 = 1.0 / z
    p2 = jnp.exp(em2 - em) / z
    tot = p1 + p2
    e_ref[...] = jnp.where(lane == 0, i1 - N_GROUPS, jnp.where(lane == 1, i2 - N_GROUPS, 0))
    w_ref[...] = jnp.where(lane == 0, p_g * p1 / tot, jnp.where(lane == 1, p_g * p2 / tot, 0.0))


def _route(logits, tm):
    t = logits.shape[0]
    tm = min(tm, t)
    spec = pl.BlockSpec((tm, ROUTER_W), lambda i: (i, 0))
    return pl.pallas_call(
        _route_kernel,
        grid=(t // tm,),
        in_specs=[spec],
        out_specs=[spec, spec],
        out_shape=[jax.ShapeDtypeStruct((t, ROUTER_W), jnp.int32),
                   jax.ShapeDtypeStruct((t, ROUTER_W), F32)],
        compiler_params=_cparams(("parallel",)),
        name="moe_route",
    )(logits)


def _expert_kernel(be_ref, nv_ref, first_ref, nxe_ref, ws_ref, j0_ref, tok_ref, dst_ref,
                   h_hbm, wg_hbm, wu_hbm, wd_hbm, yp_hbm,
                   xbuf, ybuf, wgv, wuv, wdv, sem_in, sem_out, sem_w, sem_init):
    i = pl.program_id(0)
    n = pl.num_programs(0)
    blk = MOE_BLOCK
    rt, pitch = ROW_TILES, ROW_PITCH
    plane = yp_hbm.shape[0] // (2 * rt)
    t_rows = plane - blk
    slot_rows = blk * pitch
    par = i % 2
    cur = pl.multiple_of(par * slot_rows, 8)
    oth = pl.multiple_of((1 - par) * slot_rows, 8)

    def weight_copies(e, s):
        return (pltpu.make_async_copy(wg_hbm.at[e], wgv.at[s], sem_w.at[s, 0]),
                pltpu.make_async_copy(wu_hbm.at[e], wuv.at[s], sem_w.at[s, 1]),
                pltpu.make_async_copy(wd_hbm.at[e], wdv.at[s], sem_w.at[s, 2]))

    def gather_start(b, buf0, sem, r_lo=0, r_hi=blk):
        base = j0_ref[b]
        for r in range(r_lo, r_hi):
            row0 = pl.multiple_of(tok_ref[base + r] * rt, rt)
            pltpu.make_async_copy(h_hbm.at[pl.ds(row0, rt)],
                                  xbuf.at[pl.ds(buf0 + r * pitch, rt)], sem_in.at[sem]).start()

    def gather_wait(buf0, sem):
        pltpu.make_async_copy(h_hbm.at[pl.ds(0, blk * rt)], xbuf.at[pl.ds(buf0, blk * rt)],
                              sem_in.at[sem]).wait()

    def scatter_start(b, buf0, sem, r_lo=0, r_hi=blk):
        base = j0_ref[b]
        filled = nv_ref[b]
        for r in range(r_lo, r_hi):
            dst = jnp.where(r < filled, dst_ref[base + r], t_rows + r)
            row0 = pl.multiple_of(dst * rt, rt)
            pltpu.make_async_copy(ybuf.at[pl.ds(buf0 + r * pitch, rt)],
                                  yp_hbm.at[pl.ds(row0, rt)], sem_out.at[sem]).start(priority=BULK_DMA)

    def scatter_wait(buf0, sem):
        pltpu.make_async_copy(ybuf.at[pl.ds(buf0, blk * rt)], yp_hbm.at[pl.ds(0, blk * rt)],
                              sem_out.at[sem]).wait()

    @pl.when(i == 0)
    def _():
        for cp in weight_copies(be_ref[0], 0):
            cp.start(priority=BULK_DMA)
        gather_start(0, 0, 0)
        ybuf[...] = jnp.zeros_like(ybuf)
        zeros = ybuf.at[pl.ds(slot_rows, blk * rt)]
        pltpu.make_async_copy(zeros, yp_hbm.at[pl.ds(t_rows * rt, blk * rt)], sem_out.at[0]).start()
        spare = pltpu.make_async_copy(zeros, yp_hbm.at[pl.ds((plane + t_rows) * rt, blk * rt)], sem_init)
        spare.start()
        spare.wait()

    valid = nv_ref[i] > 0
    nxt = jnp.minimum(i + 1, n - 1)
    has_next = jnp.logical_and(i + 1 < n, nv_ref[nxt] > 0)

    @pl.when(jnp.logical_and(valid, first_ref[i] == 1))
    def _():
        ws = ws_ref[i]
        for cp in weight_copies(be_ref[i], ws):
            cp.wait()

        @pl.when(nxe_ref[i] >= 0)
        def _():
            for cp in weight_copies(nxe_ref[i], 1 - ws):
                cp.start(priority=BULK_DMA)

    @pl.when(valid)
    def _():
        ws = ws_ref[i]
        gather_wait(cur, par)
        scatter_wait(cur, par)
        prev = jnp.maximum(i - 1, 0)
        per = blk // (2 * rt)

        def read_tiles(first_chunk):
            tiles = []
            for j in range(rt):
                lo = (first_chunk + j) * per
                gather_start(nxt, oth, 1 - par, lo, lo + per)
                tiles.append(xbuf[pl.ds(cur + j, blk, stride=pitch), :])
            return jnp.concatenate(tiles, axis=1).astype(BF16)

        scatter_start(prev, oth, 1 - par)
        g = jnp.dot(read_tiles(0), wgv[ws].astype(BF16), preferred_element_type=F32)
        u = jnp.dot(read_tiles(rt), wuv[ws].astype(BF16), preferred_element_type=F32)
        a = (_silu(g) * u).astype(BF16)
        y = jnp.dot(a, wdv[ws].astype(BF16), preferred_element_type=F32)
        for j in range(rt):
            ybuf[pl.ds(cur + j, blk, stride=pitch), :] = y[:, j * LANE:(j + 1) * LANE]

        @pl.when(jnp.logical_not(has_next))
        def _():
            scatter_wait(oth, 1 - par)
            scatter_start(i, cur, par)
            gather_wait(oth, 1 - par)
            scatter_wait(cur, par)


def _experts(plan, h2, w_gate, w_up, w_down):
    d = w_gate.shape[1]
    rt, pitch = ROW_TILES, ROW_PITCH
    assert d == rt * LANE
    t = h2.shape[0] // rt
    n_blocks = plan[0].shape[0]
    blk = MOE_BLOCK
    de = w_gate.shape[2]
    any_spec = pl.BlockSpec(memory_space=pl.ANY)
    return pl.pallas_call(
        _expert_kernel,
        grid_spec=pltpu.PrefetchScalarGridSpec(
            num_scalar_prefetch=len(plan),
            grid=(n_blocks,),
            in_specs=[any_spec, any_spec, any_spec, any_spec],
            out_specs=any_spec,
            scratch_shapes=[pltpu.VMEM((2 * blk * pitch, LANE), F32),
                            pltpu.VMEM((2 * blk * pitch, LANE), F32),
                            pltpu.VMEM((2, d, de), F32),
                            pltpu.VMEM((2, d, de), F32),
                            pltpu.VMEM((2, de, d), F32),
                            pltpu.SemaphoreType.DMA((2,)),
                            pltpu.SemaphoreType.DMA((2,)),
                            pltpu.SemaphoreType.DMA((2, 3)),
                            pltpu.SemaphoreType.DMA],
        ),
        out_shape=jax.ShapeDtypeStruct((2 * (t + blk) * rt, LANE), F32),
        compiler_params=_cparams(("arbitrary",)),
        name="moe_experts",
    )(*plan, h2, w_gate, w_up, w_down).reshape(2, (t + blk) * rt, LANE)


def _combine_kernel(x_ref, gt_ref, w_ref, yp_ref, o_ref):
    tm = x_ref.shape[0]
    w = w_ref[...]
    w0, w1 = w[:, 0:1], w[:, 1:2]
    moe = jnp.concatenate(
        [w0 * yp_ref[0, pl.ds(j, tm, stride=ROW_TILES), :]
         + w1 * yp_ref[1, pl.ds(j, tm, stride=ROW_TILES), :] for j in range(ROW_TILES)], axis=1)
    o_ref[...] = x_ref[...] + gt_ref[...] * moe


def _combine(x1, gt, w_pick, yp, tm):
    b, l, d = x1.shape
    tm = min(tm, l)
    nb = l // tm
    return pl.pallas_call(
        _combine_kernel,
        grid=(b, nb),
        in_specs=[pl.BlockSpec((None, tm, d), lambda bi, i: (bi, i, 0)),
                  pl.BlockSpec((None, 1, d), lambda bi, i: (bi, 0, 0)),
                  pl.BlockSpec((tm, ROUTER_W), lambda bi, i: (bi * nb + i, 0)),
                  pl.BlockSpec((2, tm * ROW_TILES, LANE), lambda bi, i: (0, bi * nb + i, 0))],
        out_specs=pl.BlockSpec((None, tm, d), lambda bi, i: (bi, i, 0)),
        out_shape=jax.ShapeDtypeStruct((b, l, d), F32),
        compiler_params=_cparams(("parallel", "parallel")),
        name="moe_combine",
    )(x1, gt, w_pick, yp)


def _dispatch_plan(eid):
    blk = MOE_BLOCK
    n_assign = eid.shape[0]
    n_tok = n_assign // 2
    n_blocks = (n_assign + blk - 1) // blk + N_EXPERTS
    order = jnp.argsort(eid).astype(jnp.int32)
    e_ids = jnp.arange(N_EXPERTS, dtype=jnp.int32)
    counts = jnp.sum((eid[None, :] == e_ids[:, None]).astype(jnp.int32), axis=1)
    start = jnp.cumsum(counts) - counts
    nblk = (counts + blk - 1) // blk
    bend = jnp.cumsum(nblk)
    b_ids = jnp.arange(n_blocks, dtype=jnp.int32)
    block_e = jnp.minimum(jnp.searchsorted(bend, b_ids, side='right'), N_EXPERTS - 1).astype(jnp.int32)
    used = b_ids < bend[-1]
    within = b_ids - (bend - nblk)[block_e]
    block_nv = jnp.where(used, jnp.clip(counts[block_e] - within * blk, 0, blk), 0).astype(jnp.int32)
    block_j0 = jnp.where(used, start[block_e] + within * blk, 0).astype(jnp.int32)
    first = jnp.concatenate([jnp.ones((1,), jnp.int32),
                             (block_e[1:] != block_e[:-1]).astype(jnp.int32)])
    w_slot = ((jnp.cumsum(first) - 1) % 2).astype(jnp.int32)
    nonempty = jnp.where(counts > 0, e_ids, N_EXPERTS)
    later = jnp.flip(lax.cummin(jnp.flip(nonempty)))
    next_e = jnp.concatenate([later[1:], jnp.full((1,), N_EXPERTS, jnp.int32)])
    next_e = jnp.where(next_e >= N_EXPERTS, -1, next_e)[block_e]
    order = jnp.concatenate([order, jnp.zeros((blk,), jnp.int32)])
    tok_sorted = order // 2
    dst_sorted = (order % 2) * (n_tok + blk) + order // 2
    return (block_e, block_nv, first, next_e, w_slot, block_j0, tok_sorted, dst_sorted)


def _rope_tables(length, with_rope):
    half = MLA_ROPE // 4
    if with_rope:
        rows = length // GRID_W
        row = jnp.repeat(jnp.arange(rows), GRID_W).astype(F32)
        col = jnp.tile(jnp.arange(GRID_W), rows).astype(F32)
        inv = 1.0 / (ROPE_THETA ** (jnp.arange(half, dtype=F32) / half))
        ang_r = row[:, None] * inv
        ang_c = col[:, None] * inv
        cos = jnp.concatenate([jnp.cos(ang_r), jnp.cos(ang_r), jnp.cos(ang_c), jnp.cos(ang_c)], axis=1)
        sin = jnp.concatenate([-jnp.sin(ang_r), jnp.sin(ang_r), -jnp.sin(ang_c), jnp.sin(ang_c)], axis=1)
    else:
        cos = jnp.ones((length, MLA_ROPE), F32)
        sin = jnp.zeros((length, MLA_ROPE), F32)
    pad = ((0, 0), (0, LANE - MLA_ROPE))
    return jnp.pad(cos, pad), jnp.pad(sin, pad)


_IN_SIZES = (GLA_QK_W, GLA_QK_W, GLA_V_W, GLA_V_W, GLA_LOWRANK, GLA_LOWRANK,
             MLA_Q_RANK, MLA_KV_RANK, MLA_ROPE, D_MODEL, D_MODEL)
(SRC_Q, SRC_K, SRC_V, SRC_R, SRC_AF, SRC_AB, SRC_CQ, SRC_CKV, SRC_KR, SRC_GA, SRC_GB,
 SRC_END) = [sum(_IN_SIZES[:i]) for i in range(len(_IN_SIZES) + 1)]


def _w_in_layout_kernel(w_ref, lat_ref, ctx_ref):
    def rows(a, b):
        return w_ref[a:b, :].astype(BF16)

    small_pad = jnp.zeros((SMALL_W - MLA_ROPE - 2 * GLA_LOWRANK, w_ref.shape[1]), BF16)
    lat_ref[COL_Q:COL_GA, :] = rows(SRC_Q, SRC_AF)
    lat_ref[COL_GA:COL_CQ, :] = rows(SRC_GA, SRC_END)
    lat_ref[COL_CQ:COL_SMALL, :] = rows(SRC_CQ, SRC_KR)
    ctx_ref[CCOL_K:CCOL_CKV, :] = rows(SRC_K, SRC_R)
    ctx_ref[CCOL_CKV:CCOL_SMALL, :] = rows(SRC_CKV, SRC_KR)
    for ref, off in ((lat_ref, COL_SMALL), (ctx_ref, CCOL_SMALL)):
        ref[off:off + MLA_ROPE, :] = rows(SRC_KR, SRC_GA)
        ref[off + MLA_ROPE:off + SM_AB + GLA_LOWRANK, :] = rows(SRC_AF, SRC_CQ)
        ref[off + SM_AB + GLA_LOWRANK:off + SMALL_W, :] = small_pad


def _w_in_layout(w_in, tk=256):
    wt = jnp.swapaxes(w_in, 1, 2)
    _, n, d = wt.shape
    assert n == SRC_END
    return pl.pallas_call(
        _w_in_layout_kernel,
        grid=(d // tk,),
        in_specs=[pl.BlockSpec((None, n, tk), lambda i: (0, 0, i))],
        out_specs=[pl.BlockSpec((NP_LAT, tk), lambda i: (0, i)),
                   pl.BlockSpec((NP_CTX, tk), lambda i: (0, i))],
        out_shape=[jax.ShapeDtypeStruct((NP_LAT, d), BF16),
                   jax.ShapeDtypeStruct((NP_CTX, d), BF16)],
        compiler_params=_cparams(("parallel",)),
        name="w_in_layout",
    )(wt)


def kernel(x, c, ctx, c_ctx, w_mod, b_mod, norm1_g, norm2_g, w_in, w_decay_f, b_decay_f, w_decay_b, b_decay_b, gla_norm_g, q_a_norm_g, w_uq, kv_a_norm_g, w_ukv, q_norm_g, k_norm_g, w_o_gla, w_o_mla, w_out, w_router_group, b_router_group, w_router_expert, b_router_expert, w_exp_gate, w_exp_up, w_exp_down):
    assert w_mod.shape[0] == 1, "single-layer block"
    b, l, d = x.shape
    lc = ctx.shape[1]
    t = b * l

    c_rows = jnp.concatenate([c, c_ctx[None, :], jnp.zeros((8 - b - 1, d), F32)], axis=0)
    mod = _modulation(c_rows, w_mod[0], b_mod[0])
    sh1, sc1, gt1, sh2, sc2, gt2 = [mod[:b, i * d:(i + 1) * d].reshape(b, 1, d) for i in range(6)]
    sh1c = jnp.broadcast_to(mod[b:b + 1, 0:d].reshape(1, 1, d), (b, 1, d))
    sc1c = jnp.broadcast_to(mod[b:b + 1, d:2 * d].reshape(1, 1, d), (b, 1, d))

    w_lat, w_ctx = _w_in_layout(w_in)
    g1 = norm1_g[0].reshape(1, d)
    p = _in_projection(x, g1, sc1, sh1, w_lat, tm=1024)
    pc = _in_projection(ctx, g1, sc1c, sh1c, w_ctx, tm=lc)

    def decay_w(w, off):
        return jnp.zeros((LANE, GLA_QK_W), F32).at[off:off + GLA_LOWRANK].set(w).astype(BF16)

    y_gla = _gla(p, pc,
                 decay_w(w_decay_f[0], SM_AF), b_decay_f[0].reshape(1, -1),
                 decay_w(w_decay_b[0], SM_AB), b_decay_b[0].reshape(1, -1),
                 gla_norm_g[0].reshape(1, -1))

    wq = jnp.pad(w_uq[0].reshape(MLA_Q_RANK, MLA_HEADS, MLA_QK_DIM),
                 ((0, 0), (0, 0), (0, MLA_QK_PAD - MLA_QK_DIM))).reshape(MLA_Q_RANK, -1).astype(BF16)
    wkv = w_ukv[0].astype(BF16)
    qn_g = jnp.pad(q_norm_g[0], (0, MLA_QK_PAD - MLA_QK_DIM)).reshape(1, -1)
    kn_nope = k_norm_g[0][:MLA_NOPE].reshape(1, -1)
    kn_rope = jnp.pad(k_norm_g[0][MLA_NOPE:], (0, LANE - MLA_ROPE)).reshape(1, -1)
    cos, sin = _rope_tables(l, True)
    cos_c, sin_c = _rope_tables(lc, False)
    q_m = _mla_queries(p, q_a_norm_g[0].reshape(1, -1), wq, qn_g, cos, sin, tm=512)
    kva = kv_a_norm_g[0].reshape(1, -1)
    k_m, v_m = _mla_keys_values(p, pc, kva, wkv, kn_nope, kn_rope,
                                jnp.concatenate([cos, cos_c]), jnp.concatenate([sin, sin_c]))
    y_mla = _attention(q_m, k_m, v_m, tq=512)

    y = _branch_merge(y_gla, y_mla, w_o_gla[0].astype(BF16), w_o_mla[0].astype(BF16), p, tm=1024, tn=512)
    w_router = jnp.concatenate(
        [w_router_group[0], w_router_expert[0],
         jnp.zeros((d, ROUTER_W - N_GROUPS - N_EXPERTS), F32)], axis=1).astype(BF16)
    b_router = jnp.concatenate(
        [b_router_group[0], b_router_expert[0],
         jnp.zeros((ROUTER_W - N_GROUPS - N_EXPERTS,), F32)]).reshape(1, -1)
    x1, h2, logits = _out_projection(y, w_out[0].astype(BF16), x, gt1, norm2_g[0].reshape(1, d),
                                     sc2, sh2, w_router, b_router, tm=256)

    e_pick, w_pick = _route(logits.reshape(t, ROUTER_W), tm=1024)
    plan = _dispatch_plan(e_pick[:, :2].reshape(-1))
    yp = _experts(plan, h2.reshape(t * ROW_TILES, LANE),
                  w_exp_gate[0], w_exp_up[0], w_exp_down[0])
    return _combine(x1, gt2, w_pick, yp, tm=512)
```

```python
import functools
import math

import jax
import jax.numpy as jnp
from jax import lax
from jax.experimental import pallas as pl
from jax.experimental.pallas import tpu as pltpu

F32 = jnp.float32
BF16 = jnp.bfloat16

D_MODEL = 2048
GRID_W = 64
EPS = 1e-6

GLA_HEADS = 4
GLA_DK = 256
GLA_DV = 512
GLA_LOWRANK = 16
GLA_TAU = 16.0
GLA_CHUNK = 64
GLA_GROUP = 8
GLA_QK_W = GLA_HEADS * GLA_DK
GLA_V_W = GLA_HEADS * GLA_DV

MLA_HEADS = 16
MLA_Q_RANK = 512
MLA_KV_RANK = 512
MLA_NOPE = 128
MLA_ROPE = 64
MLA_V = 128
MLA_QK_DIM = MLA_NOPE + MLA_ROPE
MLA_QK_PAD = 256
MLA_VT_ROWS = MLA_V + 16
ROPE_THETA = 10000.0
LOG2E = math.log2(math.e)

N_GROUPS = 8
EXP_PER_GROUP = 8
N_EXPERTS = N_GROUPS * EXP_PER_GROUP
D_EXPERT = 512
ROUTER_W = 128
MOE_BLOCK = 256
BULK_DMA = 1

LANE = 128
ROW_TILES = D_MODEL // (2 * LANE)
ROW_PITCH = 12
VMEM_LIMIT = 56 * 1024 * 1024

COL_Q = 0
COL_K = COL_Q + GLA_QK_W
COL_V = COL_K + GLA_QK_W
COL_R = COL_V + GLA_V_W
COL_GA = COL_R + GLA_V_W
COL_GB = COL_GA + D_MODEL
COL_CQ = COL_GB + D_MODEL
COL_CKV = COL_CQ + MLA_Q_RANK
COL_SMALL = COL_CKV + MLA_KV_RANK
SMALL_W = 256
NP_LAT = COL_SMALL + SMALL_W
SM_AF = MLA_ROPE
SM_AB = MLA_ROPE + GLA_LOWRANK
CCOL_K = 0
CCOL_V = CCOL_K + GLA_QK_W
CCOL_CKV = CCOL_V + GLA_V_W
CCOL_SMALL = CCOL_CKV + MLA_KV_RANK
NP_CTX = CCOL_SMALL + SMALL_W
INPROJ_TN = 1280


def _cparams(sem):
    return pltpu.CompilerParams(dimension_semantics=sem, vmem_limit_bytes=VMEM_LIMIT)


def _silu(x):
    return x * jax.nn.sigmoid(x)


def _pack_rows(x):
    half = x.shape[1] // 2
    lo = lax.bitcast_convert_type(x[:, :half].astype(BF16).astype(F32), jnp.uint32) >> 16
    hi = lax.bitcast_convert_type(x[:, half:].astype(BF16).astype(F32), jnp.uint32) & jnp.uint32(0xFFFF0000)
    packed = lo | hi
    return [packed[:, j * LANE:(j + 1) * LANE] for j in range(half // LANE)]


def _unpack_tile(t):
    lo = lax.bitcast_convert_type(t << 16, F32)
    hi = lax.bitcast_convert_type(t & jnp.uint32(0xFFFF0000), F32)
    return lo, hi


def _mod_kernel(c_ref, w_ref, b_ref, o_ref):
    a = _silu(c_ref[...]).astype(BF16)
    o_ref[...] = jnp.dot(a, w_ref[...].astype(BF16), preferred_element_type=F32) + b_ref[...]


def _modulation(c_rows, w_mod, b_mod):
    m, d = c_rows.shape
    n = w_mod.shape[1]
    tn = 1024
    return pl.pallas_call(
        _mod_kernel,
        grid=(n // tn,),
        in_specs=[pl.BlockSpec((m, d), lambda j: (0, 0)),
                  pl.BlockSpec((d, tn), lambda j: (0, j)),
                  pl.BlockSpec((1, tn), lambda j: (0, j))],
        out_specs=pl.BlockSpec((m, tn), lambda j: (0, j)),
        out_shape=jax.ShapeDtypeStruct((m, n), F32),
        compiler_params=_cparams(("parallel",)),
        name="modulation",
    )(c_rows, w_mod, b_mod.reshape(1, n))


def _inproj_kernel(x_ref, g_ref, sc_ref, sh_ref, w_ref, o_ref, h_ref):
    @pl.when(pl.program_id(2) == 0)
    def _():
        x = x_ref[...]
        ms = jnp.mean(x * x, axis=-1, keepdims=True)
        y = x * lax.rsqrt(ms + EPS) * g_ref[...]
        h_ref[...] = (y * (1.0 + sc_ref[...]) + sh_ref[...]).astype(BF16)

    o_ref[...] = lax.dot_general(h_ref[...], w_ref[...], (((1,), (1,)), ((), ())),
                                 preferred_element_type=F32).astype(o_ref.dtype)


def _in_projection(x, g, sc, sh, w, tm):
    b, l, d = x.shape
    tm = min(tm, l)
    n = w.shape[0]
    tn = INPROJ_TN
    return pl.pallas_call(
        _inproj_kernel,
        grid=(b, l // tm, n // tn),
        in_specs=[pl.BlockSpec((None, tm, d), lambda bi, i, j: (bi, i, 0)),
                  pl.BlockSpec((1, d), lambda bi, i, j: (0, 0)),
                  pl.BlockSpec((None, 1, d), lambda bi, i, j: (bi, 0, 0)),
                  pl.BlockSpec((None, 1, d), lambda bi, i, j: (bi, 0, 0)),
                  pl.BlockSpec((tn, d), lambda bi, i, j: (j, 0))],
        out_specs=pl.BlockSpec((None, tm, tn), lambda bi, i, j: (bi, i, j)),
        out_shape=jax.ShapeDtypeStruct((b, l, n), BF16),
        scratch_shapes=[pltpu.VMEM((tm, d), BF16)],
        compiler_params=_cparams(("parallel", "parallel", "arbitrary")),
        name="in_projection",
    )(x, g, sc, sh, w)


def _gla_kernel(q_ref, k_ref, v_ref, r_ref, sm_ref, kc_ref, vc_ref, smc_ref,
                wf_ref, bf_ref, wb_ref, bb_ref, gn_ref, y_ref, o_ref, sf_ref, sb_ref):
    c = GLA_CHUNK
    n_lat = q_ref.shape[0] // c
    n_ctx = kc_ref.shape[0] // c
    row = lax.broadcasted_iota(jnp.int32, (c, c), 0)
    col = lax.broadcasted_iota(jnp.int32, (c, c), 1)
    lower = row >= col
    upper = row <= col
    nt = (((1,), (1,)), ((), ()))
    tn = (((0,), (0,)), ((), ()))

    def cum_decay(sm, w_ref, b_ref, tri, n_chunks):
        z = jnp.dot(sm, w_ref[...], preferred_element_type=F32) + b_ref[...]
        g = (jnp.minimum(z, 0.0) - jnp.log(1.0 + jnp.exp(-jnp.abs(z)))) * (1.0 / GLA_TAU)
        g_hi = g.astype(BF16)
        g_lo = (g - g_hi.astype(F32)).astype(BF16)
        t = jnp.where(tri, 1.0, 0.0).astype(BF16)
        return [jnp.dot(t, g_hi[i * c:(i + 1) * c], preferred_element_type=F32)
                + jnp.dot(t, g_lo[i * c:(i + 1) * c], preferred_element_type=F32)
                for i in range(n_chunks)]

    fwd = (wf_ref, bf_ref, lower, c - 1, sf_ref)
    bwd = (wb_ref, bb_ref, upper, 0, sb_ref)

    def group(r0, n_chunks, direction, sm_r, k_r, v_r, q_r):
        w_ref, b_ref, tri, end_row, st_ref = direction
        rows = n_chunks * c
        k_all = k_r[pl.ds(r0, rows), :].astype(F32)
        v_all = v_r[pl.ds(r0, rows), :]
        q_all = None if q_r is None else q_r[pl.ds(r0, rows), :].astype(F32) * (GLA_DK ** -0.5)
        bc_all = cum_decay(sm_r[pl.ds(r0, rows), :], w_ref, b_ref, tri, n_chunks)
        local = []
        for i in range(n_chunks):
            sl = slice(i * c, (i + 1) * c)
            bc = bc_all[i]
            bend = bc[end_row:end_row + 1, :]
            k, v = k_all[sl], v_all[sl]
            ke = (k * jnp.exp(bend - bc)).astype(BF16)
            upd = lax.dot_general(v, ke, tn, preferred_element_type=F32)
            qd = o_intra = None
            if q_all is not None:
                qd = (q_all[sl] * jnp.exp(bc)).astype(BF16)
                ki = (k * jnp.exp(-bc)).astype(BF16)
                att = lax.dot_general(qd, ki, nt, preferred_element_type=F32)
                att = jnp.where(tri, att, 0.0).astype(BF16)
                o_intra = jnp.dot(att, v, preferred_element_type=F32)
            local.append((jnp.exp(bend), upd, qd, o_intra))
        st = st_ref[...]
        outs = [None] * n_chunks
        for i in (range(n_chunks) if end_row else reversed(range(n_chunks))):
            decay, upd, qd, o_intra = local[i]
            if q_all is not None:
                outs[i] = o_intra + lax.dot_general(qd, st.astype(BF16), nt, preferred_element_type=F32)
            st = st * decay + upd
        st_ref[...] = st
        return None if q_all is None else jnp.concatenate(outs, axis=0)

    def readout(r0, o):
        ms = jnp.mean(o * o, axis=-1, keepdims=True)
        yn = o * lax.rsqrt(ms + EPS) * gn_ref[...]
        r = r_ref[pl.ds(r0, o.shape[0]), :].astype(F32)
        y_ref[pl.ds(r0, o.shape[0]), :] = (yn * _silu(r)).astype(y_ref.dtype)

    sf_ref[...] = jnp.zeros_like(sf_ref)
    sb_ref[...] = jnp.zeros_like(sb_ref)
    group(0, n_ctx, fwd, smc_ref, kc_ref, vc_ref, None)
    group(0, n_ctx, bwd, smc_ref, kc_ref, vc_ref, None)

    g_lat = min(GLA_GROUP, n_lat // 2)
    assert n_lat % (2 * g_lat) == 0
    n_groups = n_lat // g_lat
    rows = g_lat * c

    def scan(i, direction):
        r0 = pl.multiple_of(i * rows, rows)
        return r0, group(r0, g_lat, direction, sm_ref, k_ref, v_ref, q_ref)

    def first_half(i, carry):
        for r0, o in (scan(i, fwd), scan(n_groups - 1 - i, bwd)):
            o_ref[pl.ds(r0, rows), :] = o
        return carry

    lax.fori_loop(0, n_groups // 2, first_half, 0)

    def second_half(i, carry):
        for r0, o in (scan(i, fwd), scan(n_groups - 1 - i, bwd)):
            readout(r0, o_ref[pl.ds(r0, rows), :] + o)
        return carry

    lax.fori_loop(n_groups // 2, n_groups, second_half, 0)


def _gla(p, pc, wf, bf, wb, bb, gn):
    b, l, _ = p.shape
    lc = pc.shape[1]
    dk, dv = GLA_DK, GLA_DV
    hmap = lambda off: (lambda bi, h: (bi, 0, off + h))
    wmap = lambda bi, h: (0, h)
    return pl.pallas_call(
        _gla_kernel,
        grid=(b, GLA_HEADS),
        in_specs=[pl.BlockSpec((None, l, dk), hmap(COL_Q // dk)),
                  pl.BlockSpec((None, l, dk), hmap(COL_K // dk)),
                  pl.BlockSpec((None, l, dv), hmap(COL_V // dv)),
                  pl.BlockSpec((None, l, dv), hmap(COL_R // dv)),
                  pl.BlockSpec((None, l, LANE), lambda bi, h: (bi, 0, COL_SMALL // LANE)),
                  pl.BlockSpec((None, lc, dk), hmap(CCOL_K // dk)),
                  pl.BlockSpec((None, lc, dv), hmap(CCOL_V // dv)),
                  pl.BlockSpec((None, lc, LANE), lambda bi, h: (bi, 0, CCOL_SMALL // LANE)),
                  pl.BlockSpec((LANE, dk), wmap),
                  pl.BlockSpec((1, dk), wmap),
                  pl.BlockSpec((LANE, dk), wmap),
                  pl.BlockSpec((1, dk), wmap),
                  pl.BlockSpec((1, dv), lambda bi, h: (0, 0))],
        out_specs=pl.BlockSpec((None, l, dv), lambda bi, h: (bi, 0, h)),
        out_shape=jax.ShapeDtypeStruct((b, l, GLA_V_W), BF16),
        scratch_shapes=[pltpu.VMEM((l, dv), F32), pltpu.VMEM((dv, dk), F32), pltpu.VMEM((dv, dk), F32)],
        compiler_params=_cparams(("parallel", "parallel")),
        name="gla",
    )(p, p, p, p, p, pc, pc, pc, wf, bf, wb, bb, gn)


def _rope_partner(t):
    lane = lax.broadcasted_iota(jnp.int32, t.shape, 1)
    first = (lane % 32) < 16
    return jnp.where(first, pltpu.roll(t, LANE - 16, 1), pltpu.roll(t, 16, 1))


def _mla_q_kernel(cq_ref, ga_ref, w_ref, gn_ref, cos_ref, sin_ref, o_ref):
    cq = cq_ref[...].astype(F32)
    cn = (cq * lax.rsqrt(jnp.mean(cq * cq, axis=-1, keepdims=True) + EPS) * ga_ref[...]).astype(BF16)
    cos = cos_ref[...]
    sin = sin_ref[...]
    scale = MLA_QK_DIM ** -0.5 * LOG2E
    for h in range(MLA_HEADS):
        qh = jnp.dot(cn, w_ref[:, h * MLA_QK_PAD:(h + 1) * MLA_QK_PAD], preferred_element_type=F32)
        ms = jnp.sum(qh * qh, axis=-1, keepdims=True) * (1.0 / MLA_QK_DIM)
        qn = qh * lax.rsqrt(ms + EPS) * gn_ref[...]
        t = qn[:, MLA_NOPE:]
        rot = t * cos + _rope_partner(t) * sin
        o_ref[h, :, :MLA_NOPE] = (qn[:, :MLA_NOPE] * scale).astype(o_ref.dtype)
        o_ref[h, :, MLA_NOPE:] = (rot * scale).astype(o_ref.dtype)


def _mla_queries(p, ga, wq, gn, cos, sin, tm):
    b, l, _ = p.shape
    tm = min(tm, l)
    return pl.pallas_call(
        _mla_q_kernel,
        grid=(b, l // tm),
        in_specs=[pl.BlockSpec((None, tm, MLA_Q_RANK), lambda bi, i: (bi, i, COL_CQ // MLA_Q_RANK)),
                  pl.BlockSpec((1, MLA_Q_RANK), lambda bi, i: (0, 0)),
                  pl.BlockSpec(wq.shape, lambda bi, i: (0, 0)),
                  pl.BlockSpec((1, MLA_QK_PAD), lambda bi, i: (0, 0)),
                  pl.BlockSpec((tm, LANE), lambda bi, i: (i, 0)),
                  pl.BlockSpec((tm, LANE), lambda bi, i: (i, 0))],
        out_specs=pl.BlockSpec((None, MLA_HEADS, tm, MLA_QK_PAD), lambda bi, i: (bi, 0, i, 0)),
        out_shape=jax.ShapeDtypeStruct((b, MLA_HEADS, l, MLA_QK_PAD), BF16),
        compiler_params=_cparams(("parallel", "parallel")),
        name="mla_queries",
    )(p, ga, wq, gn, cos, sin)


def _mla_kv_kernel(ckv_ref, sm_ref, ckvc_ref, smc_ref, ga_ref, w_ref, gk_ref, gr_ref,
                   cos_ref, sin_ref, k_ref, v_ref, *, n_lat):
    is_ctx = pl.program_id(1) >= n_lat
    ckv = jnp.where(is_ctx, ckvc_ref[...], ckv_ref[...]).astype(F32)
    cn = (ckv * lax.rsqrt(jnp.mean(ckv * ckv, axis=-1, keepdims=True) + EPS) * ga_ref[...]).astype(BF16)
    sm = jnp.where(is_ctx, smc_ref[...], sm_ref[...]).astype(F32)
    lane = lax.broadcasted_iota(jnp.int32, sm.shape, 1)
    kr = jnp.where(lane < MLA_ROPE, sm, 0.0)
    ss_r = jnp.sum(kr * kr, axis=-1, keepdims=True)
    krg = kr * gr_ref[...]
    rot = krg * cos_ref[...] + _rope_partner(krg) * sin_ref[...]
    hw = MLA_NOPE + MLA_V
    tm = ckv.shape[0]
    sub = lax.broadcasted_iota(jnp.int32, (MLA_VT_ROWS - MLA_V, tm), 0)
    ones_rows = jnp.where(sub == 0, 1.0, 0.0).astype(v_ref.dtype)
    for h in range(MLA_HEADS):
        kvh = jnp.dot(cn, w_ref[:, h * hw:(h + 1) * hw], preferred_element_type=F32)
        kn = kvh[:, :MLA_NOPE]
        ms = (jnp.sum(kn * kn, axis=-1, keepdims=True) + ss_r) * (1.0 / MLA_QK_DIM)
        rs = lax.rsqrt(ms + EPS)
        k_ref[h, :, :MLA_NOPE] = (kn * rs * gk_ref[...]).astype(k_ref.dtype)
        k_ref[h, :, MLA_NOPE:] = (rot * rs).astype(k_ref.dtype)
        v_ref[h, :MLA_V, :] = kvh[:, MLA_NOPE:].T.astype(v_ref.dtype)
        v_ref[h, MLA_V:, :] = ones_rows


def _mla_keys_values(p, pc, ga, wkv, gk, gr, cos, sin):
    b, l, _ = p.shape
    lc = pc.shape[1]
    tm = min(lc, l)
    assert l % tm == 0 and lc % tm == 0
    n_lat, n_ctx = l // tm, lc // tm
    lat = lambda blk: (lambda bi, i: (bi, jnp.minimum(i, n_lat - 1), blk))
    ctx = lambda blk: (lambda bi, i: (bi, jnp.maximum(i - n_lat, 0), blk))
    const = lambda bi, i: (0, 0)
    k_spec = pl.BlockSpec((None, MLA_HEADS, tm, MLA_QK_PAD), lambda bi, i: (bi, 0, i, 0))
    vt_spec = pl.BlockSpec((None, MLA_HEADS, MLA_VT_ROWS, tm), lambda bi, i: (bi, 0, 0, i))
    k_shape = jax.ShapeDtypeStruct((b, MLA_HEADS, l + lc, MLA_QK_PAD), BF16)
    vt_shape = jax.ShapeDtypeStruct((b, MLA_HEADS, MLA_VT_ROWS, l + lc), BF16)
    return pl.pallas_call(
        functools.partial(_mla_kv_kernel, n_lat=n_lat),
        grid=(b, n_lat + n_ctx),
        in_specs=[pl.BlockSpec((None, tm, MLA_KV_RANK), lat(COL_CKV // MLA_KV_RANK)),
                  pl.BlockSpec((None, tm, LANE), lat(COL_SMALL // LANE)),
                  pl.BlockSpec((None, tm, MLA_KV_RANK), ctx(CCOL_CKV // MLA_KV_RANK)),
                  pl.BlockSpec((None, tm, LANE), ctx(CCOL_SMALL // LANE)),
                  pl.BlockSpec((1, MLA_KV_RANK), const),
                  pl.BlockSpec(wkv.shape, const),
                  pl.BlockSpec((1, LANE), const),
                  pl.BlockSpec((1, LANE), const),
                  pl.BlockSpec((tm, LANE), lambda bi, i: (i, 0)),
                  pl.BlockSpec((tm, LANE), lambda bi, i: (i, 0))],
        out_specs=[k_spec, vt_spec],
        out_shape=[k_shape, vt_shape],
        compiler_params=_cparams(("parallel", "arbitrary")),
        name="mla_keys_values",
    )(p, p, pc, pc, ga, wkv, gk, gr, cos, sin)


def _attn_kernel(q_ref, k_ref, vt_ref, o_ref, s0_ref, s1_ref, m0_ref, m1_ref, *, tq):
    nt = (((1,), (1,)), ((), ()))
    n = q_ref.shape[0] // tq
    bufs = ((s0_ref, m0_ref), (s1_ref, m1_ref))

    def scores(i, slot):
        s_ref, m_ref = bufs[slot]
        r0 = pl.multiple_of(i * tq, tq)
        s = lax.dot_general(k_ref[...], q_ref[pl.ds(r0, tq), :], nt, preferred_element_type=F32)
        s_ref[...] = s
        m_ref[...] = jnp.max(s, axis=0, keepdims=True)

    def outputs(i, slot):
        s_ref, m_ref = bufs[slot]
        r0 = pl.multiple_of(i * tq, tq)
        p = jnp.exp2(s_ref[...] - m_ref[...]).astype(BF16)
        o = jnp.dot(vt_ref[...], p, preferred_element_type=F32)
        o_ref[pl.ds(r0, tq), :] = (o[:MLA_V] / o[MLA_V:MLA_V + 1]).T.astype(o_ref.dtype)

    scores(0, 0)
    if n % 2 == 0:
        def body(j, carry):
            scores(2 * j + 1, 1)
            outputs(2 * j, 0)
            scores(2 * j + 2, 0)
            outputs(2 * j + 1, 1)
            return carry

        lax.fori_loop(0, n // 2 - 1, body, 0)
        scores(n - 1, 1)
        outputs(n - 2, 0)
        outputs(n - 1, 1)
    else:
        assert n == 1
        outputs(0, 0)


def _attention(q, k, v, tq):
    b, h, l, dq = q.shape
    tq = min(tq, l)
    lk = k.shape[2]
    bh = lambda bi, hi: (bi, hi, 0, 0)
    return pl.pallas_call(
        functools.partial(_attn_kernel, tq=tq),
        grid=(b, h),
        in_specs=[pl.BlockSpec((None, None, l, dq), bh),
                  pl.BlockSpec((None, None, lk, dq), bh),
                  pl.BlockSpec((None, None, MLA_VT_ROWS, lk), bh)],
        out_specs=pl.BlockSpec((None, l, MLA_V), lambda bi, hi: (bi, 0, hi)),
        out_shape=jax.ShapeDtypeStruct((b, l, h * MLA_V), BF16),
        scratch_shapes=[pltpu.VMEM((lk, tq), F32), pltpu.VMEM((lk, tq), F32),
                        pltpu.VMEM((1, tq), F32), pltpu.VMEM((1, tq), F32)],
        compiler_params=_cparams(("parallel", "parallel")),
        name="mla_attention",
    )(q, k, v)


def _merge_kernel(yg_ref, ym_ref, wg_ref, wm_ref, ga_ref, gb_ref, o_ref):
    a = jnp.dot(yg_ref[...], wg_ref[...], preferred_element_type=F32)
    m = jnp.dot(ym_ref[...], wm_ref[...], preferred_element_type=F32)
    y = (jax.nn.sigmoid(ga_ref[...].astype(F32)) * a
         + jax.nn.sigmoid(gb_ref[...].astype(F32)) * m)
    o_ref[...] = y.astype(o_ref.dtype)


def _branch_merge(yg, ym, wg, wm, p, tm, tn):
    b, l, d = yg.shape
    tm = min(tm, l)
    n = wg.shape[1]
    return pl.pallas_call(
        _merge_kernel,
        grid=(b, l // tm, n // tn),
        in_specs=[pl.BlockSpec((None, tm, d), lambda bi, i, j: (bi, i, 0)),
                  pl.BlockSpec((None, tm, d), lambda bi, i, j: (bi, i, 0)),
                  pl.BlockSpec((d, tn), lambda bi, i, j: (0, j)),
                  pl.BlockSpec((d, tn), lambda bi, i, j: (0, j)),
                  pl.BlockSpec((None, tm, tn), lambda bi, i, j: (bi, i, COL_GA // tn + j)),
                  pl.BlockSpec((None, tm, tn), lambda bi, i, j: (bi, i, COL_GB // tn + j))],
        out_specs=pl.BlockSpec((None, tm, tn), lambda bi, i, j: (bi, i, j)),
        out_shape=jax.ShapeDtypeStruct((b, l, n), BF16),
        compiler_params=_cparams(("parallel", "parallel", "arbitrary")),
        name="branch_merge",
    )(yg, ym, wg, wm, p, p)


def _outproj_kernel(y_ref, w_ref, x_ref, gt_ref, g2_ref, sc_ref, sh_ref, wr_ref, br_ref,
                    x1_ref, h2_ref, lg_ref):
    x1 = x_ref[...] + gt_ref[...] * jnp.dot(y_ref[...], w_ref[...], preferred_element_type=F32)
    x1_ref[...] = x1
    ms = jnp.mean(x1 * x1, axis=-1, keepdims=True)
    h2 = x1 * lax.rsqrt(ms + EPS) * g2_ref[...] * (1.0 + sc_ref[...]) + sh_ref[...]
    tm = x1.shape[0]
    for j, tile in enumerate(_pack_rows(h2)):
        h2_ref[pl.ds(j, tm, stride=ROW_TILES), :] = tile
    lg_ref[...] = jnp.dot(h2.astype(BF16), wr_ref[...], preferred_element_type=F32) + br_ref[...]


def _out_projection(y, w, x, gt, g2, sc, sh, wr, br, tm):
    b, l, d = x.shape
    tm = min(tm, l)
    row = lambda bi, i: (bi, i, 0)
    per_b = lambda bi, i: (bi, 0, 0)
    const = lambda bi, i: (0, 0)
    return pl.pallas_call(
        _outproj_kernel,
        grid=(b, l // tm),
        in_specs=[pl.BlockSpec((None, tm, d), row),
                  pl.BlockSpec((d, d), const),
                  pl.BlockSpec((None, tm, d), row),
                  pl.BlockSpec((None, 1, d), per_b),
                  pl.BlockSpec((1, d), const),
                  pl.BlockSpec((None, 1, d), per_b),
                  pl.BlockSpec((None, 1, d), per_b),
                  pl.BlockSpec((d, ROUTER_W), const),
                  pl.BlockSpec((1, ROUTER_W), const)],
        out_specs=[pl.BlockSpec((None, tm, d), row),
                   pl.BlockSpec((None, tm * ROW_TILES, LANE), row),
                   pl.BlockSpec((None, tm, ROUTER_W), row)],
        out_shape=[jax.ShapeDtypeStruct((b, l, d), F32),
                   jax.ShapeDtypeStruct((b, l * ROW_TILES, LANE), jnp.uint32),
                   jax.ShapeDtypeStruct((b, l, ROUTER_W), F32)],
        compiler_params=_cparams(("parallel", "parallel")),
        name="out_projection",
    )(y, w, x, gt, g2, sc, sh, wr, br)


def _route_kernel(lg_ref, e_ref, w_ref):
    lg = lg_ref[...]
    lane = lax.broadcasted_iota(jnp.int32, lg.shape, 1)
    neg = jnp.float32(-jnp.inf)
    big = jnp.int32(1 << 20)
    is_g = lane < N_GROUPS
    gl = jnp.where(is_g, lg, neg)
    gm = jnp.max(gl, axis=-1, keepdims=True)
    gidx = jnp.min(jnp.where(gl == gm, lane, big), axis=-1, keepdims=True)
    gsum = jnp.sum(jnp.where(is_g, jnp.exp(gl - gm), 0.0), axis=-1, keepdims=True)
    p_g = 1.0 / gsum
    g_lo = N_GROUPS + gidx * EXP_PER_GROUP
    in_grp = (lane >= g_lo) & (lane < g_lo + EXP_PER_GROUP)
    el = jnp.where(in_grp, lg, neg)
    em = jnp.max(el, axis=-1, keepdims=True)
    z = jnp.sum(jnp.where(in_grp, jnp.exp(el - em), 0.0), axis=-1, keepdims=True)
    i1 = jnp.min(jnp.where(el == em, lane, big), axis=-1, keepdims=True)
    el2 = jnp.where(lane == i1, neg, el)
    em2 = jnp.max(el2, axis=-1, keepdims=True)
    i2 = jnp.min(jnp.where(el2 == em2, lane, big), axis=-1, keepdims=True)
    p1 = 1.0 / z
    p2 = jnp.exp(em2 - em) / z
    tot = p1 + p2
    e_ref[...] = jnp.where(lane == 0, i1 - N_GROUPS, jnp.where(lane == 1, i2 - N_GROUPS, 0))
    w_ref[...] = jnp.where(lane == 0, p_g * p1 / tot, jnp.where(lane == 1, p_g * p2 / tot, 0.0))


def _route(logits, tm):
    t = logits.shape[0]
    tm = min(tm, t)
    spec = pl.BlockSpec((tm, ROUTER_W), lambda i: (i, 0))
    return pl.pallas_call(
        _route_kernel,
        grid=(t // tm,),
        in_specs=[spec],
        out_specs=[spec, spec],
        out_shape=[jax.ShapeDtypeStruct((t, ROUTER_W), jnp.int32),
                   jax.ShapeDtypeStruct((t, ROUTER_W), F32)],
        compiler_params=_cparams(("parallel",)),
        name="moe_route",
    )(logits)


def _expert_kernel(be_ref, nv_ref, first_ref, nxe_ref, ws_ref, j0_ref, tok_ref, dst_ref,
                   h_hbm, wg_hbm, wu_hbm, wd_hbm, yp_hbm,
                   xbuf, ybuf, wgv, wuv, wdv, sem_in, sem_out, sem_w, sem_init):
    i = pl.program_id(0)
    n = pl.num_programs(0)
    blk = MOE_BLOCK
    rt, pitch = ROW_TILES, ROW_PITCH
    plane = yp_hbm.shape[0] // (2 * rt)
    t_rows = plane - blk
    slot_rows = blk * pitch
    par = i % 2
    cur = pl.multiple_of(par * slot_rows, 8)
    oth = pl.multiple_of((1 - par) * slot_rows, 8)

    def weight_copies(e, s):
        return (pltpu.make_async_copy(wg_hbm.at[e], wgv.at[s], sem_w.at[s, 0]),
                pltpu.make_async_copy(wu_hbm.at[e], wuv.at[s], sem_w.at[s, 1]),
                pltpu.make_async_copy(wd_hbm.at[e], wdv.at[s], sem_w.at[s, 2]))

    def gather_start(b, buf0, sem, r_lo=0, r_hi=blk):
        base = j0_ref[b]
        for r in range(r_lo, r_hi):
            row0 = pl.multiple_of(tok_ref[base + r] * rt, rt)
            pltpu.make_async_copy(h_hbm.at[pl.ds(row0, rt)],
                                  xbuf.at[pl.ds(buf0 + r * pitch, rt)], sem_in.at[sem]).start()

    def gather_wait(buf0, sem):
        pltpu.make_async_copy(h_hbm.at[pl.ds(0, blk * rt)], xbuf.at[pl.ds(buf0, blk * rt)],
                              sem_in.at[sem]).wait()

    def scatter_start(b, buf0, sem, r_lo=0, r_hi=blk):
        base = j0_ref[b]
        filled = nv_ref[b]
        for r in range(r_lo, r_hi):
            dst = jnp.where(r < filled, dst_ref[base + r], t_rows + r)
            row0 = pl.multiple_of(dst * rt, rt)
            pltpu.make_async_copy(ybuf.at[pl.ds(buf0 + r * pitch, rt)],
                                  yp_hbm.at[pl.ds(row0, rt)], sem_out.at[sem]).start(priority=BULK_DMA)

    def scatter_wait(buf0, sem):
        pltpu.make_async_copy(ybuf.at[pl.ds(buf0, blk * rt)], yp_hbm.at[pl.ds(0, blk * rt)],
                              sem_out.at[sem]).wait()

    @pl.when(i == 0)
    def _():
        for cp in weight_copies(be_ref[0], 0):
            cp.start(priority=BULK_DMA)
        gather_start(0, 0, 0)
        ybuf[...] = jnp.zeros_like(ybuf)
        zeros = ybuf.at[pl.ds(slot_rows, blk * rt)]
        pltpu.make_async_copy(zeros, yp_hbm.at[pl.ds(t_rows * rt, blk * rt)], sem_out.at[0]).start()
        spare = pltpu.make_async_copy(zeros, yp_hbm.at[pl.ds((plane + t_rows) * rt, blk * rt)], sem_init)
        spare.start()
        spare.wait()

    valid = nv_ref[i] > 0
    nxt = jnp.minimum(i + 1, n - 1)
    has_next = jnp.logical_and(i + 1 < n, nv_ref[nxt] > 0)

    @pl.when(jnp.logical_and(valid, first_ref[i] == 1))
    def _():
        ws = ws_ref[i]
        for cp in weight_copies(be_ref[i], ws):
            cp.wait()

        @pl.when(nxe_ref[i] >= 0)
        def _():
            for cp in weight_copies(nxe_ref[i], 1 - ws):
                cp.start(priority=BULK_DMA)

    @pl.when(valid)
    def _():
        ws = ws_ref[i]
        gather_wait(cur, par)
        scatter_wait(cur, par)
        prev = jnp.maximum(i - 1, 0)
        per = blk // (2 * rt)

        def read_tiles(first_chunk):
            halves = []
            for j in range(rt):
                lo = (first_chunk + j) * per
                gather_start(nxt, oth, 1 - par, lo, lo + per)
                halves.append(_unpack_tile(xbuf[pl.ds(cur + j, blk, stride=pitch), :]))
            return jnp.concatenate([h[0] for h in halves] + [h[1] for h in halves],
                                   axis=1).astype(BF16)

        scatter_start(prev, oth, 1 - par)
        g = jnp.dot(read_tiles(0), wgv[ws].astype(BF16), preferred_element_type=F32)
        u = jnp.dot(read_tiles(rt), wuv[ws].astype(BF16), preferred_element_type=F32)
        a = (_silu(g) * u).astype(BF16)
        y = jnp.dot(a, wdv[ws].astype(BF16), preferred_element_type=F32)
        for j, tile in enumerate(_pack_rows(y)):
            ybuf[pl.ds(cur + j, blk, stride=pitch), :] = tile

        @pl.when(jnp.logical_not(has_next))
        def _():
            scatter_wait(oth, 1 - par)
            scatter_start(i, cur, par)
            gather_wait(oth, 1 - par)
            scatter_wait(cur, par)


def _experts(plan, h2, w_gate, w_up, w_down):
    d = w_gate.shape[1]
    rt, pitch = ROW_TILES, ROW_PITCH
    assert d == 2 * rt * LANE
    t = h2.shape[0] // rt
    n_blocks = plan[0].shape[0]
    blk = MOE_BLOCK
    de = w_gate.shape[2]
    any_spec = pl.BlockSpec(memory_space=pl.ANY)
    return pl.pallas_call(
        _expert_kernel,
        grid_spec=pltpu.PrefetchScalarGridSpec(
            num_scalar_prefetch=len(plan),
            grid=(n_blocks,),
            in_specs=[any_spec, any_spec, any_spec, any_spec],
            out_specs=any_spec,
            scratch_shapes=[pltpu.VMEM((2 * blk * pitch, LANE), jnp.uint32),
                            pltpu.VMEM((2 * blk * pitch, LANE), jnp.uint32),
                            pltpu.VMEM((2, d, de), F32),
                            pltpu.VMEM((2, d, de), F32),
                            pltpu.VMEM((2, de, d), F32),
                            pltpu.SemaphoreType.DMA((2,)),
                            pltpu.SemaphoreType.DMA((2,)),
                            pltpu.SemaphoreType.DMA((2, 3)),
                            pltpu.SemaphoreType.DMA],
        ),
        out_shape=jax.ShapeDtypeStruct((2 * (t + blk) * rt, LANE), jnp.uint32),
        compiler_params=_cparams(("arbitrary",)),
        name="moe_experts",
    )(*plan, h2, w_gate, w_up, w_down).reshape(2, (t + blk) * rt, LANE)


def _combine_kernel(x_ref, gt_ref, w_ref, yp_ref, o_ref):
    tm = x_ref.shape[0]
    w = w_ref[...]
    w0, w1 = w[:, 0:1], w[:, 1:2]
    lows, highs = [], []
    for j in range(ROW_TILES):
        lo0, hi0 = _unpack_tile(yp_ref[0, pl.ds(j, tm, stride=ROW_TILES), :])
        lo1, hi1 = _unpack_tile(yp_ref[1, pl.ds(j, tm, stride=ROW_TILES), :])
        lows.append(w0 * lo0 + w1 * lo1)
        highs.append(w0 * hi0 + w1 * hi1)
    o_ref[...] = x_ref[...] + gt_ref[...] * jnp.concatenate(lows + highs, axis=1)


def _combine(x1, gt, w_pick, yp, tm):
    b, l, d = x1.shape
    tm = min(tm, l)
    nb = l // tm
    return pl.pallas_call(
        _combine_kernel,
        grid=(b, nb),
        in_specs=[pl.BlockSpec((None, tm, d), lambda bi, i: (bi, i, 0)),
                  pl.BlockSpec((None, 1, d), lambda bi, i: (bi, 0, 0)),
                  pl.BlockSpec((tm, ROUTER_W), lambda bi, i: (bi * nb + i, 0)),
                  pl.BlockSpec((2, tm * ROW_TILES, LANE), lambda bi, i: (0, bi * nb + i, 0))],
        out_specs=pl.BlockSpec((None, tm, d), lambda bi, i: (bi, i, 0)),
        out_shape=jax.ShapeDtypeStruct((b, l, d), F32),
        compiler_params=_cparams(("parallel", "parallel")),
        name="moe_combine",
    )(x1, gt, w_pick, yp)


def _dispatch_plan(eid):
    blk = MOE_BLOCK
    n_assign = eid.shape[0]
    n_tok = n_assign // 2
    n_blocks = (n_assign + blk - 1) // blk + N_EXPERTS
    order = jnp.argsort(eid).astype(jnp.int32)
    e_ids = jnp.arange(N_EXPERTS, dtype=jnp.int32)
    counts = jnp.sum((eid[None, :] == e_ids[:, None]).astype(jnp.int32), axis=1)
    start = jnp.cumsum(counts) - counts
    nblk = (counts + blk - 1) // blk
    bend = jnp.cumsum(nblk)
    b_ids = jnp.arange(n_blocks, dtype=jnp.int32)
    block_e = jnp.minimum(jnp.searchsorted(bend, b_ids, side='right'), N_EXPERTS - 1).astype(jnp.int32)
    used = b_ids < bend[-1]
    within = b_ids - (bend - nblk)[block_e]
    block_nv = jnp.where(used, jnp.clip(counts[block_e] - within * blk, 0, blk), 0).astype(jnp.int32)
    block_j0 = jnp.where(used, start[block_e] + within * blk, 0).astype(jnp.int32)
    first = jnp.concatenate([jnp.ones((1,), jnp.int32),
                             (block_e[1:] != block_e[:-1]).astype(jnp.int32)])
    w_slot = ((jnp.cumsum(first) - 1) % 2).astype(jnp.int32)
    nonempty = jnp.where(counts > 0, e_ids, N_EXPERTS)
    later = jnp.flip(lax.cummin(jnp.flip(nonempty)))
    next_e = jnp.concatenate([later[1:], jnp.full((1,), N_EXPERTS, jnp.int32)])
    next_e = jnp.where(next_e >= N_EXPERTS, -1, next_e)[block_e]
    order = jnp.concatenate([order, jnp.zeros((blk,), jnp.int32)])
    tok_sorted = order // 2
    dst_sorted = (order % 2) * (n_tok + blk) + order // 2
    return (block_e, block_nv, first, next_e, w_slot, block_j0, tok_sorted, dst_sorted)


def _rope_tables(length, with_rope):
    half = MLA_ROPE // 4
    if with_rope:
        rows = length // GRID_W
        row = jnp.repeat(jnp.arange(rows), GRID_W).astype(F32)
        col = jnp.tile(jnp.arange(GRID_W), rows).astype(F32)
        inv = 1.0 / (ROPE_THETA ** (jnp.arange(half, dtype=F32) / half))
        ang_r = row[:, None] * inv
        ang_c = col[:, None] * inv
        cos = jnp.concatenate([jnp.cos(ang_r), jnp.cos(ang_r), jnp.cos(ang_c), jnp.cos(ang_c)], axis=1)
        sin = jnp.concatenate([-jnp.sin(ang_r), jnp.sin(ang_r), -jnp.sin(ang_c), jnp.sin(ang_c)], axis=1)
    else:
        cos = jnp.ones((length, MLA_ROPE), F32)
        sin = jnp.zeros((length, MLA_ROPE), F32)
    pad = ((0, 0), (0, LANE - MLA_ROPE))
    return jnp.pad(cos, pad), jnp.pad(sin, pad)


_IN_SIZES = (GLA_QK_W, GLA_QK_W, GLA_V_W, GLA_V_W, GLA_LOWRANK, GLA_LOWRANK,
             MLA_Q_RANK, MLA_KV_RANK, MLA_ROPE, D_MODEL, D_MODEL)
(SRC_Q, SRC_K, SRC_V, SRC_R, SRC_AF, SRC_AB, SRC_CQ, SRC_CKV, SRC_KR, SRC_GA, SRC_GB,
 SRC_END) = [sum(_IN_SIZES[:i]) for i in range(len(_IN_SIZES) + 1)]


def _w_in_layout_kernel(w_ref, lat_ref, ctx_ref):
    def rows(a, b):
        return w_ref[a:b, :].astype(BF16)

    small_pad = jnp.zeros((SMALL_W - MLA_ROPE - 2 * GLA_LOWRANK, w_ref.shape[1]), BF16)
    lat_ref[COL_Q:COL_GA, :] = rows(SRC_Q, SRC_AF)
    lat_ref[COL_GA:COL_CQ, :] = rows(SRC_GA, SRC_END)
    lat_ref[COL_CQ:COL_SMALL, :] = rows(SRC_CQ, SRC_KR)
    ctx_ref[CCOL_K:CCOL_CKV, :] = rows(SRC_K, SRC_R)
    ctx_ref[CCOL_CKV:CCOL_SMALL, :] = rows(SRC_CKV, SRC_KR)
    for ref, off in ((lat_ref, COL_SMALL), (ctx_ref, CCOL_SMALL)):
        ref[off:off + MLA_ROPE, :] = rows(SRC_KR, SRC_GA)
        ref[off + MLA_ROPE:off + SM_AB + GLA_LOWRANK, :] = rows(SRC_AF, SRC_CQ)
        ref[off + SM_AB + GLA_LOWRANK:off + SMALL_W, :] = small_pad


def _w_in_layout(w_in, tk=256):
    wt = jnp.swapaxes(w_in, 1, 2)
    _, n, d = wt.shape
    assert n == SRC_END
    return pl.pallas_call(
        _w_in_layout_kernel,
        grid=(d // tk,),
        in_specs=[pl.BlockSpec((None, n, tk), lambda i: (0, 0, i))],
        out_specs=[pl.BlockSpec((NP_LAT, tk), lambda i: (0, i)),
                   pl.BlockSpec((NP_CTX, tk), lambda i: (0, i))],
        out_shape=[jax.ShapeDtypeStruct((NP_LAT, d), BF16),
                   jax.ShapeDtypeStruct((NP_CTX, d), BF16)],
        compiler_params=_cparams(("parallel",)),
        name="w_in_layout",
    )(wt)


def kernel(x, c, ctx, c_ctx, w_mod, b_mod, norm1_g, norm2_g, w_in, w_decay_f, b_decay_f, w_decay_b, b_decay_b, gla_norm_g, q_a_norm_g, w_uq, kv_a_norm_g, w_ukv, q_norm_g, k_norm_g, w_o_gla, w_o_mla, w_out, w_router_group, b_router_group, w_router_expert, b_router_expert, w_exp_gate, w_exp_up, w_exp_down):
    assert w_mod.shape[0] == 1, "single-layer block"
    b, l, d = x.shape
    lc = ctx.shape[1]
    t = b * l

    c_rows = jnp.concatenate([c, c_ctx[None, :], jnp.zeros((8 - b - 1, d), F32)], axis=0)
    mod = _modulation(c_rows, w_mod[0], b_mod[0])
    sh1, sc1, gt1, sh2, sc2, gt2 = [mod[:b, i * d:(i + 1) * d].reshape(b, 1, d) for i in range(6)]
    sh1c = jnp.broadcast_to(mod[b:b + 1, 0:d].reshape(1, 1, d), (b, 1, d))
    sc1c = jnp.broadcast_to(mod[b:b + 1, d:2 * d].reshape(1, 1, d), (b, 1, d))

    w_lat, w_ctx = _w_in_layout(w_in)
    g1 = norm1_g[0].reshape(1, d)
    p = _in_projection(x, g1, sc1, sh1, w_lat, tm=1024)
    pc = _in_projection(ctx, g1, sc1c, sh1c, w_ctx, tm=lc)

    def decay_w(w, off):
        return jnp.zeros((LANE, GLA_QK_W), F32).at[off:off + GLA_LOWRANK].set(w).astype(BF16)

    y_gla = _gla(p, pc,
                 decay_w(w_decay_f[0], SM_AF), b_decay_f[0].reshape(1, -1),
                 decay_w(w_decay_b[0], SM_AB), b_decay_b[0].reshape(1, -1),
                 gla_norm_g[0].reshape(1, -1))

    wq = jnp.pad(w_uq[0].reshape(MLA_Q_RANK, MLA_HEADS, MLA_QK_DIM),
                 ((0, 0), (0, 0), (0, MLA_QK_PAD - MLA_QK_DIM))).reshape(MLA_Q_RANK, -1).astype(BF16)
    wkv = w_ukv[0].astype(BF16)
    qn_g = jnp.pad(q_norm_g[0], (0, MLA_QK_PAD - MLA_QK_DIM)).reshape(1, -1)
    kn_nope = k_norm_g[0][:MLA_NOPE].reshape(1, -1)
    kn_rope = jnp.pad(k_norm_g[0][MLA_NOPE:], (0, LANE - MLA_ROPE)).reshape(1, -1)
    cos, sin = _rope_tables(l, True)
    cos_c, sin_c = _rope_tables(lc, False)
    q_m = _mla_queries(p, q_a_norm_g[0].reshape(1, -1), wq, qn_g, cos, sin, tm=512)
    kva = kv_a_norm_g[0].reshape(1, -1)
    k_m, v_m = _mla_keys_values(p, pc, kva, wkv, kn_nope, kn_rope,
                                jnp.concatenate([cos, cos_c]), jnp.concatenate([sin, sin_c]))
    y_mla = _attention(q_m, k_m, v_m, tq=512)

    y = _branch_merge(y_gla, y_mla, w_o_gla[0].astype(BF16), w_o_mla[0].astype(BF16), p, tm=1024, tn=512)
    w_router = jnp.concatenate(
        [w_router_group[0], w_router_expert[0],
         jnp.zeros((d, ROUTER_W - N_GROUPS - N_EXPERTS), F32)], axis=1).astype(BF16)
    b_router = jnp.concatenate(
        [b_router_group[0], b_router_expert[0],
         jnp.zeros((ROUTER_W - N_GROUPS - N_EXPERTS,), F32)]).reshape(1, -1)
    x1, h2, logits = _out_projection(y, w_out[0].astype(BF16), x, gt1, norm2_g[0].reshape(1, d),
                                     sc2, sh2, w_router, b_router, tm=256)

    e_pick, w_pick = _route(logits.reshape(t, ROUTER_W), tm=1024)
    plan = _dispatch_plan(e_pick[:, :2].reshape(-1))
    yp = _experts(plan, h2.reshape(t * ROW_TILES, LANE),
                  w_exp_gate[0], w_exp_up[0], w_exp_down[0])
    return _combine(x1, gt2, w_pick, yp, tm=512)
```

```python
import functools
import math

import jax
import jax.numpy as jnp
from jax import lax
from jax.experimental import pallas as pl
from jax.experimental.pallas import tpu as pltpu

F32 = jnp.float32
BF16 = jnp.bfloat16

D_MODEL = 2048
GRID_W = 64
EPS = 1e-6

GLA_HEADS = 4
GLA_DK = 256
GLA_DV = 512
GLA_LOWRANK = 16
GLA_TAU = 16.0
GLA_CHUNK = 64
GLA_GROUP = 8
GLA_QK_W = GLA_HEADS * GLA_DK
GLA_V_W = GLA_HEADS * GLA_DV

MLA_HEADS = 16
MLA_Q_RANK = 512
MLA_KV_RANK = 512
MLA_NOPE = 128
MLA_ROPE = 64
MLA_V = 128
MLA_QK_DIM = MLA_NOPE + MLA_ROPE
MLA_QK_PAD = 256
MLA_VT_ROWS = MLA_V + 16
ROPE_THETA = 10000.0
LOG2E = math.log2(math.e)

N_GROUPS = 8
EXP_PER_GROUP = 8
N_EXPERTS = N_GROUPS * EXP_PER_GROUP
D_EXPERT = 512
ROUTER_W = 128
MOE_BLOCK = 256
MOE_W_SLOTS = 3
BULK_DMA = 1

LANE = 128
ROW_TILES = D_MODEL // (2 * LANE)
ROW_PITCH = 12
VMEM_LIMIT = 56 * 1024 * 1024

COL_Q = 0
COL_K = COL_Q + GLA_QK_W
COL_V = COL_K + GLA_QK_W
COL_R = COL_V + GLA_V_W
COL_GA = COL_R + GLA_V_W
COL_GB = COL_GA + D_MODEL
COL_CQ = COL_GB + D_MODEL
COL_CKV = COL_CQ + MLA_Q_RANK
COL_SMALL = COL_CKV + MLA_KV_RANK
SMALL_W = 256
NP_LAT = COL_SMALL + SMALL_W
SM_AF = MLA_ROPE
SM_AB = MLA_ROPE + GLA_LOWRANK
CCOL_K = 0
CCOL_V = CCOL_K + GLA_QK_W
CCOL_CKV = CCOL_V + GLA_V_W
CCOL_SMALL = CCOL_CKV + MLA_KV_RANK
NP_CTX = CCOL_SMALL + SMALL_W
INPROJ_TN = 1280


def _cparams(sem):
    return pltpu.CompilerParams(dimension_semantics=sem, vmem_limit_bytes=VMEM_LIMIT)


def _silu(x):
    return x * jax.nn.sigmoid(x)


def _pack_rows(x):
    half = x.shape[1] // 2
    lo = lax.bitcast_convert_type(x[:, :half].astype(BF16).astype(F32), jnp.uint32) >> 16
    hi = lax.bitcast_convert_type(x[:, half:].astype(BF16).astype(F32), jnp.uint32) & jnp.uint32(0xFFFF0000)
    packed = lo | hi
    return [packed[:, j * LANE:(j + 1) * LANE] for j in range(half // LANE)]


def _unpack_tile(t):
    lo = lax.bitcast_convert_type(t << 16, F32)
    hi = lax.bitcast_convert_type(t & jnp.uint32(0xFFFF0000), F32)
    return lo, hi


def _mod_kernel(c_ref, w_ref, b_ref, o_ref):
    a = _silu(c_ref[...]).astype(BF16)
    o_ref[...] = jnp.dot(a, w_ref[...].astype(BF16), preferred_element_type=F32) + b_ref[...]


def _modulation(c_rows, w_mod, b_mod):
    m, d = c_rows.shape
    n = w_mod.shape[1]
    tn = 1024
    return pl.pallas_call(
        _mod_kernel,
        grid=(n // tn,),
        in_specs=[pl.BlockSpec((m, d), lambda j: (0, 0)),
                  pl.BlockSpec((d, tn), lambda j: (0, j)),
                  pl.BlockSpec((1, tn), lambda j: (0, j))],
        out_specs=pl.BlockSpec((m, tn), lambda j: (0, j)),
        out_shape=jax.ShapeDtypeStruct((m, n), F32),
        compiler_params=_cparams(("parallel",)),
        name="modulation",
    )(c_rows, w_mod, b_mod.reshape(1, n))


def _inproj_kernel(x_ref, g_ref, sc_ref, sh_ref, w_ref, o_ref, h_ref):
    @pl.when(pl.program_id(2) == 0)
    def _():
        x = x_ref[...]
        ms = jnp.mean(x * x, axis=-1, keepdims=True)
        y = x * lax.rsqrt(ms + EPS) * g_ref[...]
        h_ref[...] = (y * (1.0 + sc_ref[...]) + sh_ref[...]).astype(BF16)

    o_ref[...] = lax.dot_general(h_ref[...], w_ref[...], (((1,), (1,)), ((), ())),
                                 preferred_element_type=F32).astype(o_ref.dtype)


def _in_projection(x, g, sc, sh, w, tm):
    b, l, d = x.shape
    tm = min(tm, l)
    n = w.shape[0]
    tn = INPROJ_TN
    return pl.pallas_call(
        _inproj_kernel,
        grid=(b, l // tm, n // tn),
        in_specs=[pl.BlockSpec((None, tm, d), lambda bi, i, j: (bi, i, 0)),
                  pl.BlockSpec((1, d), lambda bi, i, j: (0, 0)),
                  pl.BlockSpec((None, 1, d), lambda bi, i, j: (bi, 0, 0)),
                  pl.BlockSpec((None, 1, d), lambda bi, i, j: (bi, 0, 0)),
                  pl.BlockSpec((tn, d), lambda bi, i, j: (j, 0))],
        out_specs=pl.BlockSpec((None, tm, tn), lambda bi, i, j: (bi, i, j)),
        out_shape=jax.ShapeDtypeStruct((b, l, n), BF16),
        scratch_shapes=[pltpu.VMEM((tm, d), BF16)],
        compiler_params=_cparams(("parallel", "parallel", "arbitrary")),
        name="in_projection",
    )(x, g, sc, sh, w)


def _gla_kernel(q_ref, k_ref, v_ref, r_ref, sm_ref, kc_ref, vc_ref, smc_ref,
                wf_ref, bf_ref, wb_ref, bb_ref, gn_ref, y_ref, o_ref, sf_ref, sb_ref):
    c = GLA_CHUNK
    n_lat = q_ref.shape[0] // c
    n_ctx = kc_ref.shape[0] // c
    row = lax.broadcasted_iota(jnp.int32, (c, c), 0)
    col = lax.broadcasted_iota(jnp.int32, (c, c), 1)
    lower = row >= col
    upper = row <= col
    nt = (((1,), (1,)), ((), ()))
    tn = (((0,), (0,)), ((), ()))

    def cum_decay(sm, w_ref, b_ref, tri, n_chunks):
        z = jnp.dot(sm, w_ref[...], preferred_element_type=F32) + b_ref[...]
        g = (jnp.minimum(z, 0.0) - jnp.log(1.0 + jnp.exp(-jnp.abs(z)))) * (1.0 / GLA_TAU)
        g_hi = g.astype(BF16)
        g_lo = (g - g_hi.astype(F32)).astype(BF16)
        t = jnp.where(tri, 1.0, 0.0).astype(BF16)
        return [jnp.dot(t, g_hi[i * c:(i + 1) * c], preferred_element_type=F32)
                + jnp.dot(t, g_lo[i * c:(i + 1) * c], preferred_element_type=F32)
                for i in range(n_chunks)]

    fwd = (wf_ref, bf_ref, lower, c - 1, sf_ref)
    bwd = (wb_ref, bb_ref, upper, 0, sb_ref)

    def group(r0, n_chunks, direction, sm_r, k_r, v_r, q_r):
        w_ref, b_ref, tri, end_row, st_ref = direction
        rows = n_chunks * c
        k_all = k_r[pl.ds(r0, rows), :].astype(F32)
        v_all = v_r[pl.ds(r0, rows), :]
        q_all = None if q_r is None else q_r[pl.ds(r0, rows), :].astype(F32) * (GLA_DK ** -0.5)
        bc_all = cum_decay(sm_r[pl.ds(r0, rows), :], w_ref, b_ref, tri, n_chunks)
        local = []
        for i in range(n_chunks):
            sl = slice(i * c, (i + 1) * c)
            bc = bc_all[i]
            bend = bc[end_row:end_row + 1, :]
            k, v = k_all[sl], v_all[sl]
            ke = (k * jnp.exp(bend - bc)).astype(BF16)
            upd = lax.dot_general(v, ke, tn, preferred_element_type=F32)
            qd = o_intra = None
            if q_all is not None:
                qd = (q_all[sl] * jnp.exp(bc)).astype(BF16)
                ki = (k * jnp.exp(-bc)).astype(BF16)
                att = lax.dot_general(qd, ki, nt, preferred_element_type=F32)
                att = jnp.where(tri, att, 0.0).astype(BF16)
                o_intra = jnp.dot(att, v, preferred_element_type=F32)
            local.append((jnp.exp(bend), upd, qd, o_intra))
        st = st_ref[...]
        outs = [None] * n_chunks
        for i in (range(n_chunks) if end_row else reversed(range(n_chunks))):
            decay, upd, qd, o_intra = local[i]
            if q_all is not None:
                outs[i] = o_intra + lax.dot_general(qd, st.astype(BF16), nt, preferred_element_type=F32)
            st = st * decay + upd
        st_ref[...] = st
        return None if q_all is None else jnp.concatenate(outs, axis=0)

    def readout(r0, o):
        ms = jnp.mean(o * o, axis=-1, keepdims=True)
        yn = o * lax.rsqrt(ms + EPS) * gn_ref[...]
        r = r_ref[pl.ds(r0, o.shape[0]), :].astype(F32)
        y_ref[pl.ds(r0, o.shape[0]), :] = (yn * _silu(r)).astype(y_ref.dtype)

    sf_ref[...] = jnp.zeros_like(sf_ref)
    sb_ref[...] = jnp.zeros_like(sb_ref)
    group(0, n_ctx, fwd, smc_ref, kc_ref, vc_ref, None)
    group(0, n_ctx, bwd, smc_ref, kc_ref, vc_ref, None)

    g_lat = min(GLA_GROUP, n_lat // 2)
    assert n_lat % (2 * g_lat) == 0
    n_groups = n_lat // g_lat
    rows = g_lat * c

    def scan(i, direction):
        r0 = pl.multiple_of(i * rows, rows)
        return r0, group(r0, g_lat, direction, sm_ref, k_ref, v_ref, q_ref)

    def first_half(i, carry):
        for r0, o in (scan(i, fwd), scan(n_groups - 1 - i, bwd)):
            o_ref[pl.ds(r0, rows), :] = o
        return carry

    lax.fori_loop(0, n_groups // 2, first_half, 0)

    def second_half(i, carry):
        for r0, o in (scan(i, fwd), scan(n_groups - 1 - i, bwd)):
            readout(r0, o_ref[pl.ds(r0, rows), :] + o)
        return carry

    lax.fori_loop(n_groups // 2, n_groups, second_half, 0)


def _gla(p, pc, wf, bf, wb, bb, gn):
    b, l, _ = p.shape
    lc = pc.shape[1]
    dk, dv = GLA_DK, GLA_DV
    hmap = lambda off: (lambda bi, h: (bi, 0, off + h))
    wmap = lambda bi, h: (0, h)
    return pl.pallas_call(
        _gla_kernel,
        grid=(b, GLA_HEADS),
        in_specs=[pl.BlockSpec((None, l, dk), hmap(COL_Q // dk)),
                  pl.BlockSpec((None, l, dk), hmap(COL_K // dk)),
                  pl.BlockSpec((None, l, dv), hmap(COL_V // dv)),
                  pl.BlockSpec((None, l, dv), hmap(COL_R // dv)),
                  pl.BlockSpec((None, l, LANE), lambda bi, h: (bi, 0, COL_SMALL // LANE)),
                  pl.BlockSpec((None, lc, dk), hmap(CCOL_K // dk)),
                  pl.BlockSpec((None, lc, dv), hmap(CCOL_V // dv)),
                  pl.BlockSpec((None, lc, LANE), lambda bi, h: (bi, 0, CCOL_SMALL // LANE)),
                  pl.BlockSpec((LANE, dk), wmap),
                  pl.BlockSpec((1, dk), wmap),
                  pl.BlockSpec((LANE, dk), wmap),
                  pl.BlockSpec((1, dk), wmap),
                  pl.BlockSpec((1, dv), lambda bi, h: (0, 0))],
        out_specs=pl.BlockSpec((None, l, dv), lambda bi, h: (bi, 0, h)),
        out_shape=jax.ShapeDtypeStruct((b, l, GLA_V_W), BF16),
        scratch_shapes=[pltpu.VMEM((l, dv), F32), pltpu.VMEM((dv, dk), F32), pltpu.VMEM((dv, dk), F32)],
        compiler_params=_cparams(("parallel", "parallel")),
        name="gla",
    )(p, p, p, p, p, pc, pc, pc, wf, bf, wb, bb, gn)


def _rope_partner(t):
    lane = lax.broadcasted_iota(jnp.int32, t.shape, 1)
    first = (lane % 32) < 16
    return jnp.where(first, pltpu.roll(t, LANE - 16, 1), pltpu.roll(t, 16, 1))


def _mla_q_kernel(cq_ref, ga_ref, w_ref, gn_ref, cos_ref, sin_ref, o_ref):
    cq = cq_ref[...].astype(F32)
    cn = (cq * lax.rsqrt(jnp.mean(cq * cq, axis=-1, keepdims=True) + EPS) * ga_ref[...]).astype(BF16)
    cos = cos_ref[...]
    sin = sin_ref[...]
    scale = MLA_QK_DIM ** -0.5 * LOG2E
    for h in range(MLA_HEADS):
        qh = jnp.dot(cn, w_ref[:, h * MLA_QK_PAD:(h + 1) * MLA_QK_PAD], preferred_element_type=F32)
        ms = jnp.sum(qh * qh, axis=-1, keepdims=True) * (1.0 / MLA_QK_DIM)
        qn = qh * lax.rsqrt(ms + EPS) * gn_ref[...]
        t = qn[:, MLA_NOPE:]
        rot = t * cos + _rope_partner(t) * sin
        o_ref[h, :, :MLA_NOPE] = (qn[:, :MLA_NOPE] * scale).astype(o_ref.dtype)
        o_ref[h, :, MLA_NOPE:] = (rot * scale).astype(o_ref.dtype)


def _mla_queries(p, ga, wq, gn, cos, sin, tm):
    b, l, _ = p.shape
    tm = min(tm, l)
    return pl.pallas_call(
        _mla_q_kernel,
        grid=(b, l // tm),
        in_specs=[pl.BlockSpec((None, tm, MLA_Q_RANK), lambda bi, i: (bi, i, COL_CQ // MLA_Q_RANK)),
                  pl.BlockSpec((1, MLA_Q_RANK), lambda bi, i: (0, 0)),
                  pl.BlockSpec(wq.shape, lambda bi, i: (0, 0)),
                  pl.BlockSpec((1, MLA_QK_PAD), lambda bi, i: (0, 0)),
                  pl.BlockSpec((tm, LANE), lambda bi, i: (i, 0)),
                  pl.BlockSpec((tm, LANE), lambda bi, i: (i, 0))],
        out_specs=pl.BlockSpec((None, MLA_HEADS, tm, MLA_QK_PAD), lambda bi, i: (bi, 0, i, 0)),
        out_shape=jax.ShapeDtypeStruct((b, MLA_HEADS, l, MLA_QK_PAD), BF16),
        compiler_params=_cparams(("parallel", "parallel")),
        name="mla_queries",
    )(p, ga, wq, gn, cos, sin)


def _mla_kv_kernel(ckv_ref, sm_ref, ckvc_ref, smc_ref, ga_ref, w_ref, gk_ref, gr_ref,
                   cos_ref, sin_ref, k_ref, v_ref, *, n_lat):
    is_ctx = pl.program_id(1) >= n_lat
    ckv = jnp.where(is_ctx, ckvc_ref[...], ckv_ref[...]).astype(F32)
    cn = (ckv * lax.rsqrt(jnp.mean(ckv * ckv, axis=-1, keepdims=True) + EPS) * ga_ref[...]).astype(BF16)
    sm = jnp.where(is_ctx, smc_ref[...], sm_ref[...]).astype(F32)
    lane = lax.broadcasted_iota(jnp.int32, sm.shape, 1)
    kr = jnp.where(lane < MLA_ROPE, sm, 0.0)
    ss_r = jnp.sum(kr * kr, axis=-1, keepdims=True)
    krg = kr * gr_ref[...]
    rot = krg * cos_ref[...] + _rope_partner(krg) * sin_ref[...]
    hw = MLA_NOPE + MLA_V
    tm = ckv.shape[0]
    sub = lax.broadcasted_iota(jnp.int32, (MLA_VT_ROWS - MLA_V, tm), 0)
    ones_rows = jnp.where(sub == 0, 1.0, 0.0).astype(v_ref.dtype)
    for h in range(MLA_HEADS):
        kvh = jnp.dot(cn, w_ref[:, h * hw:(h + 1) * hw], preferred_element_type=F32)
        kn = kvh[:, :MLA_NOPE]
        ms = (jnp.sum(kn * kn, axis=-1, keepdims=True) + ss_r) * (1.0 / MLA_QK_DIM)
        rs = lax.rsqrt(ms + EPS)
        k_ref[h, :, :MLA_NOPE] = (kn * rs * gk_ref[...]).astype(k_ref.dtype)
        k_ref[h, :, MLA_NOPE:] = (rot * rs).astype(k_ref.dtype)
        v_ref[h, :MLA_V, :] = kvh[:, MLA_NOPE:].T.astype(v_ref.dtype)
        v_ref[h, MLA_V:, :] = ones_rows


def _mla_keys_values(p, pc, ga, wkv, gk, gr, cos, sin):
    b, l, _ = p.shape
    lc = pc.shape[1]
    tm = min(lc, l)
    assert l % tm == 0 and lc % tm == 0
    n_lat, n_ctx = l // tm, lc // tm
    lat = lambda blk: (lambda bi, i: (bi, jnp.minimum(i, n_lat - 1), blk))
    ctx = lambda blk: (lambda bi, i: (bi, jnp.maximum(i - n_lat, 0), blk))
    const = lambda bi, i: (0, 0)
    k_spec = pl.BlockSpec((None, MLA_HEADS, tm, MLA_QK_PAD), lambda bi, i: (bi, 0, i, 0))
    vt_spec = pl.BlockSpec((None, MLA_HEADS, MLA_VT_ROWS, tm), lambda bi, i: (bi, 0, 0, i))
    k_shape = jax.ShapeDtypeStruct((b, MLA_HEADS, l + lc, MLA_QK_PAD), BF16)
    vt_shape = jax.ShapeDtypeStruct((b, MLA_HEADS, MLA_VT_ROWS, l + lc), BF16)
    return pl.pallas_call(
        functools.partial(_mla_kv_kernel, n_lat=n_lat),
        grid=(b, n_lat + n_ctx),
        in_specs=[pl.BlockSpec((None, tm, MLA_KV_RANK), lat(COL_CKV // MLA_KV_RANK)),
                  pl.BlockSpec((None, tm, LANE), lat(COL_SMALL // LANE)),
                  pl.BlockSpec((None, tm, MLA_KV_RANK), ctx(CCOL_CKV // MLA_KV_RANK)),
                  pl.BlockSpec((None, tm, LANE), ctx(CCOL_SMALL // LANE)),
                  pl.BlockSpec((1, MLA_KV_RANK), const),
                  pl.BlockSpec(wkv.shape, const),
                  pl.BlockSpec((1, LANE), const),
                  pl.BlockSpec((1, LANE), const),
                  pl.BlockSpec((tm, LANE), lambda bi, i: (i, 0)),
                  pl.BlockSpec((tm, LANE), lambda bi, i: (i, 0))],
        out_specs=[k_spec, vt_spec],
        out_shape=[k_shape, vt_shape],
        compiler_params=_cparams(("parallel", "arbitrary")),
        name="mla_keys_values",
    )(p, p, pc, pc, ga, wkv, gk, gr, cos, sin)


def _attn_kernel(q_ref, k_ref, vt_ref, o_ref, s0_ref, s1_ref, m0_ref, m1_ref, *, tq):
    nt = (((1,), (1,)), ((), ()))
    n = q_ref.shape[0] // tq
    bufs = ((s0_ref, m0_ref), (s1_ref, m1_ref))

    def scores(i, slot):
        s_ref, m_ref = bufs[slot]
        r0 = pl.multiple_of(i * tq, tq)
        s = lax.dot_general(k_ref[...], q_ref[pl.ds(r0, tq), :], nt, preferred_element_type=F32)
        s_ref[...] = s
        m_ref[...] = jnp.max(s, axis=0, keepdims=True)

    def outputs(i, slot):
        s_ref, m_ref = bufs[slot]
        r0 = pl.multiple_of(i * tq, tq)
        p = jnp.exp2(s_ref[...] - m_ref[...]).astype(BF16)
        o = jnp.dot(vt_ref[...], p, preferred_element_type=F32)
        o_ref[pl.ds(r0, tq), :] = (o[:MLA_V] / o[MLA_V:MLA_V + 1]).T.astype(o_ref.dtype)

    scores(0, 0)
    if n % 2 == 0:
        def body(j, carry):
            scores(2 * j + 1, 1)
            outputs(2 * j, 0)
            scores(2 * j + 2, 0)
            outputs(2 * j + 1, 1)
            return carry

        lax.fori_loop(0, n // 2 - 1, body, 0)
        scores(n - 1, 1)
        outputs(n - 2, 0)
        outputs(n - 1, 1)
    else:
        assert n == 1
        outputs(0, 0)


def _attention(q, k, v, tq):
    b, h, l, dq = q.shape
    tq = min(tq, l)
    lk = k.shape[2]
    bh = lambda bi, hi: (bi, hi, 0, 0)
    return pl.pallas_call(
        functools.partial(_attn_kernel, tq=tq),
        grid=(b, h),
        in_specs=[pl.BlockSpec((None, None, l, dq), bh),
                  pl.BlockSpec((None, None, lk, dq), bh),
                  pl.BlockSpec((None, None, MLA_VT_ROWS, lk), bh)],
        out_specs=pl.BlockSpec((None, l, MLA_V), lambda bi, hi: (bi, 0, hi)),
        out_shape=jax.ShapeDtypeStruct((b, l, h * MLA_V), BF16),
        scratch_shapes=[pltpu.VMEM((lk, tq), F32), pltpu.VMEM((lk, tq), F32),
                        pltpu.VMEM((1, tq), F32), pltpu.VMEM((1, tq), F32)],
        compiler_params=_cparams(("parallel", "parallel")),
        name="mla_attention",
    )(q, k, v)


def _merge_kernel(yg_ref, ym_ref, wg_ref, wm_ref, ga_ref, gb_ref, o_ref):
    a = jnp.dot(yg_ref[...], wg_ref[...], preferred_element_type=F32)
    m = jnp.dot(ym_ref[...], wm_ref[...], preferred_element_type=F32)
    y = (jax.nn.sigmoid(ga_ref[...].astype(F32)) * a
         + jax.nn.sigmoid(gb_ref[...].astype(F32)) * m)
    o_ref[...] = y.astype(o_ref.dtype)


def _branch_merge(yg, ym, wg, wm, p, tm, tn):
    b, l, d = yg.shape
    tm = min(tm, l)
    n = wg.shape[1]
    return pl.pallas_call(
        _merge_kernel,
        grid=(b, l // tm, n // tn),
        in_specs=[pl.BlockSpec((None, tm, d), lambda bi, i, j: (bi, i, 0)),
                  pl.BlockSpec((None, tm, d), lambda bi, i, j: (bi, i, 0)),
                  pl.BlockSpec((d, tn), lambda bi, i, j: (0, j)),
                  pl.BlockSpec((d, tn), lambda bi, i, j: (0, j)),
                  pl.BlockSpec((None, tm, tn), lambda bi, i, j: (bi, i, COL_GA // tn + j)),
                  pl.BlockSpec((None, tm, tn), lambda bi, i, j: (bi, i, COL_GB // tn + j))],
        out_specs=pl.BlockSpec((None, tm, tn), lambda bi, i, j: (bi, i, j)),
        out_shape=jax.ShapeDtypeStruct((b, l, n), BF16),
        compiler_params=_cparams(("parallel", "parallel", "arbitrary")),
        name="branch_merge",
    )(yg, ym, wg, wm, p, p)


def _outproj_kernel(y_ref, w_ref, x_ref, gt_ref, g2_ref, sc_ref, sh_ref, wr_ref, br_ref,
                    x1_ref, h2_ref, lg_ref):
    x1 = x_ref[...] + gt_ref[...] * jnp.dot(y_ref[...], w_ref[...], preferred_element_type=F32)
    x1_ref[...] = x1
    ms = jnp.mean(x1 * x1, axis=-1, keepdims=True)
    h2 = x1 * lax.rsqrt(ms + EPS) * g2_ref[...] * (1.0 + sc_ref[...]) + sh_ref[...]
    tm = x1.shape[0]
    for j, tile in enumerate(_pack_rows(h2)):
        h2_ref[pl.ds(j, tm, stride=ROW_TILES), :] = tile
    lg_ref[...] = jnp.dot(h2.astype(BF16), wr_ref[...], preferred_element_type=F32) + br_ref[...]


def _out_projection(y, w, x, gt, g2, sc, sh, wr, br, tm):
    b, l, d = x.shape
    tm = min(tm, l)
    row = lambda bi, i: (bi, i, 0)
    per_b = lambda bi, i: (bi, 0, 0)
    const = lambda bi, i: (0, 0)
    return pl.pallas_call(
        _outproj_kernel,
        grid=(b, l // tm),
        in_specs=[pl.BlockSpec((None, tm, d), row),
                  pl.BlockSpec((d, d), const),
                  pl.BlockSpec((None, tm, d), row),
                  pl.BlockSpec((None, 1, d), per_b),
                  pl.BlockSpec((1, d), const),
                  pl.BlockSpec((None, 1, d), per_b),
                  pl.BlockSpec((None, 1, d), per_b),
                  pl.BlockSpec((d, ROUTER_W), const),
                  pl.BlockSpec((1, ROUTER_W), const)],
        out_specs=[pl.BlockSpec((None, tm, d), row),
                   pl.BlockSpec((None, tm * ROW_TILES, LANE), row),
                   pl.BlockSpec((None, tm, ROUTER_W), row)],
        out_shape=[jax.ShapeDtypeStruct((b, l, d), F32),
                   jax.ShapeDtypeStruct((b, l * ROW_TILES, LANE), jnp.uint32),
                   jax.ShapeDtypeStruct((b, l, ROUTER_W), F32)],
        compiler_params=_cparams(("parallel", "parallel")),
        name="out_projection",
    )(y, w, x, gt, g2, sc, sh, wr, br)


def _route_kernel(lg_ref, e_ref, w_ref):
    lg = lg_ref[...]
    lane = lax.broadcasted_iota(jnp.int32, lg.shape, 1)
    neg = jnp.float32(-jnp.inf)
    big = jnp.int32(1 << 20)
    is_g = lane < N_GROUPS
    gl = jnp.where(is_g, lg, neg)
    gm = jnp.max(gl, axis=-1, keepdims=True)
    gidx = jnp.min(jnp.where(gl == gm, lane, big), axis=-1, keepdims=True)
    gsum = jnp.sum(jnp.where(is_g, jnp.exp(gl - gm), 0.0), axis=-1, keepdims=True)
    p_g = 1.0 / gsum
    g_lo = N_GROUPS + gidx * EXP_PER_GROUP
    in_grp = (lane >= g_lo) & (lane < g_lo + EXP_PER_GROUP)
    el = jnp.where(in_grp, lg, neg)
    em = jnp.max(el, axis=-1, keepdims=True)
    z = jnp.sum(jnp.where(in_grp, jnp.exp(el - em), 0.0), axis=-1, keepdims=True)
    i1 = jnp.min(jnp.where(el == em, lane, big), axis=-1, keepdims=True)
    el2 = jnp.where(lane == i1, neg, el)
    em2 = jnp.max(el2, axis=-1, keepdims=True)
    i2 = jnp.min(jnp.where(el2 == em2, lane, big), axis=-1, keepdims=True)
    p1 = 1.0 / z
    p2 = jnp.exp(em2 - em) / z
    tot = p1 + p2
    e_ref[...] = jnp.where(lane == 0, i1 - N_GROUPS, jnp.where(lane == 1, i2 - N_GROUPS, 0))
    w_ref[...] = jnp.where(lane == 0, p_g * p1 / tot, jnp.where(lane == 1, p_g * p2 / tot, 0.0))


def _route(logits, tm):
    t = logits.shape[0]
    tm = min(tm, t)
    spec = pl.BlockSpec((tm, ROUTER_W), lambda i: (i, 0))
    return pl.pallas_call(
        _route_kernel,
        grid=(t // tm,),
        in_specs=[spec],
        out_specs=[spec, spec],
        out_shape=[jax.ShapeDtypeStruct((t, ROUTER_W), jnp.int32),
                   jax.ShapeDtypeStruct((t, ROUTER_W), F32)],
        compiler_params=_cparams(("parallel",)),
        name="moe_route",
    )(logits)


def _expert_kernel(be_ref, nv_ref, first_ref, nxe_ref, nxe2_ref, ws_ref, j0_ref, tok_ref, dst_ref,
                   h_hbm, wg_hbm, wu_hbm, wd_hbm, yp_hbm,
                   xbuf, ybuf, wgv, wuv, wdv, sem_in, sem_out, sem_w, sem_init):
    i = pl.program_id(0)
    n = pl.num_programs(0)
    blk = MOE_BLOCK
    rt, pitch = ROW_TILES, ROW_PITCH
    plane = yp_hbm.shape[0] // (2 * rt)
    t_rows = plane - blk
    slot_rows = blk * pitch
    par = i % 2
    cur = pl.multiple_of(par * slot_rows, 8)
    oth = pl.multiple_of((1 - par) * slot_rows, 8)

    def weight_copies(e, s):
        return (pltpu.make_async_copy(wg_hbm.at[e], wgv.at[s], sem_w.at[s, 0]),
                pltpu.make_async_copy(wu_hbm.at[e], wuv.at[s], sem_w.at[s, 1]),
                pltpu.make_async_copy(wd_hbm.at[e], wdv.at[s], sem_w.at[s, 2]))

    def gather_start(b, buf0, sem, r_lo=0, r_hi=blk):
        base = j0_ref[b]
        for r in range(r_lo, r_hi):
            row0 = pl.multiple_of(tok_ref[base + r] * rt, rt)
            pltpu.make_async_copy(h_hbm.at[pl.ds(row0, rt)],
                                  xbuf.at[pl.ds(buf0 + r * pitch, rt)], sem_in.at[sem]).start()

    def gather_wait(buf0, sem):
        pltpu.make_async_copy(h_hbm.at[pl.ds(0, blk * rt)], xbuf.at[pl.ds(buf0, blk * rt)],
                              sem_in.at[sem]).wait()

    def scatter_start(b, buf0, sem, r_lo=0, r_hi=blk):
        base = j0_ref[b]
        filled = nv_ref[b]
        for r in range(r_lo, r_hi):
            dst = jnp.where(r < filled, dst_ref[base + r], t_rows + r)
            row0 = pl.multiple_of(dst * rt, rt)
            pltpu.make_async_copy(ybuf.at[pl.ds(buf0 + r * pitch, rt)],
                                  yp_hbm.at[pl.ds(row0, rt)], sem_out.at[sem]).start(priority=BULK_DMA)

    def scatter_wait(buf0, sem):
        pltpu.make_async_copy(ybuf.at[pl.ds(buf0, blk * rt)], yp_hbm.at[pl.ds(0, blk * rt)],
                              sem_out.at[sem]).wait()

    @pl.when(i == 0)
    def _():
        for cp in weight_copies(be_ref[0], 0):
            cp.start(priority=BULK_DMA)

        @pl.when(nxe_ref[0] >= 0)
        def _():
            for cp in weight_copies(nxe_ref[0], 1):
                cp.start(priority=BULK_DMA)

        gather_start(0, 0, 0)
        ybuf[...] = jnp.zeros_like(ybuf)
        zeros = ybuf.at[pl.ds(slot_rows, blk * rt)]
        pltpu.make_async_copy(zeros, yp_hbm.at[pl.ds(t_rows * rt, blk * rt)], sem_out.at[0]).start()
        spare = pltpu.make_async_copy(zeros, yp_hbm.at[pl.ds((plane + t_rows) * rt, blk * rt)], sem_init)
        spare.start()
        spare.wait()

    valid = nv_ref[i] > 0
    nxt = jnp.minimum(i + 1, n - 1)
    has_next = jnp.logical_and(i + 1 < n, nv_ref[nxt] > 0)

    @pl.when(jnp.logical_and(valid, first_ref[i] == 1))
    def _():
        ws = ws_ref[i]
        for cp in weight_copies(be_ref[i], ws):
            cp.wait()

        @pl.when(nxe2_ref[i] >= 0)
        def _():
            ahead = jnp.where(ws == 0, MOE_W_SLOTS - 1, ws - 1)
            for cp in weight_copies(nxe2_ref[i], ahead):
                cp.start(priority=BULK_DMA)

    @pl.when(valid)
    def _():
        ws = ws_ref[i]
        gather_wait(cur, par)
        scatter_wait(cur, par)
        prev = jnp.maximum(i - 1, 0)
        per = blk // (2 * rt)

        def read_tiles(first_chunk):
            halves = []
            for j in range(rt):
                lo = (first_chunk + j) * per
                gather_start(nxt, oth, 1 - par, lo, lo + per)
                halves.append(_unpack_tile(xbuf[pl.ds(cur + j, blk, stride=pitch), :]))
            return jnp.concatenate([h[0] for h in halves] + [h[1] for h in halves],
                                   axis=1).astype(BF16)

        scatter_start(prev, oth, 1 - par)
        g = jnp.dot(read_tiles(0), wgv[ws].astype(BF16), preferred_element_type=F32)
        u = jnp.dot(read_tiles(rt), wuv[ws].astype(BF16), preferred_element_type=F32)
        a = (_silu(g) * u).astype(BF16)
        y = jnp.dot(a, wdv[ws].astype(BF16), preferred_element_type=F32)
        for j, tile in enumerate(_pack_rows(y)):
            ybuf[pl.ds(cur + j, blk, stride=pitch), :] = tile

        @pl.when(jnp.logical_not(has_next))
        def _():
            scatter_wait(oth, 1 - par)
            scatter_start(i, cur, par)
            gather_wait(oth, 1 - par)
            scatter_wait(cur, par)


def _experts(plan, h2, w_gate, w_up, w_down):
    d = w_gate.shape[1]
    rt, pitch = ROW_TILES, ROW_PITCH
    assert d == 2 * rt * LANE
    t = h2.shape[0] // rt
    n_blocks = plan[0].shape[0]
    blk = MOE_BLOCK
    de = w_gate.shape[2]
    any_spec = pl.BlockSpec(memory_space=pl.ANY)
    return pl.pallas_call(
        _expert_kernel,
        grid_spec=pltpu.PrefetchScalarGridSpec(
            num_scalar_prefetch=len(plan),
            grid=(n_blocks,),
            in_specs=[any_spec, any_spec, any_spec, any_spec],
            out_specs=any_spec,
            scratch_shapes=[pltpu.VMEM((2 * blk * pitch, LANE), jnp.uint32),
                            pltpu.VMEM((2 * blk * pitch, LANE), jnp.uint32),
                            pltpu.VMEM((MOE_W_SLOTS, d, de), F32),
                            pltpu.VMEM((MOE_W_SLOTS, d, de), F32),
                            pltpu.VMEM((MOE_W_SLOTS, de, d), F32),
                            pltpu.SemaphoreType.DMA((2,)),
                            pltpu.SemaphoreType.DMA((2,)),
                            pltpu.SemaphoreType.DMA((MOE_W_SLOTS, 3)),
                            pltpu.SemaphoreType.DMA],
        ),
        out_shape=jax.ShapeDtypeStruct((2 * (t + blk) * rt, LANE), jnp.uint32),
        compiler_params=_cparams(("arbitrary",)),
        name="moe_experts",
    )(*plan, h2, w_gate, w_up, w_down).reshape(2, (t + blk) * rt, LANE)


def _combine_kernel(x_ref, gt_ref, w_ref, yp_ref, o_ref):
    tm = x_ref.shape[0]
    w = w_ref[...]
    w0, w1 = w[:, 0:1], w[:, 1:2]
    lows, highs = [], []
    for j in range(ROW_TILES):
        lo0, hi0 = _unpack_tile(yp_ref[0, pl.ds(j, tm, stride=ROW_TILES), :])
        lo1, hi1 = _unpack_tile(yp_ref[1, pl.ds(j, tm, stride=ROW_TILES), :])
        lows.append(w0 * lo0 + w1 * lo1)
        highs.append(w0 * hi0 + w1 * hi1)
    o_ref[...] = x_ref[...] + gt_ref[...] * jnp.concatenate(lows + highs, axis=1)


def _combine(x1, gt, w_pick, yp, tm):
    b, l, d = x1.shape
    tm = min(tm, l)
    nb = l // tm
    return pl.pallas_call(
        _combine_kernel,
        grid=(b, nb),
        in_specs=[pl.BlockSpec((None, tm, d), lambda bi, i: (bi, i, 0)),
                  pl.BlockSpec((None, 1, d), lambda bi, i: (bi, 0, 0)),
                  pl.BlockSpec((tm, ROUTER_W), lambda bi, i: (bi * nb + i, 0)),
                  pl.BlockSpec((2, tm * ROW_TILES, LANE), lambda bi, i: (0, bi * nb + i, 0))],
        out_specs=pl.BlockSpec((None, tm, d), lambda bi, i: (bi, i, 0)),
        out_shape=jax.ShapeDtypeStruct((b, l, d), F32),
        compiler_params=_cparams(("parallel", "parallel")),
        name="moe_combine",
    )(x1, gt, w_pick, yp)


def _dispatch_plan(eid):
    blk = MOE_BLOCK
    n_assign = eid.shape[0]
    n_tok = n_assign // 2
    n_blocks = (n_assign + blk - 1) // blk + N_EXPERTS
    order = jnp.argsort(eid).astype(jnp.int32)
    e_ids = jnp.arange(N_EXPERTS, dtype=jnp.int32)
    counts = jnp.sum((eid[None, :] == e_ids[:, None]).astype(jnp.int32), axis=1)
    start = jnp.cumsum(counts) - counts
    nblk = (counts + blk - 1) // blk
    bend = jnp.cumsum(nblk)
    b_ids = jnp.arange(n_blocks, dtype=jnp.int32)
    block_e = jnp.minimum(jnp.searchsorted(bend, b_ids, side='right'), N_EXPERTS - 1).astype(jnp.int32)
    used = b_ids < bend[-1]
    within = b_ids - (bend - nblk)[block_e]
    block_nv = jnp.where(used, jnp.clip(counts[block_e] - within * blk, 0, blk), 0).astype(jnp.int32)
    block_j0 = jnp.where(used, start[block_e] + within * blk, 0).astype(jnp.int32)
    first = jnp.concatenate([jnp.ones((1,), jnp.int32),
                             (block_e[1:] != block_e[:-1]).astype(jnp.int32)])
    w_slot = ((jnp.cumsum(first) - 1) % MOE_W_SLOTS).astype(jnp.int32)
    nonempty = jnp.where(counts > 0, e_ids, N_EXPERTS)
    later = jnp.flip(lax.cummin(jnp.flip(nonempty)))
    next_e = jnp.concatenate([later[1:], jnp.full((1,), N_EXPERTS, jnp.int32)])
    next_e = jnp.where(next_e >= N_EXPERTS, -1, next_e)
    next2_e = jnp.where(next_e < 0, -1, next_e[jnp.maximum(next_e, 0)])
    next_e, next2_e = next_e[block_e], next2_e[block_e]
    order = jnp.concatenate([order, jnp.zeros((blk,), jnp.int32)])
    tok_sorted = order // 2
    dst_sorted = (order % 2) * (n_tok + blk) + order // 2
    return (block_e, block_nv, first, next_e, next2_e, w_slot, block_j0, tok_sorted, dst_sorted)


def _rope_tables(length, with_rope):
    half = MLA_ROPE // 4
    if with_rope:
        rows = length // GRID_W
        row = jnp.repeat(jnp.arange(rows), GRID_W).astype(F32)
        col = jnp.tile(jnp.arange(GRID_W), rows).astype(F32)
        inv = 1.0 / (ROPE_THETA ** (jnp.arange(half, dtype=F32) / half))
        ang_r = row[:, None] * inv
        ang_c = col[:, None] * inv
        cos = jnp.concatenate([jnp.cos(ang_r), jnp.cos(ang_r), jnp.cos(ang_c), jnp.cos(ang_c)], axis=1)
        sin = jnp.concatenate([-jnp.sin(ang_r), jnp.sin(ang_r), -jnp.sin(ang_c), jnp.sin(ang_c)], axis=1)
    else:
        cos = jnp.ones((length, MLA_ROPE), F32)
        sin = jnp.zeros((length, MLA_ROPE), F32)
    pad = ((0, 0), (0, LANE - MLA_ROPE))
    return jnp.pad(cos, pad), jnp.pad(sin, pad)


_IN_SIZES = (GLA_QK_W, GLA_QK_W, GLA_V_W, GLA_V_W, GLA_LOWRANK, GLA_LOWRANK,
             MLA_Q_RANK, MLA_KV_RANK, MLA_ROPE, D_MODEL, D_MODEL)
(SRC_Q, SRC_K, SRC_V, SRC_R, SRC_AF, SRC_AB, SRC_CQ, SRC_CKV, SRC_KR, SRC_GA, SRC_GB,
 SRC_END) = [sum(_IN_SIZES[:i]) for i in range(len(_IN_SIZES) + 1)]


def _w_in_layout_kernel(w_ref, lat_ref, ctx_ref):
    def rows(a, b):
        return w_ref[a:b, :].astype(BF16)

    small_pad = jnp.zeros((SMALL_W - MLA_ROPE - 2 * GLA_LOWRANK, w_ref.shape[1]), BF16)
    lat_ref[COL_Q:COL_GA, :] = rows(SRC_Q, SRC_AF)
    lat_ref[COL_GA:COL_CQ, :] = rows(SRC_GA, SRC_END)
    lat_ref[COL_CQ:COL_SMALL, :] = rows(SRC_CQ, SRC_KR)
    ctx_ref[CCOL_K:CCOL_CKV, :] = rows(SRC_K, SRC_R)
    ctx_ref[CCOL_CKV:CCOL_SMALL, :] = rows(SRC_CKV, SRC_KR)
    for ref, off in ((lat_ref, COL_SMALL), (ctx_ref, CCOL_SMALL)):
        ref[off:off + MLA_ROPE, :] = rows(SRC_KR, SRC_GA)
        ref[off + MLA_ROPE:off + SM_AB + GLA_LOWRANK, :] = rows(SRC_AF, SRC_CQ)
        ref[off + SM_AB + GLA_LOWRANK:off + SMALL_W, :] = small_pad


def _w_in_layout(w_in, tk=256):
    wt = jnp.swapaxes(w_in, 1, 2)
    _, n, d = wt.shape
    assert n == SRC_END
    return pl.pallas_call(
        _w_in_layout_kernel,
        grid=(d // tk,),
        in_specs=[pl.BlockSpec((None, n, tk), lambda i: (0, 0, i))],
        out_specs=[pl.BlockSpec((NP_LAT, tk), lambda i: (0, i)),
                   pl.BlockSpec((NP_CTX, tk), lambda i: (0, i))],
        out_shape=[jax.ShapeDtypeStruct((NP_LAT, d), BF16),
                   jax.ShapeDtypeStruct((NP_CTX, d), BF16)],
        compiler_params=_cparams(("parallel",)),
        name="w_in_layout",
    )(wt)


def kernel(x, c, ctx, c_ctx, w_mod, b_mod, norm1_g, norm2_g, w_in, w_decay_f, b_decay_f, w_decay_b, b_decay_b, gla_norm_g, q_a_norm_g, w_uq, kv_a_norm_g, w_ukv, q_norm_g, k_norm_g, w_o_gla, w_o_mla, w_out, w_router_group, b_router_group, w_router_expert, b_router_expert, w_exp_gate, w_exp_up, w_exp_down):
    assert w_mod.shape[0] == 1, "single-layer block"
    b, l, d = x.shape
    lc = ctx.shape[1]
    t = b * l

    c_rows = jnp.concatenate([c, c_ctx[None, :], jnp.zeros((8 - b - 1, d), F32)], axis=0)
    mod = _modulation(c_rows, w_mod[0], b_mod[0])
    sh1, sc1, gt1, sh2, sc2, gt2 = [mod[:b, i * d:(i + 1) * d].reshape(b, 1, d) for i in range(6)]
    sh1c = jnp.broadcast_to(mod[b:b + 1, 0:d].reshape(1, 1, d), (b, 1, d))
    sc1c = jnp.broadcast_to(mod[b:b + 1, d:2 * d].reshape(1, 1, d), (b, 1, d))

    w_lat, w_ctx = _w_in_layout(w_in)
    g1 = norm1_g[0].reshape(1, d)
    p = _in_projection(x, g1, sc1, sh1, w_lat, tm=1024)
    pc = _in_projection(ctx, g1, sc1c, sh1c, w_ctx, tm=lc)

    def decay_w(w, off):
        return jnp.zeros((LANE, GLA_QK_W), F32).at[off:off + GLA_LOWRANK].set(w).astype(BF16)

    y_gla = _gla(p, pc,
                 decay_w(w_decay_f[0], SM_AF), b_decay_f[0].reshape(1, -1),
                 decay_w(w_decay_b[0], SM_AB), b_decay_b[0].reshape(1, -1),
                 gla_norm_g[0].reshape(1, -1))

    wq = jnp.pad(w_uq[0].reshape(MLA_Q_RANK, MLA_HEADS, MLA_QK_DIM),
                 ((0, 0), (0, 0), (0, MLA_QK_PAD - MLA_QK_DIM))).reshape(MLA_Q_RANK, -1).astype(BF16)
    wkv = w_ukv[0].astype(BF16)
    qn_g = jnp.pad(q_norm_g[0], (0, MLA_QK_PAD - MLA_QK_DIM)).reshape(1, -1)
    kn_nope = k_norm_g[0][:MLA_NOPE].reshape(1, -1)
    kn_rope = jnp.pad(k_norm_g[0][MLA_NOPE:], (0, LANE - MLA_ROPE)).reshape(1, -1)
    cos, sin = _rope_tables(l, True)
    cos_c, sin_c = _rope_tables(lc, False)
    q_m = _mla_queries(p, q_a_norm_g[0].reshape(1, -1), wq, qn_g, cos, sin, tm=512)
    kva = kv_a_norm_g[0].reshape(1, -1)
    k_m, v_m = _mla_keys_values(p, pc, kva, wkv, kn_nope, kn_rope,
                                jnp.concatenate([cos, cos_c]), jnp.concatenate([sin, sin_c]))
    y_mla = _attention(q_m, k_m, v_m, tq=512)

    y = _branch_merge(y_gla, y_mla, w_o_gla[0].astype(BF16), w_o_mla[0].astype(BF16), p, tm=1024, tn=512)
    w_router = jnp.concatenate(
        [w_router_group[0], w_router_expert[0],
         jnp.zeros((d, ROUTER_W - N_GROUPS - N_EXPERTS), F32)], axis=1).astype(BF16)
    b_router = jnp.concatenate(
        [b_router_group[0], b_router_expert[0],
         jnp.zeros((ROUTER_W - N_GROUPS - N_EXPERTS,), F32)]).reshape(1, -1)
    x1, h2, logits = _out_projection(y, w_out[0].astype(BF16), x, gt1, norm2_g[0].reshape(1, d),
                                     sc2, sh2, w_router, b_router, tm=256)

    e_pick, w_pick = _route(logits.reshape(t, ROUTER_W), tm=1024)
    plan = _dispatch_plan(e_pick[:, :2].reshape(-1))
    yp = _experts(plan, h2.reshape(t * ROW_TILES, LANE),
                  w_exp_gate[0], w_exp_up[0], w_exp_down[0])
    return _combine(x1, gt2, w_pick, yp, tm=512)
```

```python
import functools
import math

import jax
import jax.numpy as jnp
from jax import lax
from jax.experimental import pallas as pl
from jax.experimental.pallas import tpu as pltpu

F32 = jnp.float32
BF16 = jnp.bfloat16

D_MODEL = 2048
GRID_W = 64
EPS = 1e-6

GLA_HEADS = 4
GLA_DK = 256
GLA_DV = 512
GLA_LOWRANK = 16
GLA_TAU = 16.0
GLA_CHUNK = 64
GLA_GROUP = 8
GLA_QK_W = GLA_HEADS * GLA_DK
GLA_V_W = GLA_HEADS * GLA_DV

MLA_HEADS = 16
MLA_Q_RANK = 512
MLA_KV_RANK = 512
MLA_NOPE = 128
MLA_ROPE = 64
MLA_V = 128
MLA_QK_DIM = MLA_NOPE + MLA_ROPE
MLA_QK_PAD = 256
MLA_VT_ROWS = MLA_V + 16
ROPE_THETA = 10000.0
LOG2E = math.log2(math.e)

N_GROUPS = 8
EXP_PER_GROUP = 8
N_EXPERTS = N_GROUPS * EXP_PER_GROUP
D_EXPERT = 512
ROUTER_W = 128
MOE_BLOCK = 256
BULK_DMA = 1
SCATTER_BULK_EVERY = 4

LANE = 128
ROW_TILES = D_MODEL // (2 * LANE)
ROW_PITCH = 12
VMEM_LIMIT = 56 * 1024 * 1024

COL_Q = 0
COL_K = COL_Q + GLA_QK_W
COL_V = COL_K + GLA_QK_W
COL_R = COL_V + GLA_V_W
COL_GA = COL_R + GLA_V_W
COL_GB = COL_GA + D_MODEL
COL_CQ = COL_GB + D_MODEL
COL_CKV = COL_CQ + MLA_Q_RANK
COL_SMALL = COL_CKV + MLA_KV_RANK
SMALL_W = 256
NP_LAT = COL_SMALL + SMALL_W
SM_AF = MLA_ROPE
SM_AB = MLA_ROPE + GLA_LOWRANK
CCOL_K = 0
CCOL_V = CCOL_K + GLA_QK_W
CCOL_CKV = CCOL_V + GLA_V_W
CCOL_SMALL = CCOL_CKV + MLA_KV_RANK
NP_CTX = CCOL_SMALL + SMALL_W

INPROJ_TM, INPROJ_TN = 1024, 1280
INPROJ_NORM_ROWS = 128
MLA_Q_TM = 512
ATTN_TQ = 512
MERGE_TM, MERGE_TN = 1024, 512
OUTPROJ_TM = 256
ROUTE_TM = 1024
COMBINE_TM = 512


def _cparams(sem):
    return pltpu.CompilerParams(dimension_semantics=sem, vmem_limit_bytes=VMEM_LIMIT)


def _silu(x):
    return x * jax.nn.sigmoid(x)


def _pack_rows(x):
    half = x.shape[1] // 2
    lo = lax.bitcast_convert_type(x[:, :half].astype(BF16).astype(F32), jnp.uint32) >> 16
    hi = lax.bitcast_convert_type(x[:, half:].astype(BF16).astype(F32), jnp.uint32) & jnp.uint32(0xFFFF0000)
    packed = lo | hi
    return [packed[:, j * LANE:(j + 1) * LANE] for j in range(half // LANE)]


def _unpack_tile(t):
    lo = lax.bitcast_convert_type(t << 16, F32)
    hi = lax.bitcast_convert_type(t & jnp.uint32(0xFFFF0000), F32)
    return lo, hi


def _mod_kernel(c_ref, w_ref, b_ref, o_ref):
    a = _silu(c_ref[...]).astype(BF16)
    o_ref[...] = jnp.dot(a, w_ref[...].astype(BF16), preferred_element_type=F32) + b_ref[...]


def _modulation(c_rows, w_mod, b_mod):
    m, d = c_rows.shape
    n = w_mod.shape[1]
    tn = 1024
    return pl.pallas_call(
        _mod_kernel,
        grid=(n // tn,),
        in_specs=[pl.BlockSpec((m, d), lambda j: (0, 0)),
                  pl.BlockSpec((d, tn), lambda j: (0, j)),
                  pl.BlockSpec((1, tn), lambda j: (0, j))],
        out_specs=pl.BlockSpec((m, tn), lambda j: (0, j)),
        out_shape=jax.ShapeDtypeStruct((m, n), F32),
        compiler_params=_cparams(("parallel",)),
        name="modulation",
    )(c_rows, w_mod, b_mod.reshape(1, n))


def _inproj_kernel(x_ref, g_ref, sc_ref, sh_ref, w_ref, o_ref, h_ref):
    @pl.when(pl.program_id(2) == 0)
    def _():
        rows = min(INPROJ_NORM_ROWS, x_ref.shape[0])

        def norm_rows(c, carry):
            r0 = pl.multiple_of(c * rows, rows)
            x = x_ref[pl.ds(r0, rows), :]
            ms = jnp.mean(x * x, axis=-1, keepdims=True)
            y = x * lax.rsqrt(ms + EPS) * g_ref[...]
            h_ref[pl.ds(r0, rows), :] = (y * (1.0 + sc_ref[...]) + sh_ref[...]).astype(BF16)
            return carry

        lax.fori_loop(0, x_ref.shape[0] // rows, norm_rows, 0)

    o_ref[...] = lax.dot_general(h_ref[...], w_ref[...], (((1,), (1,)), ((), ())),
                                 preferred_element_type=F32).astype(o_ref.dtype)


def _in_projection(x, g, sc, sh, w, tm):
    b, l, d = x.shape
    tm = min(tm, l)
    n = w.shape[0]
    tn = INPROJ_TN
    return pl.pallas_call(
        _inproj_kernel,
        grid=(b, l // tm, n // tn),
        in_specs=[pl.BlockSpec((None, tm, d), lambda bi, i, j: (bi, i, 0)),
                  pl.BlockSpec((1, d), lambda bi, i, j: (0, 0)),
                  pl.BlockSpec((None, 1, d), lambda bi, i, j: (bi, 0, 0)),
                  pl.BlockSpec((None, 1, d), lambda bi, i, j: (bi, 0, 0)),
                  pl.BlockSpec((tn, d), lambda bi, i, j: (j, 0))],
        out_specs=pl.BlockSpec((None, tm, tn), lambda bi, i, j: (bi, i, j)),
        out_shape=jax.ShapeDtypeStruct((b, l, n), BF16),
        scratch_shapes=[pltpu.VMEM((tm, d), BF16)],
        compiler_params=_cparams(("parallel", "parallel", "arbitrary")),
        name="in_projection",
    )(x, g, sc, sh, w)


def _gla_kernel(q_ref, k_ref, v_ref, r_ref, sm_ref, kc_ref, vc_ref, smc_ref,
                wf_ref, bf_ref, wb_ref, bb_ref, gn_ref, y_ref, o_ref, sf_ref, sb_ref):
    c = GLA_CHUNK
    n_lat = q_ref.shape[0] // c
    n_ctx = kc_ref.shape[0] // c
    row = lax.broadcasted_iota(jnp.int32, (c, c), 0)
    col = lax.broadcasted_iota(jnp.int32, (c, c), 1)
    lower = row >= col
    upper = row <= col
    nt = (((1,), (1,)), ((), ()))
    tn = (((0,), (0,)), ((), ()))

    def cum_decay(sm, w_ref, b_ref, tri, n_chunks):
        z = jnp.dot(sm, w_ref[...], preferred_element_type=F32) + b_ref[...]
        g = (jnp.minimum(z, 0.0) - jnp.log(1.0 + jnp.exp(-jnp.abs(z)))) * (1.0 / GLA_TAU)
        g_hi = g.astype(BF16)
        g_lo = (g - g_hi.astype(F32)).astype(BF16)
        t = jnp.where(tri, 1.0, 0.0).astype(BF16)
        return [jnp.dot(t, g_hi[i * c:(i + 1) * c], preferred_element_type=F32)
                + jnp.dot(t, g_lo[i * c:(i + 1) * c], preferred_element_type=F32)
                for i in range(n_chunks)]

    fwd = (wf_ref, bf_ref, lower, c - 1, sf_ref)
    bwd = (wb_ref, bb_ref, upper, 0, sb_ref)

    def group(r0, n_chunks, direction, sm_r, k_r, v_r, q_r):
        w_ref, b_ref, tri, end_row, st_ref = direction
        rows = n_chunks * c
        k_all = k_r[pl.ds(r0, rows), :].astype(F32)
        v_all = v_r[pl.ds(r0, rows), :]
        q_all = None if q_r is None else q_r[pl.ds(r0, rows), :].astype(F32) * (GLA_DK ** -0.5)
        bc_all = cum_decay(sm_r[pl.ds(r0, rows), :], w_ref, b_ref, tri, n_chunks)
        local = []
        for i in range(n_chunks):
            sl = slice(i * c, (i + 1) * c)
            bc = bc_all[i]
            bend = bc[end_row:end_row + 1, :]
            k, v = k_all[sl], v_all[sl]
            ke = (k * jnp.exp(bend - bc)).astype(BF16)
            upd = lax.dot_general(v, ke, tn, preferred_element_type=F32)
            qd = o_intra = None
            if q_all is not None:
                qd = (q_all[sl] * jnp.exp(bc)).astype(BF16)
                ki = (k * jnp.exp(-bc)).astype(BF16)
                att = lax.dot_general(qd, ki, nt, preferred_element_type=F32)
                att = jnp.where(tri, att, 0.0).astype(BF16)
                o_intra = jnp.dot(att, v, preferred_element_type=F32)
            local.append((jnp.exp(bend), upd, qd, o_intra))
        st = st_ref[...]
        outs = [None] * n_chunks
        for i in (range(n_chunks) if end_row else reversed(range(n_chunks))):
            decay, upd, qd, o_intra = local[i]
            if q_all is not None:
                outs[i] = o_intra + lax.dot_general(qd, st.astype(BF16), nt, preferred_element_type=F32)
            st = st * decay + upd
        st_ref[...] = st
        return None if q_all is None else jnp.concatenate(outs, axis=0)

    def readout(r0, o):
        ms = jnp.mean(o * o, axis=-1, keepdims=True)
        yn = o * lax.rsqrt(ms + EPS) * gn_ref[...]
        r = r_ref[pl.ds(r0, o.shape[0]), :].astype(F32)
        y_ref[pl.ds(r0, o.shape[0]), :] = (yn * _silu(r)).astype(y_ref.dtype)

    sf_ref[...] = jnp.zeros_like(sf_ref)
    sb_ref[...] = jnp.zeros_like(sb_ref)
    group(0, n_ctx, fwd, smc_ref, kc_ref, vc_ref, None)
    group(0, n_ctx, bwd, smc_ref, kc_ref, vc_ref, None)

    g_lat = min(GLA_GROUP, n_lat // 2)
    assert n_lat % (2 * g_lat) == 0
    n_groups = n_lat // g_lat
    rows = g_lat * c

    def scan(i, direction):
        r0 = pl.multiple_of(i * rows, rows)
        return r0, group(r0, g_lat, direction, sm_ref, k_ref, v_ref, q_ref)

    def first_half(i, carry):
        for r0, o in (scan(i, fwd), scan(n_groups - 1 - i, bwd)):
            o_ref[pl.ds(r0, rows), :] = o
        return carry

    lax.fori_loop(0, n_groups // 2, first_half, 0)

    def second_half(i, carry):
        for r0, o in (scan(i, fwd), scan(n_groups - 1 - i, bwd)):
            readout(r0, o_ref[pl.ds(r0, rows), :] + o)
        return carry

    lax.fori_loop(n_groups // 2, n_groups, second_half, 0)


def _gla(p, pc, wf, bf, wb, bb, gn):
    b, l, _ = p.shape
    lc = pc.shape[1]
    dk, dv = GLA_DK, GLA_DV
    hmap = lambda off: (lambda bi, h: (bi, 0, off + h))
    wmap = lambda bi, h: (0, h)
    return pl.pallas_call(
        _gla_kernel,
        grid=(b, GLA_HEADS),
        in_specs=[pl.BlockSpec((None, l, dk), hmap(COL_Q // dk)),
                  pl.BlockSpec((None, l, dk), hmap(COL_K // dk)),
                  pl.BlockSpec((None, l, dv), hmap(COL_V // dv)),
                  pl.BlockSpec((None, l, dv), hmap(COL_R // dv)),
                  pl.BlockSpec((None, l, LANE), lambda bi, h: (bi, 0, COL_SMALL // LANE)),
                  pl.BlockSpec((None, lc, dk), hmap(CCOL_K // dk)),
                  pl.BlockSpec((None, lc, dv), hmap(CCOL_V // dv)),
                  pl.BlockSpec((None, lc, LANE), lambda bi, h: (bi, 0, CCOL_SMALL // LANE)),
                  pl.BlockSpec((LANE, dk), wmap),
                  pl.BlockSpec((1, dk), wmap),
                  pl.BlockSpec((LANE, dk), wmap),
                  pl.BlockSpec((1, dk), wmap),
                  pl.BlockSpec((1, dv), lambda bi, h: (0, 0))],
        out_specs=pl.BlockSpec((None, l, dv), lambda bi, h: (bi, 0, h)),
        out_shape=jax.ShapeDtypeStruct((b, l, GLA_V_W), BF16),
        scratch_shapes=[pltpu.VMEM((l, dv), F32), pltpu.VMEM((dv, dk), F32), pltpu.VMEM((dv, dk), F32)],
        compiler_params=_cparams(("parallel", "parallel")),
        name="gla",
    )(p, p, p, p, p, pc, pc, pc, wf, bf, wb, bb, gn)


def _rope_partner(t):
    lane = lax.broadcasted_iota(jnp.int32, t.shape, 1)
    first = (lane % 32) < 16
    return jnp.where(first, pltpu.roll(t, LANE - 16, 1), pltpu.roll(t, 16, 1))


def _mla_q_kernel(cq_ref, ga_ref, w_ref, gn_ref, cos_ref, sin_ref, o_ref):
    cq = cq_ref[...].astype(F32)
    cn = (cq * lax.rsqrt(jnp.mean(cq * cq, axis=-1, keepdims=True) + EPS) * ga_ref[...]).astype(BF16)
    cos = cos_ref[...]
    sin = sin_ref[...]
    scale = MLA_QK_DIM ** -0.5 * LOG2E
    for h in range(MLA_HEADS):
        qh = jnp.dot(cn, w_ref[:, h * MLA_QK_PAD:(h + 1) * MLA_QK_PAD], preferred_element_type=F32)
        ms = jnp.sum(qh * qh, axis=-1, keepdims=True) * (1.0 / MLA_QK_DIM)
        qn = qh * lax.rsqrt(ms + EPS) * gn_ref[...]
        t = qn[:, MLA_NOPE:]
        rot = t * cos + _rope_partner(t) * sin
        o_ref[h, :, :MLA_NOPE] = (qn[:, :MLA_NOPE] * scale).astype(o_ref.dtype)
        o_ref[h, :, MLA_NOPE:] = (rot * scale).astype(o_ref.dtype)


def _mla_queries(p, ga, wq, gn, cos, sin, tm):
    b, l, _ = p.shape
    tm = min(tm, l)
    return pl.pallas_call(
        _mla_q_kernel,
        grid=(b, l // tm),
        in_specs=[pl.BlockSpec((None, tm, MLA_Q_RANK), lambda bi, i: (bi, i, COL_CQ // MLA_Q_RANK)),
                  pl.BlockSpec((1, MLA_Q_RANK), lambda bi, i: (0, 0)),
                  pl.BlockSpec(wq.shape, lambda bi, i: (0, 0)),
                  pl.BlockSpec((1, MLA_QK_PAD), lambda bi, i: (0, 0)),
                  pl.BlockSpec((tm, LANE), lambda bi, i: (i, 0)),
                  pl.BlockSpec((tm, LANE), lambda bi, i: (i, 0))],
        out_specs=pl.BlockSpec((None, MLA_HEADS, tm, MLA_QK_PAD), lambda bi, i: (bi, 0, i, 0)),
        out_shape=jax.ShapeDtypeStruct((b, MLA_HEADS, l, MLA_QK_PAD), BF16),
        compiler_params=_cparams(("parallel", "parallel")),
        name="mla_queries",
    )(p, ga, wq, gn, cos, sin)


def _mla_kv_kernel(ckv_ref, sm_ref, ckvc_ref, smc_ref, ga_ref, w_ref, gk_ref, gr_ref,
                   cos_ref, sin_ref, k_ref, v_ref, *, n_lat):
    is_ctx = pl.program_id(1) >= n_lat
    ckv = jnp.where(is_ctx, ckvc_ref[...], ckv_ref[...]).astype(F32)
    cn = (ckv * lax.rsqrt(jnp.mean(ckv * ckv, axis=-1, keepdims=True) + EPS) * ga_ref[...]).astype(BF16)
    sm = jnp.where(is_ctx, smc_ref[...], sm_ref[...]).astype(F32)
    lane = lax.broadcasted_iota(jnp.int32, sm.shape, 1)
    kr = jnp.where(lane < MLA_ROPE, sm, 0.0)
    ss_r = jnp.sum(kr * kr, axis=-1, keepdims=True)
    krg = kr * gr_ref[...]
    rot = krg * cos_ref[...] + _rope_partner(krg) * sin_ref[...]
    hw = MLA_NOPE + MLA_V
    tm = ckv.shape[0]
    sub = lax.broadcasted_iota(jnp.int32, (MLA_VT_ROWS - MLA_V, tm), 0)
    ones_rows = jnp.where(sub == 0, 1.0, 0.0).astype(v_ref.dtype)
    for h in range(MLA_HEADS):
        kvh = jnp.dot(cn, w_ref[:, h * hw:(h + 1) * hw], preferred_element_type=F32)
        kn = kvh[:, :MLA_NOPE]
        ms = (jnp.sum(kn * kn, axis=-1, keepdims=True) + ss_r) * (1.0 / MLA_QK_DIM)
        rs = lax.rsqrt(ms + EPS)
        k_ref[h, :, :MLA_NOPE] = (kn * rs * gk_ref[...]).astype(k_ref.dtype)
        k_ref[h, :, MLA_NOPE:] = (rot * rs).astype(k_ref.dtype)
        v_ref[h, :MLA_V, :] = kvh[:, MLA_NOPE:].T.astype(v_ref.dtype)
        v_ref[h, MLA_V:, :] = ones_rows


def _mla_keys_values(p, pc, ga, wkv, gk, gr, cos, sin):
    b, l, _ = p.shape
    lc = pc.shape[1]
    tm = min(lc, l)
    assert l % tm == 0 and lc % tm == 0
    n_lat, n_ctx = l // tm, lc // tm
    lat = lambda blk: (lambda bi, i: (bi, jnp.minimum(i, n_lat - 1), blk))
    ctx = lambda blk: (lambda bi, i: (bi, jnp.maximum(i - n_lat, 0), blk))
    const = lambda bi, i: (0, 0)
    k_spec = pl.BlockSpec((None, MLA_HEADS, tm, MLA_QK_PAD), lambda bi, i: (bi, 0, i, 0))
    vt_spec = pl.BlockSpec((None, MLA_HEADS, MLA_VT_ROWS, tm), lambda bi, i: (bi, 0, 0, i))
    k_shape = jax.ShapeDtypeStruct((b, MLA_HEADS, l + lc, MLA_QK_PAD), BF16)
    vt_shape = jax.ShapeDtypeStruct((b, MLA_HEADS, MLA_VT_ROWS, l + lc), BF16)
    return pl.pallas_call(
        functools.partial(_mla_kv_kernel, n_lat=n_lat),
        grid=(b, n_lat + n_ctx),
        in_specs=[pl.BlockSpec((None, tm, MLA_KV_RANK), lat(COL_CKV // MLA_KV_RANK)),
                  pl.BlockSpec((None, tm, LANE), lat(COL_SMALL // LANE)),
                  pl.BlockSpec((None, tm, MLA_KV_RANK), ctx(CCOL_CKV // MLA_KV_RANK)),
                  pl.BlockSpec((None, tm, LANE), ctx(CCOL_SMALL // LANE)),
                  pl.BlockSpec((1, MLA_KV_RANK), const),
                  pl.BlockSpec(wkv.shape, const),
                  pl.BlockSpec((1, LANE), const),
                  pl.BlockSpec((1, LANE), const),
                  pl.BlockSpec((tm, LANE), lambda bi, i: (i, 0)),
                  pl.BlockSpec((tm, LANE), lambda bi, i: (i, 0))],
        out_specs=[k_spec, vt_spec],
        out_shape=[k_shape, vt_shape],
        compiler_params=_cparams(("parallel", "arbitrary")),
        name="mla_keys_values",
    )(p, p, pc, pc, ga, wkv, gk, gr, cos, sin)


def _attn_kernel(q_ref, k_ref, vt_ref, o_ref, s0_ref, s1_ref, m0_ref, m1_ref, *, tq):
    nt = (((1,), (1,)), ((), ()))
    n = q_ref.shape[0] // tq
    bufs = ((s0_ref, m0_ref), (s1_ref, m1_ref))

    def scores(i, slot):
        s_ref, m_ref = bufs[slot]
        r0 = pl.multiple_of(i * tq, tq)
        s = lax.dot_general(k_ref[...], q_ref[pl.ds(r0, tq), :], nt, preferred_element_type=F32)
        s_ref[...] = s
        m_ref[...] = jnp.max(s, axis=0, keepdims=True)

    def outputs(i, slot):
        s_ref, m_ref = bufs[slot]
        r0 = pl.multiple_of(i * tq, tq)
        p = jnp.exp2(s_ref[...] - m_ref[...]).astype(BF16)
        o = jnp.dot(vt_ref[...], p, preferred_element_type=F32)
        o_ref[pl.ds(r0, tq), :] = (o[:MLA_V] / o[MLA_V:MLA_V + 1]).T.astype(o_ref.dtype)

    scores(0, 0)
    if n % 2 == 0:
        def body(j, carry):
            scores(2 * j + 1, 1)
            outputs(2 * j, 0)
            scores(2 * j + 2, 0)
            outputs(2 * j + 1, 1)
            return carry

        lax.fori_loop(0, n // 2 - 1, body, 0)
        scores(n - 1, 1)
        outputs(n - 2, 0)
        outputs(n - 1, 1)
    else:
        assert n == 1
        outputs(0, 0)


def _attention(q, k, v, tq):
    b, h, l, dq = q.shape
    tq = min(tq, l)
    lk = k.shape[2]
    bh = lambda bi, hi: (bi, hi, 0, 0)
    return pl.pallas_call(
        functools.partial(_attn_kernel, tq=tq),
        grid=(b, h),
        in_specs=[pl.BlockSpec((None, None, l, dq), bh),
                  pl.BlockSpec((None, None, lk, dq), bh),
                  pl.BlockSpec((None, None, MLA_VT_ROWS, lk), bh)],
        out_specs=pl.BlockSpec((None, l, MLA_V), lambda bi, hi: (bi, 0, hi)),
        out_shape=jax.ShapeDtypeStruct((b, l, h * MLA_V), BF16),
        scratch_shapes=[pltpu.VMEM((lk, tq), F32), pltpu.VMEM((lk, tq), F32),
                        pltpu.VMEM((1, tq), F32), pltpu.VMEM((1, tq), F32)],
        compiler_params=_cparams(("parallel", "parallel")),
        name="mla_attention",
    )(q, k, v)


def _merge_kernel(yg_ref, ym_ref, wg_ref, wm_ref, ga_ref, gb_ref, o_ref):
    a = jnp.dot(yg_ref[...], wg_ref[...], preferred_element_type=F32)
    m = jnp.dot(ym_ref[...], wm_ref[...], preferred_element_type=F32)
    y = (jax.nn.sigmoid(ga_ref[...].astype(F32)) * a
         + jax.nn.sigmoid(gb_ref[...].astype(F32)) * m)
    o_ref[...] = y.astype(o_ref.dtype)


def _branch_merge(yg, ym, wg, wm, p, tm, tn):
    b, l, d = yg.shape
    tm = min(tm, l)
    n = wg.shape[1]
    return pl.pallas_call(
        _merge_kernel,
        grid=(b, l // tm, n // tn),
        in_specs=[pl.BlockSpec((None, tm, d), lambda bi, i, j: (bi, i, 0)),
                  pl.BlockSpec((None, tm, d), lambda bi, i, j: (bi, i, 0)),
                  pl.BlockSpec((d, tn), lambda bi, i, j: (0, j)),
                  pl.BlockSpec((d, tn), lambda bi, i, j: (0, j)),
                  pl.BlockSpec((None, tm, tn), lambda bi, i, j: (bi, i, COL_GA // tn + j)),
                  pl.BlockSpec((None, tm, tn), lambda bi, i, j: (bi, i, COL_GB // tn + j))],
        out_specs=pl.BlockSpec((None, tm, tn), lambda bi, i, j: (bi, i, j)),
        out_shape=jax.ShapeDtypeStruct((b, l, n), BF16),
        compiler_params=_cparams(("parallel", "parallel", "arbitrary")),
        name="branch_merge",
    )(yg, ym, wg, wm, p, p)


def _outproj_kernel(y_ref, w_ref, x_ref, gt_ref, g2_ref, sc_ref, sh_ref, wr_ref, br_ref,
                    x1_ref, h2_ref, lg_ref):
    x1 = x_ref[...] + gt_ref[...] * jnp.dot(y_ref[...], w_ref[...], preferred_element_type=F32)
    x1_ref[...] = x1
    ms = jnp.mean(x1 * x1, axis=-1, keepdims=True)
    h2 = x1 * lax.rsqrt(ms + EPS) * g2_ref[...] * (1.0 + sc_ref[...]) + sh_ref[...]
    tm = x1.shape[0]
    for j, tile in enumerate(_pack_rows(h2)):
        h2_ref[pl.ds(j, tm, stride=ROW_TILES), :] = tile
    lg_ref[...] = jnp.dot(h2.astype(BF16), wr_ref[...], preferred_element_type=F32) + br_ref[...]


def _out_projection(y, w, x, gt, g2, sc, sh, wr, br, tm):
    b, l, d = x.shape
    tm = min(tm, l)
    row = lambda bi, i: (bi, i, 0)
    per_b = lambda bi, i: (bi, 0, 0)
    const = lambda bi, i: (0, 0)
    return pl.pallas_call(
        _outproj_kernel,
        grid=(b, l // tm),
        in_specs=[pl.BlockSpec((None, tm, d), row),
                  pl.BlockSpec((d, d), const),
                  pl.BlockSpec((None, tm, d), row),
                  pl.BlockSpec((None, 1, d), per_b),
                  pl.BlockSpec((1, d), const),
                  pl.BlockSpec((None, 1, d), per_b),
                  pl.BlockSpec((None, 1, d), per_b),
                  pl.BlockSpec((d, ROUTER_W), const),
                  pl.BlockSpec((1, ROUTER_W), const)],
        out_specs=[pl.BlockSpec((None, tm, d), row),
                   pl.BlockSpec((None, tm * ROW_TILES, LANE), row),
                   pl.BlockSpec((None, tm, ROUTER_W), row)],
        out_shape=[jax.ShapeDtypeStruct((b, l, d), F32),
                   jax.ShapeDtypeStruct((b, l * ROW_TILES, LANE), jnp.uint32),
                   jax.ShapeDtypeStruct((b, l, ROUTER_W), F32)],
        compiler_params=_cparams(("parallel", "parallel")),
        name="out_projection",
    )(y, w, x, gt, g2, sc, sh, wr, br)


def _route_kernel(lg_ref, e_ref, w_ref):
    lg = lg_ref[...]
    lane = lax.broadcasted_iota(jnp.int32, lg.shape, 1)
    neg = jnp.float32(-jnp.inf)
    big = jnp.int32(1 << 20)
    is_g = lane < N_GROUPS
    gl = jnp.where(is_g, lg, neg)
    gm = jnp.max(gl, axis=-1, keepdims=True)
    gidx = jnp.min(jnp.where(gl == gm, lane, big), axis=-1, keepdims=True)
    gsum = jnp.sum(jnp.where(is_g, jnp.exp(gl - gm), 0.0), axis=-1, keepdims=True)
    p_g = 1.0 / gsum
    g_lo = N_GROUPS + gidx * EXP_PER_GROUP
    in_grp = (lane >= g_lo) & (lane < g_lo + EXP_PER_GROUP)
    el = jnp.where(in_grp, lg, neg)
    em = jnp.max(el, axis=-1, keepdims=True)
    z = jnp.sum(jnp.where(in_grp, jnp.exp(el - em), 0.0), axis=-1, keepdims=True)
    i1 = jnp.min(jnp.where(el == em, lane, big), axis=-1, keepdims=True)
    el2 = jnp.where(lane == i1, neg, el)
    em2 = jnp.max(el2, axis=-1, keepdims=True)
    i2 = jnp.min(jnp.where(el2 == em2, lane, big), axis=-1, keepdims=True)
    p1 = 1.0 / z
    p2 = jnp.exp(em2 - em) / z
    tot = p1 + p2
    e_ref[...] = jnp.where(lane == 0, i1 - N_GROUPS, jnp.where(lane == 1, i2 - N_GROUPS, 0))
    w_ref[...] = jnp.where(lane == 0, p_g * p1 / tot, jnp.where(lane == 1, p_g * p2 / tot, 0.0))


def _route(logits, tm):
    t = logits.shape[0]
    tm = min(tm, t)
    spec = pl.BlockSpec((tm, ROUTER_W), lambda i: (i, 0))
    return pl.pallas_call(
        _route_kernel,
        grid=(t // tm,),
        in_specs=[spec],
        out_specs=[spec, spec],
        out_shape=[jax.ShapeDtypeStruct((t, ROUTER_W), jnp.int32),
                   jax.ShapeDtypeStruct((t, ROUTER_W), F32)],
        compiler_params=_cparams(("parallel",)),
        name="moe_route",
    )(logits)


def _expert_kernel(be_ref, nv_ref, first_ref, nxe_ref, ws_ref, j0_ref, tok_ref, dst_ref,
                   h_hbm, wg_hbm, wu_hbm, wd_hbm, yp_hbm,
                   xbuf, ybuf, wgv, wuv, wdv, sem_in, sem_out, sem_w, sem_init):
    i = pl.program_id(0)
    n = pl.num_programs(0)
    blk = MOE_BLOCK
    rt, pitch = ROW_TILES, ROW_PITCH
    plane = yp_hbm.shape[0] // (2 * rt)
    t_rows = plane - blk
    slot_rows = blk * pitch
    par = i % 2
    cur = pl.multiple_of(par * slot_rows, 8)
    oth = pl.multiple_of((1 - par) * slot_rows, 8)

    def weight_copies(e, s):
        return (pltpu.make_async_copy(wg_hbm.at[e], wgv.at[s], sem_w.at[s, 0]),
                pltpu.make_async_copy(wu_hbm.at[e], wuv.at[s], sem_w.at[s, 1]),
                pltpu.make_async_copy(wd_hbm.at[e], wdv.at[s], sem_w.at[s, 2]))

    def gather_start(b, buf0, sem, r_lo=0, r_hi=blk):
        base = j0_ref[b]
        for r in range(r_lo, r_hi):
            row0 = pl.multiple_of(tok_ref[base + r] * rt, rt)
            pltpu.make_async_copy(h_hbm.at[pl.ds(row0, rt)],
                                  xbuf.at[pl.ds(buf0 + r * pitch, rt)], sem_in.at[sem]).start()

    def gather_wait(buf0, sem):
        pltpu.make_async_copy(h_hbm.at[pl.ds(0, blk * rt)], xbuf.at[pl.ds(buf0, blk * rt)],
                              sem_in.at[sem]).wait()

    def scatter_start(b, buf0, sem, r_lo=0, r_hi=blk):
        base = j0_ref[b]
        filled = nv_ref[b]
        for r in range(r_lo, r_hi):
            dst = jnp.where(r < filled, dst_ref[base + r], t_rows + r)
            row0 = pl.multiple_of(dst * rt, rt)
            pltpu.make_async_copy(ybuf.at[pl.ds(buf0 + r * pitch, rt)],
                                  yp_hbm.at[pl.ds(row0, rt)], sem_out.at[sem]).start(
                                      priority=BULK_DMA if r % SCATTER_BULK_EVERY == 0 else 0)

    def scatter_wait(buf0, sem):
        pltpu.make_async_copy(ybuf.at[pl.ds(buf0, blk * rt)], yp_hbm.at[pl.ds(0, blk * rt)],
                              sem_out.at[sem]).wait()

    @pl.when(i == 0)
    def _():
        for cp in weight_copies(be_ref[0], 0):
            cp.start(priority=BULK_DMA)
        gather_start(0, 0, 0)
        ybuf[...] = jnp.zeros_like(ybuf)
        zeros = ybuf.at[pl.ds(slot_rows, blk * rt)]
        pltpu.make_async_copy(zeros, yp_hbm.at[pl.ds(t_rows * rt, blk * rt)], sem_out.at[0]).start()
        spare = pltpu.make_async_copy(zeros, yp_hbm.at[pl.ds((plane + t_rows) * rt, blk * rt)], sem_init)
        spare.start()
        spare.wait()

    valid = nv_ref[i] > 0
    nxt = jnp.minimum(i + 1, n - 1)
    has_next = jnp.logical_and(i + 1 < n, nv_ref[nxt] > 0)

    @pl.when(jnp.logical_and(valid, first_ref[i] == 1))
    def _():
        ws = ws_ref[i]
        for cp in weight_copies(be_ref[i], ws):
            cp.wait()

        @pl.when(nxe_ref[i] >= 0)
        def _():
            for cp in weight_copies(nxe_ref[i], 1 - ws):
                cp.start(priority=BULK_DMA)

    @pl.when(valid)
    def _():
        ws = ws_ref[i]
        gather_wait(cur, par)
        scatter_wait(cur, par)
        prev = jnp.maximum(i - 1, 0)
        per = blk // (2 * rt)

        def read_tiles(first_chunk):
            halves = []
            for j in range(rt):
                lo = (first_chunk + j) * per
                gather_start(nxt, oth, 1 - par, lo, lo + per)
                halves.append(_unpack_tile(xbuf[pl.ds(cur + j, blk, stride=pitch), :]))
            return jnp.concatenate([h[0] for h in halves] + [h[1] for h in halves],
                                   axis=1).astype(BF16)

        scatter_start(prev, oth, 1 - par)
        g = jnp.dot(read_tiles(0), wgv[ws].astype(BF16), preferred_element_type=F32)
        u = jnp.dot(read_tiles(rt), wuv[ws].astype(BF16), preferred_element_type=F32)
        a = (_silu(g) * u).astype(BF16)
        y = jnp.dot(a, wdv[ws].astype(BF16), preferred_element_type=F32)
        for j, tile in enumerate(_pack_rows(y)):
            ybuf[pl.ds(cur + j, blk, stride=pitch), :] = tile

        @pl.when(jnp.logical_not(has_next))
        def _():
            scatter_wait(oth, 1 - par)
            scatter_start(i, cur, par)
            gather_wait(oth, 1 - par)
            scatter_wait(cur, par)


def _experts(plan, h2, w_gate, w_up, w_down):
    d = w_gate.shape[1]
    rt, pitch = ROW_TILES, ROW_PITCH
    assert d == 2 * rt * LANE
    t = h2.shape[0] // rt
    n_blocks = plan[0].shape[0]
    blk = MOE_BLOCK
    de = w_gate.shape[2]
    any_spec = pl.BlockSpec(memory_space=pl.ANY)
    return pl.pallas_call(
        _expert_kernel,
        grid_spec=pltpu.PrefetchScalarGridSpec(
            num_scalar_prefetch=len(plan),
            grid=(n_blocks,),
            in_specs=[any_spec, any_spec, any_spec, any_spec],
            out_specs=any_spec,
            scratch_shapes=[pltpu.VMEM((2 * blk * pitch, LANE), jnp.uint32),
                            pltpu.VMEM((2 * blk * pitch, LANE), jnp.uint32),
                            pltpu.VMEM((2, d, de), F32),
                            pltpu.VMEM((2, d, de), F32),
                            pltpu.VMEM((2, de, d), F32),
                            pltpu.SemaphoreType.DMA((2,)),
                            pltpu.SemaphoreType.DMA((2,)),
                            pltpu.SemaphoreType.DMA((2, 3)),
                            pltpu.SemaphoreType.DMA],
        ),
        out_shape=jax.ShapeDtypeStruct((2 * (t + blk) * rt, LANE), jnp.uint32),
        compiler_params=_cparams(("arbitrary",)),
        name="moe_experts",
    )(*plan, h2, w_gate, w_up, w_down).reshape(2, (t + blk) * rt, LANE)


def _combine_kernel(x_ref, gt_ref, w_ref, yp_ref, o_ref):
    tm = x_ref.shape[0]
    w = w_ref[...]
    w0, w1 = w[:, 0:1], w[:, 1:2]
    lows, highs = [], []
    for j in range(ROW_TILES):
        lo0, hi0 = _unpack_tile(yp_ref[0, pl.ds(j, tm, stride=ROW_TILES), :])
        lo1, hi1 = _unpack_tile(yp_ref[1, pl.ds(j, tm, stride=ROW_TILES), :])
        lows.append(w0 * lo0 + w1 * lo1)
        highs.append(w0 * hi0 + w1 * hi1)
    o_ref[...] = x_ref[...] + gt_ref[...] * jnp.concatenate(lows + highs, axis=1)


def _combine(x1, gt, w_pick, yp, tm):
    b, l, d = x1.shape
    tm = min(tm, l)
    nb = l // tm
    return pl.pallas_call(
        _combine_kernel,
        grid=(b, nb),
        in_specs=[pl.BlockSpec((None, tm, d), lambda bi, i: (bi, i, 0)),
                  pl.BlockSpec((None, 1, d), lambda bi, i: (bi, 0, 0)),
                  pl.BlockSpec((tm, ROUTER_W), lambda bi, i: (bi * nb + i, 0)),
                  pl.BlockSpec((2, tm * ROW_TILES, LANE), lambda bi, i: (0, bi * nb + i, 0))],
        out_specs=pl.BlockSpec((None, tm, d), lambda bi, i: (bi, i, 0)),
        out_shape=jax.ShapeDtypeStruct((b, l, d), F32),
        compiler_params=_cparams(("parallel", "parallel")),
        name="moe_combine",
    )(x1, gt, w_pick, yp)


def _dispatch_plan(eid):
    blk = MOE_BLOCK
    n_assign = eid.shape[0]
    n_tok = n_assign // 2
    n_blocks = (n_assign + blk - 1) // blk + N_EXPERTS
    order = jnp.argsort(eid).astype(jnp.int32)
    e_ids = jnp.arange(N_EXPERTS, dtype=jnp.int32)
    counts = jnp.sum((eid[None, :] == e_ids[:, None]).astype(jnp.int32), axis=1)
    start = jnp.cumsum(counts) - counts
    nblk = (counts + blk - 1) // blk
    bend = jnp.cumsum(nblk)
    b_ids = jnp.arange(n_blocks, dtype=jnp.int32)
    block_e = jnp.minimum(jnp.searchsorted(bend, b_ids, side='right'), N_EXPERTS - 1).astype(jnp.int32)
    used = b_ids < bend[-1]
    within = b_ids - (bend - nblk)[block_e]
    block_nv = jnp.where(used, jnp.clip(counts[block_e] - within * blk, 0, blk), 0).astype(jnp.int32)
    block_j0 = jnp.where(used, start[block_e] + within * blk, 0).astype(jnp.int32)
    first = jnp.concatenate([jnp.ones((1,), jnp.int32),
                             (block_e[1:] != block_e[:-1]).astype(jnp.int32)])
    w_slot = ((jnp.cumsum(first) - 1) % 2).astype(jnp.int32)
    nonempty = jnp.where(counts > 0, e_ids, N_EXPERTS)
    later = jnp.flip(lax.cummin(jnp.flip(nonempty)))
    next_e = jnp.concatenate([later[1:], jnp.full((1,), N_EXPERTS, jnp.int32)])
    next_e = jnp.where(next_e >= N_EXPERTS, -1, next_e)[block_e]
    order = jnp.concatenate([order, jnp.zeros((blk,), jnp.int32)])
    tok_sorted = order // 2
    dst_sorted = (order % 2) * (n_tok + blk) + order // 2
    return (block_e, block_nv, first, next_e, w_slot, block_j0, tok_sorted, dst_sorted)


def _rope_tables(length, with_rope):
    half = MLA_ROPE // 4
    if with_rope:
        rows = length // GRID_W
        row = jnp.repeat(jnp.arange(rows), GRID_W).astype(F32)
        col = jnp.tile(jnp.arange(GRID_W), rows).astype(F32)
        inv = 1.0 / (ROPE_THETA ** (jnp.arange(half, dtype=F32) / half))
        ang_r = row[:, None] * inv
        ang_c = col[:, None] * inv
        cos = jnp.concatenate([jnp.cos(ang_r), jnp.cos(ang_r), jnp.cos(ang_c), jnp.cos(ang_c)], axis=1)
        sin = jnp.concatenate([-jnp.sin(ang_r), jnp.sin(ang_r), -jnp.sin(ang_c), jnp.sin(ang_c)], axis=1)
    else:
        cos = jnp.ones((length, MLA_ROPE), F32)
        sin = jnp.zeros((length, MLA_ROPE), F32)
    pad = ((0, 0), (0, LANE - MLA_ROPE))
    return jnp.pad(cos, pad), jnp.pad(sin, pad)


_IN_SIZES = (GLA_QK_W, GLA_QK_W, GLA_V_W, GLA_V_W, GLA_LOWRANK, GLA_LOWRANK,
             MLA_Q_RANK, MLA_KV_RANK, MLA_ROPE, D_MODEL, D_MODEL)
(SRC_Q, SRC_K, SRC_V, SRC_R, SRC_AF, SRC_AB, SRC_CQ, SRC_CKV, SRC_KR, SRC_GA, SRC_GB,
 SRC_END) = [sum(_IN_SIZES[:i]) for i in range(len(_IN_SIZES) + 1)]


def _w_in_layout_kernel(w_ref, lat_ref, ctx_ref):
    def rows(a, b):
        return w_ref[a:b, :].astype(BF16)

    small_pad = jnp.zeros((SMALL_W - MLA_ROPE - 2 * GLA_LOWRANK, w_ref.shape[1]), BF16)
    lat_ref[COL_Q:COL_GA, :] = rows(SRC_Q, SRC_AF)
    lat_ref[COL_GA:COL_CQ, :] = rows(SRC_GA, SRC_END)
    lat_ref[COL_CQ:COL_SMALL, :] = rows(SRC_CQ, SRC_KR)
    ctx_ref[CCOL_K:CCOL_CKV, :] = rows(SRC_K, SRC_R)
    ctx_ref[CCOL_CKV:CCOL_SMALL, :] = rows(SRC_CKV, SRC_KR)
    for ref, off in ((lat_ref, COL_SMALL), (ctx_ref, CCOL_SMALL)):
        ref[off:off + MLA_ROPE, :] = rows(SRC_KR, SRC_GA)
        ref[off + MLA_ROPE:off + SM_AB + GLA_LOWRANK, :] = rows(SRC_AF, SRC_CQ)
        ref[off + SM_AB + GLA_LOWRANK:off + SMALL_W, :] = small_pad


def _w_in_layout(w_in, tk=256):
    wt = jnp.swapaxes(w_in, 1, 2)
    _, n, d = wt.shape
    assert n == SRC_END
    return pl.pallas_call(
        _w_in_layout_kernel,
        grid=(d // tk,),
        in_specs=[pl.BlockSpec((None, n, tk), lambda i: (0, 0, i))],
        out_specs=[pl.BlockSpec((NP_LAT, tk), lambda i: (0, i)),
                   pl.BlockSpec((NP_CTX, tk), lambda i: (0, i))],
        out_shape=[jax.ShapeDtypeStruct((NP_LAT, d), BF16),
                   jax.ShapeDtypeStruct((NP_CTX, d), BF16)],
        compiler_params=_cparams(("parallel",)),
        name="w_in_layout",
    )(wt)


def kernel(x, c, ctx, c_ctx, w_mod, b_mod, norm1_g, norm2_g, w_in, w_decay_f, b_decay_f, w_decay_b, b_decay_b, gla_norm_g, q_a_norm_g, w_uq, kv_a_norm_g, w_ukv, q_norm_g, k_norm_g, w_o_gla, w_o_mla, w_out, w_router_group, b_router_group, w_router_expert, b_router_expert, w_exp_gate, w_exp_up, w_exp_down):
    assert w_mod.shape[0] == 1, "single-layer block"
    b, l, d = x.shape
    lc = ctx.shape[1]
    t = b * l

    c_rows = jnp.concatenate([c, c_ctx[None, :], jnp.zeros((8 - b - 1, d), F32)], axis=0)
    mod = _modulation(c_rows, w_mod[0], b_mod[0])
    sh1, sc1, gt1, sh2, sc2, gt2 = [mod[:b, i * d:(i + 1) * d].reshape(b, 1, d) for i in range(6)]
    sh1c = jnp.broadcast_to(mod[b:b + 1, 0:d].reshape(1, 1, d), (b, 1, d))
    sc1c = jnp.broadcast_to(mod[b:b + 1, d:2 * d].reshape(1, 1, d), (b, 1, d))

    w_lat, w_ctx = _w_in_layout(w_in)
    g1 = norm1_g[0].reshape(1, d)
    p = _in_projection(x, g1, sc1, sh1, w_lat, tm=INPROJ_TM)
    pc = _in_projection(ctx, g1, sc1c, sh1c, w_ctx, tm=lc)

    def decay_w(w, off):
        return jnp.zeros((LANE, GLA_QK_W), F32).at[off:off + GLA_LOWRANK].set(w).astype(BF16)

    y_gla = _gla(p, pc,
                 decay_w(w_decay_f[0], SM_AF), b_decay_f[0].reshape(1, -1),
                 decay_w(w_decay_b[0], SM_AB), b_decay_b[0].reshape(1, -1),
                 gla_norm_g[0].reshape(1, -1))

    wq = jnp.pad(w_uq[0].reshape(MLA_Q_RANK, MLA_HEADS, MLA_QK_DIM),
                 ((0, 0), (0, 0), (0, MLA_QK_PAD - MLA_QK_DIM))).reshape(MLA_Q_RANK, -1).astype(BF16)
    wkv = w_ukv[0].astype(BF16)
    qn_g = jnp.pad(q_norm_g[0], (0, MLA_QK_PAD - MLA_QK_DIM)).reshape(1, -1)
    kn_nope = k_norm_g[0][:MLA_NOPE].reshape(1, -1)
    kn_rope = jnp.pad(k_norm_g[0][MLA_NOPE:], (0, LANE - MLA_ROPE)).reshape(1, -1)
    cos, sin = _rope_tables(l, True)
    cos_c, sin_c = _rope_tables(lc, False)
    q_m = _mla_queries(p, q_a_norm_g[0].reshape(1, -1), wq, qn_g, cos, sin, tm=MLA_Q_TM)
    kva = kv_a_norm_g[0].reshape(1, -1)
    k_m, v_m = _mla_keys_values(p, pc, kva, wkv, kn_nope, kn_rope,
                                jnp.concatenate([cos, cos_c]), jnp.concatenate([sin, sin_c]))
    y_mla = _attention(q_m, k_m, v_m, tq=ATTN_TQ)

    y = _branch_merge(y_gla, y_mla, w_o_gla[0].astype(BF16), w_o_mla[0].astype(BF16), p,
                      tm=MERGE_TM, tn=MERGE_TN)
    w_router = jnp.concatenate(
        [w_router_group[0], w_router_expert[0],
         jnp.zeros((d, ROUTER_W - N_GROUPS - N_EXPERTS), F32)], axis=1).astype(BF16)
    b_router = jnp.concatenate(
        [b_router_group[0], b_router_expert[0],
         jnp.zeros((ROUTER_W - N_GROUPS - N_EXPERTS,), F32)]).reshape(1, -1)
    x1, h2, logits = _out_projection(y, w_out[0].astype(BF16), x, gt1, norm2_g[0].reshape(1, d),
                                     sc2, sh2, w_router, b_router, tm=OUTPROJ_TM)

    e_pick, w_pick = _route(logits.reshape(t, ROUTER_W), tm=ROUTE_TM)
    plan = _dispatch_plan(e_pick[:, :2].reshape(-1))
    yp = _experts(plan, h2.reshape(t * ROW_TILES, LANE),
                  w_exp_gate[0], w_exp_up[0], w_exp_down[0])
    return _combine(x1, gt2, w_pick, yp, tm=COMBINE_TM)
```

```python
import functools
import math

import jax
import jax.numpy as jnp
from jax import lax
from jax.experimental import pallas as pl
from jax.experimental.pallas import tpu as pltpu

F32 = jnp.float32
BF16 = jnp.bfloat16

D_MODEL = 2048
GRID_W = 64
EPS = 1e-6

GLA_HEADS = 4
GLA_DK = 256
GLA_DV = 512
GLA_LOWRANK = 16
GLA_TAU = 16.0
GLA_CHUNK = 64
GLA_GROUP = 8
GLA_QK_W = GLA_HEADS * GLA_DK
GLA_V_W = GLA_HEADS * GLA_DV

MLA_HEADS = 16
MLA_Q_RANK = 512
MLA_KV_RANK = 512
MLA_NOPE = 128
MLA_ROPE = 64
MLA_V = 128
MLA_QK_DIM = MLA_NOPE + MLA_ROPE
MLA_QK_PAD = 256
BF16_TILE_ROWS = 16
MLA_VT_ROWS = MLA_V + BF16_TILE_ROWS
ROPE_THETA = 10000.0
LOG2E = math.log2(math.e)

N_GROUPS = 8
EXP_PER_GROUP = 8
N_EXPERTS = N_GROUPS * EXP_PER_GROUP
D_EXPERT = 512
ROUTER_W = 128
MOE_BLOCK = 256
BULK_DMA = 1
SCATTER_BULK_EVERY = 4

LANE = 128
ROW_TILES = D_MODEL // (2 * LANE)
ROW_PITCH = 12
VMEM_LIMIT = 56 * 1024 * 1024

COL_Q = 0
COL_K = COL_Q + GLA_QK_W
COL_V = COL_K + GLA_QK_W
COL_R = COL_V + GLA_V_W
COL_GA = COL_R + GLA_V_W
COL_GB = COL_GA + D_MODEL
COL_CQ = COL_GB + D_MODEL
COL_CKV = COL_CQ + MLA_Q_RANK
COL_SMALL = COL_CKV + MLA_KV_RANK
SMALL_W = 256
NP_LAT = COL_SMALL + SMALL_W
SM_AF = MLA_ROPE
SM_AB = MLA_ROPE + GLA_LOWRANK
CCOL_K = 0
CCOL_V = CCOL_K + GLA_QK_W
CCOL_CKV = CCOL_V + GLA_V_W
CCOL_SMALL = CCOL_CKV + MLA_KV_RANK
NP_CTX = CCOL_SMALL + SMALL_W

INPROJ_TM, INPROJ_TN = 1024, 1280
INPROJ_NORM_ROWS = 128
MLA_Q_TM = 512
ATTN_TQ = 512
MERGE_TM, MERGE_TN = 1024, 512
OUTPROJ_TM = 256
ROUTE_TM = 1024
COMBINE_TM = 512


def _cparams(sem):
    return pltpu.CompilerParams(dimension_semantics=sem, vmem_limit_bytes=VMEM_LIMIT)


def _silu(x):
    return x * jax.nn.sigmoid(x)


def _pack_rows(x):
    half = x.shape[1] // 2
    lo = lax.bitcast_convert_type(x[:, :half].astype(BF16).astype(F32), jnp.uint32) >> 16
    hi = lax.bitcast_convert_type(x[:, half:].astype(BF16).astype(F32), jnp.uint32) & jnp.uint32(0xFFFF0000)
    packed = lo | hi
    return [packed[:, j * LANE:(j + 1) * LANE] for j in range(half // LANE)]


def _unpack_tile(t):
    lo = lax.bitcast_convert_type(t << 16, F32)
    hi = lax.bitcast_convert_type(t & jnp.uint32(0xFFFF0000), F32)
    return lo, hi


def _mod_kernel(c_ref, w_ref, b_ref, o_ref):
    a = _silu(c_ref[...]).astype(BF16)
    o_ref[...] = jnp.dot(a, w_ref[...].astype(BF16), preferred_element_type=F32) + b_ref[...]


def _modulation(c_rows, w_mod, b_mod):
    m, d = c_rows.shape
    n = w_mod.shape[1]
    tn = 1024
    return pl.pallas_call(
        _mod_kernel,
        grid=(n // tn,),
        in_specs=[pl.BlockSpec((m, d), lambda j: (0, 0)),
                  pl.BlockSpec((d, tn), lambda j: (0, j)),
                  pl.BlockSpec((1, tn), lambda j: (0, j))],
        out_specs=pl.BlockSpec((m, tn), lambda j: (0, j)),
        out_shape=jax.ShapeDtypeStruct((m, n), F32),
        compiler_params=_cparams(("parallel",)),
        name="modulation",
    )(c_rows, w_mod, b_mod.reshape(1, n))


def _inproj_kernel(x_ref, g_ref, sc_ref, sh_ref, w_ref, o_ref, h_ref):
    @pl.when(pl.program_id(2) == 0)
    def _():
        rows = min(INPROJ_NORM_ROWS, x_ref.shape[0])

        def norm_rows(c, carry):
            r0 = pl.multiple_of(c * rows, rows)
            x = x_ref[pl.ds(r0, rows), :]
            ms = jnp.mean(x * x, axis=-1, keepdims=True)
            y = x * lax.rsqrt(ms + EPS) * g_ref[...]
            h_ref[pl.ds(r0, rows), :] = (y * (1.0 + sc_ref[...]) + sh_ref[...]).astype(BF16)
            return carry

        lax.fori_loop(0, x_ref.shape[0] // rows, norm_rows, 0)

    o_ref[...] = lax.dot_general(h_ref[...], w_ref[...], (((1,), (1,)), ((), ())),
                                 preferred_element_type=F32).astype(o_ref.dtype)


def _in_projection(x, g, sc, sh, w, tm):
    b, l, d = x.shape
    tm = min(tm, l)
    n = w.shape[0]
    tn = INPROJ_TN
    return pl.pallas_call(
        _inproj_kernel,
        grid=(b, l // tm, n // tn),
        in_specs=[pl.BlockSpec((None, tm, d), lambda bi, i, j: (bi, i, 0)),
                  pl.BlockSpec((1, d), lambda bi, i, j: (0, 0)),
                  pl.BlockSpec((None, 1, d), lambda bi, i, j: (bi, 0, 0)),
                  pl.BlockSpec((None, 1, d), lambda bi, i, j: (bi, 0, 0)),
                  pl.BlockSpec((tn, d), lambda bi, i, j: (j, 0))],
        out_specs=pl.BlockSpec((None, tm, tn), lambda bi, i, j: (bi, i, j)),
        out_shape=jax.ShapeDtypeStruct((b, l, n), BF16),
        scratch_shapes=[pltpu.VMEM((tm, d), BF16)],
        compiler_params=_cparams(("parallel", "parallel", "arbitrary")),
        name="in_projection",
    )(x, g, sc, sh, w)


def _gla_kernel(q_ref, k_ref, v_ref, r_ref, sm_ref, kc_ref, vc_ref, smc_ref,
                wf_ref, bf_ref, wb_ref, bb_ref, gn_ref, y_ref, o_ref, sf_ref, sb_ref):
    c = GLA_CHUNK
    n_lat = q_ref.shape[0] // c
    n_ctx = kc_ref.shape[0] // c
    row = lax.broadcasted_iota(jnp.int32, (c, c), 0)
    col = lax.broadcasted_iota(jnp.int32, (c, c), 1)
    lower = row >= col
    upper = row <= col
    nt = (((1,), (1,)), ((), ()))
    tn = (((0,), (0,)), ((), ()))

    def cum_decay(sm, w_ref, b_ref, tri, n_chunks):
        z = jnp.dot(sm, w_ref[...], preferred_element_type=F32) + b_ref[...]
        g = (jnp.minimum(z, 0.0) - jnp.log(1.0 + jnp.exp(-jnp.abs(z)))) * (1.0 / GLA_TAU)
        g_hi = g.astype(BF16)
        g_lo = (g - g_hi.astype(F32)).astype(BF16)
        t = jnp.where(tri, 1.0, 0.0).astype(BF16)
        return [jnp.dot(t, g_hi[i * c:(i + 1) * c], preferred_element_type=F32)
                + jnp.dot(t, g_lo[i * c:(i + 1) * c], preferred_element_type=F32)
                for i in range(n_chunks)]

    fwd = (wf_ref, bf_ref, lower, c - 1, sf_ref)
    bwd = (wb_ref, bb_ref, upper, 0, sb_ref)

    def group(r0, n_chunks, direction, sm_r, k_r, v_r, q_r):
        w_ref, b_ref, tri, end_row, st_ref = direction
        rows = n_chunks * c
        k_all = k_r[pl.ds(r0, rows), :].astype(F32)
        v_all = v_r[pl.ds(r0, rows), :]
        q_all = None if q_r is None else q_r[pl.ds(r0, rows), :].astype(F32) * (GLA_DK ** -0.5)
        bc_all = cum_decay(sm_r[pl.ds(r0, rows), :], w_ref, b_ref, tri, n_chunks)
        local = []
        for i in range(n_chunks):
            sl = slice(i * c, (i + 1) * c)
            bc = bc_all[i]
            bend = bc[end_row:end_row + 1, :]
            k, v = k_all[sl], v_all[sl]
            ke = (k * jnp.exp(bend - bc)).astype(BF16)
            upd = lax.dot_general(v, ke, tn, preferred_element_type=F32)
            qd = o_intra = None
            if q_all is not None:
                qd = (q_all[sl] * jnp.exp(bc)).astype(BF16)
                ki = (k * jnp.exp(-bc)).astype(BF16)
                att = lax.dot_general(qd, ki, nt, preferred_element_type=F32)
                att = jnp.where(tri, att, 0.0).astype(BF16)
                o_intra = jnp.dot(att, v, preferred_element_type=F32)
            local.append((jnp.exp(bend), upd, qd, o_intra))
        st = st_ref[...]
        outs = [None] * n_chunks
        for i in (range(n_chunks) if end_row else reversed(range(n_chunks))):
            decay, upd, qd, o_intra = local[i]
            if q_all is not None:
                outs[i] = o_intra + lax.dot_general(qd, st.astype(BF16), nt, preferred_element_type=F32)
            st = st * decay + upd
        st_ref[...] = st
        return None if q_all is None else jnp.concatenate(outs, axis=0)

    def readout(r0, o):
        ms = jnp.mean(o * o, axis=-1, keepdims=True)
        yn = o * lax.rsqrt(ms + EPS) * gn_ref[...]
        r = r_ref[pl.ds(r0, o.shape[0]), :].astype(F32)
        y_ref[pl.ds(r0, o.shape[0]), :] = (yn * _silu(r)).astype(y_ref.dtype)

    sf_ref[...] = jnp.zeros_like(sf_ref)
    sb_ref[...] = jnp.zeros_like(sb_ref)
    group(0, n_ctx, fwd, smc_ref, kc_ref, vc_ref, None)
    group(0, n_ctx, bwd, smc_ref, kc_ref, vc_ref, None)

    g_lat = min(GLA_GROUP, n_lat // 2)
    assert n_lat % (2 * g_lat) == 0
    n_groups = n_lat // g_lat
    rows = g_lat * c

    def scan(i, direction):
        r0 = pl.multiple_of(i * rows, rows)
        return r0, group(r0, g_lat, direction, sm_ref, k_ref, v_ref, q_ref)

    def first_half(i, carry):
        for r0, o in (scan(i, fwd), scan(n_groups - 1 - i, bwd)):
            o_ref[pl.ds(r0, rows), :] = o
        return carry

    lax.fori_loop(0, n_groups // 2, first_half, 0)

    def second_half(i, carry):
        for r0, o in (scan(i, fwd), scan(n_groups - 1 - i, bwd)):
            readout(r0, o_ref[pl.ds(r0, rows), :] + o)
        return carry

    lax.fori_loop(n_groups // 2, n_groups, second_half, 0)


def _gla(p, pc, wf, bf, wb, bb, gn):
    b, l, _ = p.shape
    lc = pc.shape[1]
    dk, dv = GLA_DK, GLA_DV
    hmap = lambda off: (lambda bi, h: (bi, 0, off + h))
    wmap = lambda bi, h: (0, h)
    return pl.pallas_call(
        _gla_kernel,
        grid=(b, GLA_HEADS),
        in_specs=[pl.BlockSpec((None, l, dk), hmap(COL_Q // dk)),
                  pl.BlockSpec((None, l, dk), hmap(COL_K // dk)),
                  pl.BlockSpec((None, l, dv), hmap(COL_V // dv)),
                  pl.BlockSpec((None, l, dv), hmap(COL_R // dv)),
                  pl.BlockSpec((None, l, LANE), lambda bi, h: (bi, 0, COL_SMALL // LANE)),
                  pl.BlockSpec((None, lc, dk), hmap(CCOL_K // dk)),
                  pl.BlockSpec((None, lc, dv), hmap(CCOL_V // dv)),
                  pl.BlockSpec((None, lc, LANE), lambda bi, h: (bi, 0, CCOL_SMALL // LANE)),
                  pl.BlockSpec((LANE, dk), wmap),
                  pl.BlockSpec((1, dk), wmap),
                  pl.BlockSpec((LANE, dk), wmap),
                  pl.BlockSpec((1, dk), wmap),
                  pl.BlockSpec((1, dv), lambda bi, h: (0, 0))],
        out_specs=pl.BlockSpec((None, l, dv), lambda bi, h: (bi, 0, h)),
        out_shape=jax.ShapeDtypeStruct((b, l, GLA_V_W), BF16),
        scratch_shapes=[pltpu.VMEM((l, dv), F32), pltpu.VMEM((dv, dk), F32), pltpu.VMEM((dv, dk), F32)],
        compiler_params=_cparams(("parallel", "parallel")),
        name="gla",
    )(p, p, p, p, p, pc, pc, pc, wf, bf, wb, bb, gn)


def _rope_partner(t):
    lane = lax.broadcasted_iota(jnp.int32, t.shape, 1)
    first = (lane % 32) < 16
    return jnp.where(first, pltpu.roll(t, LANE - 16, 1), pltpu.roll(t, 16, 1))


def _mla_q_kernel(cq_ref, ga_ref, w_ref, gn_ref, cos_ref, sin_ref, o_ref):
    cq = cq_ref[...].astype(F32)
    cn = (cq * lax.rsqrt(jnp.mean(cq * cq, axis=-1, keepdims=True) + EPS) * ga_ref[...]).astype(BF16)
    cos = cos_ref[...]
    sin = sin_ref[...]
    scale = MLA_QK_DIM ** -0.5 * LOG2E
    for h in range(MLA_HEADS):
        qh = jnp.dot(cn, w_ref[:, h * MLA_QK_PAD:(h + 1) * MLA_QK_PAD], preferred_element_type=F32)
        ms = jnp.sum(qh * qh, axis=-1, keepdims=True) * (1.0 / MLA_QK_DIM)
        qn = qh * lax.rsqrt(ms + EPS) * gn_ref[...]
        t = qn[:, MLA_NOPE:]
        rot = t * cos + _rope_partner(t) * sin
        o_ref[h, :, :MLA_NOPE] = (qn[:, :MLA_NOPE] * scale).astype(o_ref.dtype)
        o_ref[h, :, MLA_NOPE:] = (rot * scale).astype(o_ref.dtype)


def _mla_queries(p, ga, wq, gn, cos, sin, tm):
    b, l, _ = p.shape
    tm = min(tm, l)
    return pl.pallas_call(
        _mla_q_kernel,
        grid=(b, l // tm),
        in_specs=[pl.BlockSpec((None, tm, MLA_Q_RANK), lambda bi, i: (bi, i, COL_CQ // MLA_Q_RANK)),
                  pl.BlockSpec((1, MLA_Q_RANK), lambda bi, i: (0, 0)),
                  pl.BlockSpec(wq.shape, lambda bi, i: (0, 0)),
                  pl.BlockSpec((1, MLA_QK_PAD), lambda bi, i: (0, 0)),
                  pl.BlockSpec((tm, LANE), lambda bi, i: (i, 0)),
                  pl.BlockSpec((tm, LANE), lambda bi, i: (i, 0))],
        out_specs=pl.BlockSpec((None, MLA_HEADS, tm, MLA_QK_PAD), lambda bi, i: (bi, 0, i, 0)),
        out_shape=jax.ShapeDtypeStruct((b, MLA_HEADS, l, MLA_QK_PAD), BF16),
        compiler_params=_cparams(("parallel", "parallel")),
        name="mla_queries",
    )(p, ga, wq, gn, cos, sin)


def _mla_kv_kernel(ckv_ref, sm_ref, ckvc_ref, smc_ref, ga_ref, w_ref, gk_ref, gr_ref,
                   cos_ref, sin_ref, k_ref, v_ref, *, n_lat):
    is_ctx = pl.program_id(1) >= n_lat
    ckv = jnp.where(is_ctx, ckvc_ref[...], ckv_ref[...]).astype(F32)
    cn = (ckv * lax.rsqrt(jnp.mean(ckv * ckv, axis=-1, keepdims=True) + EPS) * ga_ref[...]).astype(BF16)
    sm = jnp.where(is_ctx, smc_ref[...], sm_ref[...]).astype(F32)
    lane = lax.broadcasted_iota(jnp.int32, sm.shape, 1)
    kr = jnp.where(lane < MLA_ROPE, sm, 0.0)
    ss_r = jnp.sum(kr * kr, axis=-1, keepdims=True)
    krg = kr * gr_ref[...]
    rot = krg * cos_ref[...] + _rope_partner(krg) * sin_ref[...]
    hw = MLA_NOPE + MLA_V
    tm = ckv.shape[0]
    sub = lax.broadcasted_iota(jnp.int32, (MLA_VT_ROWS - MLA_V, tm), 0)
    ones_rows = jnp.where(sub == 0, 1.0, 0.0).astype(v_ref.dtype)
    for h in range(MLA_HEADS):
        kvh = jnp.dot(cn, w_ref[:, h * hw:(h + 1) * hw], preferred_element_type=F32)
        kn = kvh[:, :MLA_NOPE]
        ms = (jnp.sum(kn * kn, axis=-1, keepdims=True) + ss_r) * (1.0 / MLA_QK_DIM)
        rs = lax.rsqrt(ms + EPS)
        k_ref[h, :, :MLA_NOPE] = (kn * rs * gk_ref[...]).astype(k_ref.dtype)
        k_ref[h, :, MLA_NOPE:] = (rot * rs).astype(k_ref.dtype)
        v_ref[h, :MLA_V, :] = kvh[:, MLA_NOPE:].T.astype(v_ref.dtype)
        v_ref[h, MLA_V:, :] = ones_rows


def _mla_keys_values(p, pc, ga, wkv, gk, gr, cos, sin):
    b, l, _ = p.shape
    lc = pc.shape[1]
    tm = min(lc, l)
    assert l % tm == 0 and lc % tm == 0
    n_lat, n_ctx = l // tm, lc // tm
    lat = lambda blk: (lambda bi, i: (bi, jnp.minimum(i, n_lat - 1), blk))
    ctx = lambda blk: (lambda bi, i: (bi, jnp.maximum(i - n_lat, 0), blk))
    const = lambda bi, i: (0, 0)
    k_spec = pl.BlockSpec((None, MLA_HEADS, tm, MLA_QK_PAD), lambda bi, i: (bi, 0, i, 0))
    vt_spec = pl.BlockSpec((None, MLA_HEADS, MLA_VT_ROWS, tm), lambda bi, i: (bi, 0, 0, i))
    k_shape = jax.ShapeDtypeStruct((b, MLA_HEADS, l + lc, MLA_QK_PAD), BF16)
    vt_shape = jax.ShapeDtypeStruct((b, MLA_HEADS, MLA_VT_ROWS, l + lc), BF16)
    return pl.pallas_call(
        functools.partial(_mla_kv_kernel, n_lat=n_lat),
        grid=(b, n_lat + n_ctx),
        in_specs=[pl.BlockSpec((None, tm, MLA_KV_RANK), lat(COL_CKV // MLA_KV_RANK)),
                  pl.BlockSpec((None, tm, LANE), lat(COL_SMALL // LANE)),
                  pl.BlockSpec((None, tm, MLA_KV_RANK), ctx(CCOL_CKV // MLA_KV_RANK)),
                  pl.BlockSpec((None, tm, LANE), ctx(CCOL_SMALL // LANE)),
                  pl.BlockSpec((1, MLA_KV_RANK), const),
                  pl.BlockSpec(wkv.shape, const),
                  pl.BlockSpec((1, LANE), const),
                  pl.BlockSpec((1, LANE), const),
                  pl.BlockSpec((tm, LANE), lambda bi, i: (i, 0)),
                  pl.BlockSpec((tm, LANE), lambda bi, i: (i, 0))],
        out_specs=[k_spec, vt_spec],
        out_shape=[k_shape, vt_shape],
        compiler_params=_cparams(("parallel", "arbitrary")),
        name="mla_keys_values",
    )(p, p, pc, pc, ga, wkv, gk, gr, cos, sin)


def _attn_kernel(q_ref, k_ref, vt_ref, o_ref, s0_ref, s1_ref, m0_ref, m1_ref, *, tq):
    nt = (((1,), (1,)), ((), ()))
    n = q_ref.shape[0] // tq
    bufs = ((s0_ref, m0_ref), (s1_ref, m1_ref))

    def scores(i, slot):
        s_ref, m_ref = bufs[slot]
        r0 = pl.multiple_of(i * tq, tq)
        s = lax.dot_general(k_ref[...], q_ref[pl.ds(r0, tq), :], nt, preferred_element_type=F32)
        s_ref[...] = s
        m_ref[...] = jnp.max(s, axis=0, keepdims=True)

    def outputs(i, slot):
        s_ref, m_ref = bufs[slot]
        r0 = pl.multiple_of(i * tq, tq)
        p = jnp.exp2(s_ref[...] - m_ref[...]).astype(BF16)
        o = jnp.dot(vt_ref[...], p, preferred_element_type=F32)
        o_ref[pl.ds(r0, tq), :] = (o[:MLA_V] / o[MLA_V:MLA_V + 1]).T.astype(o_ref.dtype)

    scores(0, 0)
    if n % 2 == 0:
        def body(j, carry):
            scores(2 * j + 1, 1)
            outputs(2 * j, 0)
            scores(2 * j + 2, 0)
            outputs(2 * j + 1, 1)
            return carry

        lax.fori_loop(0, n // 2 - 1, body, 0)
        scores(n - 1, 1)
        outputs(n - 2, 0)
        outputs(n - 1, 1)
    else:
        assert n == 1
        outputs(0, 0)


def _attention(q, k, v, tq):
    b, h, l, dq = q.shape
    tq = min(tq, l)
    lk = k.shape[2]
    bh = lambda bi, hi: (bi, hi, 0, 0)
    return pl.pallas_call(
        functools.partial(_attn_kernel, tq=tq),
        grid=(b, h),
        in_specs=[pl.BlockSpec((None, None, l, dq), bh),
                  pl.BlockSpec((None, None, lk, dq), bh),
                  pl.BlockSpec((None, None, MLA_VT_ROWS, lk), bh)],
        out_specs=pl.BlockSpec((None, l, MLA_V), lambda bi, hi: (bi, 0, hi)),
        out_shape=jax.ShapeDtypeStruct((b, l, h * MLA_V), BF16),
        scratch_shapes=[pltpu.VMEM((lk, tq), F32), pltpu.VMEM((lk, tq), F32),
                        pltpu.VMEM((1, tq), F32), pltpu.VMEM((1, tq), F32)],
        compiler_params=_cparams(("parallel", "parallel")),
        name="mla_attention",
    )(q, k, v)


def _merge_kernel(yg_ref, ym_ref, wg_ref, wm_ref, ga_ref, gb_ref, o_ref):
    a = jnp.dot(yg_ref[...], wg_ref[...], preferred_element_type=F32)
    m = jnp.dot(ym_ref[...], wm_ref[...], preferred_element_type=F32)
    y = (jax.nn.sigmoid(ga_ref[...].astype(F32)) * a
         + jax.nn.sigmoid(gb_ref[...].astype(F32)) * m)
    o_ref[...] = y.astype(o_ref.dtype)


def _branch_merge(yg, ym, wg, wm, p, tm, tn):
    b, l, d = yg.shape
    tm = min(tm, l)
    n = wg.shape[1]
    return pl.pallas_call(
        _merge_kernel,
        grid=(b, l // tm, n // tn),
        in_specs=[pl.BlockSpec((None, tm, d), lambda bi, i, j: (bi, i, 0)),
                  pl.BlockSpec((None, tm, d), lambda bi, i, j: (bi, i, 0)),
                  pl.BlockSpec((d, tn), lambda bi, i, j: (0, j)),
                  pl.BlockSpec((d, tn), lambda bi, i, j: (0, j)),
                  pl.BlockSpec((None, tm, tn), lambda bi, i, j: (bi, i, COL_GA // tn + j)),
                  pl.BlockSpec((None, tm, tn), lambda bi, i, j: (bi, i, COL_GB // tn + j))],
        out_specs=pl.BlockSpec((None, tm, tn), lambda bi, i, j: (bi, i, j)),
        out_shape=jax.ShapeDtypeStruct((b, l, n), BF16),
        compiler_params=_cparams(("parallel", "parallel", "arbitrary")),
        name="branch_merge",
    )(yg, ym, wg, wm, p, p)


def _outproj_kernel(y_ref, w_ref, x_ref, gt_ref, g2_ref, sc_ref, sh_ref, wr_ref, br_ref,
                    x1_ref, h2_ref, lg_ref):
    x1 = x_ref[...] + gt_ref[...] * jnp.dot(y_ref[...], w_ref[...], preferred_element_type=F32)
    x1_ref[...] = x1
    ms = jnp.mean(x1 * x1, axis=-1, keepdims=True)
    h2 = x1 * lax.rsqrt(ms + EPS) * g2_ref[...] * (1.0 + sc_ref[...]) + sh_ref[...]
    tm = x1.shape[0]
    for j, tile in enumerate(_pack_rows(h2)):
        h2_ref[pl.ds(j, tm, stride=ROW_TILES), :] = tile
    lg_ref[...] = jnp.dot(h2.astype(BF16), wr_ref[...], preferred_element_type=F32) + br_ref[...]


def _out_projection(y, w, x, gt, g2, sc, sh, wr, br, tm):
    b, l, d = x.shape
    tm = min(tm, l)
    row = lambda bi, i: (bi, i, 0)
    per_b = lambda bi, i: (bi, 0, 0)
    const = lambda bi, i: (0, 0)
    return pl.pallas_call(
        _outproj_kernel,
        grid=(b, l // tm),
        in_specs=[pl.BlockSpec((None, tm, d), row),
                  pl.BlockSpec((d, d), const),
                  pl.BlockSpec((None, tm, d), row),
                  pl.BlockSpec((None, 1, d), per_b),
                  pl.BlockSpec((1, d), const),
                  pl.BlockSpec((None, 1, d), per_b),
                  pl.BlockSpec((None, 1, d), per_b),
                  pl.BlockSpec((d, ROUTER_W), const),
                  pl.BlockSpec((1, ROUTER_W), const)],
        out_specs=[pl.BlockSpec((None, tm, d), row),
                   pl.BlockSpec((None, tm * ROW_TILES, LANE), row),
                   pl.BlockSpec((None, tm, ROUTER_W), row)],
        out_shape=[jax.ShapeDtypeStruct((b, l, d), F32),
                   jax.ShapeDtypeStruct((b, l * ROW_TILES, LANE), jnp.uint32),
                   jax.ShapeDtypeStruct((b, l, ROUTER_W), F32)],
        compiler_params=_cparams(("parallel", "parallel")),
        name="out_projection",
    )(y, w, x, gt, g2, sc, sh, wr, br)


def _route_kernel(lg_ref, e_ref, w_ref):
    lg = lg_ref[...]
    lane = lax.broadcasted_iota(jnp.int32, lg.shape, 1)
    neg = jnp.float32(-jnp.inf)
    big = jnp.int32(1 << 20)
    is_g = lane < N_GROUPS
    gl = jnp.where(is_g, lg, neg)
    gm = jnp.max(gl, axis=-1, keepdims=True)
    gidx = jnp.min(jnp.where(gl == gm, lane, big), axis=-1, keepdims=True)
    gsum = jnp.sum(jnp.where(is_g, jnp.exp(gl - gm), 0.0), axis=-1, keepdims=True)
    p_g = 1.0 / gsum
    g_lo = N_GROUPS + gidx * EXP_PER_GROUP
    in_grp = (lane >= g_lo) & (lane < g_lo + EXP_PER_GROUP)
    el = jnp.where(in_grp, lg, neg)
    em = jnp.max(el, axis=-1, keepdims=True)
    z = jnp.sum(jnp.where(in_grp, jnp.exp(el - em), 0.0), axis=-1, keepdims=True)
    i1 = jnp.min(jnp.where(el == em, lane, big), axis=-1, keepdims=True)
    el2 = jnp.where(lane == i1, neg, el)
    em2 = jnp.max(el2, axis=-1, keepdims=True)
    i2 = jnp.min(jnp.where(el2 == em2, lane, big), axis=-1, keepdims=True)
    p1 = 1.0 / z
    p2 = jnp.exp(em2 - em) / z
    tot = p1 + p2
    e_ref[...] = jnp.where(lane == 0, i1 - N_GROUPS, jnp.where(lane == 1, i2 - N_GROUPS, 0))
    w_ref[...] = jnp.where(lane == 0, p_g * p1 / tot, jnp.where(lane == 1, p_g * p2 / tot, 0.0))


def _route(logits, tm):
    t = logits.shape[0]
    tm = min(tm, t)
    spec = pl.BlockSpec((tm, ROUTER_W), lambda i: (i, 0))
    return pl.pallas_call(
        _route_kernel,
        grid=(t // tm,),
        in_specs=[spec],
        out_specs=[spec, spec],
        out_shape=[jax.ShapeDtypeStruct((t, ROUTER_W), jnp.int32),
                   jax.ShapeDtypeStruct((t, ROUTER_W), F32)],
        compiler_params=_cparams(("parallel",)),
        name="moe_route",
    )(logits)


def _expert_kernel(be_ref, nv_ref, first_ref, nxe_ref, ws_ref, j0_ref, tok_ref, dst_ref,
                   h_hbm, wg_hbm, wu_hbm, wd_hbm, yp_hbm,
                   xbuf, ybuf, wgv, wuv, wdv, sem_in, sem_out, sem_w, sem_init):
    i = pl.program_id(0)
    n = pl.num_programs(0)
    blk = MOE_BLOCK
    rt, pitch = ROW_TILES, ROW_PITCH
    plane = yp_hbm.shape[0] // (2 * rt)
    t_rows = plane - blk
    slot_rows = blk * pitch
    par = i % 2
    cur = pl.multiple_of(par * slot_rows, 8)
    oth = pl.multiple_of((1 - par) * slot_rows, 8)

    def weight_copies(e, s):
        return (pltpu.make_async_copy(wg_hbm.at[e], wgv.at[s], sem_w.at[s, 0]),
                pltpu.make_async_copy(wu_hbm.at[e], wuv.at[s], sem_w.at[s, 1]),
                pltpu.make_async_copy(wd_hbm.at[e], wdv.at[s], sem_w.at[s, 2]))

    def gather_start(b, buf0, sem, r_lo=0, r_hi=blk):
        base = j0_ref[b]
        for r in range(r_lo, r_hi):
            row0 = pl.multiple_of(tok_ref[base + r] * rt, rt)
            pltpu.make_async_copy(h_hbm.at[pl.ds(row0, rt)],
                                  xbuf.at[pl.ds(buf0 + r * pitch, rt)], sem_in.at[sem]).start()

    def gather_wait(buf0, sem):
        pltpu.make_async_copy(h_hbm.at[pl.ds(0, blk * rt)], xbuf.at[pl.ds(buf0, blk * rt)],
                              sem_in.at[sem]).wait()

    def scatter_start(b, buf0, sem, r_lo=0, r_hi=blk):
        base = j0_ref[b]
        filled = nv_ref[b]
        for r in range(r_lo, r_hi):
            dst = jnp.where(r < filled, dst_ref[base + r], t_rows + r)
            row0 = pl.multiple_of(dst * rt, rt)
            pltpu.make_async_copy(ybuf.at[pl.ds(buf0 + r * pitch, rt)],
                                  yp_hbm.at[pl.ds(row0, rt)], sem_out.at[sem]).start(
                                      priority=BULK_DMA if r % SCATTER_BULK_EVERY == 0 else 0)

    def scatter_wait(buf0, sem):
        pltpu.make_async_copy(ybuf.at[pl.ds(buf0, blk * rt)], yp_hbm.at[pl.ds(0, blk * rt)],
                              sem_out.at[sem]).wait()

    @pl.when(i == 0)
    def _():
        for cp in weight_copies(be_ref[0], 0):
            cp.start(priority=BULK_DMA)
        gather_start(0, 0, 0)
        ybuf[...] = jnp.zeros_like(ybuf)
        zeros = ybuf.at[pl.ds(slot_rows, blk * rt)]
        pltpu.make_async_copy(zeros, yp_hbm.at[pl.ds(t_rows * rt, blk * rt)], sem_out.at[0]).start()
        spare = pltpu.make_async_copy(zeros, yp_hbm.at[pl.ds((plane + t_rows) * rt, blk * rt)], sem_init)
        spare.start()
        spare.wait()

    valid = nv_ref[i] > 0
    nxt = jnp.minimum(i + 1, n - 1)
    has_next = jnp.logical_and(i + 1 < n, nv_ref[nxt] > 0)

    @pl.when(jnp.logical_and(valid, first_ref[i] == 1))
    def _():
        ws = ws_ref[i]
        for cp in weight_copies(be_ref[i], ws):
            cp.wait()

        @pl.when(nxe_ref[i] >= 0)
        def _():
            for cp in weight_copies(nxe_ref[i], 1 - ws):
                cp.start(priority=BULK_DMA)

    @pl.when(valid)
    def _():
        ws = ws_ref[i]
        gather_wait(cur, par)
        scatter_wait(cur, par)
        prev = jnp.maximum(i - 1, 0)
        per = blk // (2 * rt)

        def read_tiles(first_chunk):
            halves = []
            for j in range(rt):
                lo = (first_chunk + j) * per
                gather_start(nxt, oth, 1 - par, lo, lo + per)
                halves.append(_unpack_tile(xbuf[pl.ds(cur + j, blk, stride=pitch), :]))
            return jnp.concatenate([h[0] for h in halves] + [h[1] for h in halves],
                                   axis=1).astype(BF16)

        scatter_start(prev, oth, 1 - par)
        g = jnp.dot(read_tiles(0), wgv[ws].astype(BF16), preferred_element_type=F32)
        u = jnp.dot(read_tiles(rt), wuv[ws].astype(BF16), preferred_element_type=F32)
        a = (_silu(g) * u).astype(BF16)
        y = jnp.dot(a, wdv[ws].astype(BF16), preferred_element_type=F32)
        for j, tile in enumerate(_pack_rows(y)):
            ybuf[pl.ds(cur + j, blk, stride=pitch), :] = tile

        @pl.when(jnp.logical_not(has_next))
        def _():
            scatter_wait(oth, 1 - par)
            scatter_start(i, cur, par)
            gather_wait(oth, 1 - par)
            scatter_wait(cur, par)


def _experts(plan, h2, w_gate, w_up, w_down):
    d = w_gate.shape[1]
    rt, pitch = ROW_TILES, ROW_PITCH
    assert d == 2 * rt * LANE
    t = h2.shape[0] // rt
    n_blocks = plan[0].shape[0]
    blk = MOE_BLOCK
    de = w_gate.shape[2]
    any_spec = pl.BlockSpec(memory_space=pl.ANY)
    return pl.pallas_call(
        _expert_kernel,
        grid_spec=pltpu.PrefetchScalarGridSpec(
            num_scalar_prefetch=len(plan),
            grid=(n_blocks,),
            in_specs=[any_spec, any_spec, any_spec, any_spec],
            out_specs=any_spec,
            scratch_shapes=[pltpu.VMEM((2 * blk * pitch, LANE), jnp.uint32),
                            pltpu.VMEM((2 * blk * pitch, LANE), jnp.uint32),
                            pltpu.VMEM((2, d, de), F32),
                            pltpu.VMEM((2, d, de), F32),
                            pltpu.VMEM((2, de, d), F32),
                            pltpu.SemaphoreType.DMA((2,)),
                            pltpu.SemaphoreType.DMA((2,)),
                            pltpu.SemaphoreType.DMA((2, 3)),
                            pltpu.SemaphoreType.DMA],
        ),
        out_shape=jax.ShapeDtypeStruct((2 * (t + blk) * rt, LANE), jnp.uint32),
        compiler_params=_cparams(("arbitrary",)),
        name="moe_experts",
    )(*plan, h2, w_gate, w_up, w_down).reshape(2, (t + blk) * rt, LANE)


def _combine_kernel(x_ref, gt_ref, w_ref, yp_ref, o_ref):
    tm = x_ref.shape[0]
    w = w_ref[...]
    w0, w1 = w[:, 0:1], w[:, 1:2]
    lows, highs = [], []
    for j in range(ROW_TILES):
        lo0, hi0 = _unpack_tile(yp_ref[0, pl.ds(j, tm, stride=ROW_TILES), :])
        lo1, hi1 = _unpack_tile(yp_ref[1, pl.ds(j, tm, stride=ROW_TILES), :])
        lows.append(w0 * lo0 + w1 * lo1)
        highs.append(w0 * hi0 + w1 * hi1)
    o_ref[...] = x_ref[...] + gt_ref[...] * jnp.concatenate(lows + highs, axis=1)


def _combine(x1, gt, w_pick, yp, tm):
    b, l, d = x1.shape
    tm = min(tm, l)
    nb = l // tm
    return pl.pallas_call(
        _combine_kernel,
        grid=(b, nb),
        in_specs=[pl.BlockSpec((None, tm, d), lambda bi, i: (bi, i, 0)),
                  pl.BlockSpec((None, 1, d), lambda bi, i: (bi, 0, 0)),
                  pl.BlockSpec((tm, ROUTER_W), lambda bi, i: (bi * nb + i, 0)),
                  pl.BlockSpec((2, tm * ROW_TILES, LANE), lambda bi, i: (0, bi * nb + i, 0))],
        out_specs=pl.BlockSpec((None, tm, d), lambda bi, i: (bi, i, 0)),
        out_shape=jax.ShapeDtypeStruct((b, l, d), F32),
        compiler_params=_cparams(("parallel", "parallel")),
        name="moe_combine",
    )(x1, gt, w_pick, yp)


def _dispatch_plan(eid):
    blk = MOE_BLOCK
    n_assign = eid.shape[0]
    n_tok = n_assign // 2
    n_blocks = (n_assign + blk - 1) // blk + N_EXPERTS
    order = jnp.argsort(eid).astype(jnp.int32)
    e_ids = jnp.arange(N_EXPERTS, dtype=jnp.int32)
    counts = jnp.sum((eid[None, :] == e_ids[:, None]).astype(jnp.int32), axis=1)
    start = jnp.cumsum(counts) - counts
    nblk = (counts + blk - 1) // blk
    bend = jnp.cumsum(nblk)
    b_ids = jnp.arange(n_blocks, dtype=jnp.int32)
    block_e = jnp.minimum(jnp.searchsorted(bend, b_ids, side='right'), N_EXPERTS - 1).astype(jnp.int32)
    used = b_ids < bend[-1]
    within = b_ids - (bend - nblk)[block_e]
    block_nv = jnp.where(used, jnp.clip(counts[block_e] - within * blk, 0, blk), 0).astype(jnp.int32)
    block_j0 = jnp.where(used, start[block_e] + within * blk, 0).astype(jnp.int32)
    first = jnp.concatenate([jnp.ones((1,), jnp.int32),
                             (block_e[1:] != block_e[:-1]).astype(jnp.int32)])
    w_slot = ((jnp.cumsum(first) - 1) % 2).astype(jnp.int32)
    nonempty = jnp.where(counts > 0, e_ids, N_EXPERTS)
    later = jnp.flip(lax.cummin(jnp.flip(nonempty)))
    next_e = jnp.concatenate([later[1:], jnp.full((1,), N_EXPERTS, jnp.int32)])
    next_e = jnp.where(next_e >= N_EXPERTS, -1, next_e)[block_e]
    order = jnp.concatenate([order, jnp.zeros((blk,), jnp.int32)])
    tok_sorted = order // 2
    dst_sorted = (order % 2) * (n_tok + blk) + order // 2
    return (block_e, block_nv, first, next_e, w_slot, block_j0, tok_sorted, dst_sorted)


def _rope_tables(length, with_rope):
    half = MLA_ROPE // 4
    if with_rope:
        rows = length // GRID_W
        row = jnp.repeat(jnp.arange(rows), GRID_W).astype(F32)
        col = jnp.tile(jnp.arange(GRID_W), rows).astype(F32)
        inv = 1.0 / (ROPE_THETA ** (jnp.arange(half, dtype=F32) / half))
        ang_r = row[:, None] * inv
        ang_c = col[:, None] * inv
        cos = jnp.concatenate([jnp.cos(ang_r), jnp.cos(ang_r), jnp.cos(ang_c), jnp.cos(ang_c)], axis=1)
        sin = jnp.concatenate([-jnp.sin(ang_r), jnp.sin(ang_r), -jnp.sin(ang_c), jnp.sin(ang_c)], axis=1)
    else:
        cos = jnp.ones((length, MLA_ROPE), F32)
        sin = jnp.zeros((length, MLA_ROPE), F32)
    pad = ((0, 0), (0, LANE - MLA_ROPE))
    return jnp.pad(cos, pad), jnp.pad(sin, pad)


_IN_SIZES = (GLA_QK_W, GLA_QK_W, GLA_V_W, GLA_V_W, GLA_LOWRANK, GLA_LOWRANK,
             MLA_Q_RANK, MLA_KV_RANK, MLA_ROPE, D_MODEL, D_MODEL)
(SRC_Q, SRC_K, SRC_V, SRC_R, SRC_AF, SRC_AB, SRC_CQ, SRC_CKV, SRC_KR, SRC_GA, SRC_GB,
 SRC_END) = [sum(_IN_SIZES[:i]) for i in range(len(_IN_SIZES) + 1)]


def _w_in_layout_kernel(w_ref, lat_ref, ctx_ref):
    def rows(a, b):
        return w_ref[a:b, :].astype(BF16)

    small_pad = jnp.zeros((SMALL_W - MLA_ROPE - 2 * GLA_LOWRANK, w_ref.shape[1]), BF16)
    lat_ref[COL_Q:COL_GA, :] = rows(SRC_Q, SRC_AF)
    lat_ref[COL_GA:COL_CQ, :] = rows(SRC_GA, SRC_END)
    lat_ref[COL_CQ:COL_SMALL, :] = rows(SRC_CQ, SRC_KR)
    ctx_ref[CCOL_K:CCOL_CKV, :] = rows(SRC_K, SRC_R)
    ctx_ref[CCOL_CKV:CCOL_SMALL, :] = rows(SRC_CKV, SRC_KR)
    for ref, off in ((lat_ref, COL_SMALL), (ctx_ref, CCOL_SMALL)):
        ref[off:off + MLA_ROPE, :] = rows(SRC_KR, SRC_GA)
        ref[off + MLA_ROPE:off + SM_AB + GLA_LOWRANK, :] = rows(SRC_AF, SRC_CQ)
        ref[off + SM_AB + GLA_LOWRANK:off + SMALL_W, :] = small_pad


def _w_in_layout(w_in, tk=256):
    wt = jnp.swapaxes(w_in, 1, 2)
    _, n, d = wt.shape
    assert n == SRC_END
    return pl.pallas_call(
        _w_in_layout_kernel,
        grid=(d // tk,),
        in_specs=[pl.BlockSpec((None, n, tk), lambda i: (0, 0, i))],
        out_specs=[pl.BlockSpec((NP_LAT, tk), lambda i: (0, i)),
                   pl.BlockSpec((NP_CTX, tk), lambda i: (0, i))],
        out_shape=[jax.ShapeDtypeStruct((NP_LAT, d), BF16),
                   jax.ShapeDtypeStruct((NP_CTX, d), BF16)],
        compiler_params=_cparams(("parallel",)),
        name="w_in_layout",
    )(wt)


def kernel(x, c, ctx, c_ctx, w_mod, b_mod, norm1_g, norm2_g, w_in, w_decay_f, b_decay_f, w_decay_b, b_decay_b, gla_norm_g, q_a_norm_g, w_uq, kv_a_norm_g, w_ukv, q_norm_g, k_norm_g, w_o_gla, w_o_mla, w_out, w_router_group, b_router_group, w_router_expert, b_router_expert, w_exp_gate, w_exp_up, w_exp_down):
    assert w_mod.shape[0] == 1, "single-layer block"
    b, l, d = x.shape
    lc = ctx.shape[1]
    t = b * l

    c_rows = jnp.concatenate([c, c_ctx[None, :], jnp.zeros((8 - b - 1, d), F32)], axis=0)
    mod = _modulation(c_rows, w_mod[0], b_mod[0])
    sh1, sc1, gt1, sh2, sc2, gt2 = [mod[:b, i * d:(i + 1) * d].reshape(b, 1, d) for i in range(6)]
    sh1c = jnp.broadcast_to(mod[b:b + 1, 0:d].reshape(1, 1, d), (b, 1, d))
    sc1c = jnp.broadcast_to(mod[b:b + 1, d:2 * d].reshape(1, 1, d), (b, 1, d))

    w_lat, w_ctx = _w_in_layout(w_in)
    g1 = norm1_g[0].reshape(1, d)
    p = _in_projection(x, g1, sc1, sh1, w_lat, tm=INPROJ_TM)
    pc = _in_projection(ctx, g1, sc1c, sh1c, w_ctx, tm=lc)

    def decay_w(w, off):
        return jnp.zeros((LANE, GLA_QK_W), F32).at[off:off + GLA_LOWRANK].set(w).astype(BF16)

    y_gla = _gla(p, pc,
                 decay_w(w_decay_f[0], SM_AF), b_decay_f[0].reshape(1, -1),
                 decay_w(w_decay_b[0], SM_AB), b_decay_b[0].reshape(1, -1),
                 gla_norm_g[0].reshape(1, -1))

    wq = jnp.pad(w_uq[0].reshape(MLA_Q_RANK, MLA_HEADS, MLA_QK_DIM),
                 ((0, 0), (0, 0), (0, MLA_QK_PAD - MLA_QK_DIM))).reshape(MLA_Q_RANK, -1).astype(BF16)
    wkv = w_ukv[0].astype(BF16)
    qn_g = jnp.pad(q_norm_g[0], (0, MLA_QK_PAD - MLA_QK_DIM)).reshape(1, -1)
    kn_nope = k_norm_g[0][:MLA_NOPE].reshape(1, -1)
    kn_rope = jnp.pad(k_norm_g[0][MLA_NOPE:], (0, LANE - MLA_ROPE)).reshape(1, -1)
    cos, sin = _rope_tables(l, True)
    cos_c, sin_c = _rope_tables(lc, False)
    q_m = _mla_queries(p, q_a_norm_g[0].reshape(1, -1), wq, qn_g, cos, sin, tm=MLA_Q_TM)
    kva = kv_a_norm_g[0].reshape(1, -1)
    k_m, v_m = _mla_keys_values(p, pc, kva, wkv, kn_nope, kn_rope,
                                jnp.concatenate([cos, cos_c]), jnp.concatenate([sin, sin_c]))
    y_mla = _attention(q_m, k_m, v_m, tq=ATTN_TQ)

    y = _branch_merge(y_gla, y_mla, w_o_gla[0].astype(BF16), w_o_mla[0].astype(BF16), p,
                      tm=MERGE_TM, tn=MERGE_TN)
    w_router = jnp.concatenate(
        [w_router_group[0], w_router_expert[0],
         jnp.zeros((d, ROUTER_W - N_GROUPS - N_EXPERTS), F32)], axis=1).astype(BF16)
    b_router = jnp.concatenate(
        [b_router_group[0], b_router_expert[0],
         jnp.zeros((ROUTER_W - N_GROUPS - N_EXPERTS,), F32)]).reshape(1, -1)
    x1, h2, logits = _out_projection(y, w_out[0].astype(BF16), x, gt1, norm2_g[0].reshape(1, d),
                                     sc2, sh2, w_router, b_router, tm=OUTPROJ_TM)

    e_pick, w_pick = _route(logits.reshape(t, ROUTER_W), tm=ROUTE_TM)
    plan = _dispatch_plan(e_pick[:, :2].reshape(-1))
    yp = _experts(plan, h2.reshape(t * ROW_TILES, LANE),
                  w_exp_gate[0], w_exp_up[0], w_exp_down[0])
    return _combine(x1, gt2, w_pick, yp, tm=COMBINE_TM)
```

```python
import functools
import math

import jax
import jax.numpy as jnp
from jax import lax
from jax.experimental import pallas as pl
from jax.experimental.pallas import tpu as pltpu

F32 = jnp.float32
BF16 = jnp.bfloat16

D_MODEL = 2048
GRID_W = 64
EPS = 1e-6

GLA_HEADS = 4
GLA_DK = 256
GLA_DV = 512
GLA_LOWRANK = 16
GLA_TAU = 16.0
GLA_CHUNK = 64
GLA_GROUP = 8
GLA_QK_W = GLA_HEADS * GLA_DK
GLA_V_W = GLA_HEADS * GLA_DV

MLA_HEADS = 16
MLA_Q_RANK = 512
MLA_KV_RANK = 512
MLA_NOPE = 128
MLA_ROPE = 64
MLA_V = 128
MLA_QK_DIM = MLA_NOPE + MLA_ROPE
MLA_QK_PAD = 256
BF16_TILE_ROWS = 16
MLA_VT_ROWS = MLA_V + BF16_TILE_ROWS
ROPE_THETA = 10000.0
LOG2E = math.log2(math.e)

N_GROUPS = 8
EXP_PER_GROUP = 8
N_EXPERTS = N_GROUPS * EXP_PER_GROUP
D_EXPERT = 512
ROUTER_W = 128
MOE_BLOCK = 256
BULK_DMA = 1
SCATTER_BULK_EVERY = 16

LANE = 128
ROW_TILES = D_MODEL // (2 * LANE)
ROW_PITCH = 12
VMEM_LIMIT = 56 * 1024 * 1024

COL_Q = 0
COL_K = COL_Q + GLA_QK_W
COL_V = COL_K + GLA_QK_W
COL_R = COL_V + GLA_V_W
COL_GA = COL_R + GLA_V_W
COL_GB = COL_GA + D_MODEL
COL_CQ = COL_GB + D_MODEL
COL_CKV = COL_CQ + MLA_Q_RANK
COL_SMALL = COL_CKV + MLA_KV_RANK
SMALL_W = 256
NP_LAT = COL_SMALL + SMALL_W
SM_AF = MLA_ROPE
SM_AB = MLA_ROPE + GLA_LOWRANK
CCOL_K = 0
CCOL_V = CCOL_K + GLA_QK_W
CCOL_CKV = CCOL_V + GLA_V_W
CCOL_SMALL = CCOL_CKV + MLA_KV_RANK
NP_CTX = CCOL_SMALL + SMALL_W

INPROJ_TM, INPROJ_TN = 1024, 1280
INPROJ_NORM_ROWS = 128
MLA_Q_TM = 512
ATTN_TQ = 512
MERGE_TM, MERGE_TN = 1024, 512
OUTPROJ_TM = 256
ROUTE_TM = 1024
COMBINE_TM = 512


def _cparams(sem):
    return pltpu.CompilerParams(dimension_semantics=sem, vmem_limit_bytes=VMEM_LIMIT)


def _silu(x):
    return x * jax.nn.sigmoid(x)


def _pack_rows(x):
    half = x.shape[1] // 2
    lo = lax.bitcast_convert_type(x[:, :half].astype(BF16).astype(F32), jnp.uint32) >> 16
    hi = lax.bitcast_convert_type(x[:, half:].astype(BF16).astype(F32), jnp.uint32) & jnp.uint32(0xFFFF0000)
    packed = lo | hi
    return [packed[:, j * LANE:(j + 1) * LANE] for j in range(half // LANE)]


def _unpack_tile(t):
    lo = lax.bitcast_convert_type(t << 16, F32)
    hi = lax.bitcast_convert_type(t & jnp.uint32(0xFFFF0000), F32)
    return lo, hi


def _mod_kernel(c_ref, w_ref, b_ref, o_ref):
    a = _silu(c_ref[...]).astype(BF16)
    o_ref[...] = jnp.dot(a, w_ref[...].astype(BF16), preferred_element_type=F32) + b_ref[...]


def _modulation(c_rows, w_mod, b_mod):
    m, d = c_rows.shape
    n = w_mod.shape[1]
    tn = 1024
    return pl.pallas_call(
        _mod_kernel,
        grid=(n // tn,),
        in_specs=[pl.BlockSpec((m, d), lambda j: (0, 0)),
                  pl.BlockSpec((d, tn), lambda j: (0, j)),
                  pl.BlockSpec((1, tn), lambda j: (0, j))],
        out_specs=pl.BlockSpec((m, tn), lambda j: (0, j)),
        out_shape=jax.ShapeDtypeStruct((m, n), F32),
        compiler_params=_cparams(("parallel",)),
        name="modulation",
    )(c_rows, w_mod, b_mod.reshape(1, n))


def _inproj_kernel(x_ref, g_ref, sc_ref, sh_ref, w_ref, o_ref, h_ref):
    @pl.when(pl.program_id(2) == 0)
    def _():
        rows = min(INPROJ_NORM_ROWS, x_ref.shape[0])

        def norm_rows(c, carry):
            r0 = pl.multiple_of(c * rows, rows)
            x = x_ref[pl.ds(r0, rows), :]
            ms = jnp.mean(x * x, axis=-1, keepdims=True)
            y = x * lax.rsqrt(ms + EPS) * g_ref[...]
            h_ref[pl.ds(r0, rows), :] = (y * (1.0 + sc_ref[...]) + sh_ref[...]).astype(BF16)
            return carry

        lax.fori_loop(0, x_ref.shape[0] // rows, norm_rows, 0)

    o_ref[...] = lax.dot_general(h_ref[...], w_ref[...], (((1,), (1,)), ((), ())),
                                 preferred_element_type=F32).astype(o_ref.dtype)


def _in_projection(x, g, sc, sh, w, tm):
    b, l, d = x.shape
    tm = min(tm, l)
    n = w.shape[0]
    tn = INPROJ_TN
    return pl.pallas_call(
        _inproj_kernel,
        grid=(b, l // tm, n // tn),
        in_specs=[pl.BlockSpec((None, tm, d), lambda bi, i, j: (bi, i, 0)),
                  pl.BlockSpec((1, d), lambda bi, i, j: (0, 0)),
                  pl.BlockSpec((None, 1, d), lambda bi, i, j: (bi, 0, 0)),
                  pl.BlockSpec((None, 1, d), lambda bi, i, j: (bi, 0, 0)),
                  pl.BlockSpec((tn, d), lambda bi, i, j: (j, 0))],
        out_specs=pl.BlockSpec((None, tm, tn), lambda bi, i, j: (bi, i, j)),
        out_shape=jax.ShapeDtypeStruct((b, l, n), BF16),
        scratch_shapes=[pltpu.VMEM((tm, d), BF16)],
        compiler_params=_cparams(("parallel", "parallel", "arbitrary")),
        name="in_projection",
    )(x, g, sc, sh, w)


def _gla_kernel(q_ref, k_ref, v_ref, r_ref, sm_ref, kc_ref, vc_ref, smc_ref,
                wf_ref, bf_ref, wb_ref, bb_ref, gn_ref, y_ref, o_ref, sf_ref, sb_ref):
    c = GLA_CHUNK
    n_lat = q_ref.shape[0] // c
    n_ctx = kc_ref.shape[0] // c
    row = lax.broadcasted_iota(jnp.int32, (c, c), 0)
    col = lax.broadcasted_iota(jnp.int32, (c, c), 1)
    lower = row >= col
    upper = row <= col
    nt = (((1,), (1,)), ((), ()))
    tn = (((0,), (0,)), ((), ()))

    def cum_decay(sm, w_ref, b_ref, tri, n_chunks):
        z = jnp.dot(sm, w_ref[...], preferred_element_type=F32) + b_ref[...]
        g = (jnp.minimum(z, 0.0) - jnp.log(1.0 + jnp.exp(-jnp.abs(z)))) * (1.0 / GLA_TAU)
        g_hi = g.astype(BF16)
        g_lo = (g - g_hi.astype(F32)).astype(BF16)
        t = jnp.where(tri, 1.0, 0.0).astype(BF16)
        return [jnp.dot(t, g_hi[i * c:(i + 1) * c], preferred_element_type=F32)
                + jnp.dot(t, g_lo[i * c:(i + 1) * c], preferred_element_type=F32)
                for i in range(n_chunks)]

    fwd = (wf_ref, bf_ref, lower, c - 1, sf_ref)
    bwd = (wb_ref, bb_ref, upper, 0, sb_ref)

    def group(r0, n_chunks, direction, sm_r, k_r, v_r, q_r):
        w_ref, b_ref, tri, end_row, st_ref = direction
        rows = n_chunks * c
        k_all = k_r[pl.ds(r0, rows), :].astype(F32)
        v_all = v_r[pl.ds(r0, rows), :]
        q_all = None if q_r is None else q_r[pl.ds(r0, rows), :].astype(F32) * (GLA_DK ** -0.5)
        bc_all = cum_decay(sm_r[pl.ds(r0, rows), :], w_ref, b_ref, tri, n_chunks)
        local = []
        for i in range(n_chunks):
            sl = slice(i * c, (i + 1) * c)
            bc = bc_all[i]
            bend = bc[end_row:end_row + 1, :]
            k, v = k_all[sl], v_all[sl]
            ke = (k * jnp.exp(bend - bc)).astype(BF16)
            upd = lax.dot_general(v, ke, tn, preferred_element_type=F32)
            qd = o_intra = None
            if q_all is not None:
                qd = (q_all[sl] * jnp.exp(bc)).astype(BF16)
                ki = (k * jnp.exp(-bc)).astype(BF16)
                att = lax.dot_general(qd, ki, nt, preferred_element_type=F32)
                att = jnp.where(tri, att, 0.0).astype(BF16)
                o_intra = jnp.dot(att, v, preferred_element_type=F32)
            local.append((jnp.exp(bend), upd, qd, o_intra))
        st = st_ref[...]
        outs = [None] * n_chunks
        for i in (range(n_chunks) if end_row else reversed(range(n_chunks))):
            decay, upd, qd, o_intra = local[i]
            if q_all is not None:
                outs[i] = o_intra + lax.dot_general(qd, st.astype(BF16), nt, preferred_element_type=F32)
            st = st * decay + upd
        st_ref[...] = st
        return None if q_all is None else jnp.concatenate(outs, axis=0)

    def readout(r0, o):
        ms = jnp.mean(o * o, axis=-1, keepdims=True)
        yn = o * lax.rsqrt(ms + EPS) * gn_ref[...]
        r = r_ref[pl.ds(r0, o.shape[0]), :].astype(F32)
        y_ref[pl.ds(r0, o.shape[0]), :] = (yn * _silu(r)).astype(y_ref.dtype)

    sf_ref[...] = jnp.zeros_like(sf_ref)
    sb_ref[...] = jnp.zeros_like(sb_ref)
    group(0, n_ctx, fwd, smc_ref, kc_ref, vc_ref, None)
    group(0, n_ctx, bwd, smc_ref, kc_ref, vc_ref, None)

    g_lat = min(GLA_GROUP, n_lat // 2)
    assert n_lat % (2 * g_lat) == 0
    n_groups = n_lat // g_lat
    rows = g_lat * c

    def scan(i, direction):
        r0 = pl.multiple_of(i * rows, rows)
        return r0, group(r0, g_lat, direction, sm_ref, k_ref, v_ref, q_ref)

    def first_half(i, carry):
        for r0, o in (scan(i, fwd), scan(n_groups - 1 - i, bwd)):
            o_ref[pl.ds(r0, rows), :] = o
        return carry

    lax.fori_loop(0, n_groups // 2, first_half, 0)

    def second_half(i, carry):
        for r0, o in (scan(i, fwd), scan(n_groups - 1 - i, bwd)):
            readout(r0, o_ref[pl.ds(r0, rows), :] + o)
        return carry

    lax.fori_loop(n_groups // 2, n_groups, second_half, 0)


def _gla(p, pc, wf, bf, wb, bb, gn):
    b, l, _ = p.shape
    lc = pc.shape[1]
    dk, dv = GLA_DK, GLA_DV
    hmap = lambda off: (lambda bi, h: (bi, 0, off + h))
    wmap = lambda bi, h: (0, h)
    return pl.pallas_call(
        _gla_kernel,
        grid=(b, GLA_HEADS),
        in_specs=[pl.BlockSpec((None, l, dk), hmap(COL_Q // dk)),
                  pl.BlockSpec((None, l, dk), hmap(COL_K // dk)),
                  pl.BlockSpec((None, l, dv), hmap(COL_V // dv)),
                  pl.BlockSpec((None, l, dv), hmap(COL_R // dv)),
                  pl.BlockSpec((None, l, LANE), lambda bi, h: (bi, 0, COL_SMALL // LANE)),
                  pl.BlockSpec((None, lc, dk), hmap(CCOL_K // dk)),
                  pl.BlockSpec((None, lc, dv), hmap(CCOL_V // dv)),
                  pl.BlockSpec((None, lc, LANE), lambda bi, h: (bi, 0, CCOL_SMALL // LANE)),
                  pl.BlockSpec((LANE, dk), wmap),
                  pl.BlockSpec((1, dk), wmap),
                  pl.BlockSpec((LANE, dk), wmap),
                  pl.BlockSpec((1, dk), wmap),
                  pl.BlockSpec((1, dv), lambda bi, h: (0, 0))],
        out_specs=pl.BlockSpec((None, l, dv), lambda bi, h: (bi, 0, h)),
        out_shape=jax.ShapeDtypeStruct((b, l, GLA_V_W), BF16),
        scratch_shapes=[pltpu.VMEM((l, dv), F32), pltpu.VMEM((dv, dk), F32), pltpu.VMEM((dv, dk), F32)],
        compiler_params=_cparams(("parallel", "parallel")),
        name="gla",
    )(p, p, p, p, p, pc, pc, pc, wf, bf, wb, bb, gn)


def _rope_partner(t):
    lane = lax.broadcasted_iota(jnp.int32, t.shape, 1)
    first = (lane % 32) < 16
    return jnp.where(first, pltpu.roll(t, LANE - 16, 1), pltpu.roll(t, 16, 1))


def _mla_q_kernel(cq_ref, ga_ref, w_ref, gn_ref, cos_ref, sin_ref, o_ref):
    cq = cq_ref[...].astype(F32)
    cn = (cq * lax.rsqrt(jnp.mean(cq * cq, axis=-1, keepdims=True) + EPS) * ga_ref[...]).astype(BF16)
    cos = cos_ref[...]
    sin = sin_ref[...]
    scale = MLA_QK_DIM ** -0.5 * LOG2E
    for h in range(MLA_HEADS):
        qh = jnp.dot(cn, w_ref[:, h * MLA_QK_PAD:(h + 1) * MLA_QK_PAD], preferred_element_type=F32)
        ms = jnp.sum(qh * qh, axis=-1, keepdims=True) * (1.0 / MLA_QK_DIM)
        qn = qh * lax.rsqrt(ms + EPS) * gn_ref[...]
        t = qn[:, MLA_NOPE:]
        rot = t * cos + _rope_partner(t) * sin
        o_ref[h, :, :MLA_NOPE] = (qn[:, :MLA_NOPE] * scale).astype(o_ref.dtype)
        o_ref[h, :, MLA_NOPE:] = (rot * scale).astype(o_ref.dtype)


def _mla_queries(p, ga, wq, gn, cos, sin, tm):
    b, l, _ = p.shape
    tm = min(tm, l)
    return pl.pallas_call(
        _mla_q_kernel,
        grid=(b, l // tm),
        in_specs=[pl.BlockSpec((None, tm, MLA_Q_RANK), lambda bi, i: (bi, i, COL_CQ // MLA_Q_RANK)),
                  pl.BlockSpec((1, MLA_Q_RANK), lambda bi, i: (0, 0)),
                  pl.BlockSpec(wq.shape, lambda bi, i: (0, 0)),
                  pl.BlockSpec((1, MLA_QK_PAD), lambda bi, i: (0, 0)),
                  pl.BlockSpec((tm, LANE), lambda bi, i: (i, 0)),
                  pl.BlockSpec((tm, LANE), lambda bi, i: (i, 0))],
        out_specs=pl.BlockSpec((None, MLA_HEADS, tm, MLA_QK_PAD), lambda bi, i: (bi, 0, i, 0)),
        out_shape=jax.ShapeDtypeStruct((b, MLA_HEADS, l, MLA_QK_PAD), BF16),
        compiler_params=_cparams(("parallel", "parallel")),
        name="mla_queries",
    )(p, ga, wq, gn, cos, sin)


def _mla_kv_kernel(ckv_ref, sm_ref, ckvc_ref, smc_ref, ga_ref, w_ref, gk_ref, gr_ref,
                   cos_ref, sin_ref, k_ref, v_ref, *, n_lat):
    is_ctx = pl.program_id(1) >= n_lat
    ckv = jnp.where(is_ctx, ckvc_ref[...], ckv_ref[...]).astype(F32)
    cn = (ckv * lax.rsqrt(jnp.mean(ckv * ckv, axis=-1, keepdims=True) + EPS) * ga_ref[...]).astype(BF16)
    sm = jnp.where(is_ctx, smc_ref[...], sm_ref[...]).astype(F32)
    lane = lax.broadcasted_iota(jnp.int32, sm.shape, 1)
    kr = jnp.where(lane < MLA_ROPE, sm, 0.0)
    ss_r = jnp.sum(kr * kr, axis=-1, keepdims=True)
    krg = kr * gr_ref[...]
    rot = krg * cos_ref[...] + _rope_partner(krg) * sin_ref[...]
    hw = MLA_NOPE + MLA_V
    tm = ckv.shape[0]
    sub = lax.broadcasted_iota(jnp.int32, (MLA_VT_ROWS - MLA_V, tm), 0)
    ones_rows = jnp.where(sub == 0, 1.0, 0.0).astype(v_ref.dtype)
    for h in range(MLA_HEADS):
        kvh = jnp.dot(cn, w_ref[:, h * hw:(h + 1) * hw], preferred_element_type=F32)
        kn = kvh[:, :MLA_NOPE]
        ms = (jnp.sum(kn * kn, axis=-1, keepdims=True) + ss_r) * (1.0 / MLA_QK_DIM)
        rs = lax.rsqrt(ms + EPS)
        k_ref[h, :, :MLA_NOPE] = (kn * rs * gk_ref[...]).astype(k_ref.dtype)
        k_ref[h, :, MLA_NOPE:] = (rot * rs).astype(k_ref.dtype)
        v_ref[h, :MLA_V, :] = kvh[:, MLA_NOPE:].T.astype(v_ref.dtype)
        v_ref[h, MLA_V:, :] = ones_rows


def _mla_keys_values(p, pc, ga, wkv, gk, gr, cos, sin):
    b, l, _ = p.shape
    lc = pc.shape[1]
    tm = min(lc, l)
    assert l % tm == 0 and lc % tm == 0
    n_lat, n_ctx = l // tm, lc // tm
    lat = lambda blk: (lambda bi, i: (bi, jnp.minimum(i, n_lat - 1), blk))
    ctx = lambda blk: (lambda bi, i: (bi, jnp.maximum(i - n_lat, 0), blk))
    const = lambda bi, i: (0, 0)
    k_spec = pl.BlockSpec((None, MLA_HEADS, tm, MLA_QK_PAD), lambda bi, i: (bi, 0, i, 0))
    vt_spec = pl.BlockSpec((None, MLA_HEADS, MLA_VT_ROWS, tm), lambda bi, i: (bi, 0, 0, i))
    k_shape = jax.ShapeDtypeStruct((b, MLA_HEADS, l + lc, MLA_QK_PAD), BF16)
    vt_shape = jax.ShapeDtypeStruct((b, MLA_HEADS, MLA_VT_ROWS, l + lc), BF16)
    return pl.pallas_call(
        functools.partial(_mla_kv_kernel, n_lat=n_lat),
        grid=(b, n_lat + n_ctx),
        in_specs=[pl.BlockSpec((None, tm, MLA_KV_RANK), lat(COL_CKV // MLA_KV_RANK)),
                  pl.BlockSpec((None, tm, LANE), lat(COL_SMALL // LANE)),
                  pl.BlockSpec((None, tm, MLA_KV_RANK), ctx(CCOL_CKV // MLA_KV_RANK)),
                  pl.BlockSpec((None, tm, LANE), ctx(CCOL_SMALL // LANE)),
                  pl.BlockSpec((1, MLA_KV_RANK), const),
                  pl.BlockSpec(wkv.shape, const),
                  pl.BlockSpec((1, LANE), const),
                  pl.BlockSpec((1, LANE), const),
                  pl.BlockSpec((tm, LANE), lambda bi, i: (i, 0)),
                  pl.BlockSpec((tm, LANE), lambda bi, i: (i, 0))],
        out_specs=[k_spec, vt_spec],
        out_shape=[k_shape, vt_shape],
        compiler_params=_cparams(("parallel", "arbitrary")),
        name="mla_keys_values",
    )(p, p, pc, pc, ga, wkv, gk, gr, cos, sin)


def _attn_kernel(q_ref, k_ref, vt_ref, o_ref, s0_ref, s1_ref, m0_ref, m1_ref, *, tq):
    nt = (((1,), (1,)), ((), ()))
    n = q_ref.shape[0] // tq
    bufs = ((s0_ref, m0_ref), (s1_ref, m1_ref))

    def scores(i, slot):
        s_ref, m_ref = bufs[slot]
        r0 = pl.multiple_of(i * tq, tq)
        s = lax.dot_general(k_ref[...], q_ref[pl.ds(r0, tq), :], nt, preferred_element_type=F32)
        s_ref[...] = s
        m_ref[...] = jnp.max(s, axis=0, keepdims=True)

    def outputs(i, slot):
        s_ref, m_ref = bufs[slot]
        r0 = pl.multiple_of(i * tq, tq)
        p = jnp.exp2(s_ref[...] - m_ref[...]).astype(BF16)
        o = jnp.dot(vt_ref[...], p, preferred_element_type=F32)
        o_ref[pl.ds(r0, tq), :] = (o[:MLA_V] / o[MLA_V:MLA_V + 1]).T.astype(o_ref.dtype)

    scores(0, 0)
    if n % 2 == 0:
        def body(j, carry):
            scores(2 * j + 1, 1)
            outputs(2 * j, 0)
            scores(2 * j + 2, 0)
            outputs(2 * j + 1, 1)
            return carry

        lax.fori_loop(0, n // 2 - 1, body, 0)
        scores(n - 1, 1)
        outputs(n - 2, 0)
        outputs(n - 1, 1)
    else:
        assert n == 1
        outputs(0, 0)


def _attention(q, k, v, tq):
    b, h, l, dq = q.shape
    tq = min(tq, l)
    lk = k.shape[2]
    bh = lambda bi, hi: (bi, hi, 0, 0)
    return pl.pallas_call(
        functools.partial(_attn_kernel, tq=tq),
        grid=(b, h),
        in_specs=[pl.BlockSpec((None, None, l, dq), bh),
                  pl.BlockSpec((None, None, lk, dq), bh),
                  pl.BlockSpec((None, None, MLA_VT_ROWS, lk), bh)],
        out_specs=pl.BlockSpec((None, l, MLA_V), lambda bi, hi: (bi, 0, hi)),
        out_shape=jax.ShapeDtypeStruct((b, l, h * MLA_V), BF16),
        scratch_shapes=[pltpu.VMEM((lk, tq), F32), pltpu.VMEM((lk, tq), F32),
                        pltpu.VMEM((1, tq), F32), pltpu.VMEM((1, tq), F32)],
        compiler_params=_cparams(("parallel", "parallel")),
        name="mla_attention",
    )(q, k, v)


def _merge_kernel(yg_ref, ym_ref, wg_ref, wm_ref, ga_ref, gb_ref, o_ref):
    a = jnp.dot(yg_ref[...], wg_ref[...], preferred_element_type=F32)
    m = jnp.dot(ym_ref[...], wm_ref[...], preferred_element_type=F32)
    y = (jax.nn.sigmoid(ga_ref[...].astype(F32)) * a
         + jax.nn.sigmoid(gb_ref[...].astype(F32)) * m)
    o_ref[...] = y.astype(o_ref.dtype)


def _branch_merge(yg, ym, wg, wm, p, tm, tn):
    b, l, d = yg.shape
    tm = min(tm, l)
    n = wg.shape[1]
    return pl.pallas_call(
        _merge_kernel,
        grid=(b, l // tm, n // tn),
        in_specs=[pl.BlockSpec((None, tm, d), lambda bi, i, j: (bi, i, 0)),
                  pl.BlockSpec((None, tm, d), lambda bi, i, j: (bi, i, 0)),
                  pl.BlockSpec((d, tn), lambda bi, i, j: (0, j)),
                  pl.BlockSpec((d, tn), lambda bi, i, j: (0, j)),
                  pl.BlockSpec((None, tm, tn), lambda bi, i, j: (bi, i, COL_GA // tn + j)),
                  pl.BlockSpec((None, tm, tn), lambda bi, i, j: (bi, i, COL_GB // tn + j))],
        out_specs=pl.BlockSpec((None, tm, tn), lambda bi, i, j: (bi, i, j)),
        out_shape=jax.ShapeDtypeStruct((b, l, n), BF16),
        compiler_params=_cparams(("parallel", "parallel", "arbitrary")),
        name="branch_merge",
    )(yg, ym, wg, wm, p, p)


def _outproj_kernel(y_ref, w_ref, x_ref, gt_ref, g2_ref, sc_ref, sh_ref, wr_ref, br_ref,
                    x1_ref, h2_ref, lg_ref):
    x1 = x_ref[...] + gt_ref[...] * jnp.dot(y_ref[...], w_ref[...], preferred_element_type=F32)
    x1_ref[...] = x1
    ms = jnp.mean(x1 * x1, axis=-1, keepdims=True)
    h2 = x1 * lax.rsqrt(ms + EPS) * g2_ref[...] * (1.0 + sc_ref[...]) + sh_ref[...]
    tm = x1.shape[0]
    for j, tile in enumerate(_pack_rows(h2)):
        h2_ref[pl.ds(j, tm, stride=ROW_TILES), :] = tile
    lg_ref[...] = jnp.dot(h2.astype(BF16), wr_ref[...], preferred_element_type=F32) + br_ref[...]


def _out_projection(y, w, x, gt, g2, sc, sh, wr, br, tm):
    b, l, d = x.shape
    tm = min(tm, l)
    row = lambda bi, i: (bi, i, 0)
    per_b = lambda bi, i: (bi, 0, 0)
    const = lambda bi, i: (0, 0)
    return pl.pallas_call(
        _outproj_kernel,
        grid=(b, l // tm),
        in_specs=[pl.BlockSpec((None, tm, d), row),
                  pl.BlockSpec((d, d), const),
                  pl.BlockSpec((None, tm, d), row),
                  pl.BlockSpec((None, 1, d), per_b),
                  pl.BlockSpec((1, d), const),
                  pl.BlockSpec((None, 1, d), per_b),
                  pl.BlockSpec((None, 1, d), per_b),
                  pl.BlockSpec((d, ROUTER_W), const),
                  pl.BlockSpec((1, ROUTER_W), const)],
        out_specs=[pl.BlockSpec((None, tm, d), row),
                   pl.BlockSpec((None, tm * ROW_TILES, LANE), row),
                   pl.BlockSpec((None, tm, ROUTER_W), row)],
        out_shape=[jax.ShapeDtypeStruct((b, l, d), F32),
                   jax.ShapeDtypeStruct((b, l * ROW_TILES, LANE), jnp.uint32),
                   jax.ShapeDtypeStruct((b, l, ROUTER_W), F32)],
        compiler_params=_cparams(("parallel", "parallel")),
        name="out_projection",
    )(y, w, x, gt, g2, sc, sh, wr, br)


def _route_kernel(lg_ref, e_ref, w_ref):
    lg = lg_ref[...]
    lane = lax.broadcasted_iota(jnp.int32, lg.shape, 1)
    neg = jnp.float32(-jnp.inf)
    big = jnp.int32(1 << 20)
    is_g = lane < N_GROUPS
    gl = jnp.where(is_g, lg, neg)
    gm = jnp.max(gl, axis=-1, keepdims=True)
    gidx = jnp.min(jnp.where(gl == gm, lane, big), axis=-1, keepdims=True)
    gsum = jnp.sum(jnp.where(is_g, jnp.exp(gl - gm), 0.0), axis=-1, keepdims=True)
    p_g = 1.0 / gsum
    g_lo = N_GROUPS + gidx * EXP_PER_GROUP
    in_grp = (lane >= g_lo) & (lane < g_lo + EXP_PER_GROUP)
    el = jnp.where(in_grp, lg, neg)
    em = jnp.max(el, axis=-1, keepdims=True)
    z = jnp.sum(jnp.where(in_grp, jnp.exp(el - em), 0.0), axis=-1, keepdims=True)
    i1 = jnp.min(jnp.where(el == em, lane, big), axis=-1, keepdims=True)
    el2 = jnp.where(lane == i1, neg, el)
    em2 = jnp.max(el2, axis=-1, keepdims=True)
    i2 = jnp.min(jnp.where(el2 == em2, lane, big), axis=-1, keepdims=True)
    p1 = 1.0 / z
    p2 = jnp.exp(em2 - em) / z
    tot = p1 + p2
    e_ref[...] = jnp.where(lane == 0, i1 - N_GROUPS, jnp.where(lane == 1, i2 - N_GROUPS, 0))
    w_ref[...] = jnp.where(lane == 0, p_g * p1 / tot, jnp.where(lane == 1, p_g * p2 / tot, 0.0))


def _route(logits, tm):
    t = logits.shape[0]
    tm = min(tm, t)
    spec = pl.BlockSpec((tm, ROUTER_W), lambda i: (i, 0))
    return pl.pallas_call(
        _route_kernel,
        grid=(t // tm,),
        in_specs=[spec],
        out_specs=[spec, spec],
        out_shape=[jax.ShapeDtypeStruct((t, ROUTER_W), jnp.int32),
                   jax.ShapeDtypeStruct((t, ROUTER_W), F32)],
        compiler_params=_cparams(("parallel",)),
        name="moe_route",
    )(logits)


def _expert_kernel(be_ref, nv_ref, first_ref, nxe_ref, ws_ref, j0_ref, tok_ref, dst_ref,
                   h_hbm, wg_hbm, wu_hbm, wd_hbm, yp_hbm,
                   xbuf, ybuf, wgv, wuv, wdv, sem_in, sem_out, sem_w, sem_init):
    i = pl.program_id(0)
    n = pl.num_programs(0)
    blk = MOE_BLOCK
    rt, pitch = ROW_TILES, ROW_PITCH
    plane = yp_hbm.shape[0] // (2 * rt)
    t_rows = plane - blk
    slot_rows = blk * pitch
    par = i % 2
    cur = pl.multiple_of(par * slot_rows, 8)
    oth = pl.multiple_of((1 - par) * slot_rows, 8)

    def weight_copies(e, s):
        return (pltpu.make_async_copy(wg_hbm.at[e], wgv.at[s], sem_w.at[s, 0]),
                pltpu.make_async_copy(wu_hbm.at[e], wuv.at[s], sem_w.at[s, 1]),
                pltpu.make_async_copy(wd_hbm.at[e], wdv.at[s], sem_w.at[s, 2]))

    def gather_start(b, buf0, sem, r_lo=0, r_hi=blk):
        base = j0_ref[b]
        for r in range(r_lo, r_hi):
            row0 = pl.multiple_of(tok_ref[base + r] * rt, rt)
            pltpu.make_async_copy(h_hbm.at[pl.ds(row0, rt)],
                                  xbuf.at[pl.ds(buf0 + r * pitch, rt)], sem_in.at[sem]).start()

    def gather_wait(buf0, sem):
        pltpu.make_async_copy(h_hbm.at[pl.ds(0, blk * rt)], xbuf.at[pl.ds(buf0, blk * rt)],
                              sem_in.at[sem]).wait()

    def scatter_start(b, buf0, sem, r_lo=0, r_hi=blk):
        base = j0_ref[b]
        filled = nv_ref[b]
        for r in range(r_lo, r_hi):
            dst = jnp.where(r < filled, dst_ref[base + r], t_rows + r)
            row0 = pl.multiple_of(dst * rt, rt)
            pltpu.make_async_copy(ybuf.at[pl.ds(buf0 + r * pitch, rt)],
                                  yp_hbm.at[pl.ds(row0, rt)], sem_out.at[sem]).start(
                                      priority=BULK_DMA if r % SCATTER_BULK_EVERY == 0 else 0)

    def scatter_wait(buf0, sem):
        pltpu.make_async_copy(ybuf.at[pl.ds(buf0, blk * rt)], yp_hbm.at[pl.ds(0, blk * rt)],
                              sem_out.at[sem]).wait()

    @pl.when(i == 0)
    def _():
        for cp in weight_copies(be_ref[0], 0):
            cp.start(priority=BULK_DMA)
        gather_start(0, 0, 0)
        ybuf[...] = jnp.zeros_like(ybuf)
        zeros = ybuf.at[pl.ds(slot_rows, blk * rt)]
        pltpu.make_async_copy(zeros, yp_hbm.at[pl.ds(t_rows * rt, blk * rt)], sem_out.at[0]).start()
        spare = pltpu.make_async_copy(zeros, yp_hbm.at[pl.ds((plane + t_rows) * rt, blk * rt)], sem_init)
        spare.start()
        spare.wait()

    valid = nv_ref[i] > 0
    nxt = jnp.minimum(i + 1, n - 1)
    has_next = jnp.logical_and(i + 1 < n, nv_ref[nxt] > 0)

    @pl.when(jnp.logical_and(valid, first_ref[i] == 1))
    def _():
        ws = ws_ref[i]
        for cp in weight_copies(be_ref[i], ws):
            cp.wait()

        @pl.when(nxe_ref[i] >= 0)
        def _():
            for cp in weight_copies(nxe_ref[i], 1 - ws):
                cp.start(priority=BULK_DMA)

    @pl.when(valid)
    def _():
        ws = ws_ref[i]
        gather_wait(cur, par)
        scatter_wait(cur, par)
        prev = jnp.maximum(i - 1, 0)
        per = blk // (2 * rt)

        def read_tiles(first_chunk):
            halves = []
            for j in range(rt):
                lo = (first_chunk + j) * per
                gather_start(nxt, oth, 1 - par, lo, lo + per)
                halves.append(_unpack_tile(xbuf[pl.ds(cur + j, blk, stride=pitch), :]))
            return jnp.concatenate([h[0] for h in halves] + [h[1] for h in halves],
                                   axis=1).astype(BF16)

        scatter_start(prev, oth, 1 - par)
        g = jnp.dot(read_tiles(0), wgv[ws].astype(BF16), preferred_element_type=F32)
        u = jnp.dot(read_tiles(rt), wuv[ws].astype(BF16), preferred_element_type=F32)
        a = (_silu(g) * u).astype(BF16)
        y = jnp.dot(a, wdv[ws].astype(BF16), preferred_element_type=F32)
        for j, tile in enumerate(_pack_rows(y)):
            ybuf[pl.ds(cur + j, blk, stride=pitch), :] = tile

        @pl.when(jnp.logical_not(has_next))
        def _():
            scatter_wait(oth, 1 - par)
            scatter_start(i, cur, par)
            gather_wait(oth, 1 - par)
            scatter_wait(cur, par)


def _experts(plan, h2, w_gate, w_up, w_down):
    d = w_gate.shape[1]
    rt, pitch = ROW_TILES, ROW_PITCH
    assert d == 2 * rt * LANE
    t = h2.shape[0] // rt
    n_blocks = plan[0].shape[0]
    blk = MOE_BLOCK
    de = w_gate.shape[2]
    any_spec = pl.BlockSpec(memory_space=pl.ANY)
    return pl.pallas_call(
        _expert_kernel,
        grid_spec=pltpu.PrefetchScalarGridSpec(
            num_scalar_prefetch=len(plan),
            grid=(n_blocks,),
            in_specs=[any_spec, any_spec, any_spec, any_spec],
            out_specs=any_spec,
            scratch_shapes=[pltpu.VMEM((2 * blk * pitch, LANE), jnp.uint32),
                            pltpu.VMEM((2 * blk * pitch, LANE), jnp.uint32),
                            pltpu.VMEM((2, d, de), F32),
                            pltpu.VMEM((2, d, de), F32),
                            pltpu.VMEM((2, de, d), F32),
                            pltpu.SemaphoreType.DMA((2,)),
                            pltpu.SemaphoreType.DMA((2,)),
                            pltpu.SemaphoreType.DMA((2, 3)),
                            pltpu.SemaphoreType.DMA],
        ),
        out_shape=jax.ShapeDtypeStruct((2 * (t + blk) * rt, LANE), jnp.uint32),
        compiler_params=_cparams(("arbitrary",)),
        name="moe_experts",
    )(*plan, h2, w_gate, w_up, w_down).reshape(2, (t + blk) * rt, LANE)


def _combine_kernel(x_ref, gt_ref, w_ref, yp_ref, o_ref):
    tm = x_ref.shape[0]
    w = w_ref[...]
    w0, w1 = w[:, 0:1], w[:, 1:2]
    lows, highs = [], []
    for j in range(ROW_TILES):
        lo0, hi0 = _unpack_tile(yp_ref[0, pl.ds(j, tm, stride=ROW_TILES), :])
        lo1, hi1 = _unpack_tile(yp_ref[1, pl.ds(j, tm, stride=ROW_TILES), :])
        lows.append(w0 * lo0 + w1 * lo1)
        highs.append(w0 * hi0 + w1 * hi1)
    o_ref[...] = x_ref[...] + gt_ref[...] * jnp.concatenate(lows + highs, axis=1)


def _combine(x1, gt, w_pick, yp, tm):
    b, l, d = x1.shape
    tm = min(tm, l)
    nb = l // tm
    return pl.pallas_call(
        _combine_kernel,
        grid=(b, nb),
        in_specs=[pl.BlockSpec((None, tm, d), lambda bi, i: (bi, i, 0)),
                  pl.BlockSpec((None, 1, d), lambda bi, i: (bi, 0, 0)),
                  pl.BlockSpec((tm, ROUTER_W), lambda bi, i: (bi * nb + i, 0)),
                  pl.BlockSpec((2, tm * ROW_TILES, LANE), lambda bi, i: (0, bi * nb + i, 0))],
        out_specs=pl.BlockSpec((None, tm, d), lambda bi, i: (bi, i, 0)),
        out_shape=jax.ShapeDtypeStruct((b, l, d), F32),
        compiler_params=_cparams(("parallel", "parallel")),
        name="moe_combine",
    )(x1, gt, w_pick, yp)


def _dispatch_plan(eid):
    blk = MOE_BLOCK
    n_assign = eid.shape[0]
    n_tok = n_assign // 2
    n_blocks = (n_assign + blk - 1) // blk + N_EXPERTS
    order = jnp.argsort(eid).astype(jnp.int32)
    e_ids = jnp.arange(N_EXPERTS, dtype=jnp.int32)
    counts = jnp.sum((eid[None, :] == e_ids[:, None]).astype(jnp.int32), axis=1)
    start = jnp.cumsum(counts) - counts
    nblk = (counts + blk - 1) // blk
    bend = jnp.cumsum(nblk)
    b_ids = jnp.arange(n_blocks, dtype=jnp.int32)
    block_e = jnp.minimum(jnp.searchsorted(bend, b_ids, side='right'), N_EXPERTS - 1).astype(jnp.int32)
    used = b_ids < bend[-1]
    within = b_ids - (bend - nblk)[block_e]
    block_nv = jnp.where(used, jnp.clip(counts[block_e] - within * blk, 0, blk), 0).astype(jnp.int32)
    block_j0 = jnp.where(used, start[block_e] + within * blk, 0).astype(jnp.int32)
    first = jnp.concatenate([jnp.ones((1,), jnp.int32),
                             (block_e[1:] != block_e[:-1]).astype(jnp.int32)])
    w_slot = ((jnp.cumsum(first) - 1) % 2).astype(jnp.int32)
    nonempty = jnp.where(counts > 0, e_ids, N_EXPERTS)
    later = jnp.flip(lax.cummin(jnp.flip(nonempty)))
    next_e = jnp.concatenate([later[1:], jnp.full((1,), N_EXPERTS, jnp.int32)])
    next_e = jnp.where(next_e >= N_EXPERTS, -1, next_e)[block_e]
    order = jnp.concatenate([order, jnp.zeros((blk,), jnp.int32)])
    tok_sorted = order // 2
    dst_sorted = (order % 2) * (n_tok + blk) + order // 2
    return (block_e, block_nv, first, next_e, w_slot, block_j0, tok_sorted, dst_sorted)


def _rope_tables(length, with_rope):
    half = MLA_ROPE // 4
    if with_rope:
        rows = length // GRID_W
        row = jnp.repeat(jnp.arange(rows), GRID_W).astype(F32)
        col = jnp.tile(jnp.arange(GRID_W), rows).astype(F32)
        inv = 1.0 / (ROPE_THETA ** (jnp.arange(half, dtype=F32) / half))
        ang_r = row[:, None] * inv
        ang_c = col[:, None] * inv
        cos = jnp.concatenate([jnp.cos(ang_r), jnp.cos(ang_r), jnp.cos(ang_c), jnp.cos(ang_c)], axis=1)
        sin = jnp.concatenate([-jnp.sin(ang_r), jnp.sin(ang_r), -jnp.sin(ang_c), jnp.sin(ang_c)], axis=1)
    else:
        cos = jnp.ones((length, MLA_ROPE), F32)
        sin = jnp.zeros((length, MLA_ROPE), F32)
    pad = ((0, 0), (0, LANE - MLA_ROPE))
    return jnp.pad(cos, pad), jnp.pad(sin, pad)


_IN_SIZES = (GLA_QK_W, GLA_QK_W, GLA_V_W, GLA_V_W, GLA_LOWRANK, GLA_LOWRANK,
             MLA_Q_RANK, MLA_KV_RANK, MLA_ROPE, D_MODEL, D_MODEL)
(SRC_Q, SRC_K, SRC_V, SRC_R, SRC_AF, SRC_AB, SRC_CQ, SRC_CKV, SRC_KR, SRC_GA, SRC_GB,
 SRC_END) = [sum(_IN_SIZES[:i]) for i in range(len(_IN_SIZES) + 1)]


def _w_in_layout_kernel(w_ref, lat_ref, ctx_ref):
    def rows(a, b):
        return w_ref[a:b, :].astype(BF16)

    small_pad = jnp.zeros((SMALL_W - MLA_ROPE - 2 * GLA_LOWRANK, w_ref.shape[1]), BF16)
    lat_ref[COL_Q:COL_GA, :] = rows(SRC_Q, SRC_AF)
    lat_ref[COL_GA:COL_CQ, :] = rows(SRC_GA, SRC_END)
    lat_ref[COL_CQ:COL_SMALL, :] = rows(SRC_CQ, SRC_KR)
    ctx_ref[CCOL_K:CCOL_CKV, :] = rows(SRC_K, SRC_R)
    ctx_ref[CCOL_CKV:CCOL_SMALL, :] = rows(SRC_CKV, SRC_KR)
    for ref, off in ((lat_ref, COL_SMALL), (ctx_ref, CCOL_SMALL)):
        ref[off:off + MLA_ROPE, :] = rows(SRC_KR, SRC_GA)
        ref[off + MLA_ROPE:off + SM_AB + GLA_LOWRANK, :] = rows(SRC_AF, SRC_CQ)
        ref[off + SM_AB + GLA_LOWRANK:off + SMALL_W, :] = small_pad


def _w_in_layout(w_in, tk=256):
    wt = jnp.swapaxes(w_in, 1, 2)
    _, n, d = wt.shape
    assert n == SRC_END
    return pl.pallas_call(
        _w_in_layout_kernel,
        grid=(d // tk,),
        in_specs=[pl.BlockSpec((None, n, tk), lambda i: (0, 0, i))],
        out_specs=[pl.BlockSpec((NP_LAT, tk), lambda i: (0, i)),
                   pl.BlockSpec((NP_CTX, tk), lambda i: (0, i))],
        out_shape=[jax.ShapeDtypeStruct((NP_LAT, d), BF16),
                   jax.ShapeDtypeStruct((NP_CTX, d), BF16)],
        compiler_params=_cparams(("parallel",)),
        name="w_in_layout",
    )(wt)


def kernel(x, c, ctx, c_ctx, w_mod, b_mod, norm1_g, norm2_g, w_in, w_decay_f, b_decay_f, w_decay_b, b_decay_b, gla_norm_g, q_a_norm_g, w_uq, kv_a_norm_g, w_ukv, q_norm_g, k_norm_g, w_o_gla, w_o_mla, w_out, w_router_group, b_router_group, w_router_expert, b_router_expert, w_exp_gate, w_exp_up, w_exp_down):
    assert w_mod.shape[0] == 1, "single-layer block"
    b, l, d = x.shape
    lc = ctx.shape[1]
    t = b * l

    c_rows = jnp.concatenate([c, c_ctx[None, :], jnp.zeros((8 - b - 1, d), F32)], axis=0)
    mod = _modulation(c_rows, w_mod[0], b_mod[0])
    sh1, sc1, gt1, sh2, sc2, gt2 = [mod[:b, i * d:(i + 1) * d].reshape(b, 1, d) for i in range(6)]
    sh1c = jnp.broadcast_to(mod[b:b + 1, 0:d].reshape(1, 1, d), (b, 1, d))
    sc1c = jnp.broadcast_to(mod[b:b + 1, d:2 * d].reshape(1, 1, d), (b, 1, d))

    w_lat, w_ctx = _w_in_layout(w_in)
    g1 = norm1_g[0].reshape(1, d)
    p = _in_projection(x, g1, sc1, sh1, w_lat, tm=INPROJ_TM)
    pc = _in_projection(ctx, g1, sc1c, sh1c, w_ctx, tm=lc)

    def decay_w(w, off):
        return jnp.zeros((LANE, GLA_QK_W), F32).at[off:off + GLA_LOWRANK].set(w).astype(BF16)

    y_gla = _gla(p, pc,
                 decay_w(w_decay_f[0], SM_AF), b_decay_f[0].reshape(1, -1),
                 decay_w(w_decay_b[0], SM_AB), b_decay_b[0].reshape(1, -1),
                 gla_norm_g[0].reshape(1, -1))

    wq = jnp.pad(w_uq[0].reshape(MLA_Q_RANK, MLA_HEADS, MLA_QK_DIM),
                 ((0, 0), (0, 0), (0, MLA_QK_PAD - MLA_QK_DIM))).reshape(MLA_Q_RANK, -1).astype(BF16)
    wkv = w_ukv[0].astype(BF16)
    qn_g = jnp.pad(q_norm_g[0], (0, MLA_QK_PAD - MLA_QK_DIM)).reshape(1, -1)
    kn_nope = k_norm_g[0][:MLA_NOPE].reshape(1, -1)
    kn_rope = jnp.pad(k_norm_g[0][MLA_NOPE:], (0, LANE - MLA_ROPE)).reshape(1, -1)
    cos, sin = _rope_tables(l, True)
    cos_c, sin_c = _rope_tables(lc, False)
    q_m = _mla_queries(p, q_a_norm_g[0].reshape(1, -1), wq, qn_g, cos, sin, tm=MLA_Q_TM)
    kva = kv_a_norm_g[0].reshape(1, -1)
    k_m, v_m = _mla_keys_values(p, pc, kva, wkv, kn_nope, kn_rope,
                                jnp.concatenate([cos, cos_c]), jnp.concatenate([sin, sin_c]))
    y_mla = _attention(q_m, k_m, v_m, tq=ATTN_TQ)

    y = _branch_merge(y_gla, y_mla, w_o_gla[0].astype(BF16), w_o_mla[0].astype(BF16), p,
                      tm=MERGE_TM, tn=MERGE_TN)
    w_router = jnp.concatenate(
        [w_router_group[0], w_router_expert[0],
         jnp.zeros((d, ROUTER_W - N_GROUPS - N_EXPERTS), F32)], axis=1).astype(BF16)
    b_router = jnp.concatenate(
        [b_router_group[0], b_router_expert[0],
         jnp.zeros((ROUTER_W - N_GROUPS - N_EXPERTS,), F32)]).reshape(1, -1)
    x1, h2, logits = _out_projection(y, w_out[0].astype(BF16), x, gt1, norm2_g[0].reshape(1, d),
                                     sc2, sh2, w_router, b_router, tm=OUTPROJ_TM)

    e_pick, w_pick = _route(logits.reshape(t, ROUTER_W), tm=ROUTE_TM)
    plan = _dispatch_plan(e_pick[:, :2].reshape(-1))
    yp = _experts(plan, h2.reshape(t * ROW_TILES, LANE),
                  w_exp_gate[0], w_exp_up[0], w_exp_down[0])
    return _combine(x1, gt2, w_pick, yp, tm=COMBINE_TM)
```

```python
import functools
import math

import jax
import jax.numpy as jnp
from jax import lax
from jax.experimental import pallas as pl
from jax.experimental.pallas import tpu as pltpu

F32 = jnp.float32
BF16 = jnp.bfloat16

D_MODEL = 2048
GRID_W = 64
EPS = 1e-6

GLA_HEADS = 4
GLA_DK = 256
GLA_DV = 512
GLA_LOWRANK = 16
GLA_TAU = 16.0
GLA_CHUNK = 64
GLA_GROUP = 8
GLA_QK_W = GLA_HEADS * GLA_DK
GLA_V_W = GLA_HEADS * GLA_DV

MLA_HEADS = 16
MLA_Q_RANK = 512
MLA_KV_RANK = 512
MLA_NOPE = 128
MLA_ROPE = 64
MLA_V = 128
MLA_QK_DIM = MLA_NOPE + MLA_ROPE
MLA_QK_PAD = 256
BF16_TILE_ROWS = 16
MLA_VT_ROWS = MLA_V + BF16_TILE_ROWS
ROPE_THETA = 10000.0
LOG2E = math.log2(math.e)

N_GROUPS = 8
EXP_PER_GROUP = 8
N_EXPERTS = N_GROUPS * EXP_PER_GROUP
D_EXPERT = 512
ROUTER_W = 128
MOE_BLOCK = 256
BULK_DMA = 1
SCATTER_BULK_EVERY = 4

LANE = 128
ROW_TILES = D_MODEL // (2 * LANE)
ROW_PITCH = 12
VMEM_LIMIT = 56 * 1024 * 1024

COL_Q = 0
COL_K = COL_Q + GLA_QK_W
COL_V = COL_K + GLA_QK_W
COL_R = COL_V + GLA_V_W
COL_GA = COL_R + GLA_V_W
COL_GB = COL_GA + D_MODEL
COL_CQ = COL_GB + D_MODEL
COL_CKV = COL_CQ + MLA_Q_RANK
COL_SMALL = COL_CKV + MLA_KV_RANK
SMALL_W = 256
NP_LAT = COL_SMALL + SMALL_W
SM_AF = MLA_ROPE
SM_AB = MLA_ROPE + GLA_LOWRANK
CCOL_K = 0
CCOL_V = CCOL_K + GLA_QK_W
CCOL_CKV = CCOL_V + GLA_V_W
CCOL_SMALL = CCOL_CKV + MLA_KV_RANK
NP_CTX = CCOL_SMALL + SMALL_W

INPROJ_TM, INPROJ_TN = 1024, 1280
INPROJ_NORM_ROWS = 128
MLA_Q_TM = 512
ATTN_TQ = 512
MERGE_TM, MERGE_TN = 1024, 1024
OUTPROJ_TM = 256
ROUTE_TM = 1024
COMBINE_TM = 512


def _cparams(sem):
    return pltpu.CompilerParams(dimension_semantics=sem, vmem_limit_bytes=VMEM_LIMIT)


def _silu(x):
    return x * jax.nn.sigmoid(x)


def _pack_rows(x):
    half = x.shape[1] // 2
    lo = lax.bitcast_convert_type(x[:, :half].astype(BF16).astype(F32), jnp.uint32) >> 16
    hi = lax.bitcast_convert_type(x[:, half:].astype(BF16).astype(F32), jnp.uint32) & jnp.uint32(0xFFFF0000)
    packed = lo | hi
    return [packed[:, j * LANE:(j + 1) * LANE] for j in range(half // LANE)]


def _unpack_tile(t):
    lo = lax.bitcast_convert_type(t << 16, F32)
    hi = lax.bitcast_convert_type(t & jnp.uint32(0xFFFF0000), F32)
    return lo, hi


def _mod_kernel(c_ref, w_ref, b_ref, o_ref):
    a = _silu(c_ref[...]).astype(BF16)
    o_ref[...] = jnp.dot(a, w_ref[...].astype(BF16), preferred_element_type=F32) + b_ref[...]


def _modulation(c_rows, w_mod, b_mod):
    m, d = c_rows.shape
    n = w_mod.shape[1]
    tn = 1024
    return pl.pallas_call(
        _mod_kernel,
        grid=(n // tn,),
        in_specs=[pl.BlockSpec((m, d), lambda j: (0, 0)),
                  pl.BlockSpec((d, tn), lambda j: (0, j)),
                  pl.BlockSpec((1, tn), lambda j: (0, j))],
        out_specs=pl.BlockSpec((m, tn), lambda j: (0, j)),
        out_shape=jax.ShapeDtypeStruct((m, n), F32),
        compiler_params=_cparams(("parallel",)),
        name="modulation",
    )(c_rows, w_mod, b_mod.reshape(1, n))


def _inproj_kernel(x_ref, g_ref, sc_ref, sh_ref, w_ref, o_ref, h_ref):
    @pl.when(pl.program_id(2) == 0)
    def _():
        rows = min(INPROJ_NORM_ROWS, x_ref.shape[0])

        def norm_rows(c, carry):
            r0 = pl.multiple_of(c * rows, rows)
            x = x_ref[pl.ds(r0, rows), :]
            ms = jnp.mean(x * x, axis=-1, keepdims=True)
            y = x * lax.rsqrt(ms + EPS) * g_ref[...]
            h_ref[pl.ds(r0, rows), :] = (y * (1.0 + sc_ref[...]) + sh_ref[...]).astype(BF16)
            return carry

        lax.fori_loop(0, x_ref.shape[0] // rows, norm_rows, 0)

    o_ref[...] = lax.dot_general(h_ref[...], w_ref[...], (((1,), (1,)), ((), ())),
                                 preferred_element_type=F32).astype(o_ref.dtype)


def _in_projection(x, g, sc, sh, w, tm):
    b, l, d = x.shape
    tm = min(tm, l)
    n = w.shape[0]
    tn = INPROJ_TN
    return pl.pallas_call(
        _inproj_kernel,
        grid=(b, l // tm, n // tn),
        in_specs=[pl.BlockSpec((None, tm, d), lambda bi, i, j: (bi, i, 0)),
                  pl.BlockSpec((1, d), lambda bi, i, j: (0, 0)),
                  pl.BlockSpec((None, 1, d), lambda bi, i, j: (bi, 0, 0)),
                  pl.BlockSpec((None, 1, d), lambda bi, i, j: (bi, 0, 0)),
                  pl.BlockSpec((tn, d), lambda bi, i, j: (j, 0))],
        out_specs=pl.BlockSpec((None, tm, tn), lambda bi, i, j: (bi, i, j)),
        out_shape=jax.ShapeDtypeStruct((b, l, n), BF16),
        scratch_shapes=[pltpu.VMEM((tm, d), BF16)],
        compiler_params=_cparams(("parallel", "parallel", "arbitrary")),
        name="in_projection",
    )(x, g, sc, sh, w)


def _gla_kernel(q_ref, k_ref, v_ref, r_ref, sm_ref, kc_ref, vc_ref, smc_ref,
                wf_ref, bf_ref, wb_ref, bb_ref, gn_ref, y_ref, o_ref, sf_ref, sb_ref):
    c = GLA_CHUNK
    n_lat = q_ref.shape[0] // c
    n_ctx = kc_ref.shape[0] // c
    row = lax.broadcasted_iota(jnp.int32, (c, c), 0)
    col = lax.broadcasted_iota(jnp.int32, (c, c), 1)
    lower = row >= col
    upper = row <= col
    nt = (((1,), (1,)), ((), ()))
    tn = (((0,), (0,)), ((), ()))

    def cum_decay(sm, w_ref, b_ref, tri, n_chunks):
        z = jnp.dot(sm, w_ref[...], preferred_element_type=F32) + b_ref[...]
        g = (jnp.minimum(z, 0.0) - jnp.log(1.0 + jnp.exp(-jnp.abs(z)))) * (1.0 / GLA_TAU)
        g_hi = g.astype(BF16)
        g_lo = (g - g_hi.astype(F32)).astype(BF16)
        t = jnp.where(tri, 1.0, 0.0).astype(BF16)
        return [jnp.dot(t, g_hi[i * c:(i + 1) * c], preferred_element_type=F32)
                + jnp.dot(t, g_lo[i * c:(i + 1) * c], preferred_element_type=F32)
                for i in range(n_chunks)]

    fwd = (wf_ref, bf_ref, lower, c - 1, sf_ref)
    bwd = (wb_ref, bb_ref, upper, 0, sb_ref)

    def group(r0, n_chunks, direction, sm_r, k_r, v_r, q_r):
        w_ref, b_ref, tri, end_row, st_ref = direction
        rows = n_chunks * c
        k_all = k_r[pl.ds(r0, rows), :].astype(F32)
        v_all = v_r[pl.ds(r0, rows), :]
        q_all = None if q_r is None else q_r[pl.ds(r0, rows), :].astype(F32) * (GLA_DK ** -0.5)
        bc_all = cum_decay(sm_r[pl.ds(r0, rows), :], w_ref, b_ref, tri, n_chunks)
        local = []
        for i in range(n_chunks):
            sl = slice(i * c, (i + 1) * c)
            bc = bc_all[i]
            bend = bc[end_row:end_row + 1, :]
            k, v = k_all[sl], v_all[sl]
            ke = (k * jnp.exp(bend - bc)).astype(BF16)
            upd = lax.dot_general(v, ke, tn, preferred_element_type=F32)
            qd = o_intra = None
            if q_all is not None:
                qd = (q_all[sl] * jnp.exp(bc)).astype(BF16)
                ki = (k * jnp.exp(-bc)).astype(BF16)
                att = lax.dot_general(qd, ki, nt, preferred_element_type=F32)
                att = jnp.where(tri, att, 0.0).astype(BF16)
                o_intra = jnp.dot(att, v, preferred_element_type=F32)
            local.append((jnp.exp(bend), upd, qd, o_intra))
        st = st_ref[...]
        outs = [None] * n_chunks
        for i in (range(n_chunks) if end_row else reversed(range(n_chunks))):
            decay, upd, qd, o_intra = local[i]
            if q_all is not None:
                outs[i] = o_intra + lax.dot_general(qd, st.astype(BF16), nt, preferred_element_type=F32)
            st = st * decay + upd
        st_ref[...] = st
        return None if q_all is None else jnp.concatenate(outs, axis=0)

    def readout(r0, o):
        ms = jnp.mean(o * o, axis=-1, keepdims=True)
        yn = o * lax.rsqrt(ms + EPS) * gn_ref[...]
        r = r_ref[pl.ds(r0, o.shape[0]), :].astype(F32)
        y_ref[pl.ds(r0, o.shape[0]), :] = (yn * _silu(r)).astype(y_ref.dtype)

    sf_ref[...] = jnp.zeros_like(sf_ref)
    sb_ref[...] = jnp.zeros_like(sb_ref)
    group(0, n_ctx, fwd, smc_ref, kc_ref, vc_ref, None)
    group(0, n_ctx, bwd, smc_ref, kc_ref, vc_ref, None)

    g_lat = min(GLA_GROUP, n_lat // 2)
    assert n_lat % (2 * g_lat) == 0
    n_groups = n_lat // g_lat
    rows = g_lat * c

    def scan(i, direction):
        r0 = pl.multiple_of(i * rows, rows)
        return r0, group(r0, g_lat, direction, sm_ref, k_ref, v_ref, q_ref)

    def first_half(i, carry):
        for r0, o in (scan(i, fwd), scan(n_groups - 1 - i, bwd)):
            o_ref[pl.ds(r0, rows), :] = o
        return carry

    lax.fori_loop(0, n_groups // 2, first_half, 0)

    def second_half(i, carry):
        for r0, o in (scan(i, fwd), scan(n_groups - 1 - i, bwd)):
            readout(r0, o_ref[pl.ds(r0, rows), :] + o)
        return carry

    lax.fori_loop(n_groups // 2, n_groups, second_half, 0)


def _gla(p, pc, wf, bf, wb, bb, gn):
    b, l, _ = p.shape
    lc = pc.shape[1]
    dk, dv = GLA_DK, GLA_DV
    hmap = lambda off: (lambda bi, h: (bi, 0, off + h))
    wmap = lambda bi, h: (0, h)
    return pl.pallas_call(
        _gla_kernel,
        grid=(b, GLA_HEADS),
        in_specs=[pl.BlockSpec((None, l, dk), hmap(COL_Q // dk)),
                  pl.BlockSpec((None, l, dk), hmap(COL_K // dk)),
                  pl.BlockSpec((None, l, dv), hmap(COL_V // dv)),
                  pl.BlockSpec((None, l, dv), hmap(COL_R // dv)),
                  pl.BlockSpec((None, l, LANE), lambda bi, h: (bi, 0, COL_SMALL // LANE)),
                  pl.BlockSpec((None, lc, dk), hmap(CCOL_K // dk)),
                  pl.BlockSpec((None, lc, dv), hmap(CCOL_V // dv)),
                  pl.BlockSpec((None, lc, LANE), lambda bi, h: (bi, 0, CCOL_SMALL // LANE)),
                  pl.BlockSpec((LANE, dk), wmap),
                  pl.BlockSpec((1, dk), wmap),
                  pl.BlockSpec((LANE, dk), wmap),
                  pl.BlockSpec((1, dk), wmap),
                  pl.BlockSpec((1, dv), lambda bi, h: (0, 0))],
        out_specs=pl.BlockSpec((None, l, dv), lambda bi, h: (bi, 0, h)),
        out_shape=jax.ShapeDtypeStruct((b, l, GLA_V_W), BF16),
        scratch_shapes=[pltpu.VMEM((l, dv), F32), pltpu.VMEM((dv, dk), F32), pltpu.VMEM((dv, dk), F32)],
        compiler_params=_cparams(("parallel", "parallel")),
        name="gla",
    )(p, p, p, p, p, pc, pc, pc, wf, bf, wb, bb, gn)


def _rope_partner(t):
    lane = lax.broadcasted_iota(jnp.int32, t.shape, 1)
    first = (lane % 32) < 16
    return jnp.where(first, pltpu.roll(t, LANE - 16, 1), pltpu.roll(t, 16, 1))


def _mla_q_kernel(cq_ref, ga_ref, w_ref, gn_ref, cos_ref, sin_ref, o_ref):
    cq = cq_ref[...].astype(F32)
    cn = (cq * lax.rsqrt(jnp.mean(cq * cq, axis=-1, keepdims=True) + EPS) * ga_ref[...]).astype(BF16)
    cos = cos_ref[...]
    sin = sin_ref[...]
    scale = MLA_QK_DIM ** -0.5 * LOG2E
    for h in range(MLA_HEADS):
        qh = jnp.dot(cn, w_ref[:, h * MLA_QK_PAD:(h + 1) * MLA_QK_PAD], preferred_element_type=F32)
        ms = jnp.sum(qh * qh, axis=-1, keepdims=True) * (1.0 / MLA_QK_DIM)
        qn = qh * lax.rsqrt(ms + EPS) * gn_ref[...]
        t = qn[:, MLA_NOPE:]
        rot = t * cos + _rope_partner(t) * sin
        o_ref[h, :, :MLA_NOPE] = (qn[:, :MLA_NOPE] * scale).astype(o_ref.dtype)
        o_ref[h, :, MLA_NOPE:] = (rot * scale).astype(o_ref.dtype)


def _mla_queries(p, ga, wq, gn, cos, sin, tm):
    b, l, _ = p.shape
    tm = min(tm, l)
    return pl.pallas_call(
        _mla_q_kernel,
        grid=(b, l // tm),
        in_specs=[pl.BlockSpec((None, tm, MLA_Q_RANK), lambda bi, i: (bi, i, COL_CQ // MLA_Q_RANK)),
                  pl.BlockSpec((1, MLA_Q_RANK), lambda bi, i: (0, 0)),
                  pl.BlockSpec(wq.shape, lambda bi, i: (0, 0)),
                  pl.BlockSpec((1, MLA_QK_PAD), lambda bi, i: (0, 0)),
                  pl.BlockSpec((tm, LANE), lambda bi, i: (i, 0)),
                  pl.BlockSpec((tm, LANE), lambda bi, i: (i, 0))],
        out_specs=pl.BlockSpec((None, MLA_HEADS, tm, MLA_QK_PAD), lambda bi, i: (bi, 0, i, 0)),
        out_shape=jax.ShapeDtypeStruct((b, MLA_HEADS, l, MLA_QK_PAD), BF16),
        compiler_params=_cparams(("parallel", "parallel")),
        name="mla_queries",
    )(p, ga, wq, gn, cos, sin)


def _mla_kv_kernel(ckv_ref, sm_ref, ckvc_ref, smc_ref, ga_ref, w_ref, gk_ref, gr_ref,
                   cos_ref, sin_ref, k_ref, v_ref, *, n_lat):
    is_ctx = pl.program_id(1) >= n_lat
    ckv = jnp.where(is_ctx, ckvc_ref[...], ckv_ref[...]).astype(F32)
    cn = (ckv * lax.rsqrt(jnp.mean(ckv * ckv, axis=-1, keepdims=True) + EPS) * ga_ref[...]).astype(BF16)
    sm = jnp.where(is_ctx, smc_ref[...], sm_ref[...]).astype(F32)
    lane = lax.broadcasted_iota(jnp.int32, sm.shape, 1)
    kr = jnp.where(lane < MLA_ROPE, sm, 0.0)
    ss_r = jnp.sum(kr * kr, axis=-1, keepdims=True)
    krg = kr * gr_ref[...]
    rot = krg * cos_ref[...] + _rope_partner(krg) * sin_ref[...]
    hw = MLA_NOPE + MLA_V
    tm = ckv.shape[0]
    sub = lax.broadcasted_iota(jnp.int32, (MLA_VT_ROWS - MLA_V, tm), 0)
    ones_rows = jnp.where(sub == 0, 1.0, 0.0).astype(v_ref.dtype)
    for h in range(MLA_HEADS):
        kvh = jnp.dot(cn, w_ref[:, h * hw:(h + 1) * hw], preferred_element_type=F32)
        kn = kvh[:, :MLA_NOPE]
        ms = (jnp.sum(kn * kn, axis=-1, keepdims=True) + ss_r) * (1.0 / MLA_QK_DIM)
        rs = lax.rsqrt(ms + EPS)
        k_ref[h, :, :MLA_NOPE] = (kn * rs * gk_ref[...]).astype(k_ref.dtype)
        k_ref[h, :, MLA_NOPE:] = (rot * rs).astype(k_ref.dtype)
        v_ref[h, :MLA_V, :] = kvh[:, MLA_NOPE:].T.astype(v_ref.dtype)
        v_ref[h, MLA_V:, :] = ones_rows


def _mla_keys_values(p, pc, ga, wkv, gk, gr, cos, sin):
    b, l, _ = p.shape
    lc = pc.shape[1]
    tm = min(lc, l)
    assert l % tm == 0 and lc % tm == 0
    n_lat, n_ctx = l // tm, lc // tm
    lat = lambda blk: (lambda bi, i: (bi, jnp.minimum(i, n_lat - 1), blk))
    ctx = lambda blk: (lambda bi, i: (bi, jnp.maximum(i - n_lat, 0), blk))
    const = lambda bi, i: (0, 0)
    k_spec = pl.BlockSpec((None, MLA_HEADS, tm, MLA_QK_PAD), lambda bi, i: (bi, 0, i, 0))
    vt_spec = pl.BlockSpec((None, MLA_HEADS, MLA_VT_ROWS, tm), lambda bi, i: (bi, 0, 0, i))
    k_shape = jax.ShapeDtypeStruct((b, MLA_HEADS, l + lc, MLA_QK_PAD), BF16)
    vt_shape = jax.ShapeDtypeStruct((b, MLA_HEADS, MLA_VT_ROWS, l + lc), BF16)
    return pl.pallas_call(
        functools.partial(_mla_kv_kernel, n_lat=n_lat),
        grid=(b, n_lat + n_ctx),
        in_specs=[pl.BlockSpec((None, tm, MLA_KV_RANK), lat(COL_CKV // MLA_KV_RANK)),
                  pl.BlockSpec((None, tm, LANE), lat(COL_SMALL // LANE)),
                  pl.BlockSpec((None, tm, MLA_KV_RANK), ctx(CCOL_CKV // MLA_KV_RANK)),
                  pl.BlockSpec((None, tm, LANE), ctx(CCOL_SMALL // LANE)),
                  pl.BlockSpec((1, MLA_KV_RANK), const),
                  pl.BlockSpec(wkv.shape, const),
                  pl.BlockSpec((1, LANE), const),
                  pl.BlockSpec((1, LANE), const),
                  pl.BlockSpec((tm, LANE), lambda bi, i: (i, 0)),
                  pl.BlockSpec((tm, LANE), lambda bi, i: (i, 0))],
        out_specs=[k_spec, vt_spec],
        out_shape=[k_shape, vt_shape],
        compiler_params=_cparams(("parallel", "arbitrary")),
        name="mla_keys_values",
    )(p, p, pc, pc, ga, wkv, gk, gr, cos, sin)


def _attn_kernel(q_ref, k_ref, vt_ref, o_ref, s0_ref, s1_ref, m0_ref, m1_ref, *, tq):
    nt = (((1,), (1,)), ((), ()))
    n = q_ref.shape[0] // tq
    bufs = ((s0_ref, m0_ref), (s1_ref, m1_ref))

    def scores(i, slot):
        s_ref, m_ref = bufs[slot]
        r0 = pl.multiple_of(i * tq, tq)
        s = lax.dot_general(k_ref[...], q_ref[pl.ds(r0, tq), :], nt, preferred_element_type=F32)
        s_ref[...] = s
        m_ref[...] = jnp.max(s, axis=0, keepdims=True)

    def outputs(i, slot):
        s_ref, m_ref = bufs[slot]
        r0 = pl.multiple_of(i * tq, tq)
        p = jnp.exp2(s_ref[...] - m_ref[...]).astype(BF16)
        o = jnp.dot(vt_ref[...], p, preferred_element_type=F32)
        o_ref[pl.ds(r0, tq), :] = (o[:MLA_V] / o[MLA_V:MLA_V + 1]).T.astype(o_ref.dtype)

    scores(0, 0)
    if n % 2 == 0:
        def body(j, carry):
            scores(2 * j + 1, 1)
            outputs(2 * j, 0)
            scores(2 * j + 2, 0)
            outputs(2 * j + 1, 1)
            return carry

        lax.fori_loop(0, n // 2 - 1, body, 0)
        scores(n - 1, 1)
        outputs(n - 2, 0)
        outputs(n - 1, 1)
    else:
        assert n == 1
        outputs(0, 0)


def _attention(q, k, v, tq):
    b, h, l, dq = q.shape
    tq = min(tq, l)
    lk = k.shape[2]
    bh = lambda bi, hi: (bi, hi, 0, 0)
    return pl.pallas_call(
        functools.partial(_attn_kernel, tq=tq),
        grid=(b, h),
        in_specs=[pl.BlockSpec((None, None, l, dq), bh),
                  pl.BlockSpec((None, None, lk, dq), bh),
                  pl.BlockSpec((None, None, MLA_VT_ROWS, lk), bh)],
        out_specs=pl.BlockSpec((None, l, MLA_V), lambda bi, hi: (bi, 0, hi)),
        out_shape=jax.ShapeDtypeStruct((b, l, h * MLA_V), BF16),
        scratch_shapes=[pltpu.VMEM((lk, tq), F32), pltpu.VMEM((lk, tq), F32),
                        pltpu.VMEM((1, tq), F32), pltpu.VMEM((1, tq), F32)],
        compiler_params=_cparams(("parallel", "parallel")),
        name="mla_attention",
    )(q, k, v)


def _merge_kernel(yg_ref, ym_ref, wg_ref, wm_ref, ga_ref, gb_ref, o_ref):
    a = jnp.dot(yg_ref[...], wg_ref[...], preferred_element_type=F32)
    m = jnp.dot(ym_ref[...], wm_ref[...], preferred_element_type=F32)
    y = (jax.nn.sigmoid(ga_ref[...].astype(F32)) * a
         + jax.nn.sigmoid(gb_ref[...].astype(F32)) * m)
    o_ref[...] = y.astype(o_ref.dtype)


def _branch_merge(yg, ym, wg, wm, p, tm, tn):
    b, l, d = yg.shape
    tm = min(tm, l)
    n = wg.shape[1]
    return pl.pallas_call(
        _merge_kernel,
        grid=(b, l // tm, n // tn),
        in_specs=[pl.BlockSpec((None, tm, d), lambda bi, i, j: (bi, i, 0)),
                  pl.BlockSpec((None, tm, d), lambda bi, i, j: (bi, i, 0)),
                  pl.BlockSpec((d, tn), lambda bi, i, j: (0, j)),
                  pl.BlockSpec((d, tn), lambda bi, i, j: (0, j)),
                  pl.BlockSpec((None, tm, tn), lambda bi, i, j: (bi, i, COL_GA // tn + j)),
                  pl.BlockSpec((None, tm, tn), lambda bi, i, j: (bi, i, COL_GB // tn + j))],
        out_specs=pl.BlockSpec((None, tm, tn), lambda bi, i, j: (bi, i, j)),
        out_shape=jax.ShapeDtypeStruct((b, l, n), BF16),
        compiler_params=_cparams(("parallel", "parallel", "arbitrary")),
        name="branch_merge",
    )(yg, ym, wg, wm, p, p)


def _outproj_kernel(y_ref, w_ref, x_ref, gt_ref, g2_ref, sc_ref, sh_ref, wr_ref, br_ref,
                    x1_ref, h2_ref, lg_ref):
    x1 = x_ref[...] + gt_ref[...] * jnp.dot(y_ref[...], w_ref[...], preferred_element_type=F32)
    x1_ref[...] = x1
    ms = jnp.mean(x1 * x1, axis=-1, keepdims=True)
    h2 = x1 * lax.rsqrt(ms + EPS) * g2_ref[...] * (1.0 + sc_ref[...]) + sh_ref[...]
    tm = x1.shape[0]
    for j, tile in enumerate(_pack_rows(h2)):
        h2_ref[pl.ds(j, tm, stride=ROW_TILES), :] = tile
    lg_ref[...] = jnp.dot(h2.astype(BF16), wr_ref[...], preferred_element_type=F32) + br_ref[...]


def _out_projection(y, w, x, gt, g2, sc, sh, wr, br, tm):
    b, l, d = x.shape
    tm = min(tm, l)
    row = lambda bi, i: (bi, i, 0)
    per_b = lambda bi, i: (bi, 0, 0)
    const = lambda bi, i: (0, 0)
    return pl.pallas_call(
        _outproj_kernel,
        grid=(b, l // tm),
        in_specs=[pl.BlockSpec((None, tm, d), row),
                  pl.BlockSpec((d, d), const),
                  pl.BlockSpec((None, tm, d), row),
                  pl.BlockSpec((None, 1, d), per_b),
                  pl.BlockSpec((1, d), const),
                  pl.BlockSpec((None, 1, d), per_b),
                  pl.BlockSpec((None, 1, d), per_b),
                  pl.BlockSpec((d, ROUTER_W), const),
                  pl.BlockSpec((1, ROUTER_W), const)],
        out_specs=[pl.BlockSpec((None, tm, d), row),
                   pl.BlockSpec((None, tm * ROW_TILES, LANE), row),
                   pl.BlockSpec((None, tm, ROUTER_W), row)],
        out_shape=[jax.ShapeDtypeStruct((b, l, d), F32),
                   jax.ShapeDtypeStruct((b, l * ROW_TILES, LANE), jnp.uint32),
                   jax.ShapeDtypeStruct((b, l, ROUTER_W), F32)],
        compiler_params=_cparams(("parallel", "parallel")),
        name="out_projection",
    )(y, w, x, gt, g2, sc, sh, wr, br)


def _route_kernel(lg_ref, e_ref, w_ref):
    lg = lg_ref[...]
    lane = lax.broadcasted_iota(jnp.int32, lg.shape, 1)
    neg = jnp.float32(-jnp.inf)
    big = jnp.int32(1 << 20)
    is_g = lane < N_GROUPS
    gl = jnp.where(is_g, lg, neg)
    gm = jnp.max(gl, axis=-1, keepdims=True)
    gidx = jnp.min(jnp.where(gl == gm, lane, big), axis=-1, keepdims=True)
    gsum = jnp.sum(jnp.where(is_g, jnp.exp(gl - gm), 0.0), axis=-1, keepdims=True)
    p_g = 1.0 / gsum
    g_lo = N_GROUPS + gidx * EXP_PER_GROUP
    in_grp = (lane >= g_lo) & (lane < g_lo + EXP_PER_GROUP)
    el = jnp.where(in_grp, lg, neg)
    em = jnp.max(el, axis=-1, keepdims=True)
    z = jnp.sum(jnp.where(in_grp, jnp.exp(el - em), 0.0), axis=-1, keepdims=True)
    i1 = jnp.min(jnp.where(el == em, lane, big), axis=-1, keepdims=True)
    el2 = jnp.where(lane == i1, neg, el)
    em2 = jnp.max(el2, axis=-1, keepdims=True)
    i2 = jnp.min(jnp.where(el2 == em2, lane, big), axis=-1, keepdims=True)
    p1 = 1.0 / z
    p2 = jnp.exp(em2 - em) / z
    tot = p1 + p2
    e_ref[...] = jnp.where(lane == 0, i1 - N_GROUPS, jnp.where(lane == 1, i2 - N_GROUPS, 0))
    w_ref[...] = jnp.where(lane == 0, p_g * p1 / tot, jnp.where(lane == 1, p_g * p2 / tot, 0.0))


def _route(logits, tm):
    t = logits.shape[0]
    tm = min(tm, t)
    spec = pl.BlockSpec((tm, ROUTER_W), lambda i: (i, 0))
    return pl.pallas_call(
        _route_kernel,
        grid=(t // tm,),
        in_specs=[spec],
        out_specs=[spec, spec],
        out_shape=[jax.ShapeDtypeStruct((t, ROUTER_W), jnp.int32),
                   jax.ShapeDtypeStruct((t, ROUTER_W), F32)],
        compiler_params=_cparams(("parallel",)),
        name="moe_route",
    )(logits)


def _expert_kernel(be_ref, nv_ref, first_ref, nxe_ref, ws_ref, j0_ref, tok_ref, dst_ref,
                   h_hbm, wg_hbm, wu_hbm, wd_hbm, yp_hbm,
                   xbuf, ybuf, wgv, wuv, wdv, sem_in, sem_out, sem_w, sem_init):
    i = pl.program_id(0)
    n = pl.num_programs(0)
    blk = MOE_BLOCK
    rt, pitch = ROW_TILES, ROW_PITCH
    plane = yp_hbm.shape[0] // (2 * rt)
    t_rows = plane - blk
    slot_rows = blk * pitch
    par = i % 2
    cur = pl.multiple_of(par * slot_rows, 8)
    oth = pl.multiple_of((1 - par) * slot_rows, 8)

    def weight_copies(e, s):
        return (pltpu.make_async_copy(wg_hbm.at[e], wgv.at[s], sem_w.at[s, 0]),
                pltpu.make_async_copy(wu_hbm.at[e], wuv.at[s], sem_w.at[s, 1]),
                pltpu.make_async_copy(wd_hbm.at[e], wdv.at[s], sem_w.at[s, 2]))

    def gather_start(b, buf0, sem, r_lo=0, r_hi=blk):
        base = j0_ref[b]
        for r in range(r_lo, r_hi):
            row0 = pl.multiple_of(tok_ref[base + r] * rt, rt)
            pltpu.make_async_copy(h_hbm.at[pl.ds(row0, rt)],
                                  xbuf.at[pl.ds(buf0 + r * pitch, rt)], sem_in.at[sem]).start()

    def gather_wait(buf0, sem):
        pltpu.make_async_copy(h_hbm.at[pl.ds(0, blk * rt)], xbuf.at[pl.ds(buf0, blk * rt)],
                              sem_in.at[sem]).wait()

    def scatter_start(b, buf0, sem, r_lo=0, r_hi=blk):
        base = j0_ref[b]
        filled = nv_ref[b]
        for r in range(r_lo, r_hi):
            dst = jnp.where(r < filled, dst_ref[base + r], t_rows + r)
            row0 = pl.multiple_of(dst * rt, rt)
            pltpu.make_async_copy(ybuf.at[pl.ds(buf0 + r * pitch, rt)],
                                  yp_hbm.at[pl.ds(row0, rt)], sem_out.at[sem]).start(
                                      priority=BULK_DMA if r % SCATTER_BULK_EVERY == 0 else 0)

    def scatter_wait(buf0, sem):
        pltpu.make_async_copy(ybuf.at[pl.ds(buf0, blk * rt)], yp_hbm.at[pl.ds(0, blk * rt)],
                              sem_out.at[sem]).wait()

    @pl.when(i == 0)
    def _():
        for cp in weight_copies(be_ref[0], 0):
            cp.start(priority=BULK_DMA)
        gather_start(0, 0, 0)
        ybuf[...] = jnp.zeros_like(ybuf)
        zeros = ybuf.at[pl.ds(slot_rows, blk * rt)]
        pltpu.make_async_copy(zeros, yp_hbm.at[pl.ds(t_rows * rt, blk * rt)], sem_out.at[0]).start()
        spare = pltpu.make_async_copy(zeros, yp_hbm.at[pl.ds((plane + t_rows) * rt, blk * rt)], sem_init)
        spare.start()
        spare.wait()

    valid = nv_ref[i] > 0
    nxt = jnp.minimum(i + 1, n - 1)
    has_next = jnp.logical_and(i + 1 < n, nv_ref[nxt] > 0)

    @pl.when(jnp.logical_and(valid, first_ref[i] == 1))
    def _():
        ws = ws_ref[i]
        for cp in weight_copies(be_ref[i], ws):
            cp.wait()

        @pl.when(nxe_ref[i] >= 0)
        def _():
            for cp in weight_copies(nxe_ref[i], 1 - ws):
                cp.start(priority=BULK_DMA)

    @pl.when(valid)
    def _():
        ws = ws_ref[i]
        gather_wait(cur, par)
        scatter_wait(cur, par)
        prev = jnp.maximum(i - 1, 0)
        per = blk // (2 * rt)

        def read_tiles(first_chunk):
            halves = []
            for j in range(rt):
                lo = (first_chunk + j) * per
                gather_start(nxt, oth, 1 - par, lo, lo + per)
                halves.append(_unpack_tile(xbuf[pl.ds(cur + j, blk, stride=pitch), :]))
            return jnp.concatenate([h[0] for h in halves] + [h[1] for h in halves],
                                   axis=1).astype(BF16)

        scatter_start(prev, oth, 1 - par)
        g = jnp.dot(read_tiles(0), wgv[ws].astype(BF16), preferred_element_type=F32)
        u = jnp.dot(read_tiles(rt), wuv[ws].astype(BF16), preferred_element_type=F32)
        a = (_silu(g) * u).astype(BF16)
        y = jnp.dot(a, wdv[ws].astype(BF16), preferred_element_type=F32)
        for j, tile in enumerate(_pack_rows(y)):
            ybuf[pl.ds(cur + j, blk, stride=pitch), :] = tile

        @pl.when(jnp.logical_not(has_next))
        def _():
            scatter_wait(oth, 1 - par)
            scatter_start(i, cur, par)
            gather_wait(oth, 1 - par)
            scatter_wait(cur, par)


def _experts(plan, h2, w_gate, w_up, w_down):
    d = w_gate.shape[1]
    rt, pitch = ROW_TILES, ROW_PITCH
    assert d == 2 * rt * LANE
    t = h2.shape[0] // rt
    n_blocks = plan[0].shape[0]
    blk = MOE_BLOCK
    de = w_gate.shape[2]
    any_spec = pl.BlockSpec(memory_space=pl.ANY)
    return pl.pallas_call(
        _expert_kernel,
        grid_spec=pltpu.PrefetchScalarGridSpec(
            num_scalar_prefetch=len(plan),
            grid=(n_blocks,),
            in_specs=[any_spec, any_spec, any_spec, any_spec],
            out_specs=any_spec,
            scratch_shapes=[pltpu.VMEM((2 * blk * pitch, LANE), jnp.uint32),
                            pltpu.VMEM((2 * blk * pitch, LANE), jnp.uint32),
                            pltpu.VMEM((2, d, de), F32),
                            pltpu.VMEM((2, d, de), F32),
                            pltpu.VMEM((2, de, d), F32),
                            pltpu.SemaphoreType.DMA((2,)),
                            pltpu.SemaphoreType.DMA((2,)),
                            pltpu.SemaphoreType.DMA((2, 3)),
                            pltpu.SemaphoreType.DMA],
        ),
        out_shape=jax.ShapeDtypeStruct((2 * (t + blk) * rt, LANE), jnp.uint32),
        compiler_params=_cparams(("arbitrary",)),
        name="moe_experts",
    )(*plan, h2, w_gate, w_up, w_down).reshape(2, (t + blk) * rt, LANE)


def _combine_kernel(x_ref, gt_ref, w_ref, yp_ref, o_ref):
    tm = x_ref.shape[0]
    w = w_ref[...]
    w0, w1 = w[:, 0:1], w[:, 1:2]
    lows, highs = [], []
    for j in range(ROW_TILES):
        lo0, hi0 = _unpack_tile(yp_ref[0, pl.ds(j, tm, stride=ROW_TILES), :])
        lo1, hi1 = _unpack_tile(yp_ref[1, pl.ds(j, tm, stride=ROW_TILES), :])
        lows.append(w0 * lo0 + w1 * lo1)
        highs.append(w0 * hi0 + w1 * hi1)
    o_ref[...] = x_ref[...] + gt_ref[...] * jnp.concatenate(lows + highs, axis=1)


def _combine(x1, gt, w_pick, yp, tm):
    b, l, d = x1.shape
    tm = min(tm, l)
    nb = l // tm
    return pl.pallas_call(
        _combine_kernel,
        grid=(b, nb),
        in_specs=[pl.BlockSpec((None, tm, d), lambda bi, i: (bi, i, 0)),
                  pl.BlockSpec((None, 1, d), lambda bi, i: (bi, 0, 0)),
                  pl.BlockSpec((tm, ROUTER_W), lambda bi, i: (bi * nb + i, 0)),
                  pl.BlockSpec((2, tm * ROW_TILES, LANE), lambda bi, i: (0, bi * nb + i, 0))],
        out_specs=pl.BlockSpec((None, tm, d), lambda bi, i: (bi, i, 0)),
        out_shape=jax.ShapeDtypeStruct((b, l, d), F32),
        compiler_params=_cparams(("parallel", "parallel")),
        name="moe_combine",
    )(x1, gt, w_pick, yp)


def _dispatch_plan(eid):
    blk = MOE_BLOCK
    n_assign = eid.shape[0]
    n_tok = n_assign // 2
    n_blocks = (n_assign + blk - 1) // blk + N_EXPERTS
    order = jnp.argsort(eid).astype(jnp.int32)
    e_ids = jnp.arange(N_EXPERTS, dtype=jnp.int32)
    counts = jnp.sum((eid[None, :] == e_ids[:, None]).astype(jnp.int32), axis=1)
    start = jnp.cumsum(counts) - counts
    nblk = (counts + blk - 1) // blk
    bend = jnp.cumsum(nblk)
    b_ids = jnp.arange(n_blocks, dtype=jnp.int32)
    block_e = jnp.minimum(jnp.searchsorted(bend, b_ids, side='right'), N_EXPERTS - 1).astype(jnp.int32)
    used = b_ids < bend[-1]
    within = b_ids - (bend - nblk)[block_e]
    block_nv = jnp.where(used, jnp.clip(counts[block_e] - within * blk, 0, blk), 0).astype(jnp.int32)
    block_j0 = jnp.where(used, start[block_e] + within * blk, 0).astype(jnp.int32)
    first = jnp.concatenate([jnp.ones((1,), jnp.int32),
                             (block_e[1:] != block_e[:-1]).astype(jnp.int32)])
    w_slot = ((jnp.cumsum(first) - 1) % 2).astype(jnp.int32)
    nonempty = jnp.where(counts > 0, e_ids, N_EXPERTS)
    later = jnp.flip(lax.cummin(jnp.flip(nonempty)))
    next_e = jnp.concatenate([later[1:], jnp.full((1,), N_EXPERTS, jnp.int32)])
    next_e = jnp.where(next_e >= N_EXPERTS, -1, next_e)[block_e]
    order = jnp.concatenate([order, jnp.zeros((blk,), jnp.int32)])
    tok_sorted = order // 2
    dst_sorted = (order % 2) * (n_tok + blk) + order // 2
    return (block_e, block_nv, first, next_e, w_slot, block_j0, tok_sorted, dst_sorted)


def _rope_tables(length, with_rope):
    half = MLA_ROPE // 4
    if with_rope:
        rows = length // GRID_W
        row = jnp.repeat(jnp.arange(rows), GRID_W).astype(F32)
        col = jnp.tile(jnp.arange(GRID_W), rows).astype(F32)
        inv = 1.0 / (ROPE_THETA ** (jnp.arange(half, dtype=F32) / half))
        ang_r = row[:, None] * inv
        ang_c = col[:, None] * inv
        cos = jnp.concatenate([jnp.cos(ang_r), jnp.cos(ang_r), jnp.cos(ang_c), jnp.cos(ang_c)], axis=1)
        sin = jnp.concatenate([-jnp.sin(ang_r), jnp.sin(ang_r), -jnp.sin(ang_c), jnp.sin(ang_c)], axis=1)
    else:
        cos = jnp.ones((length, MLA_ROPE), F32)
        sin = jnp.zeros((length, MLA_ROPE), F32)
    pad = ((0, 0), (0, LANE - MLA_ROPE))
    return jnp.pad(cos, pad), jnp.pad(sin, pad)


_IN_SIZES = (GLA_QK_W, GLA_QK_W, GLA_V_W, GLA_V_W, GLA_LOWRANK, GLA_LOWRANK,
             MLA_Q_RANK, MLA_KV_RANK, MLA_ROPE, D_MODEL, D_MODEL)
(SRC_Q, SRC_K, SRC_V, SRC_R, SRC_AF, SRC_AB, SRC_CQ, SRC_CKV, SRC_KR, SRC_GA, SRC_GB,
 SRC_END) = [sum(_IN_SIZES[:i]) for i in range(len(_IN_SIZES) + 1)]


def _w_in_layout_kernel(w_ref, lat_ref, ctx_ref):
    def rows(a, b):
        return w_ref[a:b, :].astype(BF16)

    small_pad = jnp.zeros((SMALL_W - MLA_ROPE - 2 * GLA_LOWRANK, w_ref.shape[1]), BF16)
    lat_ref[COL_Q:COL_GA, :] = rows(SRC_Q, SRC_AF)
    lat_ref[COL_GA:COL_CQ, :] = rows(SRC_GA, SRC_END)
    lat_ref[COL_CQ:COL_SMALL, :] = rows(SRC_CQ, SRC_KR)
    ctx_ref[CCOL_K:CCOL_CKV, :] = rows(SRC_K, SRC_R)
    ctx_ref[CCOL_CKV:CCOL_SMALL, :] = rows(SRC_CKV, SRC_KR)
    for ref, off in ((lat_ref, COL_SMALL), (ctx_ref, CCOL_SMALL)):
        ref[off:off + MLA_ROPE, :] = rows(SRC_KR, SRC_GA)
        ref[off + MLA_ROPE:off + SM_AB + GLA_LOWRANK, :] = rows(SRC_AF, SRC_CQ)
        ref[off + SM_AB + GLA_LOWRANK:off + SMALL_W, :] = small_pad


def _w_in_layout(w_in, tk=256):
    wt = jnp.swapaxes(w_in, 1, 2)
    _, n, d = wt.shape
    assert n == SRC_END
    return pl.pallas_call(
        _w_in_layout_kernel,
        grid=(d // tk,),
        in_specs=[pl.BlockSpec((None, n, tk), lambda i: (0, 0, i))],
        out_specs=[pl.BlockSpec((NP_LAT, tk), lambda i: (0, i)),
                   pl.BlockSpec((NP_CTX, tk), lambda i: (0, i))],
        out_shape=[jax.ShapeDtypeStruct((NP_LAT, d), BF16),
                   jax.ShapeDtypeStruct((NP_CTX, d), BF16)],
        compiler_params=_cparams(("parallel",)),
        name="w_in_layout",
    )(wt)


def kernel(x, c, ctx, c_ctx, w_mod, b_mod, norm1_g, norm2_g, w_in, w_decay_f, b_decay_f, w_decay_b, b_decay_b, gla_norm_g, q_a_norm_g, w_uq, kv_a_norm_g, w_ukv, q_norm_g, k_norm_g, w_o_gla, w_o_mla, w_out, w_router_group, b_router_group, w_router_expert, b_router_expert, w_exp_gate, w_exp_up, w_exp_down):
    assert w_mod.shape[0] == 1, "single-layer block"
    b, l, d = x.shape
    lc = ctx.shape[1]
    t = b * l

    c_rows = jnp.concatenate([c, c_ctx[None, :], jnp.zeros((8 - b - 1, d), F32)], axis=0)
    mod = _modulation(c_rows, w_mod[0], b_mod[0])
    sh1, sc1, gt1, sh2, sc2, gt2 = [mod[:b, i * d:(i + 1) * d].reshape(b, 1, d) for i in range(6)]
    sh1c = jnp.broadcast_to(mod[b:b + 1, 0:d].reshape(1, 1, d), (b, 1, d))
    sc1c = jnp.broadcast_to(mod[b:b + 1, d:2 * d].reshape(1, 1, d), (b, 1, d))

    w_lat, w_ctx = _w_in_layout(w_in)
    g1 = norm1_g[0].reshape(1, d)
    p = _in_projection(x, g1, sc1, sh1, w_lat, tm=INPROJ_TM)
    pc = _in_projection(ctx, g1, sc1c, sh1c, w_ctx, tm=lc)

    def decay_w(w, off):
        return jnp.zeros((LANE, GLA_QK_W), F32).at[off:off + GLA_LOWRANK].set(w).astype(BF16)

    y_gla = _gla(p, pc,
                 decay_w(w_decay_f[0], SM_AF), b_decay_f[0].reshape(1, -1),
                 decay_w(w_decay_b[0], SM_AB), b_decay_b[0].reshape(1, -1),
                 gla_norm_g[0].reshape(1, -1))

    wq = jnp.pad(w_uq[0].reshape(MLA_Q_RANK, MLA_HEADS, MLA_QK_DIM),
                 ((0, 0), (0, 0), (0, MLA_QK_PAD - MLA_QK_DIM))).reshape(MLA_Q_RANK, -1).astype(BF16)
    wkv = w_ukv[0].astype(BF16)
    qn_g = jnp.pad(q_norm_g[0], (0, MLA_QK_PAD - MLA_QK_DIM)).reshape(1, -1)
    kn_nope = k_norm_g[0][:MLA_NOPE].reshape(1, -1)
    kn_rope = jnp.pad(k_norm_g[0][MLA_NOPE:], (0, LANE - MLA_ROPE)).reshape(1, -1)
    cos, sin = _rope_tables(l, True)
    cos_c, sin_c = _rope_tables(lc, False)
    q_m = _mla_queries(p, q_a_norm_g[0].reshape(1, -1), wq, qn_g, cos, sin, tm=MLA_Q_TM)
    kva = kv_a_norm_g[0].reshape(1, -1)
    k_m, v_m = _mla_keys_values(p, pc, kva, wkv, kn_nope, kn_rope,
                                jnp.concatenate([cos, cos_c]), jnp.concatenate([sin, sin_c]))
    y_mla = _attention(q_m, k_m, v_m, tq=ATTN_TQ)

    y = _branch_merge(y_gla, y_mla, w_o_gla[0].astype(BF16), w_o_mla[0].astype(BF16), p,
                      tm=MERGE_TM, tn=MERGE_TN)
    w_router = jnp.concatenate(
        [w_router_group[0], w_router_expert[0],
         jnp.zeros((d, ROUTER_W - N_GROUPS - N_EXPERTS), F32)], axis=1).astype(BF16)
    b_router = jnp.concatenate(
        [b_router_group[0], b_router_expert[0],
         jnp.zeros((ROUTER_W - N_GROUPS - N_EXPERTS,), F32)]).reshape(1, -1)
    x1, h2, logits = _out_projection(y, w_out[0].astype(BF16), x, gt1, norm2_g[0].reshape(1, d),
                                     sc2, sh2, w_router, b_router, tm=OUTPROJ_TM)

    e_pick, w_pick = _route(logits.reshape(t, ROUTER_W), tm=ROUTE_TM)
    plan = _dispatch_plan(e_pick[:, :2].reshape(-1))
    yp = _experts(plan, h2.reshape(t * ROW_TILES, LANE),
                  w_exp_gate[0], w_exp_up[0], w_exp_down[0])
    return _combine(x1, gt2, w_pick, yp, tm=COMBINE_TM)
```
